```python
import math
import jax, jax.numpy as jnp
from jax import lax
import numpy as np

D_MODEL = 1024
BATCH = 8
SEQ = 2048
DEPTH = 4

DIFF_HEADS = 4
DIFF_QK_DIM = 64
DIFF_V_DIM = 2 * DIFF_QK_DIM
DIFF_WIDTH = DIFF_HEADS * DIFF_V_DIM
MLA_HEADS = 8
MLA_NOPE_DIM = 64
MLA_ROPE_DIM = 32
MLA_V_DIM = 64
MLA_Q_RANK = 384
MLA_KV_RANK = 256
MLA_WIDTH = MLA_HEADS * MLA_V_DIM
MIX_WIDTH = DIFF_WIDTH + MLA_WIDTH
ROPE_THETA = 10000.0
IN_DQ = DIFF_HEADS * 2 * DIFF_QK_DIM
IN_DK = DIFF_HEADS * 2 * DIFF_QK_DIM
IN_DV = DIFF_WIDTH
IN_MQ = MLA_Q_RANK
IN_MKV = MLA_KV_RANK + MLA_ROPE_DIM
IN_WIDTH = IN_DQ + IN_DK + IN_DV + IN_MQ + IN_MKV
N_BUCKETS = 32
MAX_EXACT = 16
MAX_DISTANCE = 128
D_FF = 2816
N_EXPERTS = 8
TOP_K = 2
Q_BLOCK = 128
NORM_EPS = 1e-6
ADA_CHUNKS = 6

kernel_name = 'hybrid_diffattn_mla_moe_adaln'


def rms_norm(x, g):
    xf = x.astype(jnp.float32)
    y = xf * lax.rsqrt(jnp.mean(xf * xf, axis=-1, keepdims=True) + NORM_EPS)
    return (y * g.astype(jnp.float32)).astype(x.dtype)


def rel_bucket(dist):
    n = jnp.maximum(dist, 0)
    nf = jnp.maximum(n, 1).astype(jnp.float32)
    large = MAX_EXACT + (jnp.log(nf / MAX_EXACT) / math.log(MAX_DISTANCE / MAX_EXACT)
                         * (N_BUCKETS - MAX_EXACT)).astype(jnp.int32)
    large = jnp.minimum(large, N_BUCKETS - 1)
    return jnp.where(n < MAX_EXACT, n, large)


def rope(x, positions):
    half = MLA_ROPE_DIM // 2
    inv_freq = ROPE_THETA ** (-jnp.arange(half, dtype=jnp.float32) / half)
    ang = positions.astype(jnp.float32)[..., None] * inv_freq
    ang = ang.reshape(ang.shape[:2] + (1,) * (x.ndim - 3) + (half,))
    cos = jnp.cos(ang).astype(x.dtype)
    sin = jnp.sin(ang).astype(x.dtype)
    x1, x2 = x[..., :half], x[..., half:]
    return jnp.concatenate([x1 * cos - x2 * sin, x2 * cos + x1 * sin], axis=-1)


def diff_attention(q, k, v, lam, positions, rel_bias):
    s = q.shape[1]
    scale = DIFF_QK_DIM ** -0.5
    outs = []
    for i in range(s // Q_BLOCK):
        q0, q1 = i * Q_BLOCK, (i + 1) * Q_BLOCK
        qb, kp, vp = q[:, q0:q1], k[:, :q1], v[:, :q1]
        dist = positions[:, q0:q1, None] - positions[:, None, :q1]
        bias = jnp.transpose(rel_bias[rel_bucket(dist)], (0, 3, 1, 2)).astype(jnp.float32)
        causal = jnp.arange(q0, q1)[:, None] >= jnp.arange(q1)[None, :]
        logits = jnp.einsum('bqhmd,bkhmd->bhmqk', qb, kp).astype(jnp.float32) * scale + bias[:, :, None]
        logits = jnp.where(causal, logits, -jnp.inf)
        p = jax.nn.softmax(logits, axis=-1)
        a = p[:, :, 0] - lam * p[:, :, 1]
        outs.append(jnp.einsum('bhqk,bkhd->bqhd', a.astype(v.dtype), vp))
    return jnp.concatenate(outs, axis=1)


def mla_attention(q_nope, q_rope, k_nope, k_rope, v):
    s = q_nope.shape[1]
    scale = (MLA_NOPE_DIM + MLA_ROPE_DIM) ** -0.5
    outs = []
    for i in range(s // Q_BLOCK):
        q0, q1 = i * Q_BLOCK, (i + 1) * Q_BLOCK
        causal = jnp.arange(q0, q1)[:, None] >= jnp.arange(q1)[None, :]
        logits = (jnp.einsum('bqhd,bkhd->bhqk', q_nope[:, q0:q1], k_nope[:, :q1])
                  + jnp.einsum('bqhr,bkr->bhqk', q_rope[:, q0:q1], k_rope[:, :q1])).astype(jnp.float32) * scale
        logits = jnp.where(causal, logits, -jnp.inf)
        p = jax.nn.softmax(logits, axis=-1)
        outs.append(jnp.einsum('bhqk,bkhd->bqhd', p.astype(v.dtype), v[:, :q1]))
    return jnp.concatenate(outs, axis=1)


def token_mixer(h, positions, layer, w_in, diff_lambda, diff_subln_g, rel_bias,
                mla_q_norm, w_uq, mla_kv_norm, w_ukv, w_o):
    b, s, _ = h.shape
    proj = h @ w_in
    cuts = [IN_DQ, IN_DQ + IN_DK, IN_DQ + IN_DK + IN_DV, IN_DQ + IN_DK + IN_DV + IN_MQ]
    dq, dk, dv, mq, mkv = jnp.split(proj, cuts, axis=-1)
    lam_init = 0.8 - 0.6 * math.exp(-0.3 * layer)
    lv = diff_lambda.astype(jnp.float32)
    lam = jnp.exp(jnp.sum(lv[0] * lv[1])) - jnp.exp(jnp.sum(lv[2] * lv[3])) + lam_init
    o_diff = diff_attention(dq.reshape(b, s, DIFF_HEADS, 2, DIFF_QK_DIM),
                            dk.reshape(b, s, DIFF_HEADS, 2, DIFF_QK_DIM),
                            dv.reshape(b, s, DIFF_HEADS, DIFF_V_DIM),
                            lam, positions, rel_bias)
    o_diff = (rms_norm(o_diff, diff_subln_g) * (1.0 - lam_init)).reshape(b, s, DIFF_WIDTH)
    q = (rms_norm(mq, mla_q_norm) @ w_uq).reshape(b, s, MLA_HEADS, MLA_NOPE_DIM + MLA_ROPE_DIM)
    q_nope = q[..., :MLA_NOPE_DIM]
    q_rope = rope(q[..., MLA_NOPE_DIM:], positions)
    kv = (rms_norm(mkv[..., :MLA_KV_RANK], mla_kv_norm) @ w_ukv).reshape(b, s, MLA_HEADS, MLA_NOPE_DIM + MLA_V_DIM)
    k_nope, v = kv[..., :MLA_NOPE_DIM], kv[..., MLA_NOPE_DIM:]
    k_rope = rope(mkv[..., MLA_KV_RANK:], positions)
    o_mla = mla_attention(q_nope, q_rope, k_nope, k_rope, v).reshape(b, s, MLA_WIDTH)
    return jnp.concatenate([o_diff, o_mla], axis=-1) @ w_o


def swiglu(h, w1, w3, w2):
    return (jax.nn.silu(h @ w1) * (h @ w3)) @ w2


def moe_ffn(h, w_router, w1, w3, w2):
    logits = (h @ w_router).astype(jnp.float32)
    top_v, top_i = lax.top_k(logits, TOP_K)
    top_w = jax.nn.softmax(top_v, axis=-1)
    gates = jnp.sum(jax.nn.one_hot(top_i, N_EXPERTS, dtype=jnp.float32) * top_w[..., None], axis=-2)
    y = jnp.zeros_like(h)
    for e in range(N_EXPERTS):
        y = y + gates[..., e:e + 1].astype(h.dtype) * swiglu(h, w1[e], w3[e], w2[e])
    return y


def setup_inputs(seed: int = 0) -> dict:
    key = jax.random.key(seed)
    ks = jax.random.split(key, 32)
    n_dense = (DEPTH + 1) // 2
    n_moe = DEPTH // 2
    f32 = jnp.float32

    def nrm(k, shape, scale):
        return jax.random.normal(k, shape, f32) * scale

    def gain(k, shape):
        return 1.0 + 0.01 * jax.random.normal(k, shape, f32)

    x = jax.random.normal(ks[0], (BATCH, SEQ, D_MODEL), f32)
    c = jax.random.normal(ks[1], (BATCH, D_MODEL), f32)
    offsets = jax.random.randint(ks[2], (BATCH, 1), 0, 4096, dtype=jnp.int32)
    positions = offsets + jnp.arange(SEQ, dtype=jnp.int32)[None, :]
    return {
        'x': x,
        'c': c,
        'positions': positions,
        'w_ada': nrm(ks[3], (DEPTH, D_MODEL, ADA_CHUNKS * D_MODEL), 0.1 * D_MODEL ** -0.5),
        'b_ada': nrm(ks[4], (DEPTH, ADA_CHUNKS * D_MODEL), 0.01),
        'g_attn': gain(ks[5], (DEPTH, D_MODEL)),
        'w_in': nrm(ks[6], (DEPTH, D_MODEL, IN_WIDTH), D_MODEL ** -0.5),
        'diff_lambda': nrm(ks[7], (DEPTH, 4, DIFF_QK_DIM), 0.1),
        'diff_subln_g': gain(ks[8], (DEPTH, DIFF_V_DIM)),
        'rel_bias': nrm(ks[9], (N_BUCKETS, DIFF_HEADS), 0.5),
        'mla_q_norm': gain(ks[10], (DEPTH, MLA_Q_RANK)),
        'w_uq': nrm(ks[11], (DEPTH, MLA_Q_RANK, MLA_HEADS * (MLA_NOPE_DIM + MLA_ROPE_DIM)), MLA_Q_RANK ** -0.5),
        'mla_kv_norm': gain(ks[12], (DEPTH, MLA_KV_RANK)),
        'w_ukv': nrm(ks[13], (DEPTH, MLA_KV_RANK, MLA_HEADS * (MLA_NOPE_DIM + MLA_V_DIM)), MLA_KV_RANK ** -0.5),
        'w_o': nrm(ks[14], (DEPTH, MIX_WIDTH, D_MODEL), MIX_WIDTH ** -0.5),
        'g_ffn': gain(ks[15], (DEPTH, D_MODEL)),
        'ffn_w1': nrm(ks[16], (n_dense, D_MODEL, D_FF), D_MODEL ** -0.5),
        'ffn_w3': nrm(ks[17], (n_dense, D_MODEL, D_FF), D_MODEL ** -0.5),
        'ffn_w2': nrm(ks[18], (n_dense, D_FF, D_MODEL), D_FF ** -0.5),
        'moe_router': nrm(ks[19], (n_moe, D_MODEL, N_EXPERTS), D_MODEL ** -0.5),
        'moe_w1': nrm(ks[20], (n_moe, N_EXPERTS, D_MODEL, D_FF), D_MODEL ** -0.5),
        'moe_w3': nrm(ks[21], (n_moe, N_EXPERTS, D_MODEL, D_FF), D_MODEL ** -0.5),
        'moe_w2': nrm(ks[22], (n_moe, N_EXPERTS, D_FF, D_MODEL), D_FF ** -0.5),
        'g_final': gain(ks[23], (D_MODEL,)),
    }


def reference(x, c, positions, w_ada, b_ada, g_attn, w_in, diff_lambda, diff_subln_g, rel_bias,
              mla_q_norm, w_uq, mla_kv_norm, w_ukv, w_o, g_ffn, ffn_w1, ffn_w3, ffn_w2,
              moe_router, moe_w1, moe_w3, moe_w2, g_final):
    cond = jax.nn.silu(c)
    for l in range(DEPTH):
        mod = cond @ w_ada[l] + b_ada[l]
        sh_a, sc_a, gt_a, sh_f, sc_f, gt_f = jnp.split(mod[:, None, :], ADA_CHUNKS, axis=-1)
        h = rms_norm(x, g_attn[l]) * (1.0 + sc_a) + sh_a
        x = x + (1.0 + gt_a) * token_mixer(h, positions, l, w_in[l], diff_lambda[l], diff_subln_g[l],
                                           rel_bias, mla_q_norm[l], w_uq[l], mla_kv_norm[l],
                                           w_ukv[l], w_o[l])
        h = rms_norm(x, g_ffn[l]) * (1.0 + sc_f) + sh_f
        if l % 2 == 0:
            y = swiglu(h, ffn_w1[l // 2], ffn_w3[l // 2], ffn_w2[l // 2])
        else:
            y = moe_ffn(h, moe_router[l // 2], moe_w1[l // 2], moe_w3[l // 2], moe_w2[l // 2])
        x = x + (1.0 + gt_f) * y
    return rms_norm(x, g_final)
```

```python
import functools
import math

import jax
import jax.numpy as jnp
from jax import lax
from jax.experimental import pallas as pl
from jax.experimental.pallas import tpu as pltpu

F32 = jnp.float32
BF16 = jnp.bfloat16

DIFF_HEADS = 4
DIFF_QK_DIM = 64
DIFF_V_DIM = 128
MLA_HEADS = 8
MLA_NOPE_DIM = 64
MLA_ROPE_DIM = 32
MLA_V_DIM = 64
MLA_Q_RANK = 384
MLA_KV_RANK = 256
ROPE_THETA = 10000.0
N_BUCKETS = 32
MAX_EXACT = 16
MAX_DISTANCE = 128
N_EXPERTS = 8
NORM_EPS = 1e-6
ADA_CHUNKS = 6

LANES = 128
NEG_BIG = -1e30
VMEM_LIMIT = 56 * 1024 * 1024

ATTN_TILE = 256
TOK_TILE = 512
FFN_TILE = 512
FF_CHUNK = 256

C_DQ, C_DK, C_DV, C_MQ, C_KV, C_KR, C_END = 0, 512, 1024, 1536, 1920, 2176, 2304


def _params(*sem):
    return pltpu.CompilerParams(dimension_semantics=sem, vmem_limit_bytes=VMEM_LIMIT)


def _rms(x, g):
    return x * lax.rsqrt(jnp.mean(x * x, axis=-1, keepdims=True) + NORM_EPS) * g


def _dot_nt(a, b):
    return lax.dot_general(a, b, (((1,), (1,)), ((), ())), preferred_element_type=F32)


def _ada_kernel(c_ref, w_ref, b_ref, o_ref):
    c = c_ref[...]
    cond = c * jax.nn.sigmoid(c)
    o_ref[0] = jnp.dot(cond, w_ref[0], preferred_element_type=F32,
                       precision=lax.Precision.HIGHEST) + b_ref[0]


def _ada(c, w_ada, b_ada):
    depth, d, n = w_ada.shape
    bsz = c.shape[0]
    tn = 1536
    return pl.pallas_call(
        _ada_kernel,
        grid=(depth, n // tn),
        in_specs=[pl.BlockSpec((bsz, d), lambda l, j: (0, 0)),
                  pl.BlockSpec((1, d, tn), lambda l, j: (l, 0, j)),
                  pl.BlockSpec((1, 1, tn), lambda l, j: (l, 0, j))],
        out_specs=pl.BlockSpec((1, bsz, tn), lambda l, j: (l, 0, j)),
        out_shape=jax.ShapeDtypeStruct((depth, bsz, n), F32),
        compiler_params=_params("arbitrary", "arbitrary"),
        name="ada_mod",
    )(c, w_ada, b_ada.reshape(depth, 1, n))


def _rope_tab_kernel(pos_ref, inv_ref, c_ref, sa_ref, sb_ref):
    pos = pos_ref[0].astype(F32)
    ang = pos * inv_ref[...]
    lane = lax.broadcasted_iota(jnp.int32, ang.shape, 1)
    cos, sin = jnp.cos(ang), jnp.sin(ang)
    lo = (lane >= MLA_NOPE_DIM) & (lane < MLA_NOPE_DIM + MLA_ROPE_DIM // 2)
    hi = (lane >= MLA_NOPE_DIM + MLA_ROPE_DIM // 2) & (lane < MLA_NOPE_DIM + MLA_ROPE_DIM)
    c_ref[0] = jnp.where(lane < MLA_NOPE_DIM, 1.0, jnp.where(lo | hi, cos, 0.0))
    sa_ref[0] = jnp.where(lo, -sin, 0.0)
    sb_ref[0] = jnp.where(hi, sin, 0.0)


def _rope_tables(positions):
    bsz, s = positions.shape
    half = MLA_ROPE_DIM // 2
    inv_freq = ROPE_THETA ** (-jnp.arange(half, dtype=F32) / half)
    inv_lane = jnp.concatenate([jnp.zeros((MLA_NOPE_DIM,), F32), inv_freq, inv_freq,
                                jnp.zeros((LANES - MLA_NOPE_DIM - MLA_ROPE_DIM,), F32)]).reshape(1, LANES)
    tm = TOK_TILE
    spec = pl.BlockSpec((1, tm, LANES), lambda b, i: (b, i, 0))
    shape = jax.ShapeDtypeStruct((bsz, s, LANES), F32)
    return pl.pallas_call(
        _rope_tab_kernel,
        grid=(bsz, s // tm),
        in_specs=[pl.BlockSpec((1, tm, 1), lambda b, i: (b, i, 0)),
                  pl.BlockSpec((1, LANES), lambda b, i: (0, 0))],
        out_specs=[spec, spec, spec],
        out_shape=[shape, shape, shape],
        compiler_params=_params("arbitrary", "arbitrary"),
        name="rope_tables",
    )(positions.reshape(bsz, s, 1), inv_lane)


def _bias_tile_kernel(rb_ref, o_ref):
    h, d = pl.program_id(0), pl.program_id(1)
    t = o_ref.shape[-1]
    r = lax.broadcasted_iota(jnp.int32, (t, t), 0)
    c = lax.broadcasted_iota(jnp.int32, (t, t), 1)
    dist = d * t + r - c
    n = jnp.maximum(dist, 0)
    nf = jnp.maximum(n, 1).astype(F32)
    large = MAX_EXACT + (jnp.log(nf / MAX_EXACT) / math.log(MAX_DISTANCE / MAX_EXACT)
                         * (N_BUCKETS - MAX_EXACT)).astype(jnp.int32)
    large = jnp.minimum(large, N_BUCKETS - 1)
    bucket = jnp.where(n < MAX_EXACT, n, large)
    val = jnp.zeros((t, t), F32)
    for j in range(N_BUCKETS):
        val = jnp.where(bucket == j, rb_ref[j, h], val)
    val = val - rb_ref[N_BUCKETS - 1, h]
    o_ref[0, 0] = jnp.where(dist < 0, NEG_BIG, val)


def _bias_tiles(rel_bias):
    t = ATTN_TILE
    return pl.pallas_call(
        _bias_tile_kernel,
        grid=(DIFF_HEADS, 2),
        in_specs=[pl.BlockSpec(memory_space=pltpu.SMEM)],
        out_specs=pl.BlockSpec((1, 1, t, t), lambda h, d: (h, d, 0, 0)),
        out_shape=jax.ShapeDtypeStruct((DIFF_HEADS, 2, t, t), F32),
        compiler_params=_params("arbitrary", "arbitrary"),
        name="bias_tiles",
    )(rel_bias)


def _pre_attn_kernel(x_ref, mod_ref, g_ref, w_in_ref, gq_ref, w_uq_ref, gkv_ref, w_uk_ref, w_uv_ref,
                     rc_ref, rsa_ref, rsb_ref,
                     dq_ref, dk_ref, dv_ref, mq_ref, mk_ref, mv_ref):
    x = x_ref[0]
    sh, sc = mod_ref[0, 0:1, :], mod_ref[0, 1:2, :]
    h = (_rms(x, g_ref[...]) * (1.0 + sc) + sh).astype(BF16)
    proj = jnp.dot(h, w_in_ref[...], preferred_element_type=F32)
    dq_ref[0] = (proj[:, C_DQ:C_DK] * (DIFF_QK_DIM ** -0.5)).astype(BF16)
    dk_ref[0] = proj[:, C_DK:C_DV].astype(BF16)
    dv_ref[0] = proj[:, C_DV:C_MQ].astype(BF16)

    rc, rsa, rsb = rc_ref[0], rsa_ref[0], rsb_ref[0]

    def rope(v):
        return v * rc + pltpu.roll(v, LANES - MLA_ROPE_DIM // 2, 1) * rsa + pltpu.roll(v, MLA_ROPE_DIM // 2, 1) * rsb

    qn = _rms(proj[:, C_MQ:C_KV], gq_ref[...]).astype(BF16)
    q = jnp.dot(qn, w_uq_ref[...], preferred_element_type=F32)
    kvn = _rms(proj[:, C_KV:C_KR], gkv_ref[...]).astype(BF16)
    kn = jnp.dot(kvn, w_uk_ref[...], preferred_element_type=F32)
    mv_ref[0] = jnp.dot(kvn, w_uv_ref[...], preferred_element_type=F32).astype(BF16)
    kr = rope(proj[:, C_KR:C_END])
    q_scale = (MLA_NOPE_DIM + MLA_ROPE_DIM) ** -0.5
    for hd in range(MLA_HEADS):
        sl = slice(hd * LANES, (hd + 1) * LANES)
        mq_ref[0, :, sl] = (rope(q[:, sl]) * q_scale).astype(BF16)
        mk_ref[0, :, sl] = (kn[:, sl] + kr).astype(BF16)


def _pre_attn(x, mod, g, w_in, gq, w_uq, gkv, w_uk, w_uv, rc, rsa, rsb):
    bsz, s, d = x.shape
    tm = TOK_TILE
    tok = lambda w: pl.BlockSpec((1, tm, w), lambda b, i: (b, i, 0))
    full = lambda a: pl.BlockSpec(a.shape, lambda b, i: (0,) * a.ndim)
    widths = (512, 512, 512, MLA_HEADS * LANES, MLA_HEADS * LANES, MLA_HEADS * MLA_V_DIM)
    return pl.pallas_call(
        _pre_attn_kernel,
        grid=(bsz, s // tm),
        in_specs=[tok(d), pl.BlockSpec((1, ADA_CHUNKS, d), lambda b, i: (b, 0, 0)), full(g), full(w_in),
                  full(gq), full(w_uq), full(gkv), full(w_uk), full(w_uv), tok(LANES), tok(LANES), tok(LANES)],
        out_specs=[tok(w) for w in widths],
        out_shape=[jax.ShapeDtypeStruct((bsz, s, w), BF16) for w in widths],
        compiler_params=_params("arbitrary", "arbitrary"),
        name="pre_attn",
    )(x, mod, g, w_in, gq, w_uq, gkv, w_uk, w_uv, rc, rsa, rsb)


def _softmax_init(m_ref, l_ref, acc_ref):
    m_ref[...] = jnp.full(m_ref.shape, NEG_BIG, F32)
    l_ref[...] = jnp.zeros(l_ref.shape, F32)
    acc_ref[...] = jnp.zeros(acc_ref.shape, F32)


def _softmax_update(s, v, m_ref, l_ref, acc_ref):
    m_prev = m_ref[...]
    m_new = jnp.maximum(m_prev, jnp.max(s, axis=1, keepdims=True))
    alpha = jnp.exp(m_prev - m_new)
    p = jnp.exp(s - m_new)
    l_ref[...] = alpha * l_ref[...] + jnp.sum(p, axis=1, keepdims=True)
    acc_ref[...] = alpha * acc_ref[...] + jnp.dot(p.astype(BF16), v, preferred_element_type=F32)
    m_ref[...] = m_new


def _diff_attn_kernel(lam_init_ref, lam_ref, g_ref, bias_ref, q_ref, k_ref, v_ref, o_ref,
                      m0, l0, a0, m1, l1, a1):
    i = pl.program_id(2)
    t = q_ref.shape[1]
    _softmax_init(m0, l0, a0)
    _softmax_init(m1, l1, a1)
    q = q_ref[0]
    lane = lax.broadcasted_iota(jnp.int32, q.shape, 1)
    q0 = jnp.where(lane < DIFF_QK_DIM, q, jnp.zeros_like(q))
    q1 = jnp.where(lane >= DIFF_QK_DIM, q, jnp.zeros_like(q))

    def step(j, bias):
        start = pl.multiple_of(j * t, t)
        k = k_ref[0, pl.ds(start, t), :]
        v = v_ref[0, pl.ds(start, t), :]
        s0, s1 = _dot_nt(q0, k), _dot_nt(q1, k)
        if bias is not None:
            s0, s1 = s0 + bias, s1 + bias
        _softmax_update(s0, v, m0, l0, a0)
        _softmax_update(s1, v, m1, l1, a1)

    def far(j, carry):
        step(j, None)
        return carry

    lax.fori_loop(0, jnp.maximum(i - 1, 0), far, 0)

    @pl.when(i >= 1)
    def _():
        step(i - 1, bias_ref[0, 1])

    step(i, bias_ref[0, 0])

    lv = lam_ref[...]
    lam = (jnp.exp(jnp.sum(lv[0:1] * lv[1:2], keepdims=True)) - jnp.exp(jnp.sum(lv[2:3] * lv[3:4], keepdims=True))
           + lam_init_ref[0])
    o = a0[...] / l0[...] - lam * (a1[...] / l1[...])
    o_ref[0] = (_rms(o, g_ref[...]) * (1.0 - lam_init_ref[0])).astype(BF16)


def _diff_attn(lam_init, diff_lambda, g, bias, dq, dk, dv):
    bsz, s, _ = dq.shape
    t = ATTN_TILE
    assert MAX_DISTANCE <= t and s % t == 0
    return pl.pallas_call(
        _diff_attn_kernel,
        grid=(bsz, DIFF_HEADS, s // t),
        in_specs=[pl.BlockSpec(memory_space=pltpu.SMEM),
                  pl.BlockSpec(diff_lambda.shape, lambda b, h, i: (0, 0)),
                  pl.BlockSpec(g.shape, lambda b, h, i: (0, 0)),
                  pl.BlockSpec((1, 2, t, t), lambda b, h, i: (h, 0, 0, 0)),
                  pl.BlockSpec((1, t, LANES), lambda b, h, i: (b, i, h)),
                  pl.BlockSpec((1, s, LANES), lambda b, h, i: (b, 0, h)),
                  pl.BlockSpec((1, s, LANES), lambda b, h, i: (b, 0, h))],
        out_specs=pl.BlockSpec((1, t, LANES), lambda b, h, i: (b, i, h)),
        out_shape=jax.ShapeDtypeStruct(dq.shape, BF16),
        scratch_shapes=[pltpu.VMEM((t, 1), F32), pltpu.VMEM((t, 1), F32), pltpu.VMEM((t, LANES), F32)] * 2,
        compiler_params=_params("arbitrary", "arbitrary", "arbitrary"),
        name="diff_attn",
    )(lam_init, diff_lambda, g, bias, dq, dk, dv)


def _mla_attn_kernel(q_ref, k_ref, v_ref, o_ref, ma, la, aa, mb, lb, ab):
    i = pl.program_id(2)
    t = q_ref.shape[1]
    _softmax_init(ma, la, aa)
    _softmax_init(mb, lb, ab)
    qa, qb = q_ref[0, :, :LANES], q_ref[0, :, LANES:]

    def step(j, masked):
        start = pl.multiple_of(j * t, t)
        ka = k_ref[0, pl.ds(start, t), :LANES]
        kb = k_ref[0, pl.ds(start, t), LANES:]
        v = v_ref[0, pl.ds(start, t), :]
        sa, sb = _dot_nt(qa, ka), _dot_nt(qb, kb)
        if masked:
            r = lax.broadcasted_iota(jnp.int32, sa.shape, 0)
            c = lax.broadcasted_iota(jnp.int32, sa.shape, 1)
            sa = jnp.where(r >= c, sa, NEG_BIG)
            sb = jnp.where(r >= c, sb, NEG_BIG)
        _softmax_update(sa, v, ma, la, aa)
        _softmax_update(sb, v, mb, lb, ab)

    def far(j, carry):
        step(j, False)
        return carry

    lax.fori_loop(0, i, far, 0)
    step(i, True)
    lane = lax.broadcasted_iota(jnp.int32, aa.shape, 1)
    o_ref[0] = jnp.where(lane < MLA_V_DIM, aa[...] / la[...], ab[...] / lb[...]).astype(BF16)


def _mla_attn(mq, mk, mv):
    bsz, s, _ = mq.shape
    t = ATTN_TILE
    pairs = MLA_HEADS // 2
    return pl.pallas_call(
        _mla_attn_kernel,
        grid=(bsz, pairs, s // t),
        in_specs=[pl.BlockSpec((1, t, 2 * LANES), lambda b, u, i: (b, i, u)),
                  pl.BlockSpec((1, s, 2 * LANES), lambda b, u, i: (b, 0, u)),
                  pl.BlockSpec((1, s, LANES), lambda b, u, i: (b, 0, u))],
        out_specs=pl.BlockSpec((1, t, LANES), lambda b, u, i: (b, i, u)),
        out_shape=jax.ShapeDtypeStruct(mv.shape, BF16),
        scratch_shapes=[pltpu.VMEM((t, 1), F32), pltpu.VMEM((t, 1), F32), pltpu.VMEM((t, LANES), F32)] * 2,
        compiler_params=_params("arbitrary", "arbitrary", "arbitrary"),
        name="mla_attn",
    )(mq, mk, mv)


def _post_attn_kernel(od_ref, om_ref, x_ref, mod_ref, g_ref, wo_d_ref, wo_m_ref, wr_ref,
                      x1_ref, h_ref, gates_ref):
    y = (jnp.dot(od_ref[0], wo_d_ref[...], preferred_element_type=F32)
         + jnp.dot(om_ref[0], wo_m_ref[...], preferred_element_type=F32))
    gt_a, sh_f, sc_f = mod_ref[0, 2:3, :], mod_ref[0, 3:4, :], mod_ref[0, 4:5, :]
    x1 = x_ref[0] + (1.0 + gt_a) * y
    x1_ref[0] = x1
    h = _rms(x1, g_ref[...]) * (1.0 + sc_f) + sh_f
    h_ref[0] = h.astype(BF16)
    logits = jnp.dot(h, wr_ref[...], preferred_element_type=F32, precision=lax.Precision.HIGHEST)
    lane = lax.broadcasted_iota(jnp.int32, logits.shape, 1)
    logits = jnp.where(lane < N_EXPERTS, logits, -jnp.inf)
    v1 = jnp.max(logits, axis=1, keepdims=True)
    i1 = jnp.min(jnp.where(logits == v1, lane, LANES), axis=1, keepdims=True)
    rest = jnp.where(lane == i1, -jnp.inf, logits)
    v2 = jnp.max(rest, axis=1, keepdims=True)
    i2 = jnp.min(jnp.where(rest == v2, lane, LANES), axis=1, keepdims=True)
    e2 = jnp.exp(v2 - v1)
    w1 = 1.0 / (1.0 + e2)
    w2 = e2 / (1.0 + e2)
    gates_ref[0] = jnp.where(lane == i1, w1, 0.0) + jnp.where(lane == i2, w2, 0.0)


def _post_attn(o_diff, o_mla, x, mod, g, wo_d, wo_m, w_router):
    bsz, s, d = x.shape
    tm = TOK_TILE
    tok = lambda w: pl.BlockSpec((1, tm, w), lambda b, i: (b, i, 0))
    full = lambda a: pl.BlockSpec(a.shape, lambda b, i: (0,) * a.ndim)
    return pl.pallas_call(
        _post_attn_kernel,
        grid=(bsz, s // tm),
        in_specs=[tok(o_diff.shape[-1]), tok(o_mla.shape[-1]), tok(d),
                  pl.BlockSpec((1, ADA_CHUNKS, d), lambda b, i: (b, 0, 0)),
                  full(g), full(wo_d), full(wo_m), full(w_router)],
        out_specs=[tok(d), tok(d), tok(LANES)],
        out_shape=[jax.ShapeDtypeStruct((bsz, s, d), F32), jax.ShapeDtypeStruct((bsz, s, d), BF16),
                   jax.ShapeDtypeStruct((bsz, s, LANES), F32)],
        compiler_params=_params("arbitrary", "arbitrary"),
        name="post_attn",
    )(o_diff, o_mla, x, mod, g, wo_d, wo_m, w_router)


def _ffn_kernel(h_ref, x1_ref, gates_ref, mod_ref, w1_ref, w3_ref, w2_ref, o_ref, acc_ref, *, gated):
    e = pl.program_id(2)

    @pl.when(e == 0)
    def _():
        acc_ref[...] = jnp.zeros(acc_ref.shape, F32)

    h = h_ref[0]
    d_ff = w1_ref.shape[-1]
    y = jnp.zeros(acc_ref.shape, F32)
    for c0 in range(0, d_ff, FF_CHUNK):
        a = jnp.dot(h, w1_ref[0, :, c0:c0 + FF_CHUNK], preferred_element_type=F32)
        b = jnp.dot(h, w3_ref[0, :, c0:c0 + FF_CHUNK], preferred_element_type=F32)
        u = (a * jax.nn.sigmoid(a) * b).astype(BF16)
        y = y + jnp.dot(u, w2_ref[0, c0:c0 + FF_CHUNK, :], preferred_element_type=F32)
    if gated:
        lane = lax.broadcasted_iota(jnp.int32, gates_ref.shape[1:], 1)
        gate = jnp.sum(jnp.where(lane == e, gates_ref[0], 0.0), axis=1, keepdims=True)
        y = gate * y
    acc_ref[...] += y

    @pl.when(e == pl.num_programs(2) - 1)
    def _():
        gt_f = mod_ref[0, 5:6, :]
        o_ref[0] = x1_ref[0] + (1.0 + gt_f) * acc_ref[...]


def _ffn(h, x1, gates, mod, w1, w3, w2, gated):
    bsz, s, d = x1.shape
    n_e, _, d_ff = w1.shape
    tm = FFN_TILE
    tok = lambda w: pl.BlockSpec((1, tm, w), lambda b, i, e: (b, i, 0))
    return pl.pallas_call(
        functools.partial(_ffn_kernel, gated=gated),
        grid=(bsz, s // tm, n_e),
        in_specs=[tok(d), tok(d), tok(LANES),
                  pl.BlockSpec((1, ADA_CHUNKS, d), lambda b, i, e: (b, 0, 0)),
                  pl.BlockSpec((1, d, d_ff), lambda b, i, e: (e, 0, 0)),
                  pl.BlockSpec((1, d, d_ff), lambda b, i, e: (e, 0, 0)),
                  pl.BlockSpec((1, d_ff, d), lambda b, i, e: (e, 0, 0))],
        out_specs=tok(d),
        out_shape=jax.ShapeDtypeStruct(x1.shape, F32),
        scratch_shapes=[pltpu.VMEM((tm, d), F32)],
        compiler_params=_params("arbitrary", "arbitrary", "arbitrary"),
        name="ffn_moe" if gated else "ffn_dense",
    )(h, x1, gates, mod, w1, w3, w2)


def _final_norm_kernel(x_ref, g_ref, o_ref):
    o_ref[0] = _rms(x_ref[0], g_ref[...])


def _final_norm(x, g):
    bsz, s, d = x.shape
    tm = TOK_TILE
    tok = pl.BlockSpec((1, tm, d), lambda b, i: (b, i, 0))
    return pl.pallas_call(
        _final_norm_kernel,
        grid=(bsz, s // tm),
        in_specs=[tok, pl.BlockSpec(g.shape, lambda b, i: (0, 0))],
        out_specs=tok,
        out_shape=jax.ShapeDtypeStruct(x.shape, F32),
        compiler_params=_params("arbitrary", "arbitrary"),
        name="final_norm",
    )(x, g)


def _pad_heads(w, heads, width, lo, hi):
    k = w.shape[0]
    w = w.reshape(k, heads, width)[:, :, lo:hi]
    return jnp.pad(w, ((0, 0), (0, 0), (0, LANES - (hi - lo)))).reshape(k, heads * LANES)


def _prep_w_in(w_in):
    d = w_in.shape[0]
    kr = w_in[:, C_KR:]
    kr_block = jnp.concatenate([jnp.zeros((d, MLA_NOPE_DIM), w_in.dtype), kr,
                                jnp.zeros((d, LANES - MLA_NOPE_DIM - MLA_ROPE_DIM), w_in.dtype)], axis=1)
    return jnp.concatenate([w_in[:, :C_KR], kr_block], axis=1).astype(BF16)


def kernel(x, c, positions, w_ada, b_ada, g_attn, w_in, diff_lambda, diff_subln_g, rel_bias, mla_q_norm, w_uq, mla_kv_norm, w_ukv, w_o, g_ffn, ffn_w1, ffn_w3, ffn_w2, moe_router, moe_w1, moe_w3, moe_w2, g_final):
    depth = w_ada.shape[0]
    bsz, s, d = x.shape
    mods = _ada(c, w_ada, b_ada).reshape(depth, bsz, ADA_CHUNKS, d)
    rc, rsa, rsb = _rope_tables(positions)
    bias = _bias_tiles(rel_bias)
    ones_gates = jnp.ones((bsz, s, LANES), F32)
    qk_w = MLA_NOPE_DIM + MLA_ROPE_DIM
    kv_w = MLA_NOPE_DIM + MLA_V_DIM
    for l in range(depth):
        mod = mods[l]
        lam_init = jnp.full((1,), 0.8 - 0.6 * math.exp(-0.3 * l), F32)
        w_uq_p = _pad_heads(w_uq[l], MLA_HEADS, qk_w, 0, qk_w).astype(BF16)
        w_uk_p = _pad_heads(w_ukv[l], MLA_HEADS, kv_w, 0, MLA_NOPE_DIM).astype(BF16)
        w_uv_p = w_ukv[l].reshape(MLA_KV_RANK, MLA_HEADS, kv_w)[:, :, MLA_NOPE_DIM:].reshape(
            MLA_KV_RANK, MLA_HEADS * MLA_V_DIM).astype(BF16)
        dq, dk, dv, mq, mk, mv = _pre_attn(
            x, mod, g_attn[l].reshape(1, d), _prep_w_in(w_in[l]), mla_q_norm[l].reshape(1, -1), w_uq_p,
            mla_kv_norm[l].reshape(1, -1), w_uk_p, w_uv_p, rc, rsa, rsb)
        o_diff = _diff_attn(lam_init, diff_lambda[l], diff_subln_g[l].reshape(1, -1), bias, dq, dk, dv)
        o_mla = _mla_attn(mq, mk, mv)
        wo = w_o[l].astype(BF16)
        n_diff = DIFF_HEADS * DIFF_V_DIM
        is_moe = l % 2 == 1
        w_router = moe_router[l // 2] if is_moe else jnp.zeros((d, N_EXPERTS), F32)
        w_router = jnp.pad(w_router, ((0, 0), (0, LANES - N_EXPERTS)))
        x1, h, gates = _post_attn(o_diff, o_mla, x, mod, g_ffn[l].reshape(1, d), wo[:n_diff], wo[n_diff:], w_router)
        if is_moe:
            x = _ffn(h, x1, gates, mod, moe_w1[l // 2].astype(BF16), moe_w3[l // 2].astype(BF16),
                     moe_w2[l // 2].astype(BF16), True)
        else:
            x = _ffn(h, x1, ones_gates, mod, ffn_w1[l // 2][None].astype(BF16), ffn_w3[l // 2][None].astype(BF16),
                     ffn_w2[l // 2][None].astype(BF16), False)
    return _final_norm(x, g_final.reshape(1, d))
```

```python
import functools
import math

import jax
import jax.numpy as jnp
from jax import lax
from jax.experimental import pallas as pl
from jax.experimental.pallas import tpu as pltpu

F32 = jnp.float32
BF16 = jnp.bfloat16

DIFF_HEADS = 4
DIFF_QK_DIM = 64
DIFF_V_DIM = 128
MLA_HEADS = 8
MLA_NOPE_DIM = 64
MLA_ROPE_DIM = 32
MLA_V_DIM = 64
MLA_Q_RANK = 384
MLA_KV_RANK = 256
ROPE_THETA = 10000.0
N_BUCKETS = 32
MAX_EXACT = 16
MAX_DISTANCE = 128
N_EXPERTS = 8
NORM_EPS = 1e-6
ADA_CHUNKS = 6

LANES = 128
NEG_BIG = -1e30
LOG2E = math.log2(math.e)
VMEM_LIMIT = 56 * 1024 * 1024

ATTN_TILE = 256
AHEAD = 3
TOK_TILE = 512
FFN_TILE = 512
FF_CHUNK = 256

C_DQ, C_DK, C_DV, C_MQ, C_KV, C_KR, C_END = 0, 512, 1024, 1536, 1920, 2176, 2304


def _params(*sem):
    return pltpu.CompilerParams(dimension_semantics=sem, vmem_limit_bytes=VMEM_LIMIT)


def _rms(x, g):
    return x * lax.rsqrt(jnp.mean(x * x, axis=-1, keepdims=True) + NORM_EPS) * g


def _dot_nt(a, b):
    return lax.dot_general(a, b, (((1,), (1,)), ((), ())), preferred_element_type=F32)


def _ada_kernel(c_ref, w_ref, b_ref, o_ref):
    c = c_ref[...]
    cond = c * jax.nn.sigmoid(c)
    o_ref[0] = jnp.dot(cond, w_ref[0], preferred_element_type=F32,
                       precision=lax.Precision.HIGHEST) + b_ref[0]


def _ada(c, w_ada, b_ada):
    depth, d, n = w_ada.shape
    bsz = c.shape[0]
    tn = 1536
    return pl.pallas_call(
        _ada_kernel,
        grid=(depth, n // tn),
        in_specs=[pl.BlockSpec((bsz, d), lambda l, j: (0, 0)),
                  pl.BlockSpec((1, d, tn), lambda l, j: (l, 0, j)),
                  pl.BlockSpec((1, 1, tn), lambda l, j: (l, 0, j))],
        out_specs=pl.BlockSpec((1, bsz, tn), lambda l, j: (l, 0, j)),
        out_shape=jax.ShapeDtypeStruct((depth, bsz, n), F32),
        compiler_params=_params("arbitrary", "arbitrary"),
        name="ada_mod",
    )(c, w_ada, b_ada.reshape(depth, 1, n))


def _rope_tab_kernel(pos_ref, inv_ref, c_ref, sa_ref, sb_ref):
    pos = pos_ref[0].astype(F32)
    ang = pos * inv_ref[...]
    lane = lax.broadcasted_iota(jnp.int32, ang.shape, 1)
    cos, sin = jnp.cos(ang), jnp.sin(ang)
    lo = (lane >= MLA_NOPE_DIM) & (lane < MLA_NOPE_DIM + MLA_ROPE_DIM // 2)
    hi = (lane >= MLA_NOPE_DIM + MLA_ROPE_DIM // 2) & (lane < MLA_NOPE_DIM + MLA_ROPE_DIM)
    c_ref[0] = jnp.where(lane < MLA_NOPE_DIM, 1.0, jnp.where(lo | hi, cos, 0.0))
    sa_ref[0] = jnp.where(lo, -sin, 0.0)
    sb_ref[0] = jnp.where(hi, sin, 0.0)


def _rope_tables(positions):
    bsz, s = positions.shape
    half = MLA_ROPE_DIM // 2
    inv_freq = ROPE_THETA ** (-jnp.arange(half, dtype=F32) / half)
    inv_lane = jnp.concatenate([jnp.zeros((MLA_NOPE_DIM,), F32), inv_freq, inv_freq,
                                jnp.zeros((LANES - MLA_NOPE_DIM - MLA_ROPE_DIM,), F32)]).reshape(1, LANES)
    tm = TOK_TILE
    spec = pl.BlockSpec((1, tm, LANES), lambda b, i: (b, i, 0))
    shape = jax.ShapeDtypeStruct((bsz, s, LANES), F32)
    return pl.pallas_call(
        _rope_tab_kernel,
        grid=(bsz, s // tm),
        in_specs=[pl.BlockSpec((1, tm, 1), lambda b, i: (b, i, 0)),
                  pl.BlockSpec((1, LANES), lambda b, i: (0, 0))],
        out_specs=[spec, spec, spec],
        out_shape=[shape, shape, shape],
        compiler_params=_params("arbitrary", "arbitrary"),
        name="rope_tables",
    )(positions.reshape(bsz, s, 1), inv_lane)


def _bias_tile_kernel(rb_ref, o_ref):
    h, d = pl.program_id(0), pl.program_id(1)
    t = o_ref.shape[-1]
    key = lax.broadcasted_iota(jnp.int32, (t, t), 0)
    qry = lax.broadcasted_iota(jnp.int32, (t, t), 1)
    dist = d * t + qry - key
    n = jnp.maximum(dist, 0)
    nf = jnp.maximum(n, 1).astype(F32)
    large = MAX_EXACT + (jnp.log(nf / MAX_EXACT) / math.log(MAX_DISTANCE / MAX_EXACT)
                         * (N_BUCKETS - MAX_EXACT)).astype(jnp.int32)
    large = jnp.minimum(large, N_BUCKETS - 1)
    bucket = jnp.where(n < MAX_EXACT, n, large)
    val = jnp.zeros((t, t), F32)
    for j in range(N_BUCKETS):
        val = jnp.where(bucket == j, rb_ref[j, h], val)
    val = (rb_ref[N_BUCKETS - 1, h] - val) * LOG2E
    o_ref[0, 0] = jnp.where(dist < 0, -NEG_BIG, val)


def _bias_tiles(rel_bias):
    t = ATTN_TILE
    return pl.pallas_call(
        _bias_tile_kernel,
        grid=(DIFF_HEADS, 2),
        in_specs=[pl.BlockSpec(memory_space=pltpu.SMEM)],
        out_specs=pl.BlockSpec((1, 1, t, t), lambda h, d: (h, d, 0, 0)),
        out_shape=jax.ShapeDtypeStruct((DIFF_HEADS, 2, t, t), F32),
        compiler_params=_params("arbitrary", "arbitrary"),
        name="bias_tiles",
    )(rel_bias)


def _pre_attn_kernel(x_ref, mod_ref, g_ref, w_in_ref, gq_ref, w_uq_ref, gkv_ref, w_uk_ref, w_uv_ref,
                     rc_ref, rsa_ref, rsb_ref,
                     dq_ref, dk_ref, dv_ref, mq_ref, mk_ref, mv_ref):
    x = x_ref[0]
    sh, sc = mod_ref[0, 0:1, :], mod_ref[0, 1:2, :]
    h = (_rms(x, g_ref[...]) * (1.0 + sc) + sh).astype(BF16)
    proj = jnp.dot(h, w_in_ref[...], preferred_element_type=F32)
    dq_ref[0] = (proj[:, C_DQ:C_DK] * (DIFF_QK_DIM ** -0.5 * LOG2E)).astype(BF16)
    dk_ref[0] = proj[:, C_DK:C_DV].astype(BF16)
    dv_ref[0] = proj[:, C_DV:C_MQ].astype(BF16)

    rc, rsa, rsb = rc_ref[0], rsa_ref[0], rsb_ref[0]

    def rope(v):
        return v * rc + pltpu.roll(v, LANES - MLA_ROPE_DIM // 2, 1) * rsa + pltpu.roll(v, MLA_ROPE_DIM // 2, 1) * rsb

    qn = _rms(proj[:, C_MQ:C_KV], gq_ref[...]).astype(BF16)
    q = jnp.dot(qn, w_uq_ref[...], preferred_element_type=F32)
    kvn = _rms(proj[:, C_KV:C_KR], gkv_ref[...]).astype(BF16)
    kn = jnp.dot(kvn, w_uk_ref[...], preferred_element_type=F32)
    mv_ref[0] = jnp.dot(kvn, w_uv_ref[...], preferred_element_type=F32).astype(BF16)
    kr = rope(proj[:, C_KR:C_END])
    q_scale = (MLA_NOPE_DIM + MLA_ROPE_DIM) ** -0.5 * LOG2E
    for hd in range(MLA_HEADS):
        sl = slice(hd * LANES, (hd + 1) * LANES)
        mq_ref[0, :, sl] = (rope(q[:, sl]) * q_scale).astype(BF16)
        mk_ref[0, :, sl] = (kn[:, sl] + kr).astype(BF16)


def _pre_attn(x, mod, g, w_in, gq, w_uq, gkv, w_uk, w_uv, rc, rsa, rsb):
    bsz, s, d = x.shape
    tm = TOK_TILE
    tok = lambda w: pl.BlockSpec((1, tm, w), lambda b, i: (b, i, 0))
    full = lambda a: pl.BlockSpec(a.shape, lambda b, i: (0,) * a.ndim)
    widths = (512, 512, 512, MLA_HEADS * LANES, MLA_HEADS * LANES, MLA_HEADS * MLA_V_DIM)
    return pl.pallas_call(
        _pre_attn_kernel,
        grid=(bsz, s // tm),
        in_specs=[tok(d), pl.BlockSpec((1, ADA_CHUNKS, d), lambda b, i: (b, 0, 0)), full(g), full(w_in),
                  full(gq), full(w_uq), full(gkv), full(w_uk), full(w_uv), tok(LANES), tok(LANES), tok(LANES)],
        out_specs=[tok(w) for w in widths],
        out_shape=[jax.ShapeDtypeStruct((bsz, s, w), BF16) for w in widths],
        compiler_params=_params("arbitrary", "arbitrary"),
        name="pre_attn",
    )(x, mod, g, w_in, gq, w_uq, gkv, w_uk, w_uv, rc, rsa, rsb)


def _softmax_init(m_ref, l_ref, acc_ref):
    m_ref[...] = jnp.full(m_ref.shape, NEG_BIG, F32)
    l_ref[...] = jnp.zeros(l_ref.shape, F32)
    acc_ref[...] = jnp.zeros(acc_ref.shape, F32)


def _softmax_update(st, vt, m_ref, l_ref, acc_ref):
    m_prev = m_ref[...]
    m_new = jnp.maximum(m_prev, jnp.max(st, axis=0, keepdims=True))
    alpha = jnp.exp2(m_prev - m_new)
    p = jnp.exp2(st - m_new)
    l_ref[...] = alpha * l_ref[...] + jnp.sum(p, axis=0, keepdims=True)
    acc_ref[...] = alpha * acc_ref[...] + jnp.dot(vt, p.astype(BF16), preferred_element_type=F32)
    m_ref[...] = m_new


def _diff_attn_kernel(lam_init_ref, lam_ref, g_ref, bias_ref, q_ref, k_ref, v_ref, o_ref,
                      vt_ref, m_ref, l_ref, acc_ref):
    i = pl.program_id(1)
    t = q_ref.shape[1]

    @pl.when(i == 0)
    def _():
        vt_ref[...] = v_ref[0].T

    n_maps = 2 * DIFF_HEADS
    for c in range(n_maps):
        _softmax_init(m_ref.at[c], l_ref.at[c], acc_ref.at[c])
    qts = []
    for h in range(DIFF_HEADS):
        qt = q_ref[0, :, h * LANES:(h + 1) * LANES].T
        row = lax.broadcasted_iota(jnp.int32, qt.shape, 0)
        qts.append(jnp.where(row < DIFF_QK_DIM, qt, jnp.zeros_like(qt)))
        qts.append(jnp.where(row >= DIFF_QK_DIM, qt, jnp.zeros_like(qt)))

    def step(j, bias_idx):
        start = pl.multiple_of(j * t, t)

        def scores(c):
            k = k_ref[0, pl.ds(start, t), (c // 2) * LANES:(c // 2 + 1) * LANES]
            return jnp.dot(k, qts[c], preferred_element_type=F32)

        pending = [scores(c) for c in range(AHEAD)]
        for c in range(n_maps):
            if c + AHEAD < n_maps:
                pending.append(scores(c + AHEAD))
            st = pending.pop(0)
            h = c // 2
            if bias_idx is not None:
                st = st - bias_ref[h, bias_idx]
            vt = vt_ref[h * LANES:(h + 1) * LANES, pl.ds(start, t)]
            _softmax_update(st, vt, m_ref.at[c], l_ref.at[c], acc_ref.at[c])

    def far(j, carry):
        step(j, None)
        return carry

    lax.fori_loop(0, jnp.maximum(i - 1, 0), far, 0)

    @pl.when(i >= 1)
    def _():
        step(i - 1, 1)

    step(i, 0)

    lv = lam_ref[...]
    lam = (jnp.exp(jnp.sum(lv[0:1] * lv[1:2], keepdims=True)) - jnp.exp(jnp.sum(lv[2:3] * lv[3:4], keepdims=True))
           + lam_init_ref[0])
    for h in range(DIFF_HEADS):
        c0, c1 = 2 * h, 2 * h + 1
        ot = acc_ref[c0] / l_ref[c0] - lam * (acc_ref[c1] / l_ref[c1])
        ot = ot * lax.rsqrt(jnp.mean(ot * ot, axis=0, keepdims=True) + NORM_EPS) * g_ref[...]
        o_ref[0, :, h * LANES:(h + 1) * LANES] = (ot * (1.0 - lam_init_ref[0])).T.astype(BF16)


def _diff_attn(lam_init, diff_lambda, g, bias, dq, dk, dv):
    bsz, s, w = dq.shape
    t = ATTN_TILE
    assert MAX_DISTANCE <= t and s % t == 0
    n_maps = 2 * DIFF_HEADS
    return pl.pallas_call(
        _diff_attn_kernel,
        grid=(bsz, s // t),
        in_specs=[pl.BlockSpec(memory_space=pltpu.SMEM),
                  pl.BlockSpec(diff_lambda.shape, lambda b, i: (0, 0)),
                  pl.BlockSpec(g.shape, lambda b, i: (0, 0)),
                  pl.BlockSpec(bias.shape, lambda b, i: (0, 0, 0, 0)),
                  pl.BlockSpec((1, t, w), lambda b, i: (b, i, 0)),
                  pl.BlockSpec((1, s, w), lambda b, i: (b, 0, 0)),
                  pl.BlockSpec((1, s, w), lambda b, i: (b, 0, 0))],
        out_specs=pl.BlockSpec((1, t, w), lambda b, i: (b, i, 0)),
        out_shape=jax.ShapeDtypeStruct(dq.shape, BF16),
        scratch_shapes=[pltpu.VMEM((w, s), BF16), pltpu.VMEM((n_maps, 1, t), F32),
                        pltpu.VMEM((n_maps, 1, t), F32), pltpu.VMEM((n_maps, LANES, t), F32)],
        compiler_params=_params("arbitrary", "arbitrary"),
        name="diff_attn",
    )(lam_init, diff_lambda, g, bias, dq, dk, dv)


def _mla_attn_kernel(q_ref, k_ref, v_ref, o_ref, vt_ref, m_ref, l_ref, acc_ref):
    i = pl.program_id(1)
    t = q_ref.shape[1]

    @pl.when(i == 0)
    def _():
        vt_ref[...] = v_ref[0].T

    for h in range(MLA_HEADS):
        _softmax_init(m_ref.at[h], l_ref.at[h], acc_ref.at[h])
    qts = [q_ref[0, :, h * LANES:(h + 1) * LANES].T for h in range(MLA_HEADS)]

    def step(j, masked):
        start = pl.multiple_of(j * t, t)

        def scores(h):
            k = k_ref[0, pl.ds(start, t), h * LANES:(h + 1) * LANES]
            return jnp.dot(k, qts[h], preferred_element_type=F32)

        pending = [scores(h) for h in range(AHEAD)]
        for h in range(MLA_HEADS):
            if h + AHEAD < MLA_HEADS:
                pending.append(scores(h + AHEAD))
            st = pending.pop(0)
            vt = vt_ref[h * MLA_V_DIM:(h + 1) * MLA_V_DIM, pl.ds(start, t)]
            if masked:
                key = lax.broadcasted_iota(jnp.int32, st.shape, 0)
                qry = lax.broadcasted_iota(jnp.int32, st.shape, 1)
                st = jnp.where(key <= qry, st, NEG_BIG)
            _softmax_update(st, vt, m_ref.at[h], l_ref.at[h], acc_ref.at[h])

    def far(j, carry):
        step(j, False)
        return carry

    lax.fori_loop(0, i, far, 0)
    step(i, True)
    for u in range(MLA_HEADS // 2):
        ot = jnp.concatenate([acc_ref[2 * u] / l_ref[2 * u], acc_ref[2 * u + 1] / l_ref[2 * u + 1]], axis=0)
        o_ref[0, :, u * LANES:(u + 1) * LANES] = ot.T.astype(BF16)


def _mla_attn(mq, mk, mv):
    bsz, s, wq = mq.shape
    wv = mv.shape[-1]
    t = ATTN_TILE
    return pl.pallas_call(
        _mla_attn_kernel,
        grid=(bsz, s // t),
        in_specs=[pl.BlockSpec((1, t, wq), lambda b, i: (b, i, 0)),
                  pl.BlockSpec((1, s, wq), lambda b, i: (b, 0, 0)),
                  pl.BlockSpec((1, s, wv), lambda b, i: (b, 0, 0))],
        out_specs=pl.BlockSpec((1, t, wv), lambda b, i: (b, i, 0)),
        out_shape=jax.ShapeDtypeStruct(mv.shape, BF16),
        scratch_shapes=[pltpu.VMEM((wv, s), BF16), pltpu.VMEM((MLA_HEADS, 1, t), F32),
                        pltpu.VMEM((MLA_HEADS, 1, t), F32), pltpu.VMEM((MLA_HEADS, MLA_V_DIM, t), F32)],
        compiler_params=_params("arbitrary", "arbitrary"),
        name="mla_attn",
    )(mq, mk, mv)


def _post_attn_kernel(od_ref, om_ref, x_ref, mod_ref, g_ref, wo_d_ref, wo_m_ref, wr_ref,
                      x1_ref, h_ref, gates_ref):
    y = (jnp.dot(od_ref[0], wo_d_ref[...], preferred_element_type=F32)
         + jnp.dot(om_ref[0], wo_m_ref[...], preferred_element_type=F32))
    gt_a, sh_f, sc_f = mod_ref[0, 2:3, :], mod_ref[0, 3:4, :], mod_ref[0, 4:5, :]
    x1 = x_ref[0] + (1.0 + gt_a) * y
    x1_ref[0] = x1
    h = _rms(x1, g_ref[...]) * (1.0 + sc_f) + sh_f
    h_ref[0] = h.astype(BF16)
    logits = jnp.dot(h, wr_ref[...], preferred_element_type=F32, precision=lax.Precision.HIGHEST)
    lane = lax.broadcasted_iota(jnp.int32, logits.shape, 1)
    logits = jnp.where(lane < N_EXPERTS, logits, -jnp.inf)
    v1 = jnp.max(logits, axis=1, keepdims=True)
    i1 = jnp.min(jnp.where(logits == v1, lane, LANES), axis=1, keepdims=True)
    rest = jnp.where(lane == i1, -jnp.inf, logits)
    v2 = jnp.max(rest, axis=1, keepdims=True)
    i2 = jnp.min(jnp.where(rest == v2, lane, LANES), axis=1, keepdims=True)
    e2 = jnp.exp(v2 - v1)
    w1 = 1.0 / (1.0 + e2)
    w2 = e2 / (1.0 + e2)
    gates_ref[0] = jnp.where(lane == i1, w1, 0.0) + jnp.where(lane == i2, w2, 0.0)


def _post_attn(o_diff, o_mla, x, mod, g, wo_d, wo_m, w_router):
    bsz, s, d = x.shape
    tm = TOK_TILE
    tok = lambda w: pl.BlockSpec((1, tm, w), lambda b, i: (b, i, 0))
    full = lambda a: pl.BlockSpec(a.shape, lambda b, i: (0,) * a.ndim)
    return pl.pallas_call(
        _post_attn_kernel,
        grid=(bsz, s // tm),
        in_specs=[tok(o_diff.shape[-1]), tok(o_mla.shape[-1]), tok(d),
                  pl.BlockSpec((1, ADA_CHUNKS, d), lambda b, i: (b, 0, 0)),
                  full(g), full(wo_d), full(wo_m), full(w_router)],
        out_specs=[tok(d), tok(d), tok(LANES)],
        out_shape=[jax.ShapeDtypeStruct((bsz, s, d), F32), jax.ShapeDtypeStruct((bsz, s, d), BF16),
                   jax.ShapeDtypeStruct((bsz, s, LANES), F32)],
        compiler_params=_params("arbitrary", "arbitrary"),
        name="post_attn",
    )(o_diff, o_mla, x, mod, g, wo_d, wo_m, w_router)


def _ffn_kernel(h_ref, x1_ref, gates_ref, mod_ref, w1_ref, w3_ref, w2_ref, o_ref, acc_ref, *, gated):
    e = pl.program_id(2)

    @pl.when(e == 0)
    def _():
        acc_ref[...] = jnp.zeros(acc_ref.shape, F32)

    h = h_ref[0]
    d_ff = w1_ref.shape[-1]
    y = jnp.zeros(acc_ref.shape, F32)
    for c0 in range(0, d_ff, FF_CHUNK):
        a = jnp.dot(h, w1_ref[0, :, c0:c0 + FF_CHUNK], preferred_element_type=F32)
        b = jnp.dot(h, w3_ref[0, :, c0:c0 + FF_CHUNK], preferred_element_type=F32)
        u = (a * jax.nn.sigmoid(a) * b).astype(BF16)
        y = y + jnp.dot(u, w2_ref[0, c0:c0 + FF_CHUNK, :], preferred_element_type=F32)
    if gated:
        lane = lax.broadcasted_iota(jnp.int32, gates_ref.shape[1:], 1)
        gate = jnp.sum(jnp.where(lane == e, gates_ref[0], 0.0), axis=1, keepdims=True)
        y = gate * y
    acc_ref[...] += y

    @pl.when(e == pl.num_programs(2) - 1)
    def _():
        gt_f = mod_ref[0, 5:6, :]
        o_ref[0] = x1_ref[0] + (1.0 + gt_f) * acc_ref[...]


def _ffn(h, x1, gates, mod, w1, w3, w2, gated):
    bsz, s, d = x1.shape
    n_e, _, d_ff = w1.shape
    tm = FFN_TILE
    tok = lambda w: pl.BlockSpec((1, tm, w), lambda b, i, e: (b, i, 0))
    return pl.pallas_call(
        functools.partial(_ffn_kernel, gated=gated),
        grid=(bsz, s // tm, n_e),
        in_specs=[tok(d), tok(d), tok(LANES),
                  pl.BlockSpec((1, ADA_CHUNKS, d), lambda b, i, e: (b, 0, 0)),
                  pl.BlockSpec((1, d, d_ff), lambda b, i, e: (e, 0, 0)),
                  pl.BlockSpec((1, d, d_ff), lambda b, i, e: (e, 0, 0)),
                  pl.BlockSpec((1, d_ff, d), lambda b, i, e: (e, 0, 0))],
        out_specs=tok(d),
        out_shape=jax.ShapeDtypeStruct(x1.shape, F32),
        scratch_shapes=[pltpu.VMEM((tm, d), F32)],
        compiler_params=_params("arbitrary", "arbitrary", "arbitrary"),
        name="ffn_moe" if gated else "ffn_dense",
    )(h, x1, gates, mod, w1, w3, w2)


def _final_norm_kernel(x_ref, g_ref, o_ref):
    o_ref[0] = _rms(x_ref[0], g_ref[...])


def _final_norm(x, g):
    bsz, s, d = x.shape
    tm = TOK_TILE
    tok = pl.BlockSpec((1, tm, d), lambda b, i: (b, i, 0))
    return pl.pallas_call(
        _final_norm_kernel,
        grid=(bsz, s // tm),
        in_specs=[tok, pl.BlockSpec(g.shape, lambda b, i: (0, 0))],
        out_specs=tok,
        out_shape=jax.ShapeDtypeStruct(x.shape, F32),
        compiler_params=_params("arbitrary", "arbitrary"),
        name="final_norm",
    )(x, g)


def _pad_heads(w, heads, width, lo, hi):
    k = w.shape[0]
    w = w.reshape(k, heads, width)[:, :, lo:hi]
    return jnp.pad(w, ((0, 0), (0, 0), (0, LANES - (hi - lo)))).reshape(k, heads * LANES)


def _prep_w_in(w_in):
    d = w_in.shape[0]
    kr = w_in[:, C_KR:]
    kr_block = jnp.concatenate([jnp.zeros((d, MLA_NOPE_DIM), w_in.dtype), kr,
                                jnp.zeros((d, LANES - MLA_NOPE_DIM - MLA_ROPE_DIM), w_in.dtype)], axis=1)
    return jnp.concatenate([w_in[:, :C_KR], kr_block], axis=1).astype(BF16)


def kernel(x, c, positions, w_ada, b_ada, g_attn, w_in, diff_lambda, diff_subln_g, rel_bias, mla_q_norm, w_uq, mla_kv_norm, w_ukv, w_o, g_ffn, ffn_w1, ffn_w3, ffn_w2, moe_router, moe_w1, moe_w3, moe_w2, g_final):
    depth = w_ada.shape[0]
    bsz, s, d = x.shape
    mods = _ada(c, w_ada, b_ada).reshape(depth, bsz, ADA_CHUNKS, d)
    rc, rsa, rsb = _rope_tables(positions)
    bias = _bias_tiles(rel_bias)
    ones_gates = jnp.ones((bsz, s, LANES), F32)
    qk_w = MLA_NOPE_DIM + MLA_ROPE_DIM
    kv_w = MLA_NOPE_DIM + MLA_V_DIM
    for l in range(depth):
        mod = mods[l]
        lam_init = jnp.full((1,), 0.8 - 0.6 * math.exp(-0.3 * l), F32)
        w_uq_p = _pad_heads(w_uq[l], MLA_HEADS, qk_w, 0, qk_w).astype(BF16)
        w_uk_p = _pad_heads(w_ukv[l], MLA_HEADS, kv_w, 0, MLA_NOPE_DIM).astype(BF16)
        w_uv_p = w_ukv[l].reshape(MLA_KV_RANK, MLA_HEADS, kv_w)[:, :, MLA_NOPE_DIM:].reshape(
            MLA_KV_RANK, MLA_HEADS * MLA_V_DIM).astype(BF16)
        dq, dk, dv, mq, mk, mv = _pre_attn(
            x, mod, g_attn[l].reshape(1, d), _prep_w_in(w_in[l]), mla_q_norm[l].reshape(1, -1), w_uq_p,
            mla_kv_norm[l].reshape(1, -1), w_uk_p, w_uv_p, rc, rsa, rsb)
        o_diff = _diff_attn(lam_init, diff_lambda[l], diff_subln_g[l].reshape(-1, 1), bias, dq, dk, dv)
        o_mla = _mla_attn(mq, mk, mv)
        wo = w_o[l].astype(BF16)
        n_diff = DIFF_HEADS * DIFF_V_DIM
        is_moe = l % 2 == 1
        w_router = moe_router[l // 2] if is_moe else jnp.zeros((d, N_EXPERTS), F32)
        w_router = jnp.pad(w_router, ((0, 0), (0, LANES - N_EXPERTS)))
        x1, h, gates = _post_attn(o_diff, o_mla, x, mod, g_ffn[l].reshape(1, d), wo[:n_diff], wo[n_diff:], w_router)
        if is_moe:
            x = _ffn(h, x1, gates, mod, moe_w1[l // 2].astype(BF16), moe_w3[l // 2].astype(BF16),
                     moe_w2[l // 2].astype(BF16), True)
        else:
            x = _ffn(h, x1, ones_gates, mod, ffn_w1[l // 2][None].astype(BF16), ffn_w3[l // 2][None].astype(BF16),
                     ffn_w2[l // 2][None].astype(BF16), False)
    return _final_norm(x, g_final.reshape(1, d))
```

```python
import functools
import math

import jax
import jax.numpy as jnp
from jax import lax
from jax.experimental import pallas as pl
from jax.experimental.pallas import tpu as pltpu

F32 = jnp.float32
BF16 = jnp.bfloat16

DIFF_HEADS = 4
DIFF_QK_DIM = 64
DIFF_V_DIM = 128
MLA_HEADS = 8
MLA_NOPE_DIM = 64
MLA_ROPE_DIM = 32
MLA_V_DIM = 64
MLA_Q_RANK = 384
MLA_KV_RANK = 256
ROPE_THETA = 10000.0
N_BUCKETS = 32
MAX_EXACT = 16
MAX_DISTANCE = 128
N_EXPERTS = 8
NORM_EPS = 1e-6
ADA_CHUNKS = 6

LANES = 128
NEG_BIG = -1e30
LOG2E = math.log2(math.e)
VMEM_LIMIT = 56 * 1024 * 1024

ATTN_TILE = 256
AHEAD = 3
TOK_TILE = 512
FFN_TILE = 512
FF_CHUNK = 256

C_DQ, C_DK, C_DV, C_MQ, C_KV, C_KR, C_END = 0, 512, 1024, 1536, 1920, 2176, 2304


def _params(*sem):
    return pltpu.CompilerParams(dimension_semantics=sem, vmem_limit_bytes=VMEM_LIMIT)


def _rms(x, g):
    return x * lax.rsqrt(jnp.mean(x * x, axis=-1, keepdims=True) + NORM_EPS) * g


def _dot_nt(a, b):
    return lax.dot_general(a, b, (((1,), (1,)), ((), ())), preferred_element_type=F32)


def _ada_kernel(c_ref, w_ref, b_ref, o_ref):
    c = c_ref[...]
    cond = c * jax.nn.sigmoid(c)
    o_ref[0] = jnp.dot(cond, w_ref[0], preferred_element_type=F32,
                       precision=lax.Precision.HIGHEST) + b_ref[0]


def _ada(c, w_ada, b_ada):
    depth, d, n = w_ada.shape
    bsz = c.shape[0]
    tn = 1536
    return pl.pallas_call(
        _ada_kernel,
        grid=(depth, n // tn),
        in_specs=[pl.BlockSpec((bsz, d), lambda l, j: (0, 0)),
                  pl.BlockSpec((1, d, tn), lambda l, j: (l, 0, j)),
                  pl.BlockSpec((1, 1, tn), lambda l, j: (l, 0, j))],
        out_specs=pl.BlockSpec((1, bsz, tn), lambda l, j: (l, 0, j)),
        out_shape=jax.ShapeDtypeStruct((depth, bsz, n), F32),
        compiler_params=_params("arbitrary", "arbitrary"),
        name="ada_mod",
    )(c, w_ada, b_ada.reshape(depth, 1, n))


def _rope_tab_kernel(pos_ref, inv_ref, c_ref, sa_ref, sb_ref):
    pos = pos_ref[0].astype(F32)
    ang = pos * inv_ref[...]
    lane = lax.broadcasted_iota(jnp.int32, ang.shape, 1)
    cos, sin = jnp.cos(ang), jnp.sin(ang)
    lo = (lane >= MLA_NOPE_DIM) & (lane < MLA_NOPE_DIM + MLA_ROPE_DIM // 2)
    hi = (lane >= MLA_NOPE_DIM + MLA_ROPE_DIM // 2) & (lane < MLA_NOPE_DIM + MLA_ROPE_DIM)
    c_ref[0] = jnp.where(lane < MLA_NOPE_DIM, 1.0, jnp.where(lo | hi, cos, 0.0))
    sa_ref[0] = jnp.where(lo, -sin, 0.0)
    sb_ref[0] = jnp.where(hi, sin, 0.0)


def _rope_tables(positions):
    bsz, s = positions.shape
    half = MLA_ROPE_DIM // 2
    inv_freq = ROPE_THETA ** (-jnp.arange(half, dtype=F32) / half)
    inv_lane = jnp.concatenate([jnp.zeros((MLA_NOPE_DIM,), F32), inv_freq, inv_freq,
                                jnp.zeros((LANES - MLA_NOPE_DIM - MLA_ROPE_DIM,), F32)]).reshape(1, LANES)
    tm = TOK_TILE
    spec = pl.BlockSpec((1, tm, LANES), lambda b, i: (b, i, 0))
    shape = jax.ShapeDtypeStruct((bsz, s, LANES), F32)
    return pl.pallas_call(
        _rope_tab_kernel,
        grid=(bsz, s // tm),
        in_specs=[pl.BlockSpec((1, tm, 1), lambda b, i: (b, i, 0)),
                  pl.BlockSpec((1, LANES), lambda b, i: (0, 0))],
        out_specs=[spec, spec, spec],
        out_shape=[shape, shape, shape],
        compiler_params=_params("arbitrary", "arbitrary"),
        name="rope_tables",
    )(positions.reshape(bsz, s, 1), inv_lane)


def _bias_tile_kernel(rb_ref, o_ref):
    h, d = pl.program_id(0), pl.program_id(1)
    t = o_ref.shape[-1]
    key = lax.broadcasted_iota(jnp.int32, (t, t), 0)
    qry = lax.broadcasted_iota(jnp.int32, (t, t), 1)
    dist = d * t + qry - key
    n = jnp.maximum(dist, 0)
    nf = jnp.maximum(n, 1).astype(F32)
    large = MAX_EXACT + (jnp.log(nf / MAX_EXACT) / math.log(MAX_DISTANCE / MAX_EXACT)
                         * (N_BUCKETS - MAX_EXACT)).astype(jnp.int32)
    large = jnp.minimum(large, N_BUCKETS - 1)
    bucket = jnp.where(n < MAX_EXACT, n, large)
    val = jnp.zeros((t, t), F32)
    for j in range(N_BUCKETS):
        val = jnp.where(bucket == j, rb_ref[j, h], val)
    val = (rb_ref[N_BUCKETS - 1, h] - val) * LOG2E
    o_ref[0, 0] = jnp.where(dist < 0, -NEG_BIG, val)


def _bias_tiles(rel_bias):
    t = ATTN_TILE
    return pl.pallas_call(
        _bias_tile_kernel,
        grid=(DIFF_HEADS, 2),
        in_specs=[pl.BlockSpec(memory_space=pltpu.SMEM)],
        out_specs=pl.BlockSpec((1, 1, t, t), lambda h, d: (h, d, 0, 0)),
        out_shape=jax.ShapeDtypeStruct((DIFF_HEADS, 2, t, t), F32),
        compiler_params=_params("arbitrary", "arbitrary"),
        name="bias_tiles",
    )(rel_bias)


def _pre_attn_kernel(x_ref, mod_ref, g_ref, w_in_ref, gq_ref, w_uq_ref, gkv_ref, w_uk_ref, w_uv_ref,
                     rc_ref, rsa_ref, rsb_ref,
                     dq_ref, dk_ref, dv_ref, mq_ref, mk_ref, mv_ref):
    x = x_ref[0]
    sh, sc = mod_ref[0, 0:1, :], mod_ref[0, 1:2, :]
    h = (_rms(x, g_ref[...]) * (1.0 + sc) + sh).astype(BF16)
    proj = jnp.dot(h, w_in_ref[...], preferred_element_type=F32)
    dq_ref[0] = (proj[:, C_DQ:C_DK] * (DIFF_QK_DIM ** -0.5 * LOG2E)).astype(BF16)
    dk_ref[0] = proj[:, C_DK:C_DV].astype(BF16)
    dv_ref[0] = proj[:, C_DV:C_MQ].astype(BF16)

    rc, rsa, rsb = rc_ref[0], rsa_ref[0], rsb_ref[0]

    def rope(v):
        return v * rc + pltpu.roll(v, LANES - MLA_ROPE_DIM // 2, 1) * rsa + pltpu.roll(v, MLA_ROPE_DIM // 2, 1) * rsb

    qn = _rms(proj[:, C_MQ:C_KV], gq_ref[...]).astype(BF16)
    q = jnp.dot(qn, w_uq_ref[...], preferred_element_type=F32)
    kvn = _rms(proj[:, C_KV:C_KR], gkv_ref[...]).astype(BF16)
    kn = jnp.dot(kvn, w_uk_ref[...], preferred_element_type=F32)
    mv_ref[0] = jnp.dot(kvn, w_uv_ref[...], preferred_element_type=F32).astype(BF16)
    kr = rope(proj[:, C_KR:C_END])
    q_scale = (MLA_NOPE_DIM + MLA_ROPE_DIM) ** -0.5 * LOG2E
    for hd in range(MLA_HEADS):
        sl = slice(hd * LANES, (hd + 1) * LANES)
        mq_ref[0, :, sl] = (rope(q[:, sl]) * q_scale).astype(BF16)
        mk_ref[0, :, sl] = (kn[:, sl] + kr).astype(BF16)


def _pre_attn(x, mod, g, w_in, gq, w_uq, gkv, w_uk, w_uv, rc, rsa, rsb):
    bsz, s, d = x.shape
    tm = TOK_TILE
    tok = lambda w: pl.BlockSpec((1, tm, w), lambda b, i: (b, i, 0))
    full = lambda a: pl.BlockSpec(a.shape, lambda b, i: (0,) * a.ndim)
    widths = (512, 512, 512, MLA_HEADS * LANES, MLA_HEADS * LANES, MLA_HEADS * MLA_V_DIM)
    return pl.pallas_call(
        _pre_attn_kernel,
        grid=(bsz, s // tm),
        in_specs=[tok(d), pl.BlockSpec((1, ADA_CHUNKS, d), lambda b, i: (b, 0, 0)), full(g), full(w_in),
                  full(gq), full(w_uq), full(gkv), full(w_uk), full(w_uv), tok(LANES), tok(LANES), tok(LANES)],
        out_specs=[tok(w) for w in widths],
        out_shape=[jax.ShapeDtypeStruct((bsz, s, w), BF16) for w in widths],
        compiler_params=_params("arbitrary", "arbitrary"),
        name="pre_attn",
    )(x, mod, g, w_in, gq, w_uq, gkv, w_uk, w_uv, rc, rsa, rsb)


def _softmax_init(m_ref, l_ref, acc_ref):
    m_ref[...] = jnp.full(m_ref.shape, NEG_BIG, F32)
    l_ref[...] = jnp.zeros(l_ref.shape, F32)
    acc_ref[...] = jnp.zeros(acc_ref.shape, F32)


def _softmax_update(st, vt, m_ref, l_ref, acc_ref):
    m_prev = m_ref[...]
    m_new = jnp.maximum(m_prev, jnp.max(st, axis=0, keepdims=True))
    alpha = jnp.exp2(m_prev - m_new)
    p = jnp.exp2(st - m_new)
    l_ref[...] = alpha * l_ref[...] + jnp.sum(p, axis=0, keepdims=True)
    acc_ref[...] = alpha * acc_ref[...] + jnp.dot(vt, p.astype(BF16), preferred_element_type=F32)
    m_ref[...] = m_new


def _diff_attn_kernel(lam_init_ref, lam_ref, g_ref, bias_ref, q_ref, k_ref, v_ref, o_ref,
                      vt_ref, m_ref, l_ref, acc_ref):
    i = pl.program_id(1)
    t = q_ref.shape[1]

    @pl.when(i == 0)
    def _():
        vt_ref[...] = v_ref[0].T

    n_maps = 2 * DIFF_HEADS
    for c in range(n_maps):
        _softmax_init(m_ref.at[c], l_ref.at[c], acc_ref.at[c])
    qts = []
    for h in range(DIFF_HEADS):
        qt = q_ref[0, :, h * LANES:(h + 1) * LANES].T
        row = lax.broadcasted_iota(jnp.int32, qt.shape, 0)
        qts.append(jnp.where(row < DIFF_QK_DIM, qt, jnp.zeros_like(qt)))
        qts.append(jnp.where(row >= DIFF_QK_DIM, qt, jnp.zeros_like(qt)))

    def step(j, bias_idx):
        start = pl.multiple_of(j * t, t)

        def scores(c):
            k = k_ref[0, pl.ds(start, t), (c // 2) * LANES:(c // 2 + 1) * LANES]
            return jnp.dot(k, qts[c], preferred_element_type=F32)

        pending = [scores(c) for c in range(AHEAD)]
        for c in range(n_maps):
            if c + AHEAD < n_maps:
                pending.append(scores(c + AHEAD))
            st = pending.pop(0)
            h = c // 2
            if bias_idx is not None:
                st = st - bias_ref[h, bias_idx]
            vt = vt_ref[h * LANES:(h + 1) * LANES, pl.ds(start, t)]
            _softmax_update(st, vt, m_ref.at[c], l_ref.at[c], acc_ref.at[c])

    def far(j, carry):
        step(j, None)
        return carry

    lax.fori_loop(0, jnp.maximum(i - 1, 0), far, 0)

    @pl.when(i >= 1)
    def _():
        step(i - 1, 1)

    step(i, 0)

    lv = lam_ref[...]
    lam = (jnp.exp(jnp.sum(lv[0:1] * lv[1:2], keepdims=True)) - jnp.exp(jnp.sum(lv[2:3] * lv[3:4], keepdims=True))
           + lam_init_ref[0])
    for h in range(DIFF_HEADS):
        c0, c1 = 2 * h, 2 * h + 1
        ot = acc_ref[c0] / l_ref[c0] - lam * (acc_ref[c1] / l_ref[c1])
        ot = ot * lax.rsqrt(jnp.mean(ot * ot, axis=0, keepdims=True) + NORM_EPS) * g_ref[...]
        o_ref[0, :, h * LANES:(h + 1) * LANES] = (ot * (1.0 - lam_init_ref[0])).T.astype(BF16)


def _diff_attn(lam_init, diff_lambda, g, bias, dq, dk, dv):
    bsz, s, w = dq.shape
    t = ATTN_TILE
    assert MAX_DISTANCE <= t and s % t == 0
    n_maps = 2 * DIFF_HEADS
    return pl.pallas_call(
        _diff_attn_kernel,
        grid=(bsz, s // t),
        in_specs=[pl.BlockSpec(memory_space=pltpu.SMEM),
                  pl.BlockSpec(diff_lambda.shape, lambda b, i: (0, 0)),
                  pl.BlockSpec(g.shape, lambda b, i: (0, 0)),
                  pl.BlockSpec(bias.shape, lambda b, i: (0, 0, 0, 0)),
                  pl.BlockSpec((1, t, w), lambda b, i: (b, i, 0)),
                  pl.BlockSpec((1, s, w), lambda b, i: (b, 0, 0)),
                  pl.BlockSpec((1, s, w), lambda b, i: (b, 0, 0))],
        out_specs=pl.BlockSpec((1, t, w), lambda b, i: (b, i, 0)),
        out_shape=jax.ShapeDtypeStruct(dq.shape, BF16),
        scratch_shapes=[pltpu.VMEM((w, s), BF16), pltpu.VMEM((n_maps, 1, t), F32),
                        pltpu.VMEM((n_maps, 1, t), F32), pltpu.VMEM((n_maps, LANES, t), F32)],
        compiler_params=_params("arbitrary", "arbitrary"),
        name="diff_attn",
    )(lam_init, diff_lambda, g, bias, dq, dk, dv)


def _mla_attn_kernel(q_ref, k_ref, v_ref, o_ref, vt_ref, m_ref, l_ref, acc_ref):
    i = pl.program_id(1)
    t = q_ref.shape[1]

    @pl.when(i == 0)
    def _():
        vt_ref[...] = v_ref[0].T

    for h in range(MLA_HEADS):
        _softmax_init(m_ref.at[h], l_ref.at[h], acc_ref.at[h])
    qts = [q_ref[0, :, h * LANES:(h + 1) * LANES].T for h in range(MLA_HEADS)]

    def step(j, masked):
        start = pl.multiple_of(j * t, t)

        def scores(h):
            k = k_ref[0, pl.ds(start, t), h * LANES:(h + 1) * LANES]
            return jnp.dot(k, qts[h], preferred_element_type=F32)

        pending = [scores(h) for h in range(AHEAD)]
        for h in range(MLA_HEADS):
            if h + AHEAD < MLA_HEADS:
                pending.append(scores(h + AHEAD))
            st = pending.pop(0)
            vt = vt_ref[h * MLA_V_DIM:(h + 1) * MLA_V_DIM, pl.ds(start, t)]
            if masked:
                key = lax.broadcasted_iota(jnp.int32, st.shape, 0)
                qry = lax.broadcasted_iota(jnp.int32, st.shape, 1)
                st = jnp.where(key <= qry, st, NEG_BIG)
            _softmax_update(st, vt, m_ref.at[h], l_ref.at[h], acc_ref.at[h])

    def far(j, carry):
        step(j, False)
        return carry

    lax.fori_loop(0, i, far, 0)
    step(i, True)
    for u in range(MLA_HEADS // 2):
        ot = jnp.concatenate([acc_ref[2 * u] / l_ref[2 * u], acc_ref[2 * u + 1] / l_ref[2 * u + 1]], axis=0)
        o_ref[0, :, u * LANES:(u + 1) * LANES] = ot.T.astype(BF16)


def _mla_attn(mq, mk, mv):
    bsz, s, wq = mq.shape
    wv = mv.shape[-1]
    t = ATTN_TILE
    return pl.pallas_call(
        _mla_attn_kernel,
        grid=(bsz, s // t),
        in_specs=[pl.BlockSpec((1, t, wq), lambda b, i: (b, i, 0)),
                  pl.BlockSpec((1, s, wq), lambda b, i: (b, 0, 0)),
                  pl.BlockSpec((1, s, wv), lambda b, i: (b, 0, 0))],
        out_specs=pl.BlockSpec((1, t, wv), lambda b, i: (b, i, 0)),
        out_shape=jax.ShapeDtypeStruct(mv.shape, BF16),
        scratch_shapes=[pltpu.VMEM((wv, s), BF16), pltpu.VMEM((MLA_HEADS, 1, t), F32),
                        pltpu.VMEM((MLA_HEADS, 1, t), F32), pltpu.VMEM((MLA_HEADS, MLA_V_DIM, t), F32)],
        compiler_params=_params("arbitrary", "arbitrary"),
        name="mla_attn",
    )(mq, mk, mv)


def _post_attn_kernel(*refs, moe):
    if moe:
        od_ref, om_ref, x_ref, mod_ref, g_ref, wo_d_ref, wo_m_ref, wr_ref, x1_ref, h_ref, gates_ref, sel_ref = refs
    else:
        od_ref, om_ref, x_ref, mod_ref, g_ref, wo_d_ref, wo_m_ref, x1_ref, h_ref = refs
    y = (jnp.dot(od_ref[0], wo_d_ref[...], preferred_element_type=F32)
         + jnp.dot(om_ref[0], wo_m_ref[...], preferred_element_type=F32))
    gt_a, sh_f, sc_f = mod_ref[0, 2:3, :], mod_ref[0, 3:4, :], mod_ref[0, 4:5, :]
    x1 = x_ref[0] + (1.0 + gt_a) * y
    x1_ref[0] = x1
    h = _rms(x1, g_ref[...]) * (1.0 + sc_f) + sh_f
    h_ref[0] = h.astype(h_ref.dtype)
    if not moe:
        return
    logits = jnp.dot(h, wr_ref[...], preferred_element_type=F32, precision=lax.Precision.HIGHEST)
    lane = lax.broadcasted_iota(jnp.int32, logits.shape, 1)
    logits = jnp.where(lane < N_EXPERTS, logits, -jnp.inf)
    v1 = jnp.max(logits, axis=1, keepdims=True)
    i1 = jnp.min(jnp.where(logits == v1, lane, LANES), axis=1, keepdims=True)
    rest = jnp.where(lane == i1, -jnp.inf, logits)
    v2 = jnp.max(rest, axis=1, keepdims=True)
    i2 = jnp.min(jnp.where(rest == v2, lane, LANES), axis=1, keepdims=True)
    e2 = jnp.exp(v2 - v1)
    w1 = 1.0 / (1.0 + e2)
    w2 = e2 / (1.0 + e2)
    gates_ref[0] = jnp.where(lane == i1, w1, 0.0) + jnp.where(lane == i2, w2, 0.0)
    sel_ref[0] = jnp.where(lane == i1, 1.0, 0.0) + jnp.where(lane == i2, 2.0, 0.0)


def _post_attn(o_diff, o_mla, x, mod, g, wo_d, wo_m, w_router=None):
    bsz, s, d = x.shape
    moe = w_router is not None
    tm = TOK_TILE
    tok = lambda w: pl.BlockSpec((1, tm, w), lambda b, i: (b, i, 0))
    full = lambda a: pl.BlockSpec(a.shape, lambda b, i: (0,) * a.ndim)
    args = [o_diff, o_mla, x, mod, g, wo_d, wo_m] + ([w_router] if moe else [])
    in_specs = [tok(o_diff.shape[-1]), tok(o_mla.shape[-1]), tok(d),
                pl.BlockSpec((1, ADA_CHUNKS, d), lambda b, i: (b, 0, 0)),
                full(g), full(wo_d), full(wo_m)] + ([full(w_router)] if moe else [])
    out_specs = [tok(d), tok(d)] + ([tok(LANES), tok(LANES)] if moe else [])
    out_shape = [jax.ShapeDtypeStruct((bsz, s, d), F32), jax.ShapeDtypeStruct((bsz, s, d), F32 if moe else BF16)]
    if moe:
        out_shape += [jax.ShapeDtypeStruct((bsz, s, LANES), F32)] * 2
    return pl.pallas_call(
        functools.partial(_post_attn_kernel, moe=moe),
        grid=(bsz, s // tm),
        in_specs=in_specs,
        out_specs=out_specs,
        out_shape=out_shape,
        compiler_params=_params("arbitrary", "arbitrary"),
        name="post_attn_moe" if moe else "post_attn",
    )(*args)


def _swiglu(h, w1_ref, w3_ref, w2_ref):
    d_ff = w1_ref.shape[-1]
    y = jnp.zeros((h.shape[0], w2_ref.shape[-1]), F32)
    for c0 in range(0, d_ff, FF_CHUNK):
        a = jnp.dot(h, w1_ref[:, c0:c0 + FF_CHUNK], preferred_element_type=F32)
        b = jnp.dot(h, w3_ref[:, c0:c0 + FF_CHUNK], preferred_element_type=F32)
        u = (a * jax.nn.sigmoid(a) * b).astype(BF16)
        y = y + jnp.dot(u, w2_ref[c0:c0 + FF_CHUNK, :], preferred_element_type=F32)
    return y


def _ffn_dense_kernel(h_ref, x1_ref, mod_ref, w1_ref, w3_ref, w2_ref, o_ref):
    gt_f = mod_ref[0, 5:6, :]
    o_ref[0] = x1_ref[0] + (1.0 + gt_f) * _swiglu(h_ref[0], w1_ref, w3_ref, w2_ref)


def _ffn_dense(h, x1, mod, w1, w3, w2):
    bsz, s, d = x1.shape
    tm = FFN_TILE
    tok = lambda w: pl.BlockSpec((1, tm, w), lambda b, i: (b, i, 0))
    full = lambda a: pl.BlockSpec(a.shape, lambda b, i: (0,) * a.ndim)
    return pl.pallas_call(
        _ffn_dense_kernel,
        grid=(bsz, s // tm),
        in_specs=[tok(d), tok(d), pl.BlockSpec((1, ADA_CHUNKS, d), lambda b, i: (b, 0, 0)),
                  full(w1), full(w3), full(w2)],
        out_specs=tok(d),
        out_shape=jax.ShapeDtypeStruct(x1.shape, F32),
        compiler_params=_params("arbitrary", "arbitrary"),
        name="ffn_dense",
    )(h, x1, mod, w1, w3, w2)


def _route_kernel(sel_ref, pos_ref, te_ref, cnt_ref, off_ref, run_ref, *, row_tile):
    p, t = pl.program_id(0), pl.program_id(1)
    sel_t = sel_ref[...].T
    chosen = (sel_t > 0.0).astype(F32)
    per_expert = jnp.sum(chosen, axis=1, keepdims=True)

    @pl.when((p == 0) & (t == 0))
    def _():
        cnt_ref[...] = jnp.zeros(cnt_ref.shape, F32)
        te_ref[...] = jnp.zeros(te_ref.shape, jnp.int32)

    @pl.when(p == 0)
    def _():
        cnt_ref[...] += per_expert

    @pl.when((p == 1) & (t == 0))
    def _():
        cnt = cnt_ref[...]
        padded = jnp.ceil(cnt / row_tile) * row_tile
        row = lax.broadcasted_iota(jnp.int32, cnt.shape, 0)
        off = jnp.zeros(cnt.shape, F32)
        for e in range(N_EXPERTS):
            size_e = jnp.sum(jnp.where(row == e, padded, 0.0), keepdims=True)
            off = off + jnp.where(row > e, size_e, 0.0)
        off_ref[...] = off
        run_ref[...] = jnp.zeros(run_ref.shape, F32)
        ends = off + padded
        tile_start = lax.broadcasted_iota(jnp.int32, (LANES, LANES), 1).astype(F32) * row_tile
        erow = lax.broadcasted_iota(jnp.int32, (LANES, LANES), 0)
        done = jnp.where((erow < N_EXPERTS) & (ends <= tile_start), 1.0, 0.0)
        te = jnp.sum(done, axis=0, keepdims=True).astype(jnp.int32)
        te_ref[...] = jnp.broadcast_to(te, te_ref.shape)

    @pl.when(p == 1)
    def _():
        tm = sel_t.shape[1]
        before = (lax.broadcasted_iota(jnp.int32, (tm, tm), 0)
                  < lax.broadcasted_iota(jnp.int32, (tm, tm), 1)).astype(BF16)
        rank = jnp.dot(chosen.astype(BF16), before, preferred_element_type=F32) + run_ref[...]
        base = off_ref[...] + rank
        for k in range(2):
            pos = jnp.sum(jnp.where(sel_t == float(k + 1), base, 0.0), axis=0, keepdims=True)
            pos_ref[0, k:k + 1, :] = pos.astype(jnp.int32)
        run_ref[...] += per_expert


def _route(sel, row_tile, n_row_tiles):
    n_tok = sel.shape[0]
    assert n_row_tiles <= LANES and 2 * n_tok < 2 ** 24
    tm = TOK_TILE
    nt = n_tok // tm
    pos, te = pl.pallas_call(
        functools.partial(_route_kernel, row_tile=row_tile),
        grid=(2, nt),
        in_specs=[pl.BlockSpec((tm, LANES), lambda p, t: (t, 0))],
        out_specs=[pl.BlockSpec((1, 2, tm), lambda p, t: (p * t, 0, 0)),
                   pl.BlockSpec((8, LANES), lambda p, t: (0, 0))],
        out_shape=[jax.ShapeDtypeStruct((nt, 2, tm), jnp.int32), jax.ShapeDtypeStruct((8, LANES), jnp.int32)],
        scratch_shapes=[pltpu.VMEM((LANES, 1), F32)] * 3,
        compiler_params=_params("arbitrary", "arbitrary"),
        name="moe_route",
    )(sel)
    return pos, te[0]


def _dispatch_kernel(pos_ref, h_ref, xs_in_ref, xs_ref, sem):
    del xs_in_ref
    tm = h_ref.shape[0]

    def body(r, carry):
        for k in range(2):
            pltpu.make_async_copy(h_ref.at[pl.ds(r, 1)], xs_ref.at[pl.ds(pos_ref[0, k, r], 1)], sem).start()
        return carry

    lax.fori_loop(0, tm, body, 0, unroll=8)
    for k in range(2):
        pltpu.make_async_copy(h_ref, xs_ref.at[pl.ds(0, tm)], sem).wait()


def _dispatch(pos, h, n_rows):
    n_tok, d = h.shape
    nt, _, tm = pos.shape
    xs0 = jnp.zeros((n_rows, d), h.dtype)
    return pl.pallas_call(
        _dispatch_kernel,
        grid=(nt,),
        in_specs=[pl.BlockSpec((1, 2, tm), lambda t: (t, 0, 0), memory_space=pltpu.SMEM),
                  pl.BlockSpec((tm, d), lambda t: (t, 0)),
                  pl.BlockSpec(memory_space=pl.ANY)],
        out_specs=pl.BlockSpec(memory_space=pl.ANY),
        out_shape=jax.ShapeDtypeStruct(xs0.shape, xs0.dtype),
        scratch_shapes=[pltpu.SemaphoreType.DMA(())],
        input_output_aliases={2: 0},
        compiler_params=_params("arbitrary"),
        name="moe_dispatch",
    )(pos, h, xs0)


def _expert_kernel(te_ref, xs_ref, w1_ref, w3_ref, w2_ref, y_ref):
    used = te_ref[pl.program_id(0)] < N_EXPERTS

    @pl.when(used)
    def _():
        y_ref[...] = _swiglu(xs_ref[...].astype(BF16), w1_ref.at[0], w3_ref.at[0], w2_ref.at[0])

    @pl.when(jnp.logical_not(used))
    def _():
        y_ref[...] = jnp.zeros(y_ref.shape, y_ref.dtype)


def _experts(te, xs, w1, w3, w2, row_tile):
    n_rows, d = xs.shape
    d_ff = w1.shape[-1]
    expert = lambda n, te: (jnp.minimum(te[n], N_EXPERTS - 1), 0, 0)
    return pl.pallas_call(
        _expert_kernel,
        grid_spec=pltpu.PrefetchScalarGridSpec(
            num_scalar_prefetch=1,
            grid=(n_rows // row_tile,),
            in_specs=[pl.BlockSpec((row_tile, d), lambda n, te: (n, 0)),
                      pl.BlockSpec((1, d, d_ff), expert), pl.BlockSpec((1, d, d_ff), expert),
                      pl.BlockSpec((1, d_ff, d), expert)],
            out_specs=pl.BlockSpec((row_tile, d), lambda n, te: (n, 0))),
        out_shape=jax.ShapeDtypeStruct(xs.shape, F32),
        compiler_params=_params("arbitrary"),
        name="moe_experts",
    )(te, xs, w1, w3, w2)


def _combine_kernel(pos_ref, x1_ref, gates_ref, sel_ref, mod_ref, ys_ref, o_ref, ya_ref, yb_ref, sem):
    tm = x1_ref.shape[1]
    bufs = (ya_ref, yb_ref)

    def body(r, carry):
        for k in range(2):
            pltpu.make_async_copy(ys_ref.at[pl.ds(pos_ref[0, k, r], 1)], bufs[k].at[pl.ds(r, 1)], sem).start()
        return carry

    lax.fori_loop(0, tm, body, 0, unroll=8)
    gates, sel = gates_ref[0], sel_ref[0]
    w_a = jnp.sum(jnp.where(sel == 1.0, gates, 0.0), axis=1, keepdims=True)
    w_b = jnp.sum(jnp.where(sel == 2.0, gates, 0.0), axis=1, keepdims=True)
    for k in range(2):
        pltpu.make_async_copy(ys_ref.at[pl.ds(0, tm)], bufs[k], sem).wait()
    gt_f = mod_ref[0, 5:6, :]
    o_ref[0] = x1_ref[0] + (1.0 + gt_f) * (w_a * ya_ref[...] + w_b * yb_ref[...])


def _combine(pos, x1, gates, sel, mod, ys):
    bsz, s, d = x1.shape
    nt, _, tm = pos.shape
    per_b = s // tm
    tok = lambda w: pl.BlockSpec((1, tm, w), lambda b, i: (b, i, 0))
    return pl.pallas_call(
        _combine_kernel,
        grid=(bsz, per_b),
        in_specs=[pl.BlockSpec((1, 2, tm), lambda b, i: (b * per_b + i, 0, 0), memory_space=pltpu.SMEM),
                  tok(d), tok(LANES), tok(LANES),
                  pl.BlockSpec((1, ADA_CHUNKS, d), lambda b, i: (b, 0, 0)),
                  pl.BlockSpec(memory_space=pl.ANY)],
        out_specs=tok(d),
        out_shape=jax.ShapeDtypeStruct(x1.shape, F32),
        scratch_shapes=[pltpu.VMEM((tm, d), F32), pltpu.VMEM((tm, d), F32), pltpu.SemaphoreType.DMA(())],
        compiler_params=_params("arbitrary", "arbitrary"),
        name="moe_combine",
    )(pos, x1, gates, sel, mod, ys)


def _moe(h, x1, gates, sel, mod, w1, w3, w2):
    bsz, s, d = x1.shape
    n_tok = bsz * s
    row_tile = FFN_TILE
    n_rows = 2 * n_tok + N_EXPERTS * row_tile
    pos, te = _route(sel.reshape(n_tok, LANES), row_tile, n_rows // row_tile)
    xs = _dispatch(pos, h.reshape(n_tok, d), n_rows)
    ys = _experts(te, xs, w1, w3, w2, row_tile)
    return _combine(pos, x1, gates, sel, mod, ys)


def _final_norm_kernel(x_ref, g_ref, o_ref):
    o_ref[0] = _rms(x_ref[0], g_ref[...])


def _final_norm(x, g):
    bsz, s, d = x.shape
    tm = TOK_TILE
    tok = pl.BlockSpec((1, tm, d), lambda b, i: (b, i, 0))
    return pl.pallas_call(
        _final_norm_kernel,
        grid=(bsz, s // tm),
        in_specs=[tok, pl.BlockSpec(g.shape, lambda b, i: (0, 0))],
        out_specs=tok,
        out_shape=jax.ShapeDtypeStruct(x.shape, F32),
        compiler_params=_params("arbitrary", "arbitrary"),
        name="final_norm",
    )(x, g)


def _pad_heads(w, heads, width, lo, hi):
    k = w.shape[0]
    w = w.reshape(k, heads, width)[:, :, lo:hi]
    return jnp.pad(w, ((0, 0), (0, 0), (0, LANES - (hi - lo)))).reshape(k, heads * LANES)


def _prep_w_in(w_in):
    d = w_in.shape[0]
    kr = w_in[:, C_KR:]
    kr_block = jnp.concatenate([jnp.zeros((d, MLA_NOPE_DIM), w_in.dtype), kr,
                                jnp.zeros((d, LANES - MLA_NOPE_DIM - MLA_ROPE_DIM), w_in.dtype)], axis=1)
    return jnp.concatenate([w_in[:, :C_KR], kr_block], axis=1).astype(BF16)


def kernel(x, c, positions, w_ada, b_ada, g_attn, w_in, diff_lambda, diff_subln_g, rel_bias, mla_q_norm, w_uq, mla_kv_norm, w_ukv, w_o, g_ffn, ffn_w1, ffn_w3, ffn_w2, moe_router, moe_w1, moe_w3, moe_w2, g_final):
    depth = w_ada.shape[0]
    bsz, s, d = x.shape
    mods = _ada(c, w_ada, b_ada).reshape(depth, bsz, ADA_CHUNKS, d)
    rc, rsa, rsb = _rope_tables(positions)
    bias = _bias_tiles(rel_bias)
    qk_w = MLA_NOPE_DIM + MLA_ROPE_DIM
    kv_w = MLA_NOPE_DIM + MLA_V_DIM
    for l in range(depth):
        mod = mods[l]
        lam_init = jnp.full((1,), 0.8 - 0.6 * math.exp(-0.3 * l), F32)
        w_uq_p = _pad_heads(w_uq[l], MLA_HEADS, qk_w, 0, qk_w).astype(BF16)
        w_uk_p = _pad_heads(w_ukv[l], MLA_HEADS, kv_w, 0, MLA_NOPE_DIM).astype(BF16)
        w_uv_p = w_ukv[l].reshape(MLA_KV_RANK, MLA_HEADS, kv_w)[:, :, MLA_NOPE_DIM:].reshape(
            MLA_KV_RANK, MLA_HEADS * MLA_V_DIM).astype(BF16)
        dq, dk, dv, mq, mk, mv = _pre_attn(
            x, mod, g_attn[l].reshape(1, d), _prep_w_in(w_in[l]), mla_q_norm[l].reshape(1, -1), w_uq_p,
            mla_kv_norm[l].reshape(1, -1), w_uk_p, w_uv_p, rc, rsa, rsb)
        o_diff = _diff_attn(lam_init, diff_lambda[l], diff_subln_g[l].reshape(-1, 1), bias, dq, dk, dv)
        o_mla = _mla_attn(mq, mk, mv)
        wo = w_o[l].astype(BF16)
        n_diff = DIFF_HEADS * DIFF_V_DIM
        g_f = g_ffn[l].reshape(1, d)
        if l % 2 == 1:
            w_router = jnp.pad(moe_router[l // 2], ((0, 0), (0, LANES - N_EXPERTS)))
            x1, h, gates, sel = _post_attn(o_diff, o_mla, x, mod, g_f, wo[:n_diff], wo[n_diff:], w_router)
            x = _moe(h, x1, gates, sel, mod, moe_w1[l // 2].astype(BF16), moe_w3[l // 2].astype(BF16),
                     moe_w2[l // 2].astype(BF16))
        else:
            x1, h = _post_attn(o_diff, o_mla, x, mod, g_f, wo[:n_diff], wo[n_diff:])
            x = _ffn_dense(h, x1, mod, ffn_w1[l // 2].astype(BF16), ffn_w3[l // 2].astype(BF16),
                           ffn_w2[l // 2].astype(BF16))
    return _final_norm(x, g_final.reshape(1, d))
```

```python
import functools
import math

import jax
import jax.numpy as jnp
from jax import lax
from jax.experimental import pallas as pl
from jax.experimental.pallas import tpu as pltpu

F32 = jnp.float32
BF16 = jnp.bfloat16

DIFF_HEADS = 4
DIFF_QK_DIM = 64
DIFF_V_DIM = 128
MLA_HEADS = 8
MLA_NOPE_DIM = 64
MLA_ROPE_DIM = 32
MLA_V_DIM = 64
MLA_Q_RANK = 384
MLA_KV_RANK = 256
ROPE_THETA = 10000.0
N_BUCKETS = 32
MAX_EXACT = 16
MAX_DISTANCE = 128
N_EXPERTS = 8
NORM_EPS = 1e-6
ADA_CHUNKS = 6

LANES = 128
NEG_BIG = -1e30
LOG2E = math.log2(math.e)
VMEM_LIMIT = 56 * 1024 * 1024

ATTN_TILE = 256
AHEAD = 4
TOK_TILE = 512
FFN_TILE = 512
FF_CHUNK = 256

C_DQ, C_DK, C_DV, C_MQ, C_KV, C_KR, C_END = 0, 512, 1024, 1536, 1920, 2176, 2304


def _params(*sem):
    return pltpu.CompilerParams(dimension_semantics=sem, vmem_limit_bytes=VMEM_LIMIT)


def _rms(x, g):
    return x * lax.rsqrt(jnp.mean(x * x, axis=-1, keepdims=True) + NORM_EPS) * g


def _dot_nt(a, b):
    return lax.dot_general(a, b, (((1,), (1,)), ((), ())), preferred_element_type=F32)


def _ada_kernel(c_ref, w_ref, b_ref, o_ref):
    c = c_ref[...]
    cond = c * jax.nn.sigmoid(c)
    o_ref[0] = jnp.dot(cond, w_ref[0], preferred_element_type=F32,
                       precision=lax.Precision.HIGHEST) + b_ref[0]


def _ada(c, w_ada, b_ada):
    depth, d, n = w_ada.shape
    bsz = c.shape[0]
    tn = 1536
    return pl.pallas_call(
        _ada_kernel,
        grid=(depth, n // tn),
        in_specs=[pl.BlockSpec((bsz, d), lambda l, j: (0, 0)),
                  pl.BlockSpec((1, d, tn), lambda l, j: (l, 0, j)),
                  pl.BlockSpec((1, 1, tn), lambda l, j: (l, 0, j))],
        out_specs=pl.BlockSpec((1, bsz, tn), lambda l, j: (l, 0, j)),
        out_shape=jax.ShapeDtypeStruct((depth, bsz, n), F32),
        compiler_params=_params("arbitrary", "arbitrary"),
        name="ada_mod",
    )(c, w_ada, b_ada.reshape(depth, 1, n))


def _rope_tab_kernel(pos_ref, inv_ref, c_ref, sa_ref, sb_ref):
    pos = pos_ref[0].astype(F32)
    ang = pos * inv_ref[...]
    lane = lax.broadcasted_iota(jnp.int32, ang.shape, 1)
    cos, sin = jnp.cos(ang), jnp.sin(ang)
    lo = (lane >= MLA_NOPE_DIM) & (lane < MLA_NOPE_DIM + MLA_ROPE_DIM // 2)
    hi = (lane >= MLA_NOPE_DIM + MLA_ROPE_DIM // 2) & (lane < MLA_NOPE_DIM + MLA_ROPE_DIM)
    c_ref[0] = jnp.where(lane < MLA_NOPE_DIM, 1.0, jnp.where(lo | hi, cos, 0.0))
    sa_ref[0] = jnp.where(lo, -sin, 0.0)
    sb_ref[0] = jnp.where(hi, sin, 0.0)


def _rope_tables(positions):
    bsz, s = positions.shape
    half = MLA_ROPE_DIM // 2
    inv_freq = ROPE_THETA ** (-jnp.arange(half, dtype=F32) / half)
    inv_lane = jnp.concatenate([jnp.zeros((MLA_NOPE_DIM,), F32), inv_freq, inv_freq,
                                jnp.zeros((LANES - MLA_NOPE_DIM - MLA_ROPE_DIM,), F32)]).reshape(1, LANES)
    tm = TOK_TILE
    spec = pl.BlockSpec((1, tm, LANES), lambda b, i: (b, i, 0))
    shape = jax.ShapeDtypeStruct((bsz, s, LANES), F32)
    return pl.pallas_call(
        _rope_tab_kernel,
        grid=(bsz, s // tm),
        in_specs=[pl.BlockSpec((1, tm, 1), lambda b, i: (b, i, 0)),
                  pl.BlockSpec((1, LANES), lambda b, i: (0, 0))],
        out_specs=[spec, spec, spec],
        out_shape=[shape, shape, shape],
        compiler_params=_params("arbitrary", "arbitrary"),
        name="rope_tables",
    )(positions.reshape(bsz, s, 1), inv_lane)


def _bias_tile_kernel(rb_ref, o_ref):
    h, d = pl.program_id(0), pl.program_id(1)
    t = o_ref.shape[-1]
    key = lax.broadcasted_iota(jnp.int32, (t, t), 0)
    qry = lax.broadcasted_iota(jnp.int32, (t, t), 1)
    dist = d * t + qry - key
    n = jnp.maximum(dist, 0)
    nf = jnp.maximum(n, 1).astype(F32)
    large = MAX_EXACT + (jnp.log(nf / MAX_EXACT) / math.log(MAX_DISTANCE / MAX_EXACT)
                         * (N_BUCKETS - MAX_EXACT)).astype(jnp.int32)
    large = jnp.minimum(large, N_BUCKETS - 1)
    bucket = jnp.where(n < MAX_EXACT, n, large)
    val = jnp.zeros((t, t), F32)
    for j in range(N_BUCKETS):
        val = jnp.where(bucket == j, rb_ref[j, h], val)
    val = (rb_ref[N_BUCKETS - 1, h] - val) * LOG2E
    o_ref[0, 0] = jnp.where(dist < 0, -NEG_BIG, val)


def _bias_tiles(rel_bias):
    t = ATTN_TILE
    return pl.pallas_call(
        _bias_tile_kernel,
        grid=(DIFF_HEADS, 2),
        in_specs=[pl.BlockSpec(memory_space=pltpu.SMEM)],
        out_specs=pl.BlockSpec((1, 1, t, t), lambda h, d: (h, d, 0, 0)),
        out_shape=jax.ShapeDtypeStruct((DIFF_HEADS, 2, t, t), F32),
        compiler_params=_params("arbitrary", "arbitrary"),
        name="bias_tiles",
    )(rel_bias)


def _pre_attn_kernel(x_ref, mod_ref, g_ref, w_in_ref, gq_ref, w_uq_ref, gkv_ref, w_uk_ref, w_uv_ref,
                     rc_ref, rsa_ref, rsb_ref,
                     dq_ref, dk_ref, dv_ref, mq_ref, mk_ref, mv_ref):
    x = x_ref[0]
    sh, sc = mod_ref[0, 0:1, :], mod_ref[0, 1:2, :]
    h = (_rms(x, g_ref[...]) * (1.0 + sc) + sh).astype(BF16)
    proj = jnp.dot(h, w_in_ref[...], preferred_element_type=F32)
    dq_ref[0] = (proj[:, C_DQ:C_DK] * (DIFF_QK_DIM ** -0.5 * LOG2E)).astype(BF16)
    dk_ref[0] = proj[:, C_DK:C_DV].astype(BF16)
    dv_ref[0] = proj[:, C_DV:C_MQ].astype(BF16)

    rc, rsa, rsb = rc_ref[0], rsa_ref[0], rsb_ref[0]

    def rope(v):
        return v * rc + pltpu.roll(v, LANES - MLA_ROPE_DIM // 2, 1) * rsa + pltpu.roll(v, MLA_ROPE_DIM // 2, 1) * rsb

    qn = _rms(proj[:, C_MQ:C_KV], gq_ref[...]).astype(BF16)
    q = jnp.dot(qn, w_uq_ref[...], preferred_element_type=F32)
    kvn = _rms(proj[:, C_KV:C_KR], gkv_ref[...]).astype(BF16)
    kn = jnp.dot(kvn, w_uk_ref[...], preferred_element_type=F32)
    mv_ref[0] = jnp.dot(kvn, w_uv_ref[...], preferred_element_type=F32).astype(BF16)
    kr = rope(proj[:, C_KR:C_END])
    q_scale = (MLA_NOPE_DIM + MLA_ROPE_DIM) ** -0.5 * LOG2E
    for hd in range(MLA_HEADS):
        sl = slice(hd * LANES, (hd + 1) * LANES)
        mq_ref[0, :, sl] = (rope(q[:, sl]) * q_scale).astype(BF16)
        mk_ref[0, :, sl] = (kn[:, sl] + kr).astype(BF16)


def _pre_attn(x, mod, g, w_in, gq, w_uq, gkv, w_uk, w_uv, rc, rsa, rsb):
    bsz, s, d = x.shape
    tm = TOK_TILE
    tok = lambda w: pl.BlockSpec((1, tm, w), lambda b, i: (b, i, 0))
    full = lambda a: pl.BlockSpec(a.shape, lambda b, i: (0,) * a.ndim)
    widths = (512, 512, 512, MLA_HEADS * LANES, MLA_HEADS * LANES, MLA_HEADS * MLA_V_DIM)
    return pl.pallas_call(
        _pre_attn_kernel,
        grid=(bsz, s // tm),
        in_specs=[tok(d), pl.BlockSpec((1, ADA_CHUNKS, d), lambda b, i: (b, 0, 0)), full(g), full(w_in),
                  full(gq), full(w_uq), full(gkv), full(w_uk), full(w_uv), tok(LANES), tok(LANES), tok(LANES)],
        out_specs=[tok(w) for w in widths],
        out_shape=[jax.ShapeDtypeStruct((bsz, s, w), BF16) for w in widths],
        compiler_params=_params("arbitrary", "arbitrary"),
        name="pre_attn",
    )(x, mod, g, w_in, gq, w_uq, gkv, w_uk, w_uv, rc, rsa, rsb)


def _softmax_init(m_ref, l_ref, acc_ref):
    m_ref[...] = jnp.full(m_ref.shape, NEG_BIG, F32)
    l_ref[...] = jnp.zeros(l_ref.shape, F32)
    acc_ref[...] = jnp.zeros(acc_ref.shape, F32)


def _softmax_update(st, vt, m_ref, l_ref, acc_ref):
    m_prev = m_ref[...]
    m_new = jnp.maximum(m_prev, jnp.max(st, axis=0, keepdims=True))
    alpha = jnp.exp2(m_prev - m_new)
    p = jnp.exp2(st - m_new)
    l_ref[...] = alpha * l_ref[...] + jnp.sum(p, axis=0, keepdims=True)
    acc_ref[...] = alpha * acc_ref[...] + jnp.dot(vt, p.astype(BF16), preferred_element_type=F32)
    m_ref[...] = m_new


def _attn_prologue(scores, pend_ref):
    for c in range(AHEAD):
        pend_ref[c] = scores(0, c)


def _attn_step(n_chains, scores, consume, pend_ref, j, has_next):
    pending = [pend_ref[c] for c in range(AHEAD)]
    for c in range(n_chains):
        nxt = c + AHEAD
        if nxt < n_chains:
            pending.append(scores(j, nxt))
        elif has_next:
            pend_ref[nxt - n_chains] = scores(j + 1, nxt - n_chains)
        consume(c, pending.pop(0))


def _diff_attn_kernel(lam_init_ref, lam_ref, g_ref, bias_ref, q_ref, k_ref, v_ref, o_ref,
                      vt_ref, m_ref, l_ref, acc_ref, pend_ref):
    i = pl.program_id(1)
    t = q_ref.shape[1]

    @pl.when(i == 0)
    def _():
        vt_ref[...] = v_ref[0].T

    n_maps = 2 * DIFF_HEADS
    for c in range(n_maps):
        _softmax_init(m_ref.at[c], l_ref.at[c], acc_ref.at[c])
    qts = []
    for h in range(DIFF_HEADS):
        qt = q_ref[0, :, h * LANES:(h + 1) * LANES].T
        row = lax.broadcasted_iota(jnp.int32, qt.shape, 0)
        qts.append(jnp.where(row < DIFF_QK_DIM, qt, jnp.zeros_like(qt)))
        qts.append(jnp.where(row >= DIFF_QK_DIM, qt, jnp.zeros_like(qt)))

    def scores(j, c):
        k = k_ref[0, pl.ds(pl.multiple_of(j * t, t), t), (c // 2) * LANES:(c // 2 + 1) * LANES]
        return jnp.dot(k, qts[c], preferred_element_type=F32)

    def step(j, bias_idx, has_next):
        def consume(c, st):
            h = c // 2
            if bias_idx is not None:
                st = st - bias_ref[h, bias_idx]
            vt = vt_ref[h * LANES:(h + 1) * LANES, pl.ds(pl.multiple_of(j * t, t), t)]
            _softmax_update(st, vt, m_ref.at[c], l_ref.at[c], acc_ref.at[c])

        _attn_step(n_maps, scores, consume, pend_ref, j, has_next)

    def far(j, carry):
        step(j, None, True)
        return carry

    _attn_prologue(scores, pend_ref)
    lax.fori_loop(0, jnp.maximum(i - 1, 0), far, 0)

    @pl.when(i >= 1)
    def _():
        step(i - 1, 1, True)

    step(i, 0, False)

    lv = lam_ref[...]
    lam = (jnp.exp(jnp.sum(lv[0:1] * lv[1:2], keepdims=True)) - jnp.exp(jnp.sum(lv[2:3] * lv[3:4], keepdims=True))
           + lam_init_ref[0])
    for h in range(DIFF_HEADS):
        c0, c1 = 2 * h, 2 * h + 1
        ot = acc_ref[c0] / l_ref[c0] - lam * (acc_ref[c1] / l_ref[c1])
        ot = ot * lax.rsqrt(jnp.mean(ot * ot, axis=0, keepdims=True) + NORM_EPS) * g_ref[...]
        o_ref[0, :, h * LANES:(h + 1) * LANES] = (ot * (1.0 - lam_init_ref[0])).T.astype(BF16)


def _diff_attn(lam_init, diff_lambda, g, bias, dq, dk, dv):
    bsz, s, w = dq.shape
    t = ATTN_TILE
    assert MAX_DISTANCE <= t and s % t == 0
    n_maps = 2 * DIFF_HEADS
    return pl.pallas_call(
        _diff_attn_kernel,
        grid=(bsz, s // t),
        in_specs=[pl.BlockSpec(memory_space=pltpu.SMEM),
                  pl.BlockSpec(diff_lambda.shape, lambda b, i: (0, 0)),
                  pl.BlockSpec(g.shape, lambda b, i: (0, 0)),
                  pl.BlockSpec(bias.shape, lambda b, i: (0, 0, 0, 0)),
                  pl.BlockSpec((1, t, w), lambda b, i: (b, i, 0)),
                  pl.BlockSpec((1, s, w), lambda b, i: (b, 0, 0)),
                  pl.BlockSpec((1, s, w), lambda b, i: (b, 0, 0))],
        out_specs=pl.BlockSpec((1, t, w), lambda b, i: (b, i, 0)),
        out_shape=jax.ShapeDtypeStruct(dq.shape, BF16),
        scratch_shapes=[pltpu.VMEM((w, s), BF16), pltpu.VMEM((n_maps, 1, t), F32),
                        pltpu.VMEM((n_maps, 1, t), F32), pltpu.VMEM((n_maps, LANES, t), F32),
                        pltpu.VMEM((AHEAD, t, t), F32)],
        compiler_params=_params("arbitrary", "arbitrary"),
        name="diff_attn",
    )(lam_init, diff_lambda, g, bias, dq, dk, dv)


def _mla_attn_kernel(q_ref, k_ref, v_ref, o_ref, vt_ref, m_ref, l_ref, acc_ref, pend_ref):
    i = pl.program_id(1)
    t = q_ref.shape[1]

    @pl.when(i == 0)
    def _():
        vt_ref[...] = v_ref[0].T

    for h in range(MLA_HEADS):
        _softmax_init(m_ref.at[h], l_ref.at[h], acc_ref.at[h])
    qts = [q_ref[0, :, h * LANES:(h + 1) * LANES].T for h in range(MLA_HEADS)]

    def scores(j, h):
        k = k_ref[0, pl.ds(pl.multiple_of(j * t, t), t), h * LANES:(h + 1) * LANES]
        return jnp.dot(k, qts[h], preferred_element_type=F32)

    def step(j, masked, has_next):
        def consume(h, st):
            vt = vt_ref[h * MLA_V_DIM:(h + 1) * MLA_V_DIM, pl.ds(pl.multiple_of(j * t, t), t)]
            if masked:
                key = lax.broadcasted_iota(jnp.int32, st.shape, 0)
                qry = lax.broadcasted_iota(jnp.int32, st.shape, 1)
                st = jnp.where(key <= qry, st, NEG_BIG)
            _softmax_update(st, vt, m_ref.at[h], l_ref.at[h], acc_ref.at[h])

        _attn_step(MLA_HEADS, scores, consume, pend_ref, j, has_next)

    def far(j, carry):
        step(j, False, True)
        return carry

    _attn_prologue(scores, pend_ref)
    lax.fori_loop(0, i, far, 0)
    step(i, True, False)
    for u in range(MLA_HEADS // 2):
        ot = jnp.concatenate([acc_ref[2 * u] / l_ref[2 * u], acc_ref[2 * u + 1] / l_ref[2 * u + 1]], axis=0)
        o_ref[0, :, u * LANES:(u + 1) * LANES] = ot.T.astype(BF16)


def _mla_attn(mq, mk, mv):
    bsz, s, wq = mq.shape
    wv = mv.shape[-1]
    t = ATTN_TILE
    return pl.pallas_call(
        _mla_attn_kernel,
        grid=(bsz, s // t),
        in_specs=[pl.BlockSpec((1, t, wq), lambda b, i: (b, i, 0)),
                  pl.BlockSpec((1, s, wq), lambda b, i: (b, 0, 0)),
                  pl.BlockSpec((1, s, wv), lambda b, i: (b, 0, 0))],
        out_specs=pl.BlockSpec((1, t, wv), lambda b, i: (b, i, 0)),
        out_shape=jax.ShapeDtypeStruct(mv.shape, BF16),
        scratch_shapes=[pltpu.VMEM((wv, s), BF16), pltpu.VMEM((MLA_HEADS, 1, t), F32),
                        pltpu.VMEM((MLA_HEADS, 1, t), F32), pltpu.VMEM((MLA_HEADS, MLA_V_DIM, t), F32),
                        pltpu.VMEM((AHEAD, t, t), F32)],
        compiler_params=_params("arbitrary", "arbitrary"),
        name="mla_attn",
    )(mq, mk, mv)


def _post_attn_kernel(*refs, moe):
    if moe:
        od_ref, om_ref, x_ref, mod_ref, g_ref, wo_d_ref, wo_m_ref, wr_ref, x1_ref, h_ref, gates_ref, sel_ref = refs
    else:
        od_ref, om_ref, x_ref, mod_ref, g_ref, wo_d_ref, wo_m_ref, x1_ref, h_ref = refs
    y = (jnp.dot(od_ref[0], wo_d_ref[...], preferred_element_type=F32)
         + jnp.dot(om_ref[0], wo_m_ref[...], preferred_element_type=F32))
    gt_a, sh_f, sc_f = mod_ref[0, 2:3, :], mod_ref[0, 3:4, :], mod_ref[0, 4:5, :]
    x1 = x_ref[0] + (1.0 + gt_a) * y
    x1_ref[0] = x1
    h = _rms(x1, g_ref[...]) * (1.0 + sc_f) + sh_f
    h_ref[0] = h.astype(h_ref.dtype)
    if not moe:
        return
    logits = jnp.dot(h, wr_ref[...], preferred_element_type=F32, precision=lax.Precision.HIGHEST)
    lane = lax.broadcasted_iota(jnp.int32, logits.shape, 1)
    logits = jnp.where(lane < N_EXPERTS, logits, -jnp.inf)
    v1 = jnp.max(logits, axis=1, keepdims=True)
    i1 = jnp.min(jnp.where(logits == v1, lane, LANES), axis=1, keepdims=True)
    rest = jnp.where(lane == i1, -jnp.inf, logits)
    v2 = jnp.max(rest, axis=1, keepdims=True)
    i2 = jnp.min(jnp.where(rest == v2, lane, LANES), axis=1, keepdims=True)
    e2 = jnp.exp(v2 - v1)
    w1 = 1.0 / (1.0 + e2)
    w2 = e2 / (1.0 + e2)
    gates_ref[0] = jnp.where(lane == i1, w1, 0.0) + jnp.where(lane == i2, w2, 0.0)
    sel_ref[0] = jnp.where(lane == i1, 1.0, 0.0) + jnp.where(lane == i2, 2.0, 0.0)


def _post_attn(o_diff, o_mla, x, mod, g, wo_d, wo_m, w_router=None):
    bsz, s, d = x.shape
    moe = w_router is not None
    tm = TOK_TILE
    tok = lambda w: pl.BlockSpec((1, tm, w), lambda b, i: (b, i, 0))
    full = lambda a: pl.BlockSpec(a.shape, lambda b, i: (0,) * a.ndim)
    args = [o_diff, o_mla, x, mod, g, wo_d, wo_m] + ([w_router] if moe else [])
    in_specs = [tok(o_diff.shape[-1]), tok(o_mla.shape[-1]), tok(d),
                pl.BlockSpec((1, ADA_CHUNKS, d), lambda b, i: (b, 0, 0)),
                full(g), full(wo_d), full(wo_m)] + ([full(w_router)] if moe else [])
    out_specs = [tok(d), tok(d)] + ([tok(LANES), tok(LANES)] if moe else [])
    out_shape = [jax.ShapeDtypeStruct((bsz, s, d), F32), jax.ShapeDtypeStruct((bsz, s, d), F32 if moe else BF16)]
    if moe:
        out_shape += [jax.ShapeDtypeStruct((bsz, s, LANES), F32)] * 2
    return pl.pallas_call(
        functools.partial(_post_attn_kernel, moe=moe),
        grid=(bsz, s // tm),
        in_specs=in_specs,
        out_specs=out_specs,
        out_shape=out_shape,
        compiler_params=_params("arbitrary", "arbitrary"),
        name="post_attn_moe" if moe else "post_attn",
    )(*args)


def _swiglu(h, w1_ref, w3_ref, w2_ref):
    d_ff = w1_ref.shape[-1]
    y = jnp.zeros((h.shape[0], w2_ref.shape[-1]), F32)
    for c0 in range(0, d_ff, FF_CHUNK):
        a = jnp.dot(h, w1_ref[:, c0:c0 + FF_CHUNK], preferred_element_type=F32)
        b = jnp.dot(h, w3_ref[:, c0:c0 + FF_CHUNK], preferred_element_type=F32)
        u = (a * jax.nn.sigmoid(a) * b).astype(BF16)
        y = y + jnp.dot(u, w2_ref[c0:c0 + FF_CHUNK, :], preferred_element_type=F32)
    return y


def _ffn_dense_kernel(h_ref, x1_ref, mod_ref, w1_ref, w3_ref, w2_ref, o_ref):
    gt_f = mod_ref[0, 5:6, :]
    o_ref[0] = x1_ref[0] + (1.0 + gt_f) * _swiglu(h_ref[0], w1_ref, w3_ref, w2_ref)


def _ffn_dense(h, x1, mod, w1, w3, w2):
    bsz, s, d = x1.shape
    tm = FFN_TILE
    tok = lambda w: pl.BlockSpec((1, tm, w), lambda b, i: (b, i, 0))
    full = lambda a: pl.BlockSpec(a.shape, lambda b, i: (0,) * a.ndim)
    return pl.pallas_call(
        _ffn_dense_kernel,
        grid=(bsz, s // tm),
        in_specs=[tok(d), tok(d), pl.BlockSpec((1, ADA_CHUNKS, d), lambda b, i: (b, 0, 0)),
                  full(w1), full(w3), full(w2)],
        out_specs=tok(d),
        out_shape=jax.ShapeDtypeStruct(x1.shape, F32),
        compiler_params=_params("arbitrary", "arbitrary"),
        name="ffn_dense",
    )(h, x1, mod, w1, w3, w2)


def _route_kernel(sel_ref, pos_ref, te_ref, cnt_ref, off_ref, run_ref, *, row_tile):
    p, t = pl.program_id(0), pl.program_id(1)
    sel_t = sel_ref[...].T
    chosen = (sel_t > 0.0).astype(F32)
    per_expert = jnp.sum(chosen, axis=1, keepdims=True)

    @pl.when((p == 0) & (t == 0))
    def _():
        cnt_ref[...] = jnp.zeros(cnt_ref.shape, F32)
        te_ref[...] = jnp.zeros(te_ref.shape, jnp.int32)

    @pl.when(p == 0)
    def _():
        cnt_ref[...] += per_expert

    @pl.when((p == 1) & (t == 0))
    def _():
        cnt = cnt_ref[...]
        padded = jnp.ceil(cnt / row_tile) * row_tile
        row = lax.broadcasted_iota(jnp.int32, cnt.shape, 0)
        off = jnp.zeros(cnt.shape, F32)
        for e in range(N_EXPERTS):
            size_e = jnp.sum(jnp.where(row == e, padded, 0.0), keepdims=True)
            off = off + jnp.where(row > e, size_e, 0.0)
        off_ref[...] = off
        run_ref[...] = jnp.zeros(run_ref.shape, F32)
        ends = off + padded
        tile_start = lax.broadcasted_iota(jnp.int32, (LANES, LANES), 1).astype(F32) * row_tile
        erow = lax.broadcasted_iota(jnp.int32, (LANES, LANES), 0)
        done = jnp.where((erow < N_EXPERTS) & (ends <= tile_start), 1.0, 0.0)
        te = jnp.sum(done, axis=0, keepdims=True).astype(jnp.int32)
        te_ref[...] = jnp.broadcast_to(te, te_ref.shape)

    @pl.when(p == 1)
    def _():
        tm = sel_t.shape[1]
        before = (lax.broadcasted_iota(jnp.int32, (tm, tm), 0)
                  < lax.broadcasted_iota(jnp.int32, (tm, tm), 1)).astype(BF16)
        rank = jnp.dot(chosen.astype(BF16), before, preferred_element_type=F32) + run_ref[...]
        base = off_ref[...] + rank
        for k in range(2):
            pos = jnp.sum(jnp.where(sel_t == float(k + 1), base, 0.0), axis=0, keepdims=True)
            pos_ref[0, k:k + 1, :] = pos.astype(jnp.int32)
        run_ref[...] += per_expert


def _route(sel, row_tile, n_row_tiles):
    n_tok = sel.shape[0]
    assert n_row_tiles <= LANES and 2 * n_tok < 2 ** 24
    tm = TOK_TILE
    nt = n_tok // tm
    pos, te = pl.pallas_call(
        functools.partial(_route_kernel, row_tile=row_tile),
        grid=(2, nt),
        in_specs=[pl.BlockSpec((tm, LANES), lambda p, t: (t, 0))],
        out_specs=[pl.BlockSpec((1, 2, tm), lambda p, t: (p * t, 0, 0)),
                   pl.BlockSpec((8, LANES), lambda p, t: (0, 0))],
        out_shape=[jax.ShapeDtypeStruct((nt, 2, tm), jnp.int32), jax.ShapeDtypeStruct((8, LANES), jnp.int32)],
        scratch_shapes=[pltpu.VMEM((LANES, 1), F32)] * 3,
        compiler_params=_params("arbitrary", "arbitrary"),
        name="moe_route",
    )(sel)
    return pos, te[0]


def _dispatch_kernel(pos_ref, h_ref, xs_in_ref, xs_ref, sem):
    del xs_in_ref
    tm = h_ref.shape[0]

    def body(r, carry):
        for k in range(2):
            pltpu.make_async_copy(h_ref.at[pl.ds(r, 1)], xs_ref.at[pl.ds(pos_ref[0, k, r], 1)], sem).start()
        return carry

    lax.fori_loop(0, tm, body, 0, unroll=8)
    for k in range(2):
        pltpu.make_async_copy(h_ref, xs_ref.at[pl.ds(0, tm)], sem).wait()


def _dispatch(pos, h, n_rows):
    n_tok, d = h.shape
    nt, _, tm = pos.shape
    xs0 = jnp.zeros((n_rows, d), h.dtype)
    return pl.pallas_call(
        _dispatch_kernel,
        grid=(nt,),
        in_specs=[pl.BlockSpec((1, 2, tm), lambda t: (t, 0, 0), memory_space=pltpu.SMEM),
                  pl.BlockSpec((tm, d), lambda t: (t, 0)),
                  pl.BlockSpec(memory_space=pl.ANY)],
        out_specs=pl.BlockSpec(memory_space=pl.ANY),
        out_shape=jax.ShapeDtypeStruct(xs0.shape, xs0.dtype),
        scratch_shapes=[pltpu.SemaphoreType.DMA(())],
        input_output_aliases={2: 0},
        compiler_params=_params("arbitrary"),
        name="moe_dispatch",
    )(pos, h, xs0)


def _expert_kernel(te_ref, xs_ref, w1_ref, w3_ref, w2_ref, y_ref):
    used = te_ref[pl.program_id(0)] < N_EXPERTS

    @pl.when(used)
    def _():
        y_ref[...] = _swiglu(xs_ref[...].astype(BF16), w1_ref.at[0], w3_ref.at[0], w2_ref.at[0])

    @pl.when(jnp.logical_not(used))
    def _():
        y_ref[...] = jnp.zeros(y_ref.shape, y_ref.dtype)


def _experts(te, xs, w1, w3, w2, row_tile):
    n_rows, d = xs.shape
    d_ff = w1.shape[-1]
    expert = lambda n, te: (jnp.minimum(te[n], N_EXPERTS - 1), 0, 0)
    return pl.pallas_call(
        _expert_kernel,
        grid_spec=pltpu.PrefetchScalarGridSpec(
            num_scalar_prefetch=1,
            grid=(n_rows // row_tile,),
            in_specs=[pl.BlockSpec((row_tile, d), lambda n, te: (n, 0)),
                      pl.BlockSpec((1, d, d_ff), expert), pl.BlockSpec((1, d, d_ff), expert),
                      pl.BlockSpec((1, d_ff, d), expert)],
            out_specs=pl.BlockSpec((row_tile, d), lambda n, te: (n, 0))),
        out_shape=jax.ShapeDtypeStruct(xs.shape, F32),
        compiler_params=_params("arbitrary"),
        name="moe_experts",
    )(te, xs, w1, w3, w2)


def _combine_kernel(pos_ref, x1_ref, gates_ref, sel_ref, mod_ref, ys_ref, o_ref, ya_ref, yb_ref, sem):
    tm = x1_ref.shape[1]
    bufs = (ya_ref, yb_ref)

    def body(r, carry):
        for k in range(2):
            pltpu.make_async_copy(ys_ref.at[pl.ds(pos_ref[0, k, r], 1)], bufs[k].at[pl.ds(r, 1)], sem).start()
        return carry

    lax.fori_loop(0, tm, body, 0, unroll=8)
    gates, sel = gates_ref[0], sel_ref[0]
    w_a = jnp.sum(jnp.where(sel == 1.0, gates, 0.0), axis=1, keepdims=True)
    w_b = jnp.sum(jnp.where(sel == 2.0, gates, 0.0), axis=1, keepdims=True)
    for k in range(2):
        pltpu.make_async_copy(ys_ref.at[pl.ds(0, tm)], bufs[k], sem).wait()
    gt_f = mod_ref[0, 5:6, :]
    o_ref[0] = x1_ref[0] + (1.0 + gt_f) * (w_a * ya_ref[...] + w_b * yb_ref[...])


def _combine(pos, x1, gates, sel, mod, ys):
    bsz, s, d = x1.shape
    nt, _, tm = pos.shape
    per_b = s // tm
    tok = lambda w: pl.BlockSpec((1, tm, w), lambda b, i: (b, i, 0))
    return pl.pallas_call(
        _combine_kernel,
        grid=(bsz, per_b),
        in_specs=[pl.BlockSpec((1, 2, tm), lambda b, i: (b * per_b + i, 0, 0), memory_space=pltpu.SMEM),
                  tok(d), tok(LANES), tok(LANES),
                  pl.BlockSpec((1, ADA_CHUNKS, d), lambda b, i: (b, 0, 0)),
                  pl.BlockSpec(memory_space=pl.ANY)],
        out_specs=tok(d),
        out_shape=jax.ShapeDtypeStruct(x1.shape, F32),
        scratch_shapes=[pltpu.VMEM((tm, d), F32), pltpu.VMEM((tm, d), F32), pltpu.SemaphoreType.DMA(())],
        compiler_params=_params("arbitrary", "arbitrary"),
        name="moe_combine",
    )(pos, x1, gates, sel, mod, ys)


def _moe(h, x1, gates, sel, mod, w1, w3, w2):
    bsz, s, d = x1.shape
    n_tok = bsz * s
    row_tile = FFN_TILE
    n_rows = 2 * n_tok + N_EXPERTS * row_tile
    pos, te = _route(sel.reshape(n_tok, LANES), row_tile, n_rows // row_tile)
    xs = _dispatch(pos, h.reshape(n_tok, d), n_rows)
    ys = _experts(te, xs, w1, w3, w2, row_tile)
    return _combine(pos, x1, gates, sel, mod, ys)


def _final_norm_kernel(x_ref, g_ref, o_ref):
    o_ref[0] = _rms(x_ref[0], g_ref[...])


def _final_norm(x, g):
    bsz, s, d = x.shape
    tm = TOK_TILE
    tok = pl.BlockSpec((1, tm, d), lambda b, i: (b, i, 0))
    return pl.pallas_call(
        _final_norm_kernel,
        grid=(bsz, s // tm),
        in_specs=[tok, pl.BlockSpec(g.shape, lambda b, i: (0, 0))],
        out_specs=tok,
        out_shape=jax.ShapeDtypeStruct(x.shape, F32),
        compiler_params=_params("arbitrary", "arbitrary"),
        name="final_norm",
    )(x, g)


def _pad_heads(w, heads, width, lo, hi):
    k = w.shape[0]
    w = w.reshape(k, heads, width)[:, :, lo:hi]
    return jnp.pad(w, ((0, 0), (0, 0), (0, LANES - (hi - lo)))).reshape(k, heads * LANES)


def _prep_w_in(w_in):
    d = w_in.shape[0]
    kr = w_in[:, C_KR:]
    kr_block = jnp.concatenate([jnp.zeros((d, MLA_NOPE_DIM), w_in.dtype), kr,
                                jnp.zeros((d, LANES - MLA_NOPE_DIM - MLA_ROPE_DIM), w_in.dtype)], axis=1)
    return jnp.concatenate([w_in[:, :C_KR], kr_block], axis=1).astype(BF16)


def kernel(x, c, positions, w_ada, b_ada, g_attn, w_in, diff_lambda, diff_subln_g, rel_bias, mla_q_norm, w_uq, mla_kv_norm, w_ukv, w_o, g_ffn, ffn_w1, ffn_w3, ffn_w2, moe_router, moe_w1, moe_w3, moe_w2, g_final):
    depth = w_ada.shape[0]
    bsz, s, d = x.shape
    mods = _ada(c, w_ada, b_ada).reshape(depth, bsz, ADA_CHUNKS, d)
    rc, rsa, rsb = _rope_tables(positions)
    bias = _bias_tiles(rel_bias)
    qk_w = MLA_NOPE_DIM + MLA_ROPE_DIM
    kv_w = MLA_NOPE_DIM + MLA_V_DIM
    for l in range(depth):
        mod = mods[l]
        lam_init = jnp.full((1,), 0.8 - 0.6 * math.exp(-0.3 * l), F32)
        w_uq_p = _pad_heads(w_uq[l], MLA_HEADS, qk_w, 0, qk_w).astype(BF16)
        w_uk_p = _pad_heads(w_ukv[l], MLA_HEADS, kv_w, 0, MLA_NOPE_DIM).astype(BF16)
        w_uv_p = w_ukv[l].reshape(MLA_KV_RANK, MLA_HEADS, kv_w)[:, :, MLA_NOPE_DIM:].reshape(
            MLA_KV_RANK, MLA_HEADS * MLA_V_DIM).astype(BF16)
        dq, dk, dv, mq, mk, mv = _pre_attn(
            x, mod, g_attn[l].reshape(1, d), _prep_w_in(w_in[l]), mla_q_norm[l].reshape(1, -1), w_uq_p,
            mla_kv_norm[l].reshape(1, -1), w_uk_p, w_uv_p, rc, rsa, rsb)
        o_diff = _diff_attn(lam_init, diff_lambda[l], diff_subln_g[l].reshape(-1, 1), bias, dq, dk, dv)
        o_mla = _mla_attn(mq, mk, mv)
        wo = w_o[l].astype(BF16)
        n_diff = DIFF_HEADS * DIFF_V_DIM
        g_f = g_ffn[l].reshape(1, d)
        if l % 2 == 1:
            w_router = jnp.pad(moe_router[l // 2], ((0, 0), (0, LANES - N_EXPERTS)))
            x1, h, gates, sel = _post_attn(o_diff, o_mla, x, mod, g_f, wo[:n_diff], wo[n_diff:], w_router)
            x = _moe(h, x1, gates, sel, mod, moe_w1[l // 2].astype(BF16), moe_w3[l // 2].astype(BF16),
                     moe_w2[l // 2].astype(BF16))
        else:
            x1, h = _post_attn(o_diff, o_mla, x, mod, g_f, wo[:n_diff], wo[n_diff:])
            x = _ffn_dense(h, x1, mod, ffn_w1[l // 2].astype(BF16), ffn_w3[l // 2].astype(BF16),
                           ffn_w2[l // 2].astype(BF16))
    return _final_norm(x, g_final.reshape(1, d))
```

```python
import functools
import math

import jax
import jax.numpy as jnp
from jax import lax
from jax.experimental import pallas as pl
from jax.experimental.pallas import tpu as pltpu

F32 = jnp.float32
BF16 = jnp.bfloat16

DIFF_HEADS = 4
DIFF_QK_DIM = 64
DIFF_V_DIM = 128
MLA_HEADS = 8
MLA_NOPE_DIM = 64
MLA_ROPE_DIM = 32
MLA_V_DIM = 64
MLA_Q_RANK = 384
MLA_KV_RANK = 256
ROPE_THETA = 10000.0
N_BUCKETS = 32
MAX_EXACT = 16
MAX_DISTANCE = 128
N_EXPERTS = 8
NORM_EPS = 1e-6
ADA_CHUNKS = 6

LANES = 128
NEG_BIG = -1e30
LOG2E = math.log2(math.e)
VMEM_LIMIT = 56 * 1024 * 1024

ATTN_TILE = 256
AHEAD = 4
TOK_TILE = 512
FFN_TILE = 512
FF_CHUNK = 256

C_DQ, C_DK, C_DV, C_MQ, C_KV, C_KR, C_END = 0, 512, 1024, 1536, 1920, 2176, 2304


def _params(*sem):
    return pltpu.CompilerParams(dimension_semantics=sem, vmem_limit_bytes=VMEM_LIMIT)


def _rms(x, g):
    return x * lax.rsqrt(jnp.mean(x * x, axis=-1, keepdims=True) + NORM_EPS) * g


def _dot_nt(a, b):
    return lax.dot_general(a, b, (((1,), (1,)), ((), ())), preferred_element_type=F32)


def _ada_kernel(c_ref, w_ref, b_ref, o_ref):
    c = c_ref[...]
    cond = c * jax.nn.sigmoid(c)
    o_ref[0] = jnp.dot(cond, w_ref[0], preferred_element_type=F32,
                       precision=lax.Precision.HIGHEST) + b_ref[0]


def _ada(c, w_ada, b_ada):
    depth, d, n = w_ada.shape
    bsz = c.shape[0]
    tn = 1536
    return pl.pallas_call(
        _ada_kernel,
        grid=(depth, n // tn),
        in_specs=[pl.BlockSpec((bsz, d), lambda l, j: (0, 0)),
                  pl.BlockSpec((1, d, tn), lambda l, j: (l, 0, j)),
                  pl.BlockSpec((1, 1, tn), lambda l, j: (l, 0, j))],
        out_specs=pl.BlockSpec((1, bsz, tn), lambda l, j: (l, 0, j)),
        out_shape=jax.ShapeDtypeStruct((depth, bsz, n), F32),
        compiler_params=_params("arbitrary", "arbitrary"),
        name="ada_mod",
    )(c, w_ada, b_ada.reshape(depth, 1, n))


def _rope_tab_kernel(pos_ref, inv_ref, c_ref, sa_ref, sb_ref):
    pos = pos_ref[0].astype(F32)
    ang = pos * inv_ref[...]
    lane = lax.broadcasted_iota(jnp.int32, ang.shape, 1)
    cos, sin = jnp.cos(ang), jnp.sin(ang)
    lo = (lane >= MLA_NOPE_DIM) & (lane < MLA_NOPE_DIM + MLA_ROPE_DIM // 2)
    hi = (lane >= MLA_NOPE_DIM + MLA_ROPE_DIM // 2) & (lane < MLA_NOPE_DIM + MLA_ROPE_DIM)
    c_ref[0] = jnp.where(lane < MLA_NOPE_DIM, 1.0, jnp.where(lo | hi, cos, 0.0))
    sa_ref[0] = jnp.where(lo, -sin, 0.0)
    sb_ref[0] = jnp.where(hi, sin, 0.0)


def _rope_tables(positions):
    bsz, s = positions.shape
    half = MLA_ROPE_DIM // 2
    inv_freq = ROPE_THETA ** (-jnp.arange(half, dtype=F32) / half)
    inv_lane = jnp.concatenate([jnp.zeros((MLA_NOPE_DIM,), F32), inv_freq, inv_freq,
                                jnp.zeros((LANES - MLA_NOPE_DIM - MLA_ROPE_DIM,), F32)]).reshape(1, LANES)
    tm = TOK_TILE
    spec = pl.BlockSpec((1, tm, LANES), lambda b, i: (b, i, 0))
    shape = jax.ShapeDtypeStruct((bsz, s, LANES), F32)
    return pl.pallas_call(
        _rope_tab_kernel,
        grid=(bsz, s // tm),
        in_specs=[pl.BlockSpec((1, tm, 1), lambda b, i: (b, i, 0)),
                  pl.BlockSpec((1, LANES), lambda b, i: (0, 0))],
        out_specs=[spec, spec, spec],
        out_shape=[shape, shape, shape],
        compiler_params=_params("arbitrary", "arbitrary"),
        name="rope_tables",
    )(positions.reshape(bsz, s, 1), inv_lane)


def _bias_tile_kernel(rb_ref, o_ref):
    h, d = pl.program_id(0), pl.program_id(1)
    t = o_ref.shape[-1]
    key = lax.broadcasted_iota(jnp.int32, (t, t), 0)
    qry = lax.broadcasted_iota(jnp.int32, (t, t), 1)
    dist = d * t + qry - key
    n = jnp.maximum(dist, 0)
    nf = jnp.maximum(n, 1).astype(F32)
    large = MAX_EXACT + (jnp.log(nf / MAX_EXACT) / math.log(MAX_DISTANCE / MAX_EXACT)
                         * (N_BUCKETS - MAX_EXACT)).astype(jnp.int32)
    large = jnp.minimum(large, N_BUCKETS - 1)
    bucket = jnp.where(n < MAX_EXACT, n, large)
    val = jnp.zeros((t, t), F32)
    for j in range(N_BUCKETS):
        val = jnp.where(bucket == j, rb_ref[j, h], val)
    val = (rb_ref[N_BUCKETS - 1, h] - val) * LOG2E
    o_ref[0, 0] = jnp.where(dist < 0, -NEG_BIG, val)


def _bias_tiles(rel_bias):
    t = ATTN_TILE
    return pl.pallas_call(
        _bias_tile_kernel,
        grid=(DIFF_HEADS, 2),
        in_specs=[pl.BlockSpec(memory_space=pltpu.SMEM)],
        out_specs=pl.BlockSpec((1, 1, t, t), lambda h, d: (h, d, 0, 0)),
        out_shape=jax.ShapeDtypeStruct((DIFF_HEADS, 2, t, t), F32),
        compiler_params=_params("arbitrary", "arbitrary"),
        name="bias_tiles",
    )(rel_bias)


def _pre_attn_kernel(x_ref, mod_ref, g_ref, w_in_ref, gq_ref, w_uq_ref, gkv_ref, w_uk_ref, w_uv_ref,
                     rc_ref, rsa_ref, rsb_ref,
                     dq_ref, dk_ref, dv_ref, mq_ref, mk_ref, mv_ref):
    x = x_ref[0]
    sh, sc = mod_ref[0, 0:1, :], mod_ref[0, 1:2, :]
    h = (_rms(x, g_ref[...]) * (1.0 + sc) + sh).astype(BF16)
    proj = jnp.dot(h, w_in_ref[...], preferred_element_type=F32)
    dq_ref[0] = (proj[:, C_DQ:C_DK] * (DIFF_QK_DIM ** -0.5 * LOG2E)).astype(BF16)
    dk_ref[0] = proj[:, C_DK:C_DV].astype(BF16)
    dv_ref[0] = proj[:, C_DV:C_MQ].astype(BF16)

    rc, rsa, rsb = rc_ref[0], rsa_ref[0], rsb_ref[0]

    def rope(v):
        return v * rc + pltpu.roll(v, LANES - MLA_ROPE_DIM // 2, 1) * rsa + pltpu.roll(v, MLA_ROPE_DIM // 2, 1) * rsb

    qn = _rms(proj[:, C_MQ:C_KV], gq_ref[...]).astype(BF16)
    q = jnp.dot(qn, w_uq_ref[...], preferred_element_type=F32)
    kvn = _rms(proj[:, C_KV:C_KR], gkv_ref[...]).astype(BF16)
    kn = jnp.dot(kvn, w_uk_ref[...], preferred_element_type=F32)
    mv_ref[0] = jnp.dot(kvn, w_uv_ref[...], preferred_element_type=F32).astype(BF16)
    kr = rope(proj[:, C_KR:C_END])
    q_scale = (MLA_NOPE_DIM + MLA_ROPE_DIM) ** -0.5 * LOG2E
    for hd in range(MLA_HEADS):
        sl = slice(hd * LANES, (hd + 1) * LANES)
        mq_ref[0, :, sl] = (rope(q[:, sl]) * q_scale).astype(BF16)
        mk_ref[0, :, sl] = (kn[:, sl] + kr).astype(BF16)


def _pre_attn(x, mod, g, w_in, gq, w_uq, gkv, w_uk, w_uv, rc, rsa, rsb):
    bsz, s, d = x.shape
    tm = TOK_TILE
    tok = lambda w: pl.BlockSpec((1, tm, w), lambda b, i: (b, i, 0))
    full = lambda a: pl.BlockSpec(a.shape, lambda b, i: (0,) * a.ndim)
    widths = (512, 512, 512, MLA_HEADS * LANES, MLA_HEADS * LANES, MLA_HEADS * MLA_V_DIM)
    return pl.pallas_call(
        _pre_attn_kernel,
        grid=(bsz, s // tm),
        in_specs=[tok(d), pl.BlockSpec((1, ADA_CHUNKS, d), lambda b, i: (b, 0, 0)), full(g), full(w_in),
                  full(gq), full(w_uq), full(gkv), full(w_uk), full(w_uv), tok(LANES), tok(LANES), tok(LANES)],
        out_specs=[tok(w) for w in widths],
        out_shape=[jax.ShapeDtypeStruct((bsz, s, w), BF16) for w in widths],
        compiler_params=_params("arbitrary", "arbitrary"),
        name="pre_attn",
    )(x, mod, g, w_in, gq, w_uq, gkv, w_uk, w_uv, rc, rsa, rsb)


def _softmax_init(m_ref, l_ref, acc_ref):
    m_ref[...] = jnp.full(m_ref.shape, NEG_BIG, F32)
    l_ref[...] = jnp.zeros(l_ref.shape, F32)
    acc_ref[...] = jnp.zeros(acc_ref.shape, F32)


def _softmax_update(st, vt, m_ref, l_ref, acc_ref):
    m_prev = m_ref[...]
    m_new = jnp.maximum(m_prev, jnp.max(st, axis=0, keepdims=True))
    alpha = jnp.exp2(m_prev - m_new)
    p = jnp.exp2(st - m_new)
    l_ref[...] = alpha * l_ref[...] + jnp.sum(p, axis=0, keepdims=True)
    acc_ref[...] = alpha * acc_ref[...] + jnp.dot(vt, p.astype(BF16), preferred_element_type=F32)
    m_ref[...] = m_new


def _attn_prologue(scores, pend_ref):
    for c in range(AHEAD):
        pend_ref[c] = scores(0, c)


def _attn_step(n_chains, scores, consume, pend_ref, j, has_next):
    pending = [pend_ref[c] for c in range(AHEAD)]
    for c in range(n_chains):
        nxt = c + AHEAD
        if nxt < n_chains:
            pending.append(scores(j, nxt))
        elif has_next:
            pend_ref[nxt - n_chains] = scores(j + 1, nxt - n_chains)
        consume(c, pending.pop(0))


def _diff_attn_kernel(lam_init_ref, lam_ref, g_ref, bias_ref, q_ref, k_ref, v_ref, o_ref,
                      vt_ref, m_ref, l_ref, acc_ref, pend_ref):
    i = pl.program_id(1)
    t = q_ref.shape[1]

    @pl.when(i == 0)
    def _():
        vt_ref[...] = v_ref[0].T

    n_maps = 2 * DIFF_HEADS
    for c in range(n_maps):
        _softmax_init(m_ref.at[c], l_ref.at[c], acc_ref.at[c])
    qts = []
    for h in range(DIFF_HEADS):
        qt = q_ref[0, :, h * LANES:(h + 1) * LANES].T
        row = lax.broadcasted_iota(jnp.int32, qt.shape, 0)
        qts.append(jnp.where(row < DIFF_QK_DIM, qt, jnp.zeros_like(qt)))
        qts.append(jnp.where(row >= DIFF_QK_DIM, qt, jnp.zeros_like(qt)))

    def scores(j, c):
        k = k_ref[0, pl.ds(pl.multiple_of(j * t, t), t), (c // 2) * LANES:(c // 2 + 1) * LANES]
        return jnp.dot(k, qts[c], preferred_element_type=F32)

    def step(j, bias_idx, has_next):
        def consume(c, st):
            h = c // 2
            if bias_idx is not None:
                st = st - bias_ref[h, bias_idx]
            vt = vt_ref[h * LANES:(h + 1) * LANES, pl.ds(pl.multiple_of(j * t, t), t)]
            _softmax_update(st, vt, m_ref.at[c], l_ref.at[c], acc_ref.at[c])

        _attn_step(n_maps, scores, consume, pend_ref, j, has_next)

    def far(j, carry):
        step(j, None, True)
        return carry

    _attn_prologue(scores, pend_ref)
    lax.fori_loop(0, jnp.maximum(i - 1, 0), far, 0)

    @pl.when(i >= 1)
    def _():
        step(i - 1, 1, True)

    step(i, 0, False)

    lv = lam_ref[...]
    lam = (jnp.exp(jnp.sum(lv[0:1] * lv[1:2], keepdims=True)) - jnp.exp(jnp.sum(lv[2:3] * lv[3:4], keepdims=True))
           + lam_init_ref[0])
    for h in range(DIFF_HEADS):
        c0, c1 = 2 * h, 2 * h + 1
        ot = acc_ref[c0] / l_ref[c0] - lam * (acc_ref[c1] / l_ref[c1])
        ot = ot * lax.rsqrt(jnp.mean(ot * ot, axis=0, keepdims=True) + NORM_EPS) * g_ref[...]
        o_ref[0, :, h * LANES:(h + 1) * LANES] = (ot * (1.0 - lam_init_ref[0])).T.astype(BF16)


def _diff_attn(lam_init, diff_lambda, g, bias, dq, dk, dv):
    bsz, s, w = dq.shape
    t = ATTN_TILE
    assert MAX_DISTANCE <= t and s % t == 0
    n_maps = 2 * DIFF_HEADS
    return pl.pallas_call(
        _diff_attn_kernel,
        grid=(bsz, s // t),
        in_specs=[pl.BlockSpec(memory_space=pltpu.SMEM),
                  pl.BlockSpec(diff_lambda.shape, lambda b, i: (0, 0)),
                  pl.BlockSpec(g.shape, lambda b, i: (0, 0)),
                  pl.BlockSpec(bias.shape, lambda b, i: (0, 0, 0, 0)),
                  pl.BlockSpec((1, t, w), lambda b, i: (b, i, 0)),
                  pl.BlockSpec((1, s, w), lambda b, i: (b, 0, 0)),
                  pl.BlockSpec((1, s, w), lambda b, i: (b, 0, 0))],
        out_specs=pl.BlockSpec((1, t, w), lambda b, i: (b, i, 0)),
        out_shape=jax.ShapeDtypeStruct(dq.shape, BF16),
        scratch_shapes=[pltpu.VMEM((w, s), BF16), pltpu.VMEM((n_maps, 1, t), F32),
                        pltpu.VMEM((n_maps, 1, t), F32), pltpu.VMEM((n_maps, LANES, t), F32),
                        pltpu.VMEM((AHEAD, t, t), F32)],
        compiler_params=_params("arbitrary", "arbitrary"),
        name="diff_attn",
    )(lam_init, diff_lambda, g, bias, dq, dk, dv)


def _mla_attn_kernel(q_ref, k_ref, v_ref, o_ref, vt_ref, m_ref, l_ref, acc_ref, pend_ref):
    i = pl.program_id(1)
    t = q_ref.shape[1]

    @pl.when(i == 0)
    def _():
        vt_ref[...] = v_ref[0].T

    for h in range(MLA_HEADS):
        _softmax_init(m_ref.at[h], l_ref.at[h], acc_ref.at[h])
    qts = [q_ref[0, :, h * LANES:(h + 1) * LANES].T for h in range(MLA_HEADS)]

    def scores(j, h):
        k = k_ref[0, pl.ds(pl.multiple_of(j * t, t), t), h * LANES:(h + 1) * LANES]
        return jnp.dot(k, qts[h], preferred_element_type=F32)

    def step(j, masked, has_next):
        def consume(h, st):
            vt = vt_ref[h * MLA_V_DIM:(h + 1) * MLA_V_DIM, pl.ds(pl.multiple_of(j * t, t), t)]
            if masked:
                key = lax.broadcasted_iota(jnp.int32, st.shape, 0)
                qry = lax.broadcasted_iota(jnp.int32, st.shape, 1)
                st = jnp.where(key <= qry, st, NEG_BIG)
            _softmax_update(st, vt, m_ref.at[h], l_ref.at[h], acc_ref.at[h])

        _attn_step(MLA_HEADS, scores, consume, pend_ref, j, has_next)

    def far(j, carry):
        step(j, False, True)
        return carry

    _attn_prologue(scores, pend_ref)
    lax.fori_loop(0, i, far, 0)
    step(i, True, False)
    for u in range(MLA_HEADS // 2):
        ot = jnp.concatenate([acc_ref[2 * u] / l_ref[2 * u], acc_ref[2 * u + 1] / l_ref[2 * u + 1]], axis=0)
        o_ref[0, :, u * LANES:(u + 1) * LANES] = ot.T.astype(BF16)


def _mla_attn(mq, mk, mv):
    bsz, s, wq = mq.shape
    wv = mv.shape[-1]
    t = ATTN_TILE
    return pl.pallas_call(
        _mla_attn_kernel,
        grid=(bsz, s // t),
        in_specs=[pl.BlockSpec((1, t, wq), lambda b, i: (b, i, 0)),
                  pl.BlockSpec((1, s, wq), lambda b, i: (b, 0, 0)),
                  pl.BlockSpec((1, s, wv), lambda b, i: (b, 0, 0))],
        out_specs=pl.BlockSpec((1, t, wv), lambda b, i: (b, i, 0)),
        out_shape=jax.ShapeDtypeStruct(mv.shape, BF16),
        scratch_shapes=[pltpu.VMEM((wv, s), BF16), pltpu.VMEM((MLA_HEADS, 1, t), F32),
                        pltpu.VMEM((MLA_HEADS, 1, t), F32), pltpu.VMEM((MLA_HEADS, MLA_V_DIM, t), F32),
                        pltpu.VMEM((AHEAD, t, t), F32)],
        compiler_params=_params("arbitrary", "arbitrary"),
        name="mla_attn",
    )(mq, mk, mv)


def _post_attn_kernel(*refs, moe):
    if moe:
        od_ref, om_ref, x_ref, mod_ref, g_ref, wo_d_ref, wo_m_ref, wr_ref, x1_ref, h_ref, gates_ref, sel_ref = refs
    else:
        od_ref, om_ref, x_ref, mod_ref, g_ref, wo_d_ref, wo_m_ref, x1_ref, h_ref = refs
    y = (jnp.dot(od_ref[0], wo_d_ref[...], preferred_element_type=F32)
         + jnp.dot(om_ref[0], wo_m_ref[...], preferred_element_type=F32))
    gt_a, sh_f, sc_f = mod_ref[0, 2:3, :], mod_ref[0, 3:4, :], mod_ref[0, 4:5, :]
    x1 = x_ref[0] + (1.0 + gt_a) * y
    x1_ref[0] = x1
    h = _rms(x1, g_ref[...]) * (1.0 + sc_f) + sh_f
    h_ref[0] = h.astype(h_ref.dtype)
    if not moe:
        return
    wr = wr_ref[...]
    h_hi, wr_hi = h.astype(BF16), wr.astype(BF16)
    h_lo, wr_lo = (h - h_hi.astype(F32)).astype(BF16), (wr - wr_hi.astype(F32)).astype(BF16)
    logits = (jnp.dot(h_hi, wr_hi, preferred_element_type=F32) + jnp.dot(h_hi, wr_lo, preferred_element_type=F32)
              + jnp.dot(h_lo, wr_hi, preferred_element_type=F32))
    lane = lax.broadcasted_iota(jnp.int32, logits.shape, 1)
    logits = jnp.where(lane < N_EXPERTS, logits, -jnp.inf)
    v1 = jnp.max(logits, axis=1, keepdims=True)
    i1 = jnp.min(jnp.where(logits == v1, lane, LANES), axis=1, keepdims=True)
    rest = jnp.where(lane == i1, -jnp.inf, logits)
    v2 = jnp.max(rest, axis=1, keepdims=True)
    i2 = jnp.min(jnp.where(rest == v2, lane, LANES), axis=1, keepdims=True)
    e2 = jnp.exp(v2 - v1)
    w1 = 1.0 / (1.0 + e2)
    w2 = e2 / (1.0 + e2)
    gates_ref[0] = jnp.where(lane == i1, w1, 0.0) + jnp.where(lane == i2, w2, 0.0)
    sel_ref[0] = jnp.where(lane == i1, 1.0, 0.0) + jnp.where(lane == i2, 2.0, 0.0)


def _post_attn(o_diff, o_mla, x, mod, g, wo_d, wo_m, w_router=None):
    bsz, s, d = x.shape
    moe = w_router is not None
    tm = TOK_TILE
    tok = lambda w: pl.BlockSpec((1, tm, w), lambda b, i: (b, i, 0))
    full = lambda a: pl.BlockSpec(a.shape, lambda b, i: (0,) * a.ndim)
    args = [o_diff, o_mla, x, mod, g, wo_d, wo_m] + ([w_router] if moe else [])
    in_specs = [tok(o_diff.shape[-1]), tok(o_mla.shape[-1]), tok(d),
                pl.BlockSpec((1, ADA_CHUNKS, d), lambda b, i: (b, 0, 0)),
                full(g), full(wo_d), full(wo_m)] + ([full(w_router)] if moe else [])
    out_specs = [tok(d), tok(d)] + ([tok(LANES), tok(LANES)] if moe else [])
    out_shape = [jax.ShapeDtypeStruct((bsz, s, d), F32), jax.ShapeDtypeStruct((bsz, s, d), F32 if moe else BF16)]
    if moe:
        out_shape += [jax.ShapeDtypeStruct((bsz, s, LANES), F32)] * 2
    return pl.pallas_call(
        functools.partial(_post_attn_kernel, moe=moe),
        grid=(bsz, s // tm),
        in_specs=in_specs,
        out_specs=out_specs,
        out_shape=out_shape,
        compiler_params=_params("arbitrary", "arbitrary"),
        name="post_attn_moe" if moe else "post_attn",
    )(*args)


def _swiglu(h, w1_ref, w3_ref, w2_ref):
    d_ff = w1_ref.shape[-1]
    y = jnp.zeros((h.shape[0], w2_ref.shape[-1]), F32)
    for c0 in range(0, d_ff, FF_CHUNK):
        a = jnp.dot(h, w1_ref[:, c0:c0 + FF_CHUNK], preferred_element_type=F32)
        b = jnp.dot(h, w3_ref[:, c0:c0 + FF_CHUNK], preferred_element_type=F32)
        u = (a * jax.nn.sigmoid(a) * b).astype(BF16)
        y = y + jnp.dot(u, w2_ref[c0:c0 + FF_CHUNK, :], preferred_element_type=F32)
    return y


def _ffn_dense_kernel(h_ref, x1_ref, mod_ref, w1_ref, w3_ref, w2_ref, o_ref):
    gt_f = mod_ref[0, 5:6, :]
    o_ref[0] = x1_ref[0] + (1.0 + gt_f) * _swiglu(h_ref[0], w1_ref, w3_ref, w2_ref)


def _ffn_dense(h, x1, mod, w1, w3, w2):
    bsz, s, d = x1.shape
    tm = FFN_TILE
    tok = lambda w: pl.BlockSpec((1, tm, w), lambda b, i: (b, i, 0))
    full = lambda a: pl.BlockSpec(a.shape, lambda b, i: (0,) * a.ndim)
    return pl.pallas_call(
        _ffn_dense_kernel,
        grid=(bsz, s // tm),
        in_specs=[tok(d), tok(d), pl.BlockSpec((1, ADA_CHUNKS, d), lambda b, i: (b, 0, 0)),
                  full(w1), full(w3), full(w2)],
        out_specs=tok(d),
        out_shape=jax.ShapeDtypeStruct(x1.shape, F32),
        compiler_params=_params("arbitrary", "arbitrary"),
        name="ffn_dense",
    )(h, x1, mod, w1, w3, w2)


def _route_kernel(sel_ref, pos_ref, te_ref, cnt_ref, off_ref, run_ref, *, row_tile):
    p, t = pl.program_id(0), pl.program_id(1)
    sel_t = sel_ref[...].T
    chosen = (sel_t > 0.0).astype(F32)
    per_expert = jnp.sum(chosen, axis=1, keepdims=True)

    @pl.when((p == 0) & (t == 0))
    def _():
        cnt_ref[...] = jnp.zeros(cnt_ref.shape, F32)
        te_ref[...] = jnp.zeros(te_ref.shape, jnp.int32)

    @pl.when(p == 0)
    def _():
        cnt_ref[...] += per_expert

    @pl.when((p == 1) & (t == 0))
    def _():
        cnt = cnt_ref[...]
        padded = jnp.ceil(cnt / row_tile) * row_tile
        row = lax.broadcasted_iota(jnp.int32, cnt.shape, 0)
        off = jnp.zeros(cnt.shape, F32)
        for e in range(N_EXPERTS):
            size_e = jnp.sum(jnp.where(row == e, padded, 0.0), keepdims=True)
            off = off + jnp.where(row > e, size_e, 0.0)
        off_ref[...] = off
        run_ref[...] = jnp.zeros(run_ref.shape, F32)
        ends = off + padded
        tile_start = lax.broadcasted_iota(jnp.int32, (LANES, LANES), 1).astype(F32) * row_tile
        erow = lax.broadcasted_iota(jnp.int32, (LANES, LANES), 0)
        done = jnp.where((erow < N_EXPERTS) & (ends <= tile_start), 1.0, 0.0)
        te = jnp.sum(done, axis=0, keepdims=True).astype(jnp.int32)
        te_ref[...] = jnp.broadcast_to(te, te_ref.shape)

    @pl.when(p == 1)
    def _():
        tm = sel_t.shape[1]
        before = (lax.broadcasted_iota(jnp.int32, (tm, tm), 0)
                  < lax.broadcasted_iota(jnp.int32, (tm, tm), 1)).astype(BF16)
        rank = jnp.dot(chosen.astype(BF16), before, preferred_element_type=F32) + run_ref[...]
        base = off_ref[...] + rank
        for k in range(2):
            pos = jnp.sum(jnp.where(sel_t == float(k + 1), base, 0.0), axis=0, keepdims=True)
            pos_ref[0, k:k + 1, :] = pos.astype(jnp.int32)
        run_ref[...] += per_expert


def _route(sel, row_tile, n_row_tiles):
    n_tok = sel.shape[0]
    assert n_row_tiles <= LANES and 2 * n_tok < 2 ** 24
    tm = TOK_TILE
    nt = n_tok // tm
    pos, te = pl.pallas_call(
        functools.partial(_route_kernel, row_tile=row_tile),
        grid=(2, nt),
        in_specs=[pl.BlockSpec((tm, LANES), lambda p, t: (t, 0))],
        out_specs=[pl.BlockSpec((1, 2, tm), lambda p, t: (p * t, 0, 0)),
                   pl.BlockSpec((8, LANES), lambda p, t: (0, 0))],
        out_shape=[jax.ShapeDtypeStruct((nt, 2, tm), jnp.int32), jax.ShapeDtypeStruct((8, LANES), jnp.int32)],
        scratch_shapes=[pltpu.VMEM((LANES, 1), F32)] * 3,
        compiler_params=_params("arbitrary", "arbitrary"),
        name="moe_route",
    )(sel)
    return pos, te[0]


def _dispatch_kernel(pos_ref, h_ref, xs_in_ref, xs_ref, sem):
    del xs_in_ref
    tm = h_ref.shape[0]

    def body(r, carry):
        for k in range(2):
            pltpu.make_async_copy(h_ref.at[pl.ds(r, 1)], xs_ref.at[pl.ds(pos_ref[0, k, r], 1)], sem).start(priority=k)
        return carry

    lax.fori_loop(0, tm, body, 0, unroll=8)
    for k in range(2):
        pltpu.make_async_copy(h_ref, xs_ref.at[pl.ds(0, tm)], sem).wait()


def _dispatch(pos, h, xs0):
    n_tok, d = h.shape
    nt, _, tm = pos.shape
    return pl.pallas_call(
        _dispatch_kernel,
        grid=(nt,),
        in_specs=[pl.BlockSpec((1, 2, tm), lambda t: (t, 0, 0), memory_space=pltpu.SMEM),
                  pl.BlockSpec((tm, d), lambda t: (t, 0)),
                  pl.BlockSpec(memory_space=pl.ANY)],
        out_specs=pl.BlockSpec(memory_space=pl.ANY),
        out_shape=jax.ShapeDtypeStruct(xs0.shape, xs0.dtype),
        scratch_shapes=[pltpu.SemaphoreType.DMA(())],
        input_output_aliases={2: 0},
        compiler_params=_params("arbitrary"),
        name="moe_dispatch",
    )(pos, h, xs0)


def _expert_kernel(te_ref, xs_ref, w1_ref, w3_ref, w2_ref, y_ref):
    used = te_ref[pl.program_id(0)] < N_EXPERTS

    @pl.when(used)
    def _():
        y_ref[...] = _swiglu(xs_ref[...].astype(BF16), w1_ref.at[0], w3_ref.at[0], w2_ref.at[0])

    @pl.when(jnp.logical_not(used))
    def _():
        y_ref[...] = jnp.zeros(y_ref.shape, y_ref.dtype)


def _experts(te, xs, w1, w3, w2, row_tile, first_expert):
    n_rows, d = xs.shape
    d_ff = w1.shape[-1]
    expert = lambda n, te: (first_expert + jnp.minimum(te[n], N_EXPERTS - 1), 0, 0)
    return pl.pallas_call(
        _expert_kernel,
        grid_spec=pltpu.PrefetchScalarGridSpec(
            num_scalar_prefetch=1,
            grid=(n_rows // row_tile,),
            in_specs=[pl.BlockSpec((row_tile, d), lambda n, te: (n, 0)),
                      pl.BlockSpec((1, d, d_ff), expert), pl.BlockSpec((1, d, d_ff), expert),
                      pl.BlockSpec((1, d_ff, d), expert)],
            out_specs=pl.BlockSpec((row_tile, d), lambda n, te: (n, 0))),
        out_shape=jax.ShapeDtypeStruct(xs.shape, F32),
        compiler_params=_params("arbitrary"),
        name="moe_experts",
    )(te, xs, w1, w3, w2)


def _combine_kernel(pos_ref, x1_ref, gates_ref, sel_ref, mod_ref, gfin_ref, ys_ref, o_ref, ya_ref, yb_ref, sem,
                    *, final):
    tm = x1_ref.shape[1]
    bufs = (ya_ref, yb_ref)

    def body(r, carry):
        for k in range(2):
            pltpu.make_async_copy(ys_ref.at[pl.ds(pos_ref[0, k, r], 1)], bufs[k].at[pl.ds(r, 1)], sem).start(priority=k)
        return carry

    lax.fori_loop(0, tm, body, 0, unroll=8)
    gates, sel = gates_ref[0], sel_ref[0]
    w_a = jnp.sum(jnp.where(sel == 1.0, gates, 0.0), axis=1, keepdims=True)
    w_b = jnp.sum(jnp.where(sel == 2.0, gates, 0.0), axis=1, keepdims=True)
    for k in range(2):
        pltpu.make_async_copy(ys_ref.at[pl.ds(0, tm)], bufs[k], sem).wait()
    gt_f = mod_ref[0, 5:6, :]
    x = x1_ref[0] + (1.0 + gt_f) * (w_a * ya_ref[...] + w_b * yb_ref[...])
    o_ref[0] = _rms(x, gfin_ref[...]) if final else x


def _combine(pos, x1, gates, sel, mod, ys, g_final=None):
    bsz, s, d = x1.shape
    nt, _, tm = pos.shape
    per_b = s // tm
    tok = lambda w: pl.BlockSpec((1, tm, w), lambda b, i: (b, i, 0))
    final = g_final is not None
    gfin = g_final if final else jnp.ones((1, d), F32)
    return pl.pallas_call(
        functools.partial(_combine_kernel, final=final),
        grid=(bsz, per_b),
        in_specs=[pl.BlockSpec((1, 2, tm), lambda b, i: (b * per_b + i, 0, 0), memory_space=pltpu.SMEM),
                  tok(d), tok(LANES), tok(LANES),
                  pl.BlockSpec((1, ADA_CHUNKS, d), lambda b, i: (b, 0, 0)),
                  pl.BlockSpec((1, d), lambda b, i: (0, 0)),
                  pl.BlockSpec(memory_space=pl.ANY)],
        out_specs=tok(d),
        out_shape=jax.ShapeDtypeStruct(x1.shape, F32),
        scratch_shapes=[pltpu.VMEM((tm, d), F32), pltpu.VMEM((tm, d), F32), pltpu.SemaphoreType.DMA(())],
        compiler_params=_params("arbitrary", "arbitrary"),
        name="moe_combine",
    )(pos, x1, gates, sel, mod, gfin, ys)


def _moe(h, x1, gates, sel, mod, w1, w3, w2, first_expert, xs_buf, g_final):
    bsz, s, d = x1.shape
    n_tok = bsz * s
    row_tile = FFN_TILE
    n_rows = xs_buf.shape[0]
    assert n_rows == _moe_rows(n_tok)
    pos, te = _route(sel.reshape(n_tok, LANES), row_tile, n_rows // row_tile)
    xs = _dispatch(pos, h.reshape(n_tok, d), xs_buf)
    ys = _experts(te, xs, w1, w3, w2, row_tile, first_expert)
    return _combine(pos, x1, gates, sel, mod, ys, g_final), xs


def _moe_rows(n_tok):
    return 2 * n_tok + N_EXPERTS * FFN_TILE


def _final_norm_kernel(x_ref, g_ref, o_ref):
    o_ref[0] = _rms(x_ref[0], g_ref[...])


def _final_norm(x, g):
    bsz, s, d = x.shape
    tm = TOK_TILE
    tok = pl.BlockSpec((1, tm, d), lambda b, i: (b, i, 0))
    return pl.pallas_call(
        _final_norm_kernel,
        grid=(bsz, s // tm),
        in_specs=[tok, pl.BlockSpec(g.shape, lambda b, i: (0, 0))],
        out_specs=tok,
        out_shape=jax.ShapeDtypeStruct(x.shape, F32),
        compiler_params=_params("arbitrary", "arbitrary"),
        name="final_norm",
    )(x, g)


def _pad_heads(w, heads, width, lo, hi):
    k = w.shape[0]
    w = w.reshape(k, heads, width)[:, :, lo:hi]
    return jnp.pad(w, ((0, 0), (0, 0), (0, LANES - (hi - lo)))).reshape(k, heads * LANES)


def _prep_w_in(w_in):
    d = w_in.shape[0]
    kr = w_in[:, C_KR:]
    kr_block = jnp.concatenate([jnp.zeros((d, MLA_NOPE_DIM), w_in.dtype), kr,
                                jnp.zeros((d, LANES - MLA_NOPE_DIM - MLA_ROPE_DIM), w_in.dtype)], axis=1)
    return jnp.concatenate([w_in[:, :C_KR], kr_block], axis=1).astype(BF16)


def kernel(x, c, positions, w_ada, b_ada, g_attn, w_in, diff_lambda, diff_subln_g, rel_bias, mla_q_norm, w_uq, mla_kv_norm, w_ukv, w_o, g_ffn, ffn_w1, ffn_w3, ffn_w2, moe_router, moe_w1, moe_w3, moe_w2, g_final):
    depth = w_ada.shape[0]
    bsz, s, d = x.shape
    mods = _ada(c, w_ada, b_ada).reshape(depth, bsz, ADA_CHUNKS, d)
    rc, rsa, rsb = _rope_tables(positions)
    bias = _bias_tiles(rel_bias)
    qk_w = MLA_NOPE_DIM + MLA_ROPE_DIM
    kv_w = MLA_NOPE_DIM + MLA_V_DIM
    d_ff = moe_w1.shape[-1]
    moe_w1_b = moe_w1.astype(BF16).reshape(-1, d, d_ff)
    moe_w3_b = moe_w3.astype(BF16).reshape(-1, d, d_ff)
    moe_w2_b = moe_w2.astype(BF16).reshape(-1, d_ff, d)
    xs_buf = jnp.zeros((_moe_rows(bsz * s), d), F32)
    for l in range(depth):
        mod = mods[l]
        lam_init = jnp.full((1,), 0.8 - 0.6 * math.exp(-0.3 * l), F32)
        w_uq_p = _pad_heads(w_uq[l], MLA_HEADS, qk_w, 0, qk_w).astype(BF16)
        w_uk_p = _pad_heads(w_ukv[l], MLA_HEADS, kv_w, 0, MLA_NOPE_DIM).astype(BF16)
        w_uv_p = w_ukv[l].reshape(MLA_KV_RANK, MLA_HEADS, kv_w)[:, :, MLA_NOPE_DIM:].reshape(
            MLA_KV_RANK, MLA_HEADS * MLA_V_DIM).astype(BF16)
        dq, dk, dv, mq, mk, mv = _pre_attn(
            x, mod, g_attn[l].reshape(1, d), _prep_w_in(w_in[l]), mla_q_norm[l].reshape(1, -1), w_uq_p,
            mla_kv_norm[l].reshape(1, -1), w_uk_p, w_uv_p, rc, rsa, rsb)
        o_diff = _diff_attn(lam_init, diff_lambda[l], diff_subln_g[l].reshape(-1, 1), bias, dq, dk, dv)
        o_mla = _mla_attn(mq, mk, mv)
        wo = w_o[l].astype(BF16)
        n_diff = DIFF_HEADS * DIFF_V_DIM
        g_f = g_ffn[l].reshape(1, d)
        if l % 2 == 1:
            w_router = jnp.pad(moe_router[l // 2], ((0, 0), (0, LANES - N_EXPERTS)))
            x1, h, gates, sel = _post_attn(o_diff, o_mla, x, mod, g_f, wo[:n_diff], wo[n_diff:], w_router)
            g_fin = g_final.reshape(1, d) if l == depth - 1 else None
            x, xs_buf = _moe(h, x1, gates, sel, mod, moe_w1_b, moe_w3_b, moe_w2_b, (l // 2) * N_EXPERTS,
                             xs_buf, g_fin)
        else:
            x1, h = _post_attn(o_diff, o_mla, x, mod, g_f, wo[:n_diff], wo[n_diff:])
            x = _ffn_dense(h, x1, mod, ffn_w1[l // 2].astype(BF16), ffn_w3[l // 2].astype(BF16),
                           ffn_w2[l // 2].astype(BF16))
    return x if depth % 2 == 0 else _final_norm(x, g_final.reshape(1, d))
```

```python
import functools
import math

import jax
import jax.numpy as jnp
from jax import lax
from jax.experimental import pallas as pl
from jax.experimental.pallas import tpu as pltpu

F32 = jnp.float32
BF16 = jnp.bfloat16

DIFF_HEADS = 4
DIFF_QK_DIM = 64
DIFF_V_DIM = 128
MLA_HEADS = 8
MLA_NOPE_DIM = 64
MLA_ROPE_DIM = 32
MLA_V_DIM = 64
MLA_Q_RANK = 384
MLA_KV_RANK = 256
ROPE_THETA = 10000.0
N_BUCKETS = 32
MAX_EXACT = 16
MAX_DISTANCE = 128
N_EXPERTS = 8
NORM_EPS = 1e-6
ADA_CHUNKS = 6

LANES = 128
NEG_BIG = -1e30
LOG2E = math.log2(math.e)
VMEM_LIMIT = 56 * 1024 * 1024

ATTN_TILE = 256
AHEAD = 4
TOK_TILE = 512
FFN_TILE = 512
FF_CHUNK = 256

C_DQ, C_DK, C_DV, C_MQ, C_KV, C_KR, C_END = 0, 512, 1024, 1536, 1920, 2176, 2304


def _params(*sem):
    return pltpu.CompilerParams(dimension_semantics=sem, vmem_limit_bytes=VMEM_LIMIT)


def _rms(x, g):
    return x * lax.rsqrt(jnp.mean(x * x, axis=-1, keepdims=True) + NORM_EPS) * g


def _dot_nt(a, b):
    return lax.dot_general(a, b, (((1,), (1,)), ((), ())), preferred_element_type=F32)


ROW_SUB = 8


def _rows_to_tiles(x, ref):
    n = x.shape[0]
    for j in range(ROW_SUB):
        ref[pl.ds(j, n, stride=ROW_SUB), :] = x[:, j * LANES:(j + 1) * LANES]


def _tiles_to_rows(ref):
    n = ref.shape[0] // ROW_SUB
    return jnp.concatenate([ref[pl.ds(j, n, stride=ROW_SUB), :] for j in range(ROW_SUB)], axis=1)


def _tile_row(ref, r):
    return ref.at[pl.ds(pl.multiple_of(r * ROW_SUB, ROW_SUB), ROW_SUB)]


def _ada_kernel(c_ref, w_ref, b_ref, o_ref):
    c = c_ref[...]
    cond = c * jax.nn.sigmoid(c)
    o_ref[0] = jnp.dot(cond, w_ref[0], preferred_element_type=F32,
                       precision=lax.Precision.HIGHEST) + b_ref[0]


def _ada(c, w_ada, b_ada):
    depth, d, n = w_ada.shape
    bsz = c.shape[0]
    tn = 1536
    return pl.pallas_call(
        _ada_kernel,
        grid=(depth, n // tn),
        in_specs=[pl.BlockSpec((bsz, d), lambda l, j: (0, 0)),
                  pl.BlockSpec((1, d, tn), lambda l, j: (l, 0, j)),
                  pl.BlockSpec((1, 1, tn), lambda l, j: (l, 0, j))],
        out_specs=pl.BlockSpec((1, bsz, tn), lambda l, j: (l, 0, j)),
        out_shape=jax.ShapeDtypeStruct((depth, bsz, n), F32),
        compiler_params=_params("arbitrary", "arbitrary"),
        name="ada_mod",
    )(c, w_ada, b_ada.reshape(depth, 1, n))


def _rope_tab_kernel(pos_ref, inv_ref, c_ref, sa_ref, sb_ref):
    pos = pos_ref[0].astype(F32)
    ang = pos * inv_ref[...]
    lane = lax.broadcasted_iota(jnp.int32, ang.shape, 1)
    cos, sin = jnp.cos(ang), jnp.sin(ang)
    lo = (lane >= MLA_NOPE_DIM) & (lane < MLA_NOPE_DIM + MLA_ROPE_DIM // 2)
    hi = (lane >= MLA_NOPE_DIM + MLA_ROPE_DIM // 2) & (lane < MLA_NOPE_DIM + MLA_ROPE_DIM)
    c_ref[0] = jnp.where(lane < MLA_NOPE_DIM, 1.0, jnp.where(lo | hi, cos, 0.0))
    sa_ref[0] = jnp.where(lo, -sin, 0.0)
    sb_ref[0] = jnp.where(hi, sin, 0.0)


def _rope_tables(positions):
    bsz, s = positions.shape
    half = MLA_ROPE_DIM // 2
    inv_freq = ROPE_THETA ** (-jnp.arange(half, dtype=F32) / half)
    inv_lane = jnp.concatenate([jnp.zeros((MLA_NOPE_DIM,), F32), inv_freq, inv_freq,
                                jnp.zeros((LANES - MLA_NOPE_DIM - MLA_ROPE_DIM,), F32)]).reshape(1, LANES)
    tm = TOK_TILE
    spec = pl.BlockSpec((1, tm, LANES), lambda b, i: (b, i, 0))
    shape = jax.ShapeDtypeStruct((bsz, s, LANES), F32)
    return pl.pallas_call(
        _rope_tab_kernel,
        grid=(bsz, s // tm),
        in_specs=[pl.BlockSpec((1, tm, 1), lambda b, i: (b, i, 0)),
                  pl.BlockSpec((1, LANES), lambda b, i: (0, 0))],
        out_specs=[spec, spec, spec],
        out_shape=[shape, shape, shape],
        compiler_params=_params("arbitrary", "arbitrary"),
        name="rope_tables",
    )(positions.reshape(bsz, s, 1), inv_lane)


def _bias_tile_kernel(rb_ref, o_ref):
    h, d = pl.program_id(0), pl.program_id(1)
    t = o_ref.shape[-1]
    key = lax.broadcasted_iota(jnp.int32, (t, t), 0)
    qry = lax.broadcasted_iota(jnp.int32, (t, t), 1)
    dist = d * t + qry - key
    n = jnp.maximum(dist, 0)
    nf = jnp.maximum(n, 1).astype(F32)
    large = MAX_EXACT + (jnp.log(nf / MAX_EXACT) / math.log(MAX_DISTANCE / MAX_EXACT)
                         * (N_BUCKETS - MAX_EXACT)).astype(jnp.int32)
    large = jnp.minimum(large, N_BUCKETS - 1)
    bucket = jnp.where(n < MAX_EXACT, n, large)
    val = jnp.zeros((t, t), F32)
    for j in range(N_BUCKETS):
        val = jnp.where(bucket == j, rb_ref[j, h], val)
    val = (rb_ref[N_BUCKETS - 1, h] - val) * LOG2E
    o_ref[0, 0] = jnp.where(dist < 0, -NEG_BIG, val)


def _bias_tiles(rel_bias):
    t = ATTN_TILE
    return pl.pallas_call(
        _bias_tile_kernel,
        grid=(DIFF_HEADS, 2),
        in_specs=[pl.BlockSpec(memory_space=pltpu.SMEM)],
        out_specs=pl.BlockSpec((1, 1, t, t), lambda h, d: (h, d, 0, 0)),
        out_shape=jax.ShapeDtypeStruct((DIFF_HEADS, 2, t, t), F32),
        compiler_params=_params("arbitrary", "arbitrary"),
        name="bias_tiles",
    )(rel_bias)


def _pre_attn_kernel(x_ref, mod_ref, g_ref, w_in_ref, gq_ref, w_uq_ref, gkv_ref, w_uk_ref, w_uv_ref,
                     rc_ref, rsa_ref, rsb_ref,
                     dq_ref, dk_ref, dv_ref, mq_ref, mk_ref, mv_ref):
    x = x_ref[0]
    sh, sc = mod_ref[0, 0:1, :], mod_ref[0, 1:2, :]
    h = (_rms(x, g_ref[...]) * (1.0 + sc) + sh).astype(BF16)
    proj = jnp.dot(h, w_in_ref[...], preferred_element_type=F32)
    dq_ref[0] = (proj[:, C_DQ:C_DK] * (DIFF_QK_DIM ** -0.5 * LOG2E)).astype(BF16)
    dk_ref[0] = proj[:, C_DK:C_DV].astype(BF16)
    dv_ref[0] = proj[:, C_DV:C_MQ].astype(BF16)

    rc, rsa, rsb = rc_ref[0], rsa_ref[0], rsb_ref[0]

    def rope(v):
        return v * rc + pltpu.roll(v, LANES - MLA_ROPE_DIM // 2, 1) * rsa + pltpu.roll(v, MLA_ROPE_DIM // 2, 1) * rsb

    qn = _rms(proj[:, C_MQ:C_KV], gq_ref[...]).astype(BF16)
    q = jnp.dot(qn, w_uq_ref[...], preferred_element_type=F32)
    kvn = _rms(proj[:, C_KV:C_KR], gkv_ref[...]).astype(BF16)
    kn = jnp.dot(kvn, w_uk_ref[...], preferred_element_type=F32)
    mv_ref[0] = jnp.dot(kvn, w_uv_ref[...], preferred_element_type=F32).astype(BF16)
    kr = rope(proj[:, C_KR:C_END])
    q_scale = (MLA_NOPE_DIM + MLA_ROPE_DIM) ** -0.5 * LOG2E
    for hd in range(MLA_HEADS):
        sl = slice(hd * LANES, (hd + 1) * LANES)
        mq_ref[0, :, sl] = (rope(q[:, sl]) * q_scale).astype(BF16)
        mk_ref[0, :, sl] = (kn[:, sl] + kr).astype(BF16)


def _pre_attn(x, mod, g, w_in, gq, w_uq, gkv, w_uk, w_uv, rc, rsa, rsb):
    bsz, s, d = x.shape
    tm = TOK_TILE
    tok = lambda w: pl.BlockSpec((1, tm, w), lambda b, i: (b, i, 0))
    full = lambda a: pl.BlockSpec(a.shape, lambda b, i: (0,) * a.ndim)
    widths = (512, 512, 512, MLA_HEADS * LANES, MLA_HEADS * LANES, MLA_HEADS * MLA_V_DIM)
    return pl.pallas_call(
        _pre_attn_kernel,
        grid=(bsz, s // tm),
        in_specs=[tok(d), pl.BlockSpec((1, ADA_CHUNKS, d), lambda b, i: (b, 0, 0)), full(g), full(w_in),
                  full(gq), full(w_uq), full(gkv), full(w_uk), full(w_uv), tok(LANES), tok(LANES), tok(LANES)],
        out_specs=[tok(w) for w in widths],
        out_shape=[jax.ShapeDtypeStruct((bsz, s, w), BF16) for w in widths],
        compiler_params=_params("arbitrary", "arbitrary"),
        name="pre_attn",
    )(x, mod, g, w_in, gq, w_uq, gkv, w_uk, w_uv, rc, rsa, rsb)


def _softmax_init(m_ref, l_ref, acc_ref):
    m_ref[...] = jnp.full(m_ref.shape, NEG_BIG, F32)
    l_ref[...] = jnp.zeros(l_ref.shape, F32)
    acc_ref[...] = jnp.zeros(acc_ref.shape, F32)


def _softmax_update(st, vt, m_ref, l_ref, acc_ref):
    m_prev = m_ref[...]
    m_new = jnp.maximum(m_prev, jnp.max(st, axis=0, keepdims=True))
    alpha = jnp.exp2(m_prev - m_new)
    p = jnp.exp2(st - m_new)
    l_ref[...] = alpha * l_ref[...] + jnp.sum(p, axis=0, keepdims=True)
    acc_ref[...] = alpha * acc_ref[...] + jnp.dot(vt, p.astype(BF16), preferred_element_type=F32)
    m_ref[...] = m_new


def _attn_prologue(scores, pend_ref):
    for c in range(AHEAD):
        pend_ref[c] = scores(0, c)


def _attn_step(n_chains, scores, consume, pend_ref, j, has_next):
    pending = [pend_ref[c] for c in range(AHEAD)]
    for c in range(n_chains):
        nxt = c + AHEAD
        if nxt < n_chains:
            pending.append(scores(j, nxt))
        elif has_next:
            pend_ref[nxt - n_chains] = scores(j + 1, nxt - n_chains)
        consume(c, pending.pop(0))


def _diff_attn_kernel(lam_init_ref, lam_ref, g_ref, bias_ref, q_ref, k_ref, v_ref, o_ref,
                      vt_ref, m_ref, l_ref, acc_ref, pend_ref):
    i = pl.program_id(1)
    t = q_ref.shape[1]

    @pl.when(i == 0)
    def _():
        vt_ref[...] = v_ref[0].T

    n_maps = 2 * DIFF_HEADS
    for c in range(n_maps):
        _softmax_init(m_ref.at[c], l_ref.at[c], acc_ref.at[c])
    qts = []
    for h in range(DIFF_HEADS):
        qt = q_ref[0, :, h * LANES:(h + 1) * LANES].T
        row = lax.broadcasted_iota(jnp.int32, qt.shape, 0)
        qts.append(jnp.where(row < DIFF_QK_DIM, qt, jnp.zeros_like(qt)))
        qts.append(jnp.where(row >= DIFF_QK_DIM, qt, jnp.zeros_like(qt)))

    def scores(j, c):
        k = k_ref[0, pl.ds(pl.multiple_of(j * t, t), t), (c // 2) * LANES:(c // 2 + 1) * LANES]
        return jnp.dot(k, qts[c], preferred_element_type=F32)

    def step(j, bias_idx, has_next):
        def consume(c, st):
            h = c // 2
            if bias_idx is not None:
                st = st - bias_ref[h, bias_idx]
            vt = vt_ref[h * LANES:(h + 1) * LANES, pl.ds(pl.multiple_of(j * t, t), t)]
            _softmax_update(st, vt, m_ref.at[c], l_ref.at[c], acc_ref.at[c])

        _attn_step(n_maps, scores, consume, pend_ref, j, has_next)

    def far(j, carry):
        step(j, None, True)
        return carry

    _attn_prologue(scores, pend_ref)
    lax.fori_loop(0, jnp.maximum(i - 1, 0), far, 0)

    @pl.when(i >= 1)
    def _():
        step(i - 1, 1, True)

    step(i, 0, False)

    lv = lam_ref[...]
    lam = (jnp.exp(jnp.sum(lv[0:1] * lv[1:2], keepdims=True)) - jnp.exp(jnp.sum(lv[2:3] * lv[3:4], keepdims=True))
           + lam_init_ref[0])
    for h in range(DIFF_HEADS):
        c0, c1 = 2 * h, 2 * h + 1
        ot = acc_ref[c0] / l_ref[c0] - lam * (acc_ref[c1] / l_ref[c1])
        ot = ot * lax.rsqrt(jnp.mean(ot * ot, axis=0, keepdims=True) + NORM_EPS) * g_ref[...]
        o_ref[0, :, h * LANES:(h + 1) * LANES] = (ot * (1.0 - lam_init_ref[0])).T.astype(BF16)


def _diff_attn(lam_init, diff_lambda, g, bias, dq, dk, dv):
    bsz, s, w = dq.shape
    t = ATTN_TILE
    assert MAX_DISTANCE <= t and s % t == 0
    n_maps = 2 * DIFF_HEADS
    return pl.pallas_call(
        _diff_attn_kernel,
        grid=(bsz, s // t),
        in_specs=[pl.BlockSpec(memory_space=pltpu.SMEM),
                  pl.BlockSpec(diff_lambda.shape, lambda b, i: (0, 0)),
                  pl.BlockSpec(g.shape, lambda b, i: (0, 0)),
                  pl.BlockSpec(bias.shape, lambda b, i: (0, 0, 0, 0)),
                  pl.BlockSpec((1, t, w), lambda b, i: (b, i, 0)),
                  pl.BlockSpec((1, s, w), lambda b, i: (b, 0, 0)),
                  pl.BlockSpec((1, s, w), lambda b, i: (b, 0, 0))],
        out_specs=pl.BlockSpec((1, t, w), lambda b, i: (b, i, 0)),
        out_shape=jax.ShapeDtypeStruct(dq.shape, BF16),
        scratch_shapes=[pltpu.VMEM((w, s), BF16), pltpu.VMEM((n_maps, 1, t), F32),
                        pltpu.VMEM((n_maps, 1, t), F32), pltpu.VMEM((n_maps, LANES, t), F32),
                        pltpu.VMEM((AHEAD, t, t), F32)],
        compiler_params=_params("arbitrary", "arbitrary"),
        name="diff_attn",
    )(lam_init, diff_lambda, g, bias, dq, dk, dv)


def _mla_attn_kernel(q_ref, k_ref, v_ref, o_ref, vt_ref, m_ref, l_ref, acc_ref, pend_ref):
    i = pl.program_id(1)
    t = q_ref.shape[1]

    @pl.when(i == 0)
    def _():
        vt_ref[...] = v_ref[0].T

    for h in range(MLA_HEADS):
        _softmax_init(m_ref.at[h], l_ref.at[h], acc_ref.at[h])
    qts = [q_ref[0, :, h * LANES:(h + 1) * LANES].T for h in range(MLA_HEADS)]

    def scores(j, h):
        k = k_ref[0, pl.ds(pl.multiple_of(j * t, t), t), h * LANES:(h + 1) * LANES]
        return jnp.dot(k, qts[h], preferred_element_type=F32)

    def step(j, masked, has_next):
        def consume(h, st):
            vt = vt_ref[h * MLA_V_DIM:(h + 1) * MLA_V_DIM, pl.ds(pl.multiple_of(j * t, t), t)]
            if masked:
                key = lax.broadcasted_iota(jnp.int32, st.shape, 0)
                qry = lax.broadcasted_iota(jnp.int32, st.shape, 1)
                st = jnp.where(key <= qry, st, NEG_BIG)
            _softmax_update(st, vt, m_ref.at[h], l_ref.at[h], acc_ref.at[h])

        _attn_step(MLA_HEADS, scores, consume, pend_ref, j, has_next)

    def far(j, carry):
        step(j, False, True)
        return carry

    _attn_prologue(scores, pend_ref)
    lax.fori_loop(0, i, far, 0)
    step(i, True, False)
    for u in range(MLA_HEADS // 2):
        ot = jnp.concatenate([acc_ref[2 * u] / l_ref[2 * u], acc_ref[2 * u + 1] / l_ref[2 * u + 1]], axis=0)
        o_ref[0, :, u * LANES:(u + 1) * LANES] = ot.T.astype(BF16)


def _mla_attn(mq, mk, mv):
    bsz, s, wq = mq.shape
    wv = mv.shape[-1]
    t = ATTN_TILE
    return pl.pallas_call(
        _mla_attn_kernel,
        grid=(bsz, s // t),
        in_specs=[pl.BlockSpec((1, t, wq), lambda b, i: (b, i, 0)),
                  pl.BlockSpec((1, s, wq), lambda b, i: (b, 0, 0)),
                  pl.BlockSpec((1, s, wv), lambda b, i: (b, 0, 0))],
        out_specs=pl.BlockSpec((1, t, wv), lambda b, i: (b, i, 0)),
        out_shape=jax.ShapeDtypeStruct(mv.shape, BF16),
        scratch_shapes=[pltpu.VMEM((wv, s), BF16), pltpu.VMEM((MLA_HEADS, 1, t), F32),
                        pltpu.VMEM((MLA_HEADS, 1, t), F32), pltpu.VMEM((MLA_HEADS, MLA_V_DIM, t), F32),
                        pltpu.VMEM((AHEAD, t, t), F32)],
        compiler_params=_params("arbitrary", "arbitrary"),
        name="mla_attn",
    )(mq, mk, mv)


def _post_attn_kernel(*refs, moe):
    if moe:
        od_ref, om_ref, x_ref, mod_ref, g_ref, wo_d_ref, wo_m_ref, wr_ref, x1_ref, h_ref, gates_ref, sel_ref = refs
    else:
        od_ref, om_ref, x_ref, mod_ref, g_ref, wo_d_ref, wo_m_ref, x1_ref, h_ref = refs
    y = (jnp.dot(od_ref[0], wo_d_ref[...], preferred_element_type=F32)
         + jnp.dot(om_ref[0], wo_m_ref[...], preferred_element_type=F32))
    gt_a, sh_f, sc_f = mod_ref[0, 2:3, :], mod_ref[0, 3:4, :], mod_ref[0, 4:5, :]
    x1 = x_ref[0] + (1.0 + gt_a) * y
    x1_ref[0] = x1
    h = _rms(x1, g_ref[...]) * (1.0 + sc_f) + sh_f
    if not moe:
        h_ref[0] = h.astype(BF16)
        return
    _rows_to_tiles(h, h_ref.at[0])
    wr = wr_ref[...]
    h_hi, wr_hi = h.astype(BF16), wr.astype(BF16)
    h_lo, wr_lo = (h - h_hi.astype(F32)).astype(BF16), (wr - wr_hi.astype(F32)).astype(BF16)
    logits = (jnp.dot(h_hi, wr_hi, preferred_element_type=F32) + jnp.dot(h_hi, wr_lo, preferred_element_type=F32)
              + jnp.dot(h_lo, wr_hi, preferred_element_type=F32))
    lane = lax.broadcasted_iota(jnp.int32, logits.shape, 1)
    logits = jnp.where(lane < N_EXPERTS, logits, -jnp.inf)
    v1 = jnp.max(logits, axis=1, keepdims=True)
    i1 = jnp.min(jnp.where(logits == v1, lane, LANES), axis=1, keepdims=True)
    rest = jnp.where(lane == i1, -jnp.inf, logits)
    v2 = jnp.max(rest, axis=1, keepdims=True)
    i2 = jnp.min(jnp.where(rest == v2, lane, LANES), axis=1, keepdims=True)
    e2 = jnp.exp(v2 - v1)
    w1 = 1.0 / (1.0 + e2)
    w2 = e2 / (1.0 + e2)
    gates_ref[0] = jnp.where(lane == i1, w1, 0.0) + jnp.where(lane == i2, w2, 0.0)
    sel_ref[0] = jnp.where(lane == i1, 1.0, 0.0) + jnp.where(lane == i2, 2.0, 0.0)


def _post_attn(o_diff, o_mla, x, mod, g, wo_d, wo_m, w_router=None):
    bsz, s, d = x.shape
    moe = w_router is not None
    tm = TOK_TILE
    tok = lambda w: pl.BlockSpec((1, tm, w), lambda b, i: (b, i, 0))
    full = lambda a: pl.BlockSpec(a.shape, lambda b, i: (0,) * a.ndim)
    args = [o_diff, o_mla, x, mod, g, wo_d, wo_m] + ([w_router] if moe else [])
    in_specs = [tok(o_diff.shape[-1]), tok(o_mla.shape[-1]), tok(d),
                pl.BlockSpec((1, ADA_CHUNKS, d), lambda b, i: (b, 0, 0)),
                full(g), full(wo_d), full(wo_m)] + ([full(w_router)] if moe else [])
    if moe:
        assert d == ROW_SUB * LANES
        h_spec = pl.BlockSpec((1, tm * ROW_SUB, LANES), lambda b, i: (b, i, 0))
        out_specs = [tok(d), h_spec, tok(LANES), tok(LANES)]
        out_shape = [jax.ShapeDtypeStruct((bsz, s, d), F32), jax.ShapeDtypeStruct((bsz, s * ROW_SUB, LANES), F32)]
        out_shape += [jax.ShapeDtypeStruct((bsz, s, LANES), F32)] * 2
    else:
        out_specs = [tok(d), tok(d)]
        out_shape = [jax.ShapeDtypeStruct((bsz, s, d), F32), jax.ShapeDtypeStruct((bsz, s, d), BF16)]
    return pl.pallas_call(
        functools.partial(_post_attn_kernel, moe=moe),
        grid=(bsz, s // tm),
        in_specs=in_specs,
        out_specs=out_specs,
        out_shape=out_shape,
        compiler_params=_params("arbitrary", "arbitrary"),
        name="post_attn_moe" if moe else "post_attn",
    )(*args)


def _swiglu(h, w1_ref, w3_ref, w2_ref):
    d_ff = w1_ref.shape[-1]
    y = jnp.zeros((h.shape[0], w2_ref.shape[-1]), F32)
    for c0 in range(0, d_ff, FF_CHUNK):
        a = jnp.dot(h, w1_ref[:, c0:c0 + FF_CHUNK], preferred_element_type=F32)
        b = jnp.dot(h, w3_ref[:, c0:c0 + FF_CHUNK], preferred_element_type=F32)
        u = (a * jax.nn.sigmoid(a) * b).astype(BF16)
        y = y + jnp.dot(u, w2_ref[c0:c0 + FF_CHUNK, :], preferred_element_type=F32)
    return y


def _ffn_dense_kernel(h_ref, x1_ref, mod_ref, w1_ref, w3_ref, w2_ref, o_ref):
    gt_f = mod_ref[0, 5:6, :]
    o_ref[0] = x1_ref[0] + (1.0 + gt_f) * _swiglu(h_ref[0], w1_ref, w3_ref, w2_ref)


def _ffn_dense(h, x1, mod, w1, w3, w2):
    bsz, s, d = x1.shape
    tm = FFN_TILE
    tok = lambda w: pl.BlockSpec((1, tm, w), lambda b, i: (b, i, 0))
    full = lambda a: pl.BlockSpec(a.shape, lambda b, i: (0,) * a.ndim)
    return pl.pallas_call(
        _ffn_dense_kernel,
        grid=(bsz, s // tm),
        in_specs=[tok(d), tok(d), pl.BlockSpec((1, ADA_CHUNKS, d), lambda b, i: (b, 0, 0)),
                  full(w1), full(w3), full(w2)],
        out_specs=tok(d),
        out_shape=jax.ShapeDtypeStruct(x1.shape, F32),
        compiler_params=_params("arbitrary", "arbitrary"),
        name="ffn_dense",
    )(h, x1, mod, w1, w3, w2)


def _route_kernel(sel_ref, pos_ref, te_ref, cnt_ref, off_ref, run_ref, *, row_tile):
    p, t = pl.program_id(0), pl.program_id(1)
    sel_t = sel_ref[...].T
    chosen = (sel_t > 0.0).astype(F32)
    per_expert = jnp.sum(chosen, axis=1, keepdims=True)

    @pl.when((p == 0) & (t == 0))
    def _():
        cnt_ref[...] = jnp.zeros(cnt_ref.shape, F32)
        te_ref[...] = jnp.zeros(te_ref.shape, jnp.int32)

    @pl.when(p == 0)
    def _():
        cnt_ref[...] += per_expert

    @pl.when((p == 1) & (t == 0))
    def _():
        cnt = cnt_ref[...]
        padded = jnp.ceil(cnt / row_tile) * row_tile
        row = lax.broadcasted_iota(jnp.int32, cnt.shape, 0)
        off = jnp.zeros(cnt.shape, F32)
        for e in range(N_EXPERTS):
            size_e = jnp.sum(jnp.where(row == e, padded, 0.0), keepdims=True)
            off = off + jnp.where(row > e, size_e, 0.0)
        off_ref[...] = off
        run_ref[...] = jnp.zeros(run_ref.shape, F32)
        ends = off + padded
        tile_start = lax.broadcasted_iota(jnp.int32, (LANES, LANES), 1).astype(F32) * row_tile
        erow = lax.broadcasted_iota(jnp.int32, (LANES, LANES), 0)
        done = jnp.where((erow < N_EXPERTS) & (ends <= tile_start), 1.0, 0.0)
        te = jnp.sum(done, axis=0, keepdims=True).astype(jnp.int32)
        te_ref[...] = jnp.broadcast_to(te, te_ref.shape)

    @pl.when(p == 1)
    def _():
        tm = sel_t.shape[1]
        before = (lax.broadcasted_iota(jnp.int32, (tm, tm), 0)
                  < lax.broadcasted_iota(jnp.int32, (tm, tm), 1)).astype(BF16)
        rank = jnp.dot(chosen.astype(BF16), before, preferred_element_type=F32) + run_ref[...]
        base = off_ref[...] + rank
        for k in range(2):
            pos = jnp.sum(jnp.where(sel_t == float(k + 1), base, 0.0), axis=0, keepdims=True)
            pos_ref[0, k:k + 1, :] = pos.astype(jnp.int32)
        run_ref[...] += per_expert


def _route(sel, row_tile, n_row_tiles):
    n_tok = sel.shape[0]
    assert n_row_tiles <= LANES and 2 * n_tok < 2 ** 24
    tm = TOK_TILE
    nt = n_tok // tm
    pos, te = pl.pallas_call(
        functools.partial(_route_kernel, row_tile=row_tile),
        grid=(2, nt),
        in_specs=[pl.BlockSpec((tm, LANES), lambda p, t: (t, 0))],
        out_specs=[pl.BlockSpec((1, 2, tm), lambda p, t: (p * t, 0, 0)),
                   pl.BlockSpec((8, LANES), lambda p, t: (0, 0))],
        out_shape=[jax.ShapeDtypeStruct((nt, 2, tm), jnp.int32), jax.ShapeDtypeStruct((8, LANES), jnp.int32)],
        scratch_shapes=[pltpu.VMEM((LANES, 1), F32)] * 3,
        compiler_params=_params("arbitrary", "arbitrary"),
        name="moe_route",
    )(sel)
    return pos, te[0]


def _dispatch_kernel(pos_ref, h_ref, xs_in_ref, xs_ref, sem):
    del xs_in_ref
    tm = h_ref.shape[0] // ROW_SUB

    def body(r, carry):
        for k in range(2):
            pltpu.make_async_copy(_tile_row(h_ref, r), _tile_row(xs_ref, pos_ref[0, k, r]), sem).start(priority=k)
        return carry

    lax.fori_loop(0, tm, body, 0, unroll=8)
    for k in range(2):
        pltpu.make_async_copy(h_ref, xs_ref.at[pl.ds(0, tm * ROW_SUB)], sem).wait()


def _dispatch(pos, h, xs0):
    nt, _, tm = pos.shape
    return pl.pallas_call(
        _dispatch_kernel,
        grid=(nt,),
        in_specs=[pl.BlockSpec((1, 2, tm), lambda t: (t, 0, 0), memory_space=pltpu.SMEM),
                  pl.BlockSpec((tm * ROW_SUB, LANES), lambda t: (t, 0)),
                  pl.BlockSpec(memory_space=pl.ANY)],
        out_specs=pl.BlockSpec(memory_space=pl.ANY),
        out_shape=jax.ShapeDtypeStruct(xs0.shape, xs0.dtype),
        scratch_shapes=[pltpu.SemaphoreType.DMA(())],
        input_output_aliases={2: 0},
        compiler_params=_params("arbitrary"),
        name="moe_dispatch",
    )(pos, h, xs0)


def _expert_kernel(te_ref, xs_ref, w1_ref, w3_ref, w2_ref, y_ref):
    used = te_ref[pl.program_id(0)] < N_EXPERTS

    @pl.when(used)
    def _():
        y = _swiglu(_tiles_to_rows(xs_ref).astype(BF16), w1_ref.at[0], w3_ref.at[0], w2_ref.at[0])
        _rows_to_tiles(y, y_ref)

    @pl.when(jnp.logical_not(used))
    def _():
        y_ref[...] = jnp.zeros(y_ref.shape, y_ref.dtype)


def _experts(te, xs, w1, w3, w2, row_tile, first_expert):
    n_rows = xs.shape[0] // ROW_SUB
    _, d, d_ff = w1.shape
    expert = lambda n, te: (first_expert + jnp.minimum(te[n], N_EXPERTS - 1), 0, 0)
    rows = pl.BlockSpec((row_tile * ROW_SUB, LANES), lambda n, te: (n, 0))
    return pl.pallas_call(
        _expert_kernel,
        grid_spec=pltpu.PrefetchScalarGridSpec(
            num_scalar_prefetch=1,
            grid=(n_rows // row_tile,),
            in_specs=[rows, pl.BlockSpec((1, d, d_ff), expert), pl.BlockSpec((1, d, d_ff), expert),
                      pl.BlockSpec((1, d_ff, d), expert)],
            out_specs=rows),
        out_shape=jax.ShapeDtypeStruct(xs.shape, F32),
        compiler_params=_params("arbitrary"),
        name="moe_experts",
    )(te, xs, w1, w3, w2)


def _combine_kernel(pos_ref, x1_ref, gates_ref, sel_ref, mod_ref, gfin_ref, ys_ref, o_ref, ya_ref, yb_ref, sem,
                    *, final):
    tm = x1_ref.shape[1]
    bufs = (ya_ref, yb_ref)

    def body(r, carry):
        for k in range(2):
            pltpu.make_async_copy(_tile_row(ys_ref, pos_ref[0, k, r]), _tile_row(bufs[k], r), sem).start(priority=k)
        return carry

    lax.fori_loop(0, tm, body, 0, unroll=8)
    gates, sel = gates_ref[0], sel_ref[0]
    w_a = jnp.sum(jnp.where(sel == 1.0, gates, 0.0), axis=1, keepdims=True)
    w_b = jnp.sum(jnp.where(sel == 2.0, gates, 0.0), axis=1, keepdims=True)
    for k in range(2):
        pltpu.make_async_copy(ys_ref.at[pl.ds(0, tm * ROW_SUB)], bufs[k], sem).wait()
    gt_f = mod_ref[0, 5:6, :]
    x = x1_ref[0] + (1.0 + gt_f) * (w_a * _tiles_to_rows(ya_ref) + w_b * _tiles_to_rows(yb_ref))
    o_ref[0] = _rms(x, gfin_ref[...]) if final else x


def _combine(pos, x1, gates, sel, mod, ys, g_final=None):
    bsz, s, d = x1.shape
    nt, _, tm = pos.shape
    per_b = s // tm
    tok = lambda w: pl.BlockSpec((1, tm, w), lambda b, i: (b, i, 0))
    final = g_final is not None
    gfin = g_final if final else jnp.ones((1, d), F32)
    return pl.pallas_call(
        functools.partial(_combine_kernel, final=final),
        grid=(bsz, per_b),
        in_specs=[pl.BlockSpec((1, 2, tm), lambda b, i: (b * per_b + i, 0, 0), memory_space=pltpu.SMEM),
                  tok(d), tok(LANES), tok(LANES),
                  pl.BlockSpec((1, ADA_CHUNKS, d), lambda b, i: (b, 0, 0)),
                  pl.BlockSpec((1, d), lambda b, i: (0, 0)),
                  pl.BlockSpec(memory_space=pl.ANY)],
        out_specs=tok(d),
        out_shape=jax.ShapeDtypeStruct(x1.shape, F32),
        scratch_shapes=[pltpu.VMEM((tm * ROW_SUB, LANES), F32), pltpu.VMEM((tm * ROW_SUB, LANES), F32),
                        pltpu.SemaphoreType.DMA(())],
        compiler_params=_params("arbitrary", "arbitrary"),
        name="moe_combine",
    )(pos, x1, gates, sel, mod, gfin, ys)


def _moe(h, x1, gates, sel, mod, w1, w3, w2, first_expert, xs_buf, g_final):
    bsz, s, d = x1.shape
    n_tok = bsz * s
    row_tile = FFN_TILE
    n_rows = xs_buf.shape[0] // ROW_SUB
    assert n_rows == _moe_rows(n_tok)
    pos, te = _route(sel.reshape(n_tok, LANES), row_tile, n_rows // row_tile)
    xs = _dispatch(pos, h.reshape(n_tok * ROW_SUB, LANES), xs_buf)
    ys = _experts(te, xs, w1, w3, w2, row_tile, first_expert)
    return _combine(pos, x1, gates, sel, mod, ys, g_final), xs


def _moe_rows(n_tok):
    return 2 * n_tok + N_EXPERTS * FFN_TILE


def _final_norm_kernel(x_ref, g_ref, o_ref):
    o_ref[0] = _rms(x_ref[0], g_ref[...])


def _final_norm(x, g):
    bsz, s, d = x.shape
    tm = TOK_TILE
    tok = pl.BlockSpec((1, tm, d), lambda b, i: (b, i, 0))
    return pl.pallas_call(
        _final_norm_kernel,
        grid=(bsz, s // tm),
        in_specs=[tok, pl.BlockSpec(g.shape, lambda b, i: (0, 0))],
        out_specs=tok,
        out_shape=jax.ShapeDtypeStruct(x.shape, F32),
        compiler_params=_params("arbitrary", "arbitrary"),
        name="final_norm",
    )(x, g)


def _pad_heads(w, heads, width, lo, hi):
    k = w.shape[0]
    w = w.reshape(k, heads, width)[:, :, lo:hi]
    return jnp.pad(w, ((0, 0), (0, 0), (0, LANES - (hi - lo)))).reshape(k, heads * LANES)


def _prep_w_in(w_in):
    d = w_in.shape[0]
    kr = w_in[:, C_KR:]
    kr_block = jnp.concatenate([jnp.zeros((d, MLA_NOPE_DIM), w_in.dtype), kr,
                                jnp.zeros((d, LANES - MLA_NOPE_DIM - MLA_ROPE_DIM), w_in.dtype)], axis=1)
    return jnp.concatenate([w_in[:, :C_KR], kr_block], axis=1).astype(BF16)


def kernel(x, c, positions, w_ada, b_ada, g_attn, w_in, diff_lambda, diff_subln_g, rel_bias, mla_q_norm, w_uq, mla_kv_norm, w_ukv, w_o, g_ffn, ffn_w1, ffn_w3, ffn_w2, moe_router, moe_w1, moe_w3, moe_w2, g_final):
    depth = w_ada.shape[0]
    bsz, s, d = x.shape
    mods = _ada(c, w_ada, b_ada).reshape(depth, bsz, ADA_CHUNKS, d)
    rc, rsa, rsb = _rope_tables(positions)
    bias = _bias_tiles(rel_bias)
    qk_w = MLA_NOPE_DIM + MLA_ROPE_DIM
    kv_w = MLA_NOPE_DIM + MLA_V_DIM
    d_ff = moe_w1.shape[-1]
    moe_w1_b = moe_w1.astype(BF16).reshape(-1, d, d_ff)
    moe_w3_b = moe_w3.astype(BF16).reshape(-1, d, d_ff)
    moe_w2_b = moe_w2.astype(BF16).reshape(-1, d_ff, d)
    xs_buf = jnp.zeros((_moe_rows(bsz * s) * ROW_SUB, LANES), F32)
    for l in range(depth):
        mod = mods[l]
        lam_init = jnp.full((1,), 0.8 - 0.6 * math.exp(-0.3 * l), F32)
        w_uq_p = _pad_heads(w_uq[l], MLA_HEADS, qk_w, 0, qk_w).astype(BF16)
        w_uk_p = _pad_heads(w_ukv[l], MLA_HEADS, kv_w, 0, MLA_NOPE_DIM).astype(BF16)
        w_uv_p = w_ukv[l].reshape(MLA_KV_RANK, MLA_HEADS, kv_w)[:, :, MLA_NOPE_DIM:].reshape(
            MLA_KV_RANK, MLA_HEADS * MLA_V_DIM).astype(BF16)
        dq, dk, dv, mq, mk, mv = _pre_attn(
            x, mod, g_attn[l].reshape(1, d), _prep_w_in(w_in[l]), mla_q_norm[l].reshape(1, -1), w_uq_p,
            mla_kv_norm[l].reshape(1, -1), w_uk_p, w_uv_p, rc, rsa, rsb)
        o_diff = _diff_attn(lam_init, diff_lambda[l], diff_subln_g[l].reshape(-1, 1), bias, dq, dk, dv)
        o_mla = _mla_attn(mq, mk, mv)
        wo = w_o[l].astype(BF16)
        n_diff = DIFF_HEADS * DIFF_V_DIM
        g_f = g_ffn[l].reshape(1, d)
        if l % 2 == 1:
            w_router = jnp.pad(moe_router[l // 2], ((0, 0), (0, LANES - N_EXPERTS)))
            x1, h, gates, sel = _post_attn(o_diff, o_mla, x, mod, g_f, wo[:n_diff], wo[n_diff:], w_router)
            g_fin = g_final.reshape(1, d) if l == depth - 1 else None
            x, xs_buf = _moe(h, x1, gates, sel, mod, moe_w1_b, moe_w3_b, moe_w2_b, (l // 2) * N_EXPERTS,
                             xs_buf, g_fin)
        else:
            x1, h = _post_attn(o_diff, o_mla, x, mod, g_f, wo[:n_diff], wo[n_diff:])
            x = _ffn_dense(h, x1, mod, ffn_w1[l // 2].astype(BF16), ffn_w3[l // 2].astype(BF16),
                           ffn_w2[l // 2].astype(BF16))
    return x if depth % 2 == 0 else _final_norm(x, g_final.reshape(1, d))
```

```python
import functools
import math

import jax
import jax.numpy as jnp
from jax import lax
from jax.experimental import pallas as pl
from jax.experimental.pallas import tpu as pltpu

F32 = jnp.float32
BF16 = jnp.bfloat16

DIFF_HEADS = 4
DIFF_QK_DIM = 64
DIFF_V_DIM = 128
MLA_HEADS = 8
MLA_NOPE_DIM = 64
MLA_ROPE_DIM = 32
MLA_V_DIM = 64
MLA_Q_RANK = 384
MLA_KV_RANK = 256
ROPE_THETA = 10000.0
N_BUCKETS = 32
MAX_EXACT = 16
MAX_DISTANCE = 128
N_EXPERTS = 8
NORM_EPS = 1e-6
ADA_CHUNKS = 6

LANES = 128
NEG_BIG = -1e30
LOG2E = math.log2(math.e)
VMEM_LIMIT = 56 * 1024 * 1024

ATTN_TILE = 256
AHEAD = 4
NEXT_SLOT = 2
DEFER = 2
TOK_TILE = 512
FFN_TILE = 512
FF_CHUNK = 256

C_DQ, C_DK, C_DV, C_MQ, C_KV, C_KR, C_END = 0, 512, 1024, 1536, 1920, 2176, 2304


def _params(*sem):
    return pltpu.CompilerParams(dimension_semantics=sem, vmem_limit_bytes=VMEM_LIMIT)


def _rms(x, g):
    return x * lax.rsqrt(jnp.mean(x * x, axis=-1, keepdims=True) + NORM_EPS) * g


def _dot_nt(a, b):
    return lax.dot_general(a, b, (((1,), (1,)), ((), ())), preferred_element_type=F32)


ROW_SUB = 8


def _rows_to_tiles(x, ref):
    n = x.shape[0]
    for j in range(ROW_SUB):
        ref[pl.ds(j, n, stride=ROW_SUB), :] = x[:, j * LANES:(j + 1) * LANES]


def _tiles_to_rows(ref):
    n = ref.shape[0] // ROW_SUB
    return jnp.concatenate([ref[pl.ds(j, n, stride=ROW_SUB), :] for j in range(ROW_SUB)], axis=1)


def _tile_row(ref, r):
    return ref.at[pl.ds(pl.multiple_of(r * ROW_SUB, ROW_SUB), ROW_SUB)]


def _ada_kernel(c_ref, w_ref, b_ref, o_ref):
    c = c_ref[...]
    cond = c * jax.nn.sigmoid(c)
    o_ref[0] = jnp.dot(cond, w_ref[0], preferred_element_type=F32,
                       precision=lax.Precision.HIGHEST) + b_ref[0]


def _ada(c, w_ada, b_ada):
    depth, d, n = w_ada.shape
    bsz = c.shape[0]
    tn = 1536
    return pl.pallas_call(
        _ada_kernel,
        grid=(depth, n // tn),
        in_specs=[pl.BlockSpec((bsz, d), lambda l, j: (0, 0)),
                  pl.BlockSpec((1, d, tn), lambda l, j: (l, 0, j)),
                  pl.BlockSpec((1, 1, tn), lambda l, j: (l, 0, j))],
        out_specs=pl.BlockSpec((1, bsz, tn), lambda l, j: (l, 0, j)),
        out_shape=jax.ShapeDtypeStruct((depth, bsz, n), F32),
        compiler_params=_params("arbitrary", "arbitrary"),
        name="ada_mod",
    )(c, w_ada, b_ada.reshape(depth, 1, n))


def _rope_tab_kernel(pos_ref, inv_ref, c_ref, sa_ref, sb_ref):
    pos = pos_ref[0].astype(F32)
    ang = pos * inv_ref[...]
    lane = lax.broadcasted_iota(jnp.int32, ang.shape, 1)
    cos, sin = jnp.cos(ang), jnp.sin(ang)
    lo = (lane >= MLA_NOPE_DIM) & (lane < MLA_NOPE_DIM + MLA_ROPE_DIM // 2)
    hi = (lane >= MLA_NOPE_DIM + MLA_ROPE_DIM // 2) & (lane < MLA_NOPE_DIM + MLA_ROPE_DIM)
    c_ref[0] = jnp.where(lane < MLA_NOPE_DIM, 1.0, jnp.where(lo | hi, cos, 0.0))
    sa_ref[0] = jnp.where(lo, -sin, 0.0)
    sb_ref[0] = jnp.where(hi, sin, 0.0)


def _rope_tables(positions):
    bsz, s = positions.shape
    half = MLA_ROPE_DIM // 2
    inv_freq = ROPE_THETA ** (-jnp.arange(half, dtype=F32) / half)
    inv_lane = jnp.concatenate([jnp.zeros((MLA_NOPE_DIM,), F32), inv_freq, inv_freq,
                                jnp.zeros((LANES - MLA_NOPE_DIM - MLA_ROPE_DIM,), F32)]).reshape(1, LANES)
    tm = TOK_TILE
    spec = pl.BlockSpec((1, tm, LANES), lambda b, i: (b, i, 0))
    shape = jax.ShapeDtypeStruct((bsz, s, LANES), F32)
    return pl.pallas_call(
        _rope_tab_kernel,
        grid=(bsz, s // tm),
        in_specs=[pl.BlockSpec((1, tm, 1), lambda b, i: (b, i, 0)),
                  pl.BlockSpec((1, LANES), lambda b, i: (0, 0))],
        out_specs=[spec, spec, spec],
        out_shape=[shape, shape, shape],
        compiler_params=_params("arbitrary", "arbitrary"),
        name="rope_tables",
    )(positions.reshape(bsz, s, 1), inv_lane)


def _bias_tile_kernel(rb_ref, o_ref):
    h, d = pl.program_id(0), pl.program_id(1)
    t = o_ref.shape[-1]
    key = lax.broadcasted_iota(jnp.int32, (t, t), 0)
    qry = lax.broadcasted_iota(jnp.int32, (t, t), 1)
    dist = d * t + qry - key
    n = jnp.maximum(dist, 0)
    nf = jnp.maximum(n, 1).astype(F32)
    large = MAX_EXACT + (jnp.log(nf / MAX_EXACT) / math.log(MAX_DISTANCE / MAX_EXACT)
                         * (N_BUCKETS - MAX_EXACT)).astype(jnp.int32)
    large = jnp.minimum(large, N_BUCKETS - 1)
    bucket = jnp.where(n < MAX_EXACT, n, large)
    val = jnp.zeros((t, t), F32)
    for j in range(N_BUCKETS):
        val = jnp.where(bucket == j, rb_ref[j, h], val)
    val = (rb_ref[N_BUCKETS - 1, h] - val) * LOG2E
    o_ref[0, 0] = jnp.where(dist < 0, -NEG_BIG, val)


def _bias_tiles(rel_bias):
    t = ATTN_TILE
    return pl.pallas_call(
        _bias_tile_kernel,
        grid=(DIFF_HEADS, 2),
        in_specs=[pl.BlockSpec(memory_space=pltpu.SMEM)],
        out_specs=pl.BlockSpec((1, 1, t, t), lambda h, d: (h, d, 0, 0)),
        out_shape=jax.ShapeDtypeStruct((DIFF_HEADS, 2, t, t), F32),
        compiler_params=_params("arbitrary", "arbitrary"),
        name="bias_tiles",
    )(rel_bias)


def _pre_attn_kernel(x_ref, mod_ref, g_ref, w_in_ref, gq_ref, w_uq_ref, gkv_ref, w_uk_ref, w_uv_ref,
                     rc_ref, rsa_ref, rsb_ref,
                     dq_ref, dk_ref, dv_ref, mq_ref, mk_ref, mv_ref):
    x = x_ref[0]
    sh, sc = mod_ref[0, 0:1, :], mod_ref[0, 1:2, :]
    h = (_rms(x, g_ref[...]) * (1.0 + sc) + sh).astype(BF16)
    proj = jnp.dot(h, w_in_ref[...], preferred_element_type=F32)
    dq_ref[0] = (proj[:, C_DQ:C_DK] * (DIFF_QK_DIM ** -0.5 * LOG2E)).astype(BF16)
    dk_ref[0] = proj[:, C_DK:C_DV].astype(BF16)
    dv_ref[0] = proj[:, C_DV:C_MQ].astype(BF16)

    rc, rsa, rsb = rc_ref[0], rsa_ref[0], rsb_ref[0]

    def rope(v):
        return v * rc + pltpu.roll(v, LANES - MLA_ROPE_DIM // 2, 1) * rsa + pltpu.roll(v, MLA_ROPE_DIM // 2, 1) * rsb

    qn = _rms(proj[:, C_MQ:C_KV], gq_ref[...]).astype(BF16)
    q = jnp.dot(qn, w_uq_ref[...], preferred_element_type=F32)
    kvn = _rms(proj[:, C_KV:C_KR], gkv_ref[...]).astype(BF16)
    kn = jnp.dot(kvn, w_uk_ref[...], preferred_element_type=F32)
    mv_ref[0] = jnp.dot(kvn, w_uv_ref[...], preferred_element_type=F32).astype(BF16)
    kr = rope(proj[:, C_KR:C_END])
    q_scale = (MLA_NOPE_DIM + MLA_ROPE_DIM) ** -0.5 * LOG2E
    for hd in range(MLA_HEADS):
        sl = slice(hd * LANES, (hd + 1) * LANES)
        mq_ref[0, :, sl] = (rope(q[:, sl]) * q_scale).astype(BF16)
        mk_ref[0, :, sl] = (kn[:, sl] + kr).astype(BF16)


def _pre_attn(x, mod, g, w_in, gq, w_uq, gkv, w_uk, w_uv, rc, rsa, rsb):
    bsz, s, d = x.shape
    tm = TOK_TILE
    tok = lambda w: pl.BlockSpec((1, tm, w), lambda b, i: (b, i, 0))
    full = lambda a: pl.BlockSpec(a.shape, lambda b, i: (0,) * a.ndim)
    widths = (512, 512, 512, MLA_HEADS * LANES, MLA_HEADS * LANES, MLA_HEADS * MLA_V_DIM)
    return pl.pallas_call(
        _pre_attn_kernel,
        grid=(bsz, s // tm),
        in_specs=[tok(d), pl.BlockSpec((1, ADA_CHUNKS, d), lambda b, i: (b, 0, 0)), full(g), full(w_in),
                  full(gq), full(w_uq), full(gkv), full(w_uk), full(w_uv), tok(LANES), tok(LANES), tok(LANES)],
        out_specs=[tok(w) for w in widths],
        out_shape=[jax.ShapeDtypeStruct((bsz, s, w), BF16) for w in widths],
        compiler_params=_params("arbitrary", "arbitrary"),
        name="pre_attn",
    )(x, mod, g, w_in, gq, w_uq, gkv, w_uk, w_uv, rc, rsa, rsb)


def _softmax_init(m_ref, l_ref, acc_ref):
    m_ref[...] = jnp.full(m_ref.shape, NEG_BIG, F32)
    l_ref[...] = jnp.zeros(l_ref.shape, F32)
    acc_ref[...] = jnp.zeros(acc_ref.shape, F32)


def _softmax_probs(st, m_ref, l_ref):
    m_prev = m_ref[...]
    m_new = jnp.maximum(m_prev, jnp.max(st, axis=0, keepdims=True))
    alpha = jnp.exp2(m_prev - m_new)
    p = jnp.exp2(st - m_new)
    l_ref[...] = alpha * l_ref[...] + jnp.sum(p, axis=0, keepdims=True)
    m_ref[...] = m_new
    return p.astype(BF16), alpha


def _acc_update(acc_ref, alpha, vt, p):
    acc_ref[...] = alpha * acc_ref[...] + jnp.dot(vt, p, preferred_element_type=F32)


def _attn_scratch(n_chains, dv, t):
    return [pltpu.VMEM((n_chains, 1, t), F32), pltpu.VMEM((n_chains, 1, t), F32), pltpu.VMEM((n_chains, dv, t), F32),
            pltpu.VMEM((AHEAD, t, t), F32), pltpu.VMEM((DEFER, t, t), BF16), pltpu.VMEM((DEFER, 1, t), F32)]


def _attn_prologue(n_chains, scores, refs):
    m_ref, l_ref, acc_ref, pend_ref, pp_ref, pa_ref = refs
    for c in range(n_chains):
        _softmax_init(m_ref.at[c], l_ref.at[c], acc_ref.at[c])
    for c in range(AHEAD):
        pend_ref[c] = scores(0, c)
    pp_ref[...] = jnp.zeros(pp_ref.shape, pp_ref.dtype)
    pa_ref[...] = jnp.ones(pa_ref.shape, pa_ref.dtype)


def _attn_deferred(n_chains, values, refs, j):
    _, _, acc_ref, _, pp_ref, pa_ref = refs
    for d in range(DEFER):
        c = n_chains - DEFER + d
        _acc_update(acc_ref.at[c], pa_ref[d], values(j, c), pp_ref[d])


def _attn_step(n_chains, scores, adjust, values, refs, j, has_next):
    m_ref, l_ref, acc_ref, pend_ref, pp_ref, pa_ref = refs
    _attn_deferred(n_chains, values, refs, jnp.maximum(j - 1, 0))
    pending = [pend_ref[c] for c in range(AHEAD)]
    for c in range(n_chains):
        if c + AHEAD < n_chains:
            pending.append(scores(j, c + AHEAD))
        if has_next and 0 <= c - NEXT_SLOT < AHEAD:
            pend_ref[c - NEXT_SLOT] = scores(j + 1, c - NEXT_SLOT)
        p, alpha = _softmax_probs(adjust(c, pending.pop(0)), m_ref.at[c], l_ref.at[c])
        if c < n_chains - DEFER:
            _acc_update(acc_ref.at[c], alpha, values(j, c), p)
        else:
            pp_ref[c - (n_chains - DEFER)] = p
            pa_ref[c - (n_chains - DEFER)] = alpha


def _diff_attn_kernel(lam_init_ref, lam_ref, g_ref, bias_ref, q_ref, k_ref, v_ref, o_ref,
                      vt_ref, *refs):
    i = pl.program_id(1)
    t = q_ref.shape[1]
    _, l_ref, acc_ref = refs[:3]

    @pl.when(i == 0)
    def _():
        vt_ref[...] = v_ref[0].T

    n_maps = 2 * DIFF_HEADS
    qts = []
    for h in range(DIFF_HEADS):
        qt = q_ref[0, :, h * LANES:(h + 1) * LANES].T
        row = lax.broadcasted_iota(jnp.int32, qt.shape, 0)
        qts.append(jnp.where(row < DIFF_QK_DIM, qt, jnp.zeros_like(qt)))
        qts.append(jnp.where(row >= DIFF_QK_DIM, qt, jnp.zeros_like(qt)))

    def scores(j, c):
        k = k_ref[0, pl.ds(pl.multiple_of(j * t, t), t), (c // 2) * LANES:(c // 2 + 1) * LANES]
        return jnp.dot(k, qts[c], preferred_element_type=F32)

    def values(j, c):
        return vt_ref[(c // 2) * LANES:(c // 2 + 1) * LANES, pl.ds(pl.multiple_of(j * t, t), t)]

    def step(j, bias_idx, has_next):
        def adjust(c, st):
            if bias_idx is None:
                return st
            return st - bias_ref[c // 2, bias_idx]

        _attn_step(n_maps, scores, adjust, values, refs, j, has_next)

    def far(j, carry):
        step(j, None, True)
        return carry

    _attn_prologue(n_maps, scores, refs)
    lax.fori_loop(0, jnp.maximum(i - 1, 0), far, 0)

    @pl.when(i >= 1)
    def _():
        step(i - 1, 1, True)

    step(i, 0, False)
    _attn_deferred(n_maps, values, refs, i)

    lv = lam_ref[...]
    lam = (jnp.exp(jnp.sum(lv[0:1] * lv[1:2], keepdims=True)) - jnp.exp(jnp.sum(lv[2:3] * lv[3:4], keepdims=True))
           + lam_init_ref[0])
    for h in range(DIFF_HEADS):
        c0, c1 = 2 * h, 2 * h + 1
        ot = acc_ref[c0] / l_ref[c0] - lam * (acc_ref[c1] / l_ref[c1])
        ot = ot * lax.rsqrt(jnp.mean(ot * ot, axis=0, keepdims=True) + NORM_EPS) * g_ref[...]
        o_ref[0, :, h * LANES:(h + 1) * LANES] = (ot * (1.0 - lam_init_ref[0])).T.astype(BF16)


def _diff_attn(lam_init, diff_lambda, g, bias, dq, dk, dv):
    bsz, s, w = dq.shape
    t = ATTN_TILE
    assert MAX_DISTANCE <= t and s % t == 0
    n_maps = 2 * DIFF_HEADS
    return pl.pallas_call(
        _diff_attn_kernel,
        grid=(bsz, s // t),
        in_specs=[pl.BlockSpec(memory_space=pltpu.SMEM),
                  pl.BlockSpec(diff_lambda.shape, lambda b, i: (0, 0)),
                  pl.BlockSpec(g.shape, lambda b, i: (0, 0)),
                  pl.BlockSpec(bias.shape, lambda b, i: (0, 0, 0, 0)),
                  pl.BlockSpec((1, t, w), lambda b, i: (b, i, 0)),
                  pl.BlockSpec((1, s, w), lambda b, i: (b, 0, 0)),
                  pl.BlockSpec((1, s, w), lambda b, i: (b, 0, 0))],
        out_specs=pl.BlockSpec((1, t, w), lambda b, i: (b, i, 0)),
        out_shape=jax.ShapeDtypeStruct(dq.shape, BF16),
        scratch_shapes=[pltpu.VMEM((w, s), BF16)] + _attn_scratch(n_maps, LANES, t),
        compiler_params=_params("arbitrary", "arbitrary"),
        name="diff_attn",
    )(lam_init, diff_lambda, g, bias, dq, dk, dv)


def _mla_attn_kernel(q_ref, k_ref, v_ref, o_ref, vt_ref, *refs):
    i = pl.program_id(1)
    t = q_ref.shape[1]
    _, l_ref, acc_ref = refs[:3]

    @pl.when(i == 0)
    def _():
        vt_ref[...] = v_ref[0].T

    qts = [q_ref[0, :, h * LANES:(h + 1) * LANES].T for h in range(MLA_HEADS)]

    def scores(j, h):
        k = k_ref[0, pl.ds(pl.multiple_of(j * t, t), t), h * LANES:(h + 1) * LANES]
        return jnp.dot(k, qts[h], preferred_element_type=F32)

    def values(j, h):
        return vt_ref[h * MLA_V_DIM:(h + 1) * MLA_V_DIM, pl.ds(pl.multiple_of(j * t, t), t)]

    def step(j, masked, has_next):
        def adjust(h, st):
            if not masked:
                return st
            key = lax.broadcasted_iota(jnp.int32, st.shape, 0)
            qry = lax.broadcasted_iota(jnp.int32, st.shape, 1)
            return jnp.where(key <= qry, st, NEG_BIG)

        _attn_step(MLA_HEADS, scores, adjust, values, refs, j, has_next)

    def far(j, carry):
        step(j, False, True)
        return carry

    _attn_prologue(MLA_HEADS, scores, refs)
    lax.fori_loop(0, i, far, 0)
    step(i, True, False)
    _attn_deferred(MLA_HEADS, values, refs, i)
    for u in range(MLA_HEADS // 2):
        ot = jnp.concatenate([acc_ref[2 * u] / l_ref[2 * u], acc_ref[2 * u + 1] / l_ref[2 * u + 1]], axis=0)
        o_ref[0, :, u * LANES:(u + 1) * LANES] = ot.T.astype(BF16)


def _mla_attn(mq, mk, mv):
    bsz, s, wq = mq.shape
    wv = mv.shape[-1]
    t = ATTN_TILE
    return pl.pallas_call(
        _mla_attn_kernel,
        grid=(bsz, s // t),
        in_specs=[pl.BlockSpec((1, t, wq), lambda b, i: (b, i, 0)),
                  pl.BlockSpec((1, s, wq), lambda b, i: (b, 0, 0)),
                  pl.BlockSpec((1, s, wv), lambda b, i: (b, 0, 0))],
        out_specs=pl.BlockSpec((1, t, wv), lambda b, i: (b, i, 0)),
        out_shape=jax.ShapeDtypeStruct(mv.shape, BF16),
        scratch_shapes=[pltpu.VMEM((wv, s), BF16)] + _attn_scratch(MLA_HEADS, MLA_V_DIM, t),
        compiler_params=_params("arbitrary", "arbitrary"),
        name="mla_attn",
    )(mq, mk, mv)


def _post_attn_kernel(*refs, moe):
    if moe:
        od_ref, om_ref, x_ref, mod_ref, g_ref, wo_d_ref, wo_m_ref, wr_ref, x1_ref, h_ref, gates_ref, sel_ref = refs
    else:
        od_ref, om_ref, x_ref, mod_ref, g_ref, wo_d_ref, wo_m_ref, x1_ref, h_ref = refs
    y = (jnp.dot(od_ref[0], wo_d_ref[...], preferred_element_type=F32)
         + jnp.dot(om_ref[0], wo_m_ref[...], preferred_element_type=F32))
    gt_a, sh_f, sc_f = mod_ref[0, 2:3, :], mod_ref[0, 3:4, :], mod_ref[0, 4:5, :]
    x1 = x_ref[0] + (1.0 + gt_a) * y
    x1_ref[0] = x1
    h = _rms(x1, g_ref[...]) * (1.0 + sc_f) + sh_f
    if not moe:
        h_ref[0] = h.astype(BF16)
        return
    _rows_to_tiles(h, h_ref.at[0])
    wr = wr_ref[...]
    h_hi, wr_hi = h.astype(BF16), wr.astype(BF16)
    h_lo, wr_lo = (h - h_hi.astype(F32)).astype(BF16), (wr - wr_hi.astype(F32)).astype(BF16)
    logits = (jnp.dot(h_hi, wr_hi, preferred_element_type=F32) + jnp.dot(h_hi, wr_lo, preferred_element_type=F32)
              + jnp.dot(h_lo, wr_hi, preferred_element_type=F32))
    lane = lax.broadcasted_iota(jnp.int32, logits.shape, 1)
    logits = jnp.where(lane < N_EXPERTS, logits, -jnp.inf)
    v1 = jnp.max(logits, axis=1, keepdims=True)
    i1 = jnp.min(jnp.where(logits == v1, lane, LANES), axis=1, keepdims=True)
    rest = jnp.where(lane == i1, -jnp.inf, logits)
    v2 = jnp.max(rest, axis=1, keepdims=True)
    i2 = jnp.min(jnp.where(rest == v2, lane, LANES), axis=1, keepdims=True)
    e2 = jnp.exp(v2 - v1)
    w1 = 1.0 / (1.0 + e2)
    w2 = e2 / (1.0 + e2)
    gates_ref[0] = jnp.where(lane == i1, w1, 0.0) + jnp.where(lane == i2, w2, 0.0)
    sel_ref[0] = jnp.where(lane == i1, 1.0, 0.0) + jnp.where(lane == i2, 2.0, 0.0)


def _post_attn(o_diff, o_mla, x, mod, g, wo_d, wo_m, w_router=None):
    bsz, s, d = x.shape
    moe = w_router is not None
    tm = TOK_TILE
    tok = lambda w: pl.BlockSpec((1, tm, w), lambda b, i: (b, i, 0))
    full = lambda a: pl.BlockSpec(a.shape, lambda b, i: (0,) * a.ndim)
    args = [o_diff, o_mla, x, mod, g, wo_d, wo_m] + ([w_router] if moe else [])
    in_specs = [tok(o_diff.shape[-1]), tok(o_mla.shape[-1]), tok(d),
                pl.BlockSpec((1, ADA_CHUNKS, d), lambda b, i: (b, 0, 0)),
                full(g), full(wo_d), full(wo_m)] + ([full(w_router)] if moe else [])
    if moe:
        assert d == ROW_SUB * LANES
        h_spec = pl.BlockSpec((1, tm * ROW_SUB, LANES), lambda b, i: (b, i, 0))
        out_specs = [tok(d), h_spec, tok(LANES), tok(LANES)]
        out_shape = [jax.ShapeDtypeStruct((bsz, s, d), F32), jax.ShapeDtypeStruct((bsz, s * ROW_SUB, LANES), F32)]
        out_shape += [jax.ShapeDtypeStruct((bsz, s, LANES), F32)] * 2
    else:
        out_specs = [tok(d), tok(d)]
        out_shape = [jax.ShapeDtypeStruct((bsz, s, d), F32), jax.ShapeDtypeStruct((bsz, s, d), BF16)]
    return pl.pallas_call(
        functools.partial(_post_attn_kernel, moe=moe),
        grid=(bsz, s // tm),
        in_specs=in_specs,
        out_specs=out_specs,
        out_shape=out_shape,
        compiler_params=_params("arbitrary", "arbitrary"),
        name="post_attn_moe" if moe else "post_attn",
    )(*args)


def _swiglu(h, w1_ref, w3_ref, w2_ref):
    d_ff = w1_ref.shape[-1]
    y = jnp.zeros((h.shape[0], w2_ref.shape[-1]), F32)
    for c0 in range(0, d_ff, FF_CHUNK):
        a = jnp.dot(h, w1_ref[:, c0:c0 + FF_CHUNK], preferred_element_type=F32)
        b = jnp.dot(h, w3_ref[:, c0:c0 + FF_CHUNK], preferred_element_type=F32)
        u = (a * jax.nn.sigmoid(a) * b).astype(BF16)
        y = y + jnp.dot(u, w2_ref[c0:c0 + FF_CHUNK, :], preferred_element_type=F32)
    return y


def _ffn_dense_kernel(h_ref, x1_ref, mod_ref, w1_ref, w3_ref, w2_ref, o_ref):
    gt_f = mod_ref[0, 5:6, :]
    o_ref[0] = x1_ref[0] + (1.0 + gt_f) * _swiglu(h_ref[0], w1_ref, w3_ref, w2_ref)


def _ffn_dense(h, x1, mod, w1, w3, w2):
    bsz, s, d = x1.shape
    tm = FFN_TILE
    tok = lambda w: pl.BlockSpec((1, tm, w), lambda b, i: (b, i, 0))
    full = lambda a: pl.BlockSpec(a.shape, lambda b, i: (0,) * a.ndim)
    return pl.pallas_call(
        _ffn_dense_kernel,
        grid=(bsz, s // tm),
        in_specs=[tok(d), tok(d), pl.BlockSpec((1, ADA_CHUNKS, d), lambda b, i: (b, 0, 0)),
                  full(w1), full(w3), full(w2)],
        out_specs=tok(d),
        out_shape=jax.ShapeDtypeStruct(x1.shape, F32),
        compiler_params=_params("arbitrary", "arbitrary"),
        name="ffn_dense",
    )(h, x1, mod, w1, w3, w2)


def _route_kernel(sel_ref, pos_ref, te_ref, cnt_ref, off_ref, run_ref, *, row_tile):
    p, t = pl.program_id(0), pl.program_id(1)
    sel_t = sel_ref[...].T
    chosen = (sel_t > 0.0).astype(F32)
    per_expert = jnp.sum(chosen, axis=1, keepdims=True)

    @pl.when((p == 0) & (t == 0))
    def _():
        cnt_ref[...] = jnp.zeros(cnt_ref.shape, F32)
        te_ref[...] = jnp.zeros(te_ref.shape, jnp.int32)

    @pl.when(p == 0)
    def _():
        cnt_ref[...] += per_expert

    @pl.when((p == 1) & (t == 0))
    def _():
        cnt = cnt_ref[...]
        padded = jnp.ceil(cnt / row_tile) * row_tile
        row = lax.broadcasted_iota(jnp.int32, cnt.shape, 0)
        off = jnp.zeros(cnt.shape, F32)
        for e in range(N_EXPERTS):
            size_e = jnp.sum(jnp.where(row == e, padded, 0.0), keepdims=True)
            off = off + jnp.where(row > e, size_e, 0.0)
        off_ref[...] = off
        run_ref[...] = jnp.zeros(run_ref.shape, F32)
        ends = off + padded
        tile_start = lax.broadcasted_iota(jnp.int32, (LANES, LANES), 1).astype(F32) * row_tile
        erow = lax.broadcasted_iota(jnp.int32, (LANES, LANES), 0)
        done = jnp.where((erow < N_EXPERTS) & (ends <= tile_start), 1.0, 0.0)
        te = jnp.sum(done, axis=0, keepdims=True).astype(jnp.int32)
        te_ref[...] = jnp.broadcast_to(te, te_ref.shape)

    @pl.when(p == 1)
    def _():
        tm = sel_t.shape[1]
        before = (lax.broadcasted_iota(jnp.int32, (tm, tm), 0)
                  < lax.broadcasted_iota(jnp.int32, (tm, tm), 1)).astype(BF16)
        rank = jnp.dot(chosen.astype(BF16), before, preferred_element_type=F32) + run_ref[...]
        base = off_ref[...] + rank
        for k in range(2):
            pos = jnp.sum(jnp.where(sel_t == float(k + 1), base, 0.0), axis=0, keepdims=True)
            pos_ref[0, k:k + 1, :] = pos.astype(jnp.int32)
        run_ref[...] += per_expert


def _route(sel, row_tile, n_row_tiles):
    n_tok = sel.shape[0]
    assert n_row_tiles <= LANES and 2 * n_tok < 2 ** 24
    tm = TOK_TILE
    nt = n_tok // tm
    pos, te = pl.pallas_call(
        functools.partial(_route_kernel, row_tile=row_tile),
        grid=(2, nt),
        in_specs=[pl.BlockSpec((tm, LANES), lambda p, t: (t, 0))],
        out_specs=[pl.BlockSpec((1, 2, tm), lambda p, t: (p * t, 0, 0)),
                   pl.BlockSpec((8, LANES), lambda p, t: (0, 0))],
        out_shape=[jax.ShapeDtypeStruct((nt, 2, tm), jnp.int32), jax.ShapeDtypeStruct((8, LANES), jnp.int32)],
        scratch_shapes=[pltpu.VMEM((LANES, 1), F32)] * 3,
        compiler_params=_params("arbitrary", "arbitrary"),
        name="moe_route",
    )(sel)
    return pos, te[0]


def _dispatch_kernel(pos_ref, h_ref, xs_in_ref, xs_ref, sem):
    del xs_in_ref
    tm = h_ref.shape[0] // ROW_SUB

    def body(r, carry):
        for k in range(2):
            pltpu.make_async_copy(_tile_row(h_ref, r), _tile_row(xs_ref, pos_ref[0, k, r]), sem).start(priority=k)
        return carry

    lax.fori_loop(0, tm, body, 0, unroll=8)
    for k in range(2):
        pltpu.make_async_copy(h_ref, xs_ref.at[pl.ds(0, tm * ROW_SUB)], sem).wait()


def _dispatch(pos, h, xs0):
    nt, _, tm = pos.shape
    return pl.pallas_call(
        _dispatch_kernel,
        grid=(nt,),
        in_specs=[pl.BlockSpec((1, 2, tm), lambda t: (t, 0, 0), memory_space=pltpu.SMEM),
                  pl.BlockSpec((tm * ROW_SUB, LANES), lambda t: (t, 0)),
                  pl.BlockSpec(memory_space=pl.ANY)],
        out_specs=pl.BlockSpec(memory_space=pl.ANY),
        out_shape=jax.ShapeDtypeStruct(xs0.shape, xs0.dtype),
        scratch_shapes=[pltpu.SemaphoreType.DMA(())],
        input_output_aliases={2: 0},
        compiler_params=_params("arbitrary"),
        name="moe_dispatch",
    )(pos, h, xs0)


def _expert_kernel(te_ref, xs_ref, w1_ref, w3_ref, w2_ref, y_ref):
    used = te_ref[pl.program_id(0)] < N_EXPERTS

    @pl.when(used)
    def _():
        y = _swiglu(_tiles_to_rows(xs_ref).astype(BF16), w1_ref.at[0], w3_ref.at[0], w2_ref.at[0])
        _rows_to_tiles(y, y_ref)

    @pl.when(jnp.logical_not(used))
    def _():
        y_ref[...] = jnp.zeros(y_ref.shape, y_ref.dtype)


def _experts(te, xs, w1, w3, w2, row_tile, first_expert):
    n_rows = xs.shape[0] // ROW_SUB
    _, d, d_ff = w1.shape
    expert = lambda n, te: (first_expert + jnp.minimum(te[n], N_EXPERTS - 1), 0, 0)
    rows = pl.BlockSpec((row_tile * ROW_SUB, LANES), lambda n, te: (n, 0))
    return pl.pallas_call(
        _expert_kernel,
        grid_spec=pltpu.PrefetchScalarGridSpec(
            num_scalar_prefetch=1,
            grid=(n_rows // row_tile,),
            in_specs=[rows, pl.BlockSpec((1, d, d_ff), expert), pl.BlockSpec((1, d, d_ff), expert),
                      pl.BlockSpec((1, d_ff, d), expert)],
            out_specs=rows),
        out_shape=jax.ShapeDtypeStruct(xs.shape, F32),
        compiler_params=_params("arbitrary"),
        name="moe_experts",
    )(te, xs, w1, w3, w2)


def _combine_kernel(pos_ref, x1_ref, gates_ref, sel_ref, mod_ref, gfin_ref, ys_ref, o_ref, ya_ref, yb_ref, sem,
                    *, final):
    tm = x1_ref.shape[1]
    bufs = (ya_ref, yb_ref)

    def body(r, carry):
        for k in range(2):
            pltpu.make_async_copy(_tile_row(ys_ref, pos_ref[0, k, r]), _tile_row(bufs[k], r), sem).start(priority=k)
        return carry

    lax.fori_loop(0, tm, body, 0, unroll=8)
    gates, sel = gates_ref[0], sel_ref[0]
    w_a = jnp.sum(jnp.where(sel == 1.0, gates, 0.0), axis=1, keepdims=True)
    w_b = jnp.sum(jnp.where(sel == 2.0, gates, 0.0), axis=1, keepdims=True)
    for k in range(2):
        pltpu.make_async_copy(ys_ref.at[pl.ds(0, tm * ROW_SUB)], bufs[k], sem).wait()
    gt_f = mod_ref[0, 5:6, :]
    x = x1_ref[0] + (1.0 + gt_f) * (w_a * _tiles_to_rows(ya_ref) + w_b * _tiles_to_rows(yb_ref))
    o_ref[0] = _rms(x, gfin_ref[...]) if final else x


def _combine(pos, x1, gates, sel, mod, ys, g_final=None):
    bsz, s, d = x1.shape
    nt, _, tm = pos.shape
    per_b = s // tm
    tok = lambda w: pl.BlockSpec((1, tm, w), lambda b, i: (b, i, 0))
    final = g_final is not None
    gfin = g_final if final else jnp.ones((1, d), F32)
    return pl.pallas_call(
        functools.partial(_combine_kernel, final=final),
        grid=(bsz, per_b),
        in_specs=[pl.BlockSpec((1, 2, tm), lambda b, i: (b * per_b + i, 0, 0), memory_space=pltpu.SMEM),
                  tok(d), tok(LANES), tok(LANES),
                  pl.BlockSpec((1, ADA_CHUNKS, d), lambda b, i: (b, 0, 0)),
                  pl.BlockSpec((1, d), lambda b, i: (0, 0)),
                  pl.BlockSpec(memory_space=pl.ANY)],
        out_specs=tok(d),
        out_shape=jax.ShapeDtypeStruct(x1.shape, F32),
        scratch_shapes=[pltpu.VMEM((tm * ROW_SUB, LANES), F32), pltpu.VMEM((tm * ROW_SUB, LANES), F32),
                        pltpu.SemaphoreType.DMA(())],
        compiler_params=_params("arbitrary", "arbitrary"),
        name="moe_combine",
    )(pos, x1, gates, sel, mod, gfin, ys)


def _moe(h, x1, gates, sel, mod, w1, w3, w2, first_expert, xs_buf, g_final):
    bsz, s, d = x1.shape
    n_tok = bsz * s
    row_tile = FFN_TILE
    n_rows = xs_buf.shape[0] // ROW_SUB
    assert n_rows == _moe_rows(n_tok)
    pos, te = _route(sel.reshape(n_tok, LANES), row_tile, n_rows // row_tile)
    xs = _dispatch(pos, h.reshape(n_tok * ROW_SUB, LANES), xs_buf)
    ys = _experts(te, xs, w1, w3, w2, row_tile, first_expert)
    return _combine(pos, x1, gates, sel, mod, ys, g_final), xs


def _moe_rows(n_tok):
    return 2 * n_tok + N_EXPERTS * FFN_TILE


def _final_norm_kernel(x_ref, g_ref, o_ref):
    o_ref[0] = _rms(x_ref[0], g_ref[...])


def _final_norm(x, g):
    bsz, s, d = x.shape
    tm = TOK_TILE
    tok = pl.BlockSpec((1, tm, d), lambda b, i: (b, i, 0))
    return pl.pallas_call(
        _final_norm_kernel,
        grid=(bsz, s // tm),
        in_specs=[tok, pl.BlockSpec(g.shape, lambda b, i: (0, 0))],
        out_specs=tok,
        out_shape=jax.ShapeDtypeStruct(x.shape, F32),
        compiler_params=_params("arbitrary", "arbitrary"),
        name="final_norm",
    )(x, g)


def _pad_heads(w, heads, width, lo, hi):
    k = w.shape[0]
    w = w.reshape(k, heads, width)[:, :, lo:hi]
    return jnp.pad(w, ((0, 0), (0, 0), (0, LANES - (hi - lo)))).reshape(k, heads * LANES)


def _prep_w_in(w_in):
    d = w_in.shape[0]
    kr = w_in[:, C_KR:]
    kr_block = jnp.concatenate([jnp.zeros((d, MLA_NOPE_DIM), w_in.dtype), kr,
                                jnp.zeros((d, LANES - MLA_NOPE_DIM - MLA_ROPE_DIM), w_in.dtype)], axis=1)
    return jnp.concatenate([w_in[:, :C_KR], kr_block], axis=1).astype(BF16)


def kernel(x, c, positions, w_ada, b_ada, g_attn, w_in, diff_lambda, diff_subln_g, rel_bias, mla_q_norm, w_uq, mla_kv_norm, w_ukv, w_o, g_ffn, ffn_w1, ffn_w3, ffn_w2, moe_router, moe_w1, moe_w3, moe_w2, g_final):
    depth = w_ada.shape[0]
    bsz, s, d = x.shape
    mods = _ada(c, w_ada, b_ada).reshape(depth, bsz, ADA_CHUNKS, d)
    rc, rsa, rsb = _rope_tables(positions)
    bias = _bias_tiles(rel_bias)
    qk_w = MLA_NOPE_DIM + MLA_ROPE_DIM
    kv_w = MLA_NOPE_DIM + MLA_V_DIM
    d_ff = moe_w1.shape[-1]
    moe_w1_b = moe_w1.astype(BF16).reshape(-1, d, d_ff)
    moe_w3_b = moe_w3.astype(BF16).reshape(-1, d, d_ff)
    moe_w2_b = moe_w2.astype(BF16).reshape(-1, d_ff, d)
    xs_buf = jnp.zeros((_moe_rows(bsz * s) * ROW_SUB, LANES), F32)
    for l in range(depth):
        mod = mods[l]
        lam_init = jnp.full((1,), 0.8 - 0.6 * math.exp(-0.3 * l), F32)
        w_uq_p = _pad_heads(w_uq[l], MLA_HEADS, qk_w, 0, qk_w).astype(BF16)
        w_uk_p = _pad_heads(w_ukv[l], MLA_HEADS, kv_w, 0, MLA_NOPE_DIM).astype(BF16)
        w_uv_p = w_ukv[l].reshape(MLA_KV_RANK, MLA_HEADS, kv_w)[:, :, MLA_NOPE_DIM:].reshape(
            MLA_KV_RANK, MLA_HEADS * MLA_V_DIM).astype(BF16)
        dq, dk, dv, mq, mk, mv = _pre_attn(
            x, mod, g_attn[l].reshape(1, d), _prep_w_in(w_in[l]), mla_q_norm[l].reshape(1, -1), w_uq_p,
            mla_kv_norm[l].reshape(1, -1), w_uk_p, w_uv_p, rc, rsa, rsb)
        o_diff = _diff_attn(lam_init, diff_lambda[l], diff_subln_g[l].reshape(-1, 1), bias, dq, dk, dv)
        o_mla = _mla_attn(mq, mk, mv)
        wo = w_o[l].astype(BF16)
        n_diff = DIFF_HEADS * DIFF_V_DIM
        g_f = g_ffn[l].reshape(1, d)
        if l % 2 == 1:
            w_router = jnp.pad(moe_router[l // 2], ((0, 0), (0, LANES - N_EXPERTS)))
            x1, h, gates, sel = _post_attn(o_diff, o_mla, x, mod, g_f, wo[:n_diff], wo[n_diff:], w_router)
            g_fin = g_final.reshape(1, d) if l == depth - 1 else None
            x, xs_buf = _moe(h, x1, gates, sel, mod, moe_w1_b, moe_w3_b, moe_w2_b, (l // 2) * N_EXPERTS,
                             xs_buf, g_fin)
        else:
            x1, h = _post_attn(o_diff, o_mla, x, mod, g_f, wo[:n_diff], wo[n_diff:])
            x = _ffn_dense(h, x1, mod, ffn_w1[l // 2].astype(BF16), ffn_w3[l // 2].astype(BF16),
                           ffn_w2[l // 2].astype(BF16))
    return x if depth % 2 == 0 else _final_norm(x, g_final.reshape(1, d))
```

```python
import functools
import math

import jax
import jax.numpy as jnp
from jax import lax
from jax.experimental import pallas as pl
from jax.experimental.pallas import tpu as pltpu

F32 = jnp.float32
BF16 = jnp.bfloat16

DIFF_HEADS = 4
DIFF_QK_DIM = 64
DIFF_V_DIM = 128
MLA_HEADS = 8
MLA_NOPE_DIM = 64
MLA_ROPE_DIM = 32
MLA_V_DIM = 64
MLA_Q_RANK = 384
MLA_KV_RANK = 256
ROPE_THETA = 10000.0
N_BUCKETS = 32
MAX_EXACT = 16
MAX_DISTANCE = 128
N_EXPERTS = 8
NORM_EPS = 1e-6
ADA_CHUNKS = 6

LANES = 128
NEG_BIG = -1e30
LOG2E = math.log2(math.e)
VMEM_LIMIT = 56 * 1024 * 1024

ATTN_TILE = 256
AHEAD = 4
TOK_TILE = 512
FFN_TILE = 512
FF_CHUNK = 256

C_DQ, C_DK, C_DV, C_MQ, C_KV, C_KR, C_END = 0, 512, 1024, 1536, 1920, 2176, 2304


def _params(*sem):
    return pltpu.CompilerParams(dimension_semantics=sem, vmem_limit_bytes=VMEM_LIMIT)


def _rms(x, g):
    return x * lax.rsqrt(jnp.mean(x * x, axis=-1, keepdims=True) + NORM_EPS) * g


def _dot_nt(a, b):
    return lax.dot_general(a, b, (((1,), (1,)), ((), ())), preferred_element_type=F32)


ROW_SUB = 8


def _rows_to_tiles(x, ref):
    n = x.shape[0]
    for j in range(ROW_SUB):
        ref[pl.ds(j, n, stride=ROW_SUB), :] = x[:, j * LANES:(j + 1) * LANES]


def _tiles_to_rows(ref):
    n = ref.shape[0] // ROW_SUB
    return jnp.concatenate([ref[pl.ds(j, n, stride=ROW_SUB), :] for j in range(ROW_SUB)], axis=1)


def _tile_row(ref, r):
    return ref.at[pl.ds(pl.multiple_of(r * ROW_SUB, ROW_SUB), ROW_SUB)]


def _ada_kernel(c_ref, w_ref, b_ref, o_ref):
    c = c_ref[...]
    cond = c * jax.nn.sigmoid(c)
    o_ref[0] = jnp.dot(cond, w_ref[0], preferred_element_type=F32,
                       precision=lax.Precision.HIGHEST) + b_ref[0]


def _ada(c, w_ada, b_ada):
    depth, d, n = w_ada.shape
    bsz = c.shape[0]
    tn = 1536
    return pl.pallas_call(
        _ada_kernel,
        grid=(depth, n // tn),
        in_specs=[pl.BlockSpec((bsz, d), lambda l, j: (0, 0)),
                  pl.BlockSpec((1, d, tn), lambda l, j: (l, 0, j)),
                  pl.BlockSpec((1, 1, tn), lambda l, j: (l, 0, j))],
        out_specs=pl.BlockSpec((1, bsz, tn), lambda l, j: (l, 0, j)),
        out_shape=jax.ShapeDtypeStruct((depth, bsz, n), F32),
        compiler_params=_params("arbitrary", "arbitrary"),
        name="ada_mod",
    )(c, w_ada, b_ada.reshape(depth, 1, n))


def _rope_tab_kernel(pos_ref, inv_ref, c_ref, sa_ref, sb_ref):
    pos = pos_ref[0].astype(F32)
    ang = pos * inv_ref[...]
    lane = lax.broadcasted_iota(jnp.int32, ang.shape, 1)
    cos, sin = jnp.cos(ang), jnp.sin(ang)
    lo = (lane >= MLA_NOPE_DIM) & (lane < MLA_NOPE_DIM + MLA_ROPE_DIM // 2)
    hi = (lane >= MLA_NOPE_DIM + MLA_ROPE_DIM // 2) & (lane < MLA_NOPE_DIM + MLA_ROPE_DIM)
    c_ref[0] = jnp.where(lane < MLA_NOPE_DIM, 1.0, jnp.where(lo | hi, cos, 0.0))
    sa_ref[0] = jnp.where(lo, -sin, 0.0)
    sb_ref[0] = jnp.where(hi, sin, 0.0)


def _rope_tables(positions):
    bsz, s = positions.shape
    half = MLA_ROPE_DIM // 2
    inv_freq = ROPE_THETA ** (-jnp.arange(half, dtype=F32) / half)
    inv_lane = jnp.concatenate([jnp.zeros((MLA_NOPE_DIM,), F32), inv_freq, inv_freq,
                                jnp.zeros((LANES - MLA_NOPE_DIM - MLA_ROPE_DIM,), F32)]).reshape(1, LANES)
    tm = TOK_TILE
    spec = pl.BlockSpec((1, tm, LANES), lambda b, i: (b, i, 0))
    shape = jax.ShapeDtypeStruct((bsz, s, LANES), F32)
    return pl.pallas_call(
        _rope_tab_kernel,
        grid=(bsz, s // tm),
        in_specs=[pl.BlockSpec((1, tm, 1), lambda b, i: (b, i, 0)),
                  pl.BlockSpec((1, LANES), lambda b, i: (0, 0))],
        out_specs=[spec, spec, spec],
        out_shape=[shape, shape, shape],
        compiler_params=_params("arbitrary", "arbitrary"),
        name="rope_tables",
    )(positions.reshape(bsz, s, 1), inv_lane)


def _bias_tile_kernel(rb_ref, o_ref):
    h, d = pl.program_id(0), pl.program_id(1)
    t = o_ref.shape[-1]
    key = lax.broadcasted_iota(jnp.int32, (t, t), 0)
    qry = lax.broadcasted_iota(jnp.int32, (t, t), 1)
    dist = d * t + qry - key
    n = jnp.maximum(dist, 0)
    nf = jnp.maximum(n, 1).astype(F32)
    large = MAX_EXACT + (jnp.log(nf / MAX_EXACT) / math.log(MAX_DISTANCE / MAX_EXACT)
                         * (N_BUCKETS - MAX_EXACT)).astype(jnp.int32)
    large = jnp.minimum(large, N_BUCKETS - 1)
    bucket = jnp.where(n < MAX_EXACT, n, large)
    val = jnp.zeros((t, t), F32)
    for j in range(N_BUCKETS):
        val = jnp.where(bucket == j, rb_ref[j, h], val)
    val = (rb_ref[N_BUCKETS - 1, h] - val) * LOG2E
    o_ref[0, 0] = jnp.where(dist < 0, -NEG_BIG, val)


def _bias_tiles(rel_bias):
    t = ATTN_TILE
    return pl.pallas_call(
        _bias_tile_kernel,
        grid=(DIFF_HEADS, 2),
        in_specs=[pl.BlockSpec(memory_space=pltpu.SMEM)],
        out_specs=pl.BlockSpec((1, 1, t, t), lambda h, d: (h, d, 0, 0)),
        out_shape=jax.ShapeDtypeStruct((DIFF_HEADS, 2, t, t), F32),
        compiler_params=_params("arbitrary", "arbitrary"),
        name="bias_tiles",
    )(rel_bias)


def _pre_attn_kernel(x_ref, mod_ref, g_ref, w_in_ref, gq_ref, w_uq_ref, gkv_ref, w_uk_ref, w_uv_ref,
                     rc_ref, rsa_ref, rsb_ref,
                     dq_ref, dk_ref, dv_ref, mq_ref, mk_ref, mv_ref):
    x = x_ref[0]
    sh, sc = mod_ref[0, 0:1, :], mod_ref[0, 1:2, :]
    h = (_rms(x, g_ref[...]) * (1.0 + sc) + sh).astype(BF16)
    proj = jnp.dot(h, w_in_ref[...], preferred_element_type=F32)
    dq_ref[0] = (proj[:, C_DQ:C_DK] * (DIFF_QK_DIM ** -0.5 * LOG2E)).astype(BF16)
    dk_ref[0] = proj[:, C_DK:C_DV].astype(BF16)
    dv_ref[0] = proj[:, C_DV:C_MQ].astype(BF16)

    rc, rsa, rsb = rc_ref[0], rsa_ref[0], rsb_ref[0]

    def rope(v):
        return v * rc + pltpu.roll(v, LANES - MLA_ROPE_DIM // 2, 1) * rsa + pltpu.roll(v, MLA_ROPE_DIM // 2, 1) * rsb

    qn = _rms(proj[:, C_MQ:C_KV], gq_ref[...]).astype(BF16)
    q = jnp.dot(qn, w_uq_ref[...], preferred_element_type=F32)
    kvn = _rms(proj[:, C_KV:C_KR], gkv_ref[...]).astype(BF16)
    kn = jnp.dot(kvn, w_uk_ref[...], preferred_element_type=F32)
    mv_ref[0] = jnp.dot(kvn, w_uv_ref[...], preferred_element_type=F32).astype(BF16)
    kr = rope(proj[:, C_KR:C_END])
    q_scale = (MLA_NOPE_DIM + MLA_ROPE_DIM) ** -0.5 * LOG2E
    for hd in range(MLA_HEADS):
        sl = slice(hd * LANES, (hd + 1) * LANES)
        mq_ref[0, :, sl] = (rope(q[:, sl]) * q_scale).astype(BF16)
        mk_ref[0, :, sl] = (kn[:, sl] + kr).astype(BF16)


def _pre_attn(x, mod, g, w_in, gq, w_uq, gkv, w_uk, w_uv, rc, rsa, rsb):
    bsz, s, d = x.shape
    tm = TOK_TILE
    tok = lambda w: pl.BlockSpec((1, tm, w), lambda b, i: (b, i, 0))
    full = lambda a: pl.BlockSpec(a.shape, lambda b, i: (0,) * a.ndim)
    widths = (512, 512, 512, MLA_HEADS * LANES, MLA_HEADS * LANES, MLA_HEADS * MLA_V_DIM)
    return pl.pallas_call(
        _pre_attn_kernel,
        grid=(bsz, s // tm),
        in_specs=[tok(d), pl.BlockSpec((1, ADA_CHUNKS, d), lambda b, i: (b, 0, 0)), full(g), full(w_in),
                  full(gq), full(w_uq), full(gkv), full(w_uk), full(w_uv), tok(LANES), tok(LANES), tok(LANES)],
        out_specs=[tok(w) for w in widths],
        out_shape=[jax.ShapeDtypeStruct((bsz, s, w), BF16) for w in widths],
        compiler_params=_params("arbitrary", "arbitrary"),
        name="pre_attn",
    )(x, mod, g, w_in, gq, w_uq, gkv, w_uk, w_uv, rc, rsa, rsb)


def _softmax_init(m_ref, l_ref, acc_ref):
    m_ref[...] = jnp.full(m_ref.shape, NEG_BIG, F32)
    l_ref[...] = jnp.zeros(l_ref.shape, F32)
    acc_ref[...] = jnp.zeros(acc_ref.shape, F32)


def _softmax_probs(st, m_ref, l_ref):
    m_prev = m_ref[...]
    m_new = jnp.maximum(m_prev, jnp.max(st, axis=0, keepdims=True))
    alpha = jnp.exp2(m_prev - m_new)
    p = jnp.exp2(st - m_new)
    l_ref[...] = alpha * l_ref[...] + jnp.sum(p, axis=0, keepdims=True)
    m_ref[...] = m_new
    return p.astype(BF16), alpha


def _acc_update(acc_ref, alpha, vt, p):
    acc_ref[...] = alpha * acc_ref[...] + jnp.dot(vt, p, preferred_element_type=F32)


def _attn_scratch(n_chains, dv, t):
    return [pltpu.VMEM((n_chains, 1, t), F32), pltpu.VMEM((n_chains, 1, t), F32), pltpu.VMEM((n_chains, dv, t), F32),
            pltpu.VMEM((AHEAD, t, t), F32)]


def _attn_prologue(n_chains, scores, refs):
    m_ref, l_ref, acc_ref, pend_ref = refs
    for c in range(n_chains):
        _softmax_init(m_ref.at[c], l_ref.at[c], acc_ref.at[c])
    for c in range(AHEAD):
        pend_ref[c] = scores(0, c)


def _attn_step(n_chains, scores, adjust, values, refs, j, has_next):
    m_ref, l_ref, acc_ref, pend_ref = refs
    pending = [pend_ref[c] for c in range(AHEAD)]
    for c in range(n_chains):
        nxt = c + AHEAD
        if nxt < n_chains:
            pending.append(scores(j, nxt))
        elif has_next:
            pend_ref[nxt - n_chains] = scores(j + 1, nxt - n_chains)
        p, alpha = _softmax_probs(adjust(c, pending.pop(0)), m_ref.at[c], l_ref.at[c])
        _acc_update(acc_ref.at[c], alpha, values(j, c), p)


def _diff_attn_kernel(lam_init_ref, lam_ref, g_ref, bias_ref, q_ref, k_ref, v_ref, o_ref,
                      vt_ref, *refs):
    i = pl.program_id(1)
    t = q_ref.shape[1]
    _, l_ref, acc_ref = refs[:3]

    @pl.when(i == 0)
    def _():
        vt_ref[...] = v_ref[0].T

    n_maps = 2 * DIFF_HEADS
    qts = []
    for h in range(DIFF_HEADS):
        qt = q_ref[0, :, h * LANES:(h + 1) * LANES].T
        row = lax.broadcasted_iota(jnp.int32, qt.shape, 0)
        qts.append(jnp.where(row < DIFF_QK_DIM, qt, jnp.zeros_like(qt)))
        qts.append(jnp.where(row >= DIFF_QK_DIM, qt, jnp.zeros_like(qt)))

    def scores(j, c):
        k = k_ref[0, pl.ds(pl.multiple_of(j * t, t), t), (c // 2) * LANES:(c // 2 + 1) * LANES]
        return jnp.dot(k, qts[c], preferred_element_type=F32)

    def values(j, c):
        return vt_ref[(c // 2) * LANES:(c // 2 + 1) * LANES, pl.ds(pl.multiple_of(j * t, t), t)]

    def step(j, bias_idx, has_next):
        def adjust(c, st):
            if bias_idx is None:
                return st
            return st - bias_ref[c // 2, bias_idx]

        _attn_step(n_maps, scores, adjust, values, refs, j, has_next)

    def far(j, carry):
        step(j, None, True)
        return carry

    _attn_prologue(n_maps, scores, refs)
    lax.fori_loop(0, jnp.maximum(i - 1, 0), far, 0)

    @pl.when(i >= 1)
    def _():
        step(i - 1, 1, True)

    step(i, 0, False)

    lv = lam_ref[...]
    lam = (jnp.exp(jnp.sum(lv[0:1] * lv[1:2], keepdims=True)) - jnp.exp(jnp.sum(lv[2:3] * lv[3:4], keepdims=True))
           + lam_init_ref[0])
    for h in range(DIFF_HEADS):
        c0, c1 = 2 * h, 2 * h + 1
        ot = acc_ref[c0] / l_ref[c0] - lam * (acc_ref[c1] / l_ref[c1])
        ot = ot * lax.rsqrt(jnp.mean(ot * ot, axis=0, keepdims=True) + NORM_EPS) * g_ref[...]
        o_ref[0, :, h * LANES:(h + 1) * LANES] = (ot * (1.0 - lam_init_ref[0])).T.astype(BF16)


def _diff_attn(lam_init, diff_lambda, g, bias, dq, dk, dv):
    bsz, s, w = dq.shape
    t = ATTN_TILE
    assert MAX_DISTANCE <= t and s % t == 0
    n_maps = 2 * DIFF_HEADS
    return pl.pallas_call(
        _diff_attn_kernel,
        grid=(bsz, s // t),
        in_specs=[pl.BlockSpec(memory_space=pltpu.SMEM),
                  pl.BlockSpec(diff_lambda.shape, lambda b, i: (0, 0)),
                  pl.BlockSpec(g.shape, lambda b, i: (0, 0)),
                  pl.BlockSpec(bias.shape, lambda b, i: (0, 0, 0, 0)),
                  pl.BlockSpec((1, t, w), lambda b, i: (b, i, 0)),
                  pl.BlockSpec((1, s, w), lambda b, i: (b, 0, 0)),
                  pl.BlockSpec((1, s, w), lambda b, i: (b, 0, 0))],
        out_specs=pl.BlockSpec((1, t, w), lambda b, i: (b, i, 0)),
        out_shape=jax.ShapeDtypeStruct(dq.shape, BF16),
        scratch_shapes=[pltpu.VMEM((w, s), BF16)] + _attn_scratch(n_maps, LANES, t),
        compiler_params=_params("arbitrary", "arbitrary"),
        name="diff_attn",
    )(lam_init, diff_lambda, g, bias, dq, dk, dv)


def _mla_attn_kernel(q_ref, k_ref, v_ref, o_ref, vt_ref, *refs):
    i = pl.program_id(1)
    t = q_ref.shape[1]
    _, l_ref, acc_ref = refs[:3]

    @pl.when(i == 0)
    def _():
        vt_ref[...] = v_ref[0].T

    qts = [q_ref[0, :, h * LANES:(h + 1) * LANES].T for h in range(MLA_HEADS)]

    def scores(j, h):
        k = k_ref[0, pl.ds(pl.multiple_of(j * t, t), t), h * LANES:(h + 1) * LANES]
        return jnp.dot(k, qts[h], preferred_element_type=F32)

    def values(j, h):
        return vt_ref[h * MLA_V_DIM:(h + 1) * MLA_V_DIM, pl.ds(pl.multiple_of(j * t, t), t)]

    def step(j, masked, has_next):
        def adjust(h, st):
            if not masked:
                return st
            key = lax.broadcasted_iota(jnp.int32, st.shape, 0)
            qry = lax.broadcasted_iota(jnp.int32, st.shape, 1)
            return jnp.where(key <= qry, st, NEG_BIG)

        _attn_step(MLA_HEADS, scores, adjust, values, refs, j, has_next)

    def far(j, carry):
        step(j, False, True)
        return carry

    _attn_prologue(MLA_HEADS, scores, refs)
    lax.fori_loop(0, i, far, 0)
    step(i, True, False)
    for u in range(MLA_HEADS // 2):
        ot = jnp.concatenate([acc_ref[2 * u] / l_ref[2 * u], acc_ref[2 * u + 1] / l_ref[2 * u + 1]], axis=0)
        o_ref[0, :, u * LANES:(u + 1) * LANES] = ot.T.astype(BF16)


def _mla_attn(mq, mk, mv):
    bsz, s, wq = mq.shape
    wv = mv.shape[-1]
    t = ATTN_TILE
    return pl.pallas_call(
        _mla_attn_kernel,
        grid=(bsz, s // t),
        in_specs=[pl.BlockSpec((1, t, wq), lambda b, i: (b, i, 0)),
                  pl.BlockSpec((1, s, wq), lambda b, i: (b, 0, 0)),
                  pl.BlockSpec((1, s, wv), lambda b, i: (b, 0, 0))],
        out_specs=pl.BlockSpec((1, t, wv), lambda b, i: (b, i, 0)),
        out_shape=jax.ShapeDtypeStruct(mv.shape, BF16),
        scratch_shapes=[pltpu.VMEM((wv, s), BF16)] + _attn_scratch(MLA_HEADS, MLA_V_DIM, t),
        compiler_params=_params("arbitrary", "arbitrary"),
        name="mla_attn",
    )(mq, mk, mv)


def _attn_kernel(lam_init_ref, lam_ref, g_ref, bias_ref, dq_ref, dk_ref, dv_ref, mq_ref, mk_ref, mv_ref, o_ref,
                 dvt_ref, mvt_ref, m_ref, l_ref, dacc_ref, macc_ref, pend_ref):
    i = pl.program_id(1)
    t = dq_ref.shape[1]
    n_diff = 2 * DIFF_HEADS
    n_chains = n_diff + MLA_HEADS

    @pl.when(i == 0)
    def _():
        dvt_ref[...] = dv_ref[0].T
        mvt_ref[...] = mv_ref[0].T

    qts = []
    for h in range(DIFF_HEADS):
        qt = dq_ref[0, :, h * LANES:(h + 1) * LANES].T
        row = lax.broadcasted_iota(jnp.int32, qt.shape, 0)
        qts.append(jnp.where(row < DIFF_QK_DIM, qt, jnp.zeros_like(qt)))
        qts.append(jnp.where(row >= DIFF_QK_DIM, qt, jnp.zeros_like(qt)))
    qts += [mq_ref[0, :, h * LANES:(h + 1) * LANES].T for h in range(MLA_HEADS)]

    def scores(j, c):
        rows = pl.ds(pl.multiple_of(j * t, t), t)
        if c < n_diff:
            k = dk_ref[0, rows, (c // 2) * LANES:(c // 2 + 1) * LANES]
        else:
            k = mk_ref[0, rows, (c - n_diff) * LANES:(c - n_diff + 1) * LANES]
        return jnp.dot(k, qts[c], preferred_element_type=F32)

    def values(j, c):
        cols = pl.ds(pl.multiple_of(j * t, t), t)
        if c < n_diff:
            return dvt_ref[(c // 2) * LANES:(c // 2 + 1) * LANES, cols]
        return mvt_ref[(c - n_diff) * MLA_V_DIM:(c - n_diff + 1) * MLA_V_DIM, cols]

    def acc_at(c):
        return dacc_ref.at[c] if c < n_diff else macc_ref.at[c - n_diff]

    def step(j, bias_idx, has_next):
        hidden = None
        if bias_idx == 0:
            key = lax.broadcasted_iota(jnp.int32, (t, t), 0)
            qry = lax.broadcasted_iota(jnp.int32, (t, t), 1)
            hidden = jnp.where(key <= qry, 0.0, -NEG_BIG)

        def adjust(c, st):
            if c < n_diff:
                return st if bias_idx is None else st - bias_ref[c // 2, bias_idx]
            return st if hidden is None else st - hidden

        pending = [pend_ref[c] for c in range(AHEAD)]
        for c in range(n_chains):
            nxt = c + AHEAD
            if nxt < n_chains:
                pending.append(scores(j, nxt))
            elif has_next:
                pend_ref[nxt - n_chains] = scores(j + 1, nxt - n_chains)
            p, alpha = _softmax_probs(adjust(c, pending.pop(0)), m_ref.at[c], l_ref.at[c])
            _acc_update(acc_at(c), alpha, values(j, c), p)

    def far(j, carry):
        step(j, None, True)
        return carry

    for c in range(n_chains):
        _softmax_init(m_ref.at[c], l_ref.at[c], acc_at(c))
    for c in range(AHEAD):
        pend_ref[c] = scores(0, c)
    lax.fori_loop(0, jnp.maximum(i - 1, 0), far, 0)

    @pl.when(i >= 1)
    def _():
        step(i - 1, 1, True)

    step(i, 0, False)

    lv = lam_ref[...]
    lam = (jnp.exp(jnp.sum(lv[0:1] * lv[1:2], keepdims=True)) - jnp.exp(jnp.sum(lv[2:3] * lv[3:4], keepdims=True))
           + lam_init_ref[0])
    for h in range(DIFF_HEADS):
        c0, c1 = 2 * h, 2 * h + 1
        ot = dacc_ref[c0] / l_ref[c0] - lam * (dacc_ref[c1] / l_ref[c1])
        ot = ot * lax.rsqrt(jnp.mean(ot * ot, axis=0, keepdims=True) + NORM_EPS) * g_ref[...]
        o_ref[0, :, h * LANES:(h + 1) * LANES] = (ot * (1.0 - lam_init_ref[0])).T.astype(BF16)
    base = DIFF_HEADS * LANES
    for u in range(MLA_HEADS // 2):
        ot = jnp.concatenate([macc_ref[2 * u] / l_ref[n_diff + 2 * u],
                              macc_ref[2 * u + 1] / l_ref[n_diff + 2 * u + 1]], axis=0)
        o_ref[0, :, base + u * LANES:base + (u + 1) * LANES] = ot.T.astype(BF16)


def _attention(lam_init, diff_lambda, g, bias, dq, dk, dv, mq, mk, mv):
    bsz, s, wd = dq.shape
    wq, wv = mq.shape[-1], mv.shape[-1]
    t = ATTN_TILE
    assert MAX_DISTANCE <= t and s % t == 0
    n_diff = 2 * DIFF_HEADS
    n_chains = n_diff + MLA_HEADS
    qtile = lambda w: pl.BlockSpec((1, t, w), lambda b, i: (b, i, 0))
    whole = lambda w: pl.BlockSpec((1, s, w), lambda b, i: (b, 0, 0))
    return pl.pallas_call(
        _attn_kernel,
        grid=(bsz, s // t),
        in_specs=[pl.BlockSpec(memory_space=pltpu.SMEM),
                  pl.BlockSpec(diff_lambda.shape, lambda b, i: (0, 0)),
                  pl.BlockSpec(g.shape, lambda b, i: (0, 0)),
                  pl.BlockSpec(bias.shape, lambda b, i: (0, 0, 0, 0)),
                  qtile(wd), whole(wd), whole(wd), qtile(wq), whole(wq), whole(wv)],
        out_specs=qtile(wd + wv),
        out_shape=jax.ShapeDtypeStruct((bsz, s, wd + wv), BF16),
        scratch_shapes=[pltpu.VMEM((wd, s), BF16), pltpu.VMEM((wv, s), BF16),
                        pltpu.VMEM((n_chains, 1, t), F32), pltpu.VMEM((n_chains, 1, t), F32),
                        pltpu.VMEM((n_diff, LANES, t), F32), pltpu.VMEM((MLA_HEADS, MLA_V_DIM, t), F32),
                        pltpu.VMEM((AHEAD, t, t), F32)],
        compiler_params=_params("arbitrary", "arbitrary"),
        name="attention",
    )(lam_init, diff_lambda, g, bias, dq, dk, dv, mq, mk, mv)


def _post_attn_kernel(*refs, moe):
    if moe:
        o_ref, x_ref, mod_ref, g_ref, wo_ref, wr_ref, x1_ref, h_ref, gates_ref, sel_ref = refs
    else:
        o_ref, x_ref, mod_ref, g_ref, wo_ref, x1_ref, h_ref = refs
    y = jnp.dot(o_ref[0], wo_ref[...], preferred_element_type=F32)
    gt_a, sh_f, sc_f = mod_ref[0, 2:3, :], mod_ref[0, 3:4, :], mod_ref[0, 4:5, :]
    x1 = x_ref[0] + (1.0 + gt_a) * y
    x1_ref[0] = x1
    h = _rms(x1, g_ref[...]) * (1.0 + sc_f) + sh_f
    if not moe:
        h_ref[0] = h.astype(BF16)
        return
    _rows_to_tiles(h, h_ref.at[0])
    wr = wr_ref[...]
    h_hi, wr_hi = h.astype(BF16), wr.astype(BF16)
    h_lo, wr_lo = (h - h_hi.astype(F32)).astype(BF16), (wr - wr_hi.astype(F32)).astype(BF16)
    logits = (jnp.dot(h_hi, wr_hi, preferred_element_type=F32) + jnp.dot(h_hi, wr_lo, preferred_element_type=F32)
              + jnp.dot(h_lo, wr_hi, preferred_element_type=F32))
    lane = lax.broadcasted_iota(jnp.int32, logits.shape, 1)
    logits = jnp.where(lane < N_EXPERTS, logits, -jnp.inf)
    v1 = jnp.max(logits, axis=1, keepdims=True)
    i1 = jnp.min(jnp.where(logits == v1, lane, LANES), axis=1, keepdims=True)
    rest = jnp.where(lane == i1, -jnp.inf, logits)
    v2 = jnp.max(rest, axis=1, keepdims=True)
    i2 = jnp.min(jnp.where(rest == v2, lane, LANES), axis=1, keepdims=True)
    e2 = jnp.exp(v2 - v1)
    w1 = 1.0 / (1.0 + e2)
    w2 = e2 / (1.0 + e2)
    gates_ref[0] = jnp.where(lane == i1, w1, 0.0) + jnp.where(lane == i2, w2, 0.0)
    sel_ref[0] = jnp.where(lane == i1, 1.0, 0.0) + jnp.where(lane == i2, 2.0, 0.0)


def _post_attn(o, x, mod, g, wo, w_router=None):
    bsz, s, d = x.shape
    moe = w_router is not None
    tm = TOK_TILE
    tok = lambda w: pl.BlockSpec((1, tm, w), lambda b, i: (b, i, 0))
    full = lambda a: pl.BlockSpec(a.shape, lambda b, i: (0,) * a.ndim)
    args = [o, x, mod, g, wo] + ([w_router] if moe else [])
    in_specs = [tok(o.shape[-1]), tok(d), pl.BlockSpec((1, ADA_CHUNKS, d), lambda b, i: (b, 0, 0)),
                full(g), full(wo)] + ([full(w_router)] if moe else [])
    if moe:
        assert d == ROW_SUB * LANES
        h_spec = pl.BlockSpec((1, tm * ROW_SUB, LANES), lambda b, i: (b, i, 0))
        out_specs = [tok(d), h_spec, tok(LANES), tok(LANES)]
        out_shape = [jax.ShapeDtypeStruct((bsz, s, d), F32), jax.ShapeDtypeStruct((bsz, s * ROW_SUB, LANES), F32)]
        out_shape += [jax.ShapeDtypeStruct((bsz, s, LANES), F32)] * 2
    else:
        out_specs = [tok(d), tok(d)]
        out_shape = [jax.ShapeDtypeStruct((bsz, s, d), F32), jax.ShapeDtypeStruct((bsz, s, d), BF16)]
    return pl.pallas_call(
        functools.partial(_post_attn_kernel, moe=moe),
        grid=(bsz, s // tm),
        in_specs=in_specs,
        out_specs=out_specs,
        out_shape=out_shape,
        compiler_params=_params("arbitrary", "arbitrary"),
        name="post_attn_moe" if moe else "post_attn",
    )(*args)


def _swiglu(h, w1_ref, w3_ref, w2_ref):
    d_ff = w1_ref.shape[-1]
    y = jnp.zeros((h.shape[0], w2_ref.shape[-1]), F32)
    for c0 in range(0, d_ff, FF_CHUNK):
        a = jnp.dot(h, w1_ref[:, c0:c0 + FF_CHUNK], preferred_element_type=F32)
        b = jnp.dot(h, w3_ref[:, c0:c0 + FF_CHUNK], preferred_element_type=F32)
        u = (a * jax.nn.sigmoid(a) * b).astype(BF16)
        y = y + jnp.dot(u, w2_ref[c0:c0 + FF_CHUNK, :], preferred_element_type=F32)
    return y


def _ffn_dense_kernel(h_ref, x1_ref, mod_ref, w1_ref, w3_ref, w2_ref, o_ref):
    gt_f = mod_ref[0, 5:6, :]
    o_ref[0] = x1_ref[0] + (1.0 + gt_f) * _swiglu(h_ref[0], w1_ref, w3_ref, w2_ref)


def _ffn_dense(h, x1, mod, w1, w3, w2):
    bsz, s, d = x1.shape
    tm = FFN_TILE
    tok = lambda w: pl.BlockSpec((1, tm, w), lambda b, i: (b, i, 0))
    full = lambda a: pl.BlockSpec(a.shape, lambda b, i: (0,) * a.ndim)
    return pl.pallas_call(
        _ffn_dense_kernel,
        grid=(bsz, s // tm),
        in_specs=[tok(d), tok(d), pl.BlockSpec((1, ADA_CHUNKS, d), lambda b, i: (b, 0, 0)),
                  full(w1), full(w3), full(w2)],
        out_specs=tok(d),
        out_shape=jax.ShapeDtypeStruct(x1.shape, F32),
        compiler_params=_params("arbitrary", "arbitrary"),
        name="ffn_dense",
    )(h, x1, mod, w1, w3, w2)


def _route_kernel(sel_ref, pos_ref, te_ref, cnt_ref, off_ref, run_ref, *, row_tile):
    p, t = pl.program_id(0), pl.program_id(1)
    sel_t = sel_ref[...].T
    chosen = (sel_t > 0.0).astype(F32)
    per_expert = jnp.sum(chosen, axis=1, keepdims=True)

    @pl.when((p == 0) & (t == 0))
    def _():
        cnt_ref[...] = jnp.zeros(cnt_ref.shape, F32)
        te_ref[...] = jnp.zeros(te_ref.shape, jnp.int32)

    @pl.when(p == 0)
    def _():
        cnt_ref[...] += per_expert

    @pl.when((p == 1) & (t == 0))
    def _():
        cnt = cnt_ref[...]
        padded = jnp.ceil(cnt / row_tile) * row_tile
        row = lax.broadcasted_iota(jnp.int32, cnt.shape, 0)
        off = jnp.zeros(cnt.shape, F32)
        for e in range(N_EXPERTS):
            size_e = jnp.sum(jnp.where(row == e, padded, 0.0), keepdims=True)
            off = off + jnp.where(row > e, size_e, 0.0)
        off_ref[...] = off
        run_ref[...] = jnp.zeros(run_ref.shape, F32)
        ends = off + padded
        tile_start = lax.broadcasted_iota(jnp.int32, (LANES, LANES), 1).astype(F32) * row_tile
        erow = lax.broadcasted_iota(jnp.int32, (LANES, LANES), 0)
        done = jnp.where((erow < N_EXPERTS) & (ends <= tile_start), 1.0, 0.0)
        te = jnp.sum(done, axis=0, keepdims=True).astype(jnp.int32)
        te_ref[...] = jnp.broadcast_to(te, te_ref.shape)

    @pl.when(p == 1)
    def _():
        tm = sel_t.shape[1]
        before = (lax.broadcasted_iota(jnp.int32, (tm, tm), 0)
                  < lax.broadcasted_iota(jnp.int32, (tm, tm), 1)).astype(BF16)
        rank = jnp.dot(chosen.astype(BF16), before, preferred_element_type=F32) + run_ref[...]
        base = off_ref[...] + rank
        for k in range(2):
            pos = jnp.sum(jnp.where(sel_t == float(k + 1), base, 0.0), axis=0, keepdims=True)
            pos_ref[0, k:k + 1, :] = pos.astype(jnp.int32)
        run_ref[...] += per_expert


def _route(sel, row_tile, n_row_tiles):
    n_tok = sel.shape[0]
    assert n_row_tiles <= LANES and 2 * n_tok < 2 ** 24
    tm = TOK_TILE
    nt = n_tok // tm
    pos, te = pl.pallas_call(
        functools.partial(_route_kernel, row_tile=row_tile),
        grid=(2, nt),
        in_specs=[pl.BlockSpec((tm, LANES), lambda p, t: (t, 0))],
        out_specs=[pl.BlockSpec((1, 2, tm), lambda p, t: (p * t, 0, 0)),
                   pl.BlockSpec((8, LANES), lambda p, t: (0, 0))],
        out_shape=[jax.ShapeDtypeStruct((nt, 2, tm), jnp.int32), jax.ShapeDtypeStruct((8, LANES), jnp.int32)],
        scratch_shapes=[pltpu.VMEM((LANES, 1), F32)] * 3,
        compiler_params=_params("arbitrary", "arbitrary"),
        name="moe_route",
    )(sel)
    return pos, te[0]


def _dispatch_kernel(pos_ref, h_ref, xs_in_ref, xs_ref, sem):
    del xs_in_ref
    tm = h_ref.shape[0] // ROW_SUB

    def body(r, carry):
        for k in range(2):
            pltpu.make_async_copy(_tile_row(h_ref, r), _tile_row(xs_ref, pos_ref[0, k, r]), sem).start(priority=k)
        return carry

    lax.fori_loop(0, tm, body, 0, unroll=8)
    for k in range(2):
        pltpu.make_async_copy(h_ref, xs_ref.at[pl.ds(0, tm * ROW_SUB)], sem).wait()


def _dispatch(pos, h, xs0):
    nt, _, tm = pos.shape
    return pl.pallas_call(
        _dispatch_kernel,
        grid=(nt,),
        in_specs=[pl.BlockSpec((1, 2, tm), lambda t: (t, 0, 0), memory_space=pltpu.SMEM),
                  pl.BlockSpec((tm * ROW_SUB, LANES), lambda t: (t, 0)),
                  pl.BlockSpec(memory_space=pl.ANY)],
        out_specs=pl.BlockSpec(memory_space=pl.ANY),
        out_shape=jax.ShapeDtypeStruct(xs0.shape, xs0.dtype),
        scratch_shapes=[pltpu.SemaphoreType.DMA(())],
        input_output_aliases={2: 0},
        compiler_params=_params("arbitrary"),
        name="moe_dispatch",
    )(pos, h, xs0)


def _expert_kernel(te_ref, xs_ref, w1_ref, w3_ref, w2_ref, y_ref):
    used = te_ref[pl.program_id(0)] < N_EXPERTS

    @pl.when(used)
    def _():
        y = _swiglu(_tiles_to_rows(xs_ref).astype(BF16), w1_ref.at[0], w3_ref.at[0], w2_ref.at[0])
        _rows_to_tiles(y, y_ref)

    @pl.when(jnp.logical_not(used))
    def _():
        y_ref[...] = jnp.zeros(y_ref.shape, y_ref.dtype)


def _experts(te, xs, w1, w3, w2, row_tile, first_expert):
    n_rows = xs.shape[0] // ROW_SUB
    _, d, d_ff = w1.shape
    expert = lambda n, te: (first_expert + jnp.minimum(te[n], N_EXPERTS - 1), 0, 0)
    rows = pl.BlockSpec((row_tile * ROW_SUB, LANES), lambda n, te: (n, 0))
    return pl.pallas_call(
        _expert_kernel,
        grid_spec=pltpu.PrefetchScalarGridSpec(
            num_scalar_prefetch=1,
            grid=(n_rows // row_tile,),
            in_specs=[rows, pl.BlockSpec((1, d, d_ff), expert), pl.BlockSpec((1, d, d_ff), expert),
                      pl.BlockSpec((1, d_ff, d), expert)],
            out_specs=rows),
        out_shape=jax.ShapeDtypeStruct(xs.shape, F32),
        compiler_params=_params("arbitrary"),
        name="moe_experts",
    )(te, xs, w1, w3, w2)


def _combine_kernel(pos_ref, x1_ref, gates_ref, sel_ref, mod_ref, gfin_ref, ys_ref, o_ref, ya_ref, yb_ref, sem,
                    *, final):
    tm = x1_ref.shape[1]
    bufs = (ya_ref, yb_ref)

    def body(r, carry):
        for k in range(2):
            pltpu.make_async_copy(_tile_row(ys_ref, pos_ref[0, k, r]), _tile_row(bufs[k], r), sem).start(priority=k)
        return carry

    lax.fori_loop(0, tm, body, 0, unroll=8)
    gates, sel = gates_ref[0], sel_ref[0]
    w_a = jnp.sum(jnp.where(sel == 1.0, gates, 0.0), axis=1, keepdims=True)
    w_b = jnp.sum(jnp.where(sel == 2.0, gates, 0.0), axis=1, keepdims=True)
    for k in range(2):
        pltpu.make_async_copy(ys_ref.at[pl.ds(0, tm * ROW_SUB)], bufs[k], sem).wait()
    gt_f = mod_ref[0, 5:6, :]
    x = x1_ref[0] + (1.0 + gt_f) * (w_a * _tiles_to_rows(ya_ref) + w_b * _tiles_to_rows(yb_ref))
    o_ref[0] = _rms(x, gfin_ref[...]) if final else x


def _combine(pos, x1, gates, sel, mod, ys, g_final=None):
    bsz, s, d = x1.shape
    nt, _, tm = pos.shape
    per_b = s // tm
    tok = lambda w: pl.BlockSpec((1, tm, w), lambda b, i: (b, i, 0))
    final = g_final is not None
    gfin = g_final if final else jnp.ones((1, d), F32)
    return pl.pallas_call(
        functools.partial(_combine_kernel, final=final),
        grid=(bsz, per_b),
        in_specs=[pl.BlockSpec((1, 2, tm), lambda b, i: (b * per_b + i, 0, 0), memory_space=pltpu.SMEM),
                  tok(d), tok(LANES), tok(LANES),
                  pl.BlockSpec((1, ADA_CHUNKS, d), lambda b, i: (b, 0, 0)),
                  pl.BlockSpec((1, d), lambda b, i: (0, 0)),
                  pl.BlockSpec(memory_space=pl.ANY)],
        out_specs=tok(d),
        out_shape=jax.ShapeDtypeStruct(x1.shape, F32),
        scratch_shapes=[pltpu.VMEM((tm * ROW_SUB, LANES), F32), pltpu.VMEM((tm * ROW_SUB, LANES), F32),
                        pltpu.SemaphoreType.DMA(())],
        compiler_params=_params("arbitrary", "arbitrary"),
        name="moe_combine",
    )(pos, x1, gates, sel, mod, gfin, ys)


def _moe(h, x1, gates, sel, mod, w1, w3, w2, first_expert, xs_buf, g_final):
    bsz, s, d = x1.shape
    n_tok = bsz * s
    row_tile = FFN_TILE
    n_rows = xs_buf.shape[0] // ROW_SUB
    assert n_rows == _moe_rows(n_tok)
    pos, te = _route(sel.reshape(n_tok, LANES), row_tile, n_rows // row_tile)
    xs = _dispatch(pos, h.reshape(n_tok * ROW_SUB, LANES), xs_buf)
    ys = _experts(te, xs, w1, w3, w2, row_tile, first_expert)
    return _combine(pos, x1, gates, sel, mod, ys, g_final), xs


def _moe_rows(n_tok):
    return 2 * n_tok + N_EXPERTS * FFN_TILE


def _final_norm_kernel(x_ref, g_ref, o_ref):
    o_ref[0] = _rms(x_ref[0], g_ref[...])


def _final_norm(x, g):
    bsz, s, d = x.shape
    tm = TOK_TILE
    tok = pl.BlockSpec((1, tm, d), lambda b, i: (b, i, 0))
    return pl.pallas_call(
        _final_norm_kernel,
        grid=(bsz, s // tm),
        in_specs=[tok, pl.BlockSpec(g.shape, lambda b, i: (0, 0))],
        out_specs=tok,
        out_shape=jax.ShapeDtypeStruct(x.shape, F32),
        compiler_params=_params("arbitrary", "arbitrary"),
        name="final_norm",
    )(x, g)


def _pad_heads(w, heads, width, lo, hi):
    k = w.shape[0]
    w = w.reshape(k, heads, width)[:, :, lo:hi]
    return jnp.pad(w, ((0, 0), (0, 0), (0, LANES - (hi - lo)))).reshape(k, heads * LANES)


def _prep_w_in(w_in):
    d = w_in.shape[0]
    kr = w_in[:, C_KR:]
    kr_block = jnp.concatenate([jnp.zeros((d, MLA_NOPE_DIM), w_in.dtype), kr,
                                jnp.zeros((d, LANES - MLA_NOPE_DIM - MLA_ROPE_DIM), w_in.dtype)], axis=1)
    return jnp.concatenate([w_in[:, :C_KR], kr_block], axis=1).astype(BF16)


def kernel(x, c, positions, w_ada, b_ada, g_attn, w_in, diff_lambda, diff_subln_g, rel_bias, mla_q_norm, w_uq, mla_kv_norm, w_ukv, w_o, g_ffn, ffn_w1, ffn_w3, ffn_w2, moe_router, moe_w1, moe_w3, moe_w2, g_final):
    depth = w_ada.shape[0]
    bsz, s, d = x.shape
    mods = _ada(c, w_ada, b_ada).reshape(depth, bsz, ADA_CHUNKS, d)
    rc, rsa, rsb = _rope_tables(positions)
    bias = _bias_tiles(rel_bias)
    qk_w = MLA_NOPE_DIM + MLA_ROPE_DIM
    kv_w = MLA_NOPE_DIM + MLA_V_DIM
    d_ff = moe_w1.shape[-1]
    moe_w1_b = moe_w1.astype(BF16).reshape(-1, d, d_ff)
    moe_w3_b = moe_w3.astype(BF16).reshape(-1, d, d_ff)
    moe_w2_b = moe_w2.astype(BF16).reshape(-1, d_ff, d)
    xs_buf = jnp.zeros((_moe_rows(bsz * s) * ROW_SUB, LANES), F32)
    for l in range(depth):
        mod = mods[l]
        lam_init = jnp.full((1,), 0.8 - 0.6 * math.exp(-0.3 * l), F32)
        w_uq_p = _pad_heads(w_uq[l], MLA_HEADS, qk_w, 0, qk_w).astype(BF16)
        w_uk_p = _pad_heads(w_ukv[l], MLA_HEADS, kv_w, 0, MLA_NOPE_DIM).astype(BF16)
        w_uv_p = w_ukv[l].reshape(MLA_KV_RANK, MLA_HEADS, kv_w)[:, :, MLA_NOPE_DIM:].reshape(
            MLA_KV_RANK, MLA_HEADS * MLA_V_DIM).astype(BF16)
        dq, dk, dv, mq, mk, mv = _pre_attn(
            x, mod, g_attn[l].reshape(1, d), _prep_w_in(w_in[l]), mla_q_norm[l].reshape(1, -1), w_uq_p,
            mla_kv_norm[l].reshape(1, -1), w_uk_p, w_uv_p, rc, rsa, rsb)
        o = _attention(lam_init, diff_lambda[l], diff_subln_g[l].reshape(-1, 1), bias, dq, dk, dv, mq, mk, mv)
        wo = w_o[l].astype(BF16)
        g_f = g_ffn[l].reshape(1, d)
        if l % 2 == 1:
            w_router = jnp.pad(moe_router[l // 2], ((0, 0), (0, LANES - N_EXPERTS)))
            x1, h, gates, sel = _post_attn(o, x, mod, g_f, wo, w_router)
            g_fin = g_final.reshape(1, d) if l == depth - 1 else None
            x, xs_buf = _moe(h, x1, gates, sel, mod, moe_w1_b, moe_w3_b, moe_w2_b, (l // 2) * N_EXPERTS,
                             xs_buf, g_fin)
        else:
            x1, h = _post_attn(o, x, mod, g_f, wo)
            x = _ffn_dense(h, x1, mod, ffn_w1[l // 2].astype(BF16), ffn_w3[l // 2].astype(BF16),
                           ffn_w2[l // 2].astype(BF16))
    return x if depth % 2 == 0 else _final_norm(x, g_final.reshape(1, d))
```

```python
import functools
import math

import jax
import jax.numpy as jnp
from jax import lax
from jax.experimental import pallas as pl
from jax.experimental.pallas import tpu as pltpu

F32 = jnp.float32
BF16 = jnp.bfloat16

DIFF_HEADS = 4
DIFF_QK_DIM = 64
DIFF_V_DIM = 128
MLA_HEADS = 8
MLA_NOPE_DIM = 64
MLA_ROPE_DIM = 32
MLA_V_DIM = 64
MLA_Q_RANK = 384
MLA_KV_RANK = 256
ROPE_THETA = 10000.0
N_BUCKETS = 32
MAX_EXACT = 16
MAX_DISTANCE = 128
N_EXPERTS = 8
NORM_EPS = 1e-6
ADA_CHUNKS = 6

LANES = 128
NEG_BIG = -1e30
LOG2E = math.log2(math.e)
VMEM_LIMIT = 56 * 1024 * 1024

ATTN_TILE = 256
AHEAD = 4
TOK_TILE = 512
FFN_TILE = 512
FF_CHUNK = 256

C_DQ, C_DK, C_DV, C_MQ, C_KV, C_KR, C_END = 0, 512, 1024, 1536, 1920, 2176, 2304


def _params(*sem):
    return pltpu.CompilerParams(dimension_semantics=sem, vmem_limit_bytes=VMEM_LIMIT)


def _rms(x, g):
    return x * lax.rsqrt(jnp.mean(x * x, axis=-1, keepdims=True) + NORM_EPS) * g


ROW_SUB = 8


def _rows_to_tiles(x, ref):
    n = x.shape[0]
    for j in range(ROW_SUB):
        ref[pl.ds(j, n, stride=ROW_SUB), :] = x[:, j * LANES:(j + 1) * LANES]


def _tiles_to_rows(ref):
    n = ref.shape[0] // ROW_SUB
    return jnp.concatenate([ref[pl.ds(j, n, stride=ROW_SUB), :] for j in range(ROW_SUB)], axis=1)


def _tile_row(ref, r):
    return ref.at[pl.ds(pl.multiple_of(r * ROW_SUB, ROW_SUB), ROW_SUB)]


def _ada_kernel(c_ref, w_ref, b_ref, o_ref):
    c = c_ref[...]
    cond = c * jax.nn.sigmoid(c)
    o_ref[0] = jnp.dot(cond, w_ref[0], preferred_element_type=F32,
                       precision=lax.Precision.HIGHEST) + b_ref[0]


def _ada(c, w_ada, b_ada):
    depth, d, n = w_ada.shape
    bsz = c.shape[0]
    tn = 1536
    return pl.pallas_call(
        _ada_kernel,
        grid=(depth, n // tn),
        in_specs=[pl.BlockSpec((bsz, d), lambda l, j: (0, 0)),
                  pl.BlockSpec((1, d, tn), lambda l, j: (l, 0, j)),
                  pl.BlockSpec((1, 1, tn), lambda l, j: (l, 0, j))],
        out_specs=pl.BlockSpec((1, bsz, tn), lambda l, j: (l, 0, j)),
        out_shape=jax.ShapeDtypeStruct((depth, bsz, n), F32),
        compiler_params=_params("arbitrary", "arbitrary"),
        name="ada_mod",
    )(c, w_ada, b_ada.reshape(depth, 1, n))


def _rope_tab_kernel(pos_ref, inv_ref, c_ref, sa_ref, sb_ref):
    pos = pos_ref[0].astype(F32)
    ang = pos * inv_ref[...]
    lane = lax.broadcasted_iota(jnp.int32, ang.shape, 1)
    cos, sin = jnp.cos(ang), jnp.sin(ang)
    lo = (lane >= MLA_NOPE_DIM) & (lane < MLA_NOPE_DIM + MLA_ROPE_DIM // 2)
    hi = (lane >= MLA_NOPE_DIM + MLA_ROPE_DIM // 2) & (lane < MLA_NOPE_DIM + MLA_ROPE_DIM)
    c_ref[0] = jnp.where(lane < MLA_NOPE_DIM, 1.0, jnp.where(lo | hi, cos, 0.0))
    sa_ref[0] = jnp.where(lo, -sin, 0.0)
    sb_ref[0] = jnp.where(hi, sin, 0.0)


def _rope_tables(positions):
    bsz, s = positions.shape
    half = MLA_ROPE_DIM // 2
    inv_freq = ROPE_THETA ** (-jnp.arange(half, dtype=F32) / half)
    inv_lane = jnp.concatenate([jnp.zeros((MLA_NOPE_DIM,), F32), inv_freq, inv_freq,
                                jnp.zeros((LANES - MLA_NOPE_DIM - MLA_ROPE_DIM,), F32)]).reshape(1, LANES)
    tm = TOK_TILE
    spec = pl.BlockSpec((1, tm, LANES), lambda b, i: (b, i, 0))
    shape = jax.ShapeDtypeStruct((bsz, s, LANES), F32)
    return pl.pallas_call(
        _rope_tab_kernel,
        grid=(bsz, s // tm),
        in_specs=[pl.BlockSpec((1, tm, 1), lambda b, i: (b, i, 0)),
                  pl.BlockSpec((1, LANES), lambda b, i: (0, 0))],
        out_specs=[spec, spec, spec],
        out_shape=[shape, shape, shape],
        compiler_params=_params("arbitrary", "arbitrary"),
        name="rope_tables",
    )(positions.reshape(bsz, s, 1), inv_lane)


def _bias_tile_kernel(rb_ref, o_ref):
    h, d = pl.program_id(0), pl.program_id(1)
    t = o_ref.shape[-1]
    key = lax.broadcasted_iota(jnp.int32, (t, t), 0)
    qry = lax.broadcasted_iota(jnp.int32, (t, t), 1)
    dist = d * t + qry - key
    n = jnp.maximum(dist, 0)
    nf = jnp.maximum(n, 1).astype(F32)
    large = MAX_EXACT + (jnp.log(nf / MAX_EXACT) / math.log(MAX_DISTANCE / MAX_EXACT)
                         * (N_BUCKETS - MAX_EXACT)).astype(jnp.int32)
    large = jnp.minimum(large, N_BUCKETS - 1)
    bucket = jnp.where(n < MAX_EXACT, n, large)
    val = jnp.zeros((t, t), F32)
    for j in range(N_BUCKETS):
        val = jnp.where(bucket == j, rb_ref[j, h], val)
    val = (rb_ref[N_BUCKETS - 1, h] - val) * LOG2E
    o_ref[0, 0] = jnp.where(dist < 0, -NEG_BIG, val)


def _bias_tiles(rel_bias):
    t = ATTN_TILE
    return pl.pallas_call(
        _bias_tile_kernel,
        grid=(DIFF_HEADS, 2),
        in_specs=[pl.BlockSpec(memory_space=pltpu.SMEM)],
        out_specs=pl.BlockSpec((1, 1, t, t), lambda h, d: (h, d, 0, 0)),
        out_shape=jax.ShapeDtypeStruct((DIFF_HEADS, 2, t, t), F32),
        compiler_params=_params("arbitrary", "arbitrary"),
        name="bias_tiles",
    )(rel_bias)


def _pre_attn_kernel(x_ref, mod_ref, g_ref, w_in_ref, gq_ref, w_uq_ref, gkv_ref, w_uk_ref, w_uv_ref,
                     rc_ref, rsa_ref, rsb_ref,
                     dq_ref, dk_ref, dv_ref, mq_ref, mk_ref, mv_ref):
    x = x_ref[0]
    sh, sc = mod_ref[0, 0:1, :], mod_ref[0, 1:2, :]
    h = (_rms(x, g_ref[...]) * (1.0 + sc) + sh).astype(BF16)
    proj = jnp.dot(h, w_in_ref[...], preferred_element_type=F32)
    dq_ref[0] = (proj[:, C_DQ:C_DK] * (DIFF_QK_DIM ** -0.5 * LOG2E)).astype(BF16)
    dk_ref[0] = proj[:, C_DK:C_DV].astype(BF16)
    dv_ref[0] = proj[:, C_DV:C_MQ].astype(BF16)

    rc, rsa, rsb = rc_ref[0], rsa_ref[0], rsb_ref[0]

    def rope(v):
        return v * rc + pltpu.roll(v, LANES - MLA_ROPE_DIM // 2, 1) * rsa + pltpu.roll(v, MLA_ROPE_DIM // 2, 1) * rsb

    qn = _rms(proj[:, C_MQ:C_KV], gq_ref[...]).astype(BF16)
    q = jnp.dot(qn, w_uq_ref[...], preferred_element_type=F32)
    kvn = _rms(proj[:, C_KV:C_KR], gkv_ref[...]).astype(BF16)
    kn = jnp.dot(kvn, w_uk_ref[...], preferred_element_type=F32)
    mv_ref[0] = jnp.dot(kvn, w_uv_ref[...], preferred_element_type=F32).astype(BF16)
    kr = rope(proj[:, C_KR:C_END])
    q_scale = (MLA_NOPE_DIM + MLA_ROPE_DIM) ** -0.5 * LOG2E
    for hd in range(MLA_HEADS):
        sl = slice(hd * LANES, (hd + 1) * LANES)
        mq_ref[0, :, sl] = (rope(q[:, sl]) * q_scale).astype(BF16)
        mk_ref[0, :, sl] = (kn[:, sl] + kr).astype(BF16)


def _pre_attn(x, mod, g, w_in, gq, w_uq, gkv, w_uk, w_uv, rc, rsa, rsb):
    bsz, s, d = x.shape
    tm = TOK_TILE
    tok = lambda w: pl.BlockSpec((1, tm, w), lambda b, i: (b, i, 0))
    full = lambda a: pl.BlockSpec(a.shape, lambda b, i: (0,) * a.ndim)
    widths = (512, 512, 512, MLA_HEADS * LANES, MLA_HEADS * LANES, MLA_HEADS * MLA_V_DIM)
    return pl.pallas_call(
        _pre_attn_kernel,
        grid=(bsz, s // tm),
        in_specs=[tok(d), pl.BlockSpec((1, ADA_CHUNKS, d), lambda b, i: (b, 0, 0)), full(g), full(w_in),
                  full(gq), full(w_uq), full(gkv), full(w_uk), full(w_uv), tok(LANES), tok(LANES), tok(LANES)],
        out_specs=[tok(w) for w in widths],
        out_shape=[jax.ShapeDtypeStruct((bsz, s, w), BF16) for w in widths],
        compiler_params=_params("arbitrary", "arbitrary"),
        name="pre_attn",
    )(x, mod, g, w_in, gq, w_uq, gkv, w_uk, w_uv, rc, rsa, rsb)


def _softmax_init(m_ref, l_ref, acc_ref):
    m_ref[...] = jnp.full(m_ref.shape, NEG_BIG, F32)
    l_ref[...] = jnp.zeros(l_ref.shape, F32)
    acc_ref[...] = jnp.zeros(acc_ref.shape, F32)


def _softmax_probs(st, m_ref, l_ref):
    m_prev = m_ref[...]
    m_new = jnp.maximum(m_prev, jnp.max(st, axis=0, keepdims=True))
    alpha = jnp.exp2(m_prev - m_new)
    p = jnp.exp2(st - m_new)
    l_ref[...] = alpha * l_ref[...] + jnp.sum(p, axis=0, keepdims=True)
    m_ref[...] = m_new
    return p.astype(BF16), alpha


def _acc_update(acc_ref, alpha, vt, p):
    acc_ref[...] = alpha * acc_ref[...] + jnp.dot(vt, p, preferred_element_type=F32)


def _attn_kernel(lam_init_ref, lam_ref, g_ref, bias_ref, dq_ref, dk_ref, dv_ref, mq_ref, mk_ref, mv_ref, o_ref,
                 dvt_ref, mvt_ref, m_ref, l_ref, dacc_ref, macc_ref, pend_ref):
    i = pl.program_id(1)
    t = dq_ref.shape[1]
    n_diff = 2 * DIFF_HEADS
    n_chains = n_diff + MLA_HEADS

    @pl.when(i == 0)
    def _():
        dvt_ref[...] = dv_ref[0].T
        mvt_ref[...] = mv_ref[0].T

    qts = []
    for h in range(DIFF_HEADS):
        qt = dq_ref[0, :, h * LANES:(h + 1) * LANES].T
        row = lax.broadcasted_iota(jnp.int32, qt.shape, 0)
        qts.append(jnp.where(row < DIFF_QK_DIM, qt, jnp.zeros_like(qt)))
        qts.append(jnp.where(row >= DIFF_QK_DIM, qt, jnp.zeros_like(qt)))
    qts += [mq_ref[0, :, h * LANES:(h + 1) * LANES].T for h in range(MLA_HEADS)]

    def scores(j, c):
        rows = pl.ds(pl.multiple_of(j * t, t), t)
        if c < n_diff:
            k = dk_ref[0, rows, (c // 2) * LANES:(c // 2 + 1) * LANES]
        else:
            k = mk_ref[0, rows, (c - n_diff) * LANES:(c - n_diff + 1) * LANES]
        return jnp.dot(k, qts[c], preferred_element_type=F32)

    def values(j, c):
        cols = pl.ds(pl.multiple_of(j * t, t), t)
        if c < n_diff:
            return dvt_ref[(c // 2) * LANES:(c // 2 + 1) * LANES, cols]
        return mvt_ref[(c - n_diff) * MLA_V_DIM:(c - n_diff + 1) * MLA_V_DIM, cols]

    def acc_at(c):
        return dacc_ref.at[c] if c < n_diff else macc_ref.at[c - n_diff]

    def step(j, bias_idx, has_next):
        hidden = None
        if bias_idx == 0:
            key = lax.broadcasted_iota(jnp.int32, (t, t), 0)
            qry = lax.broadcasted_iota(jnp.int32, (t, t), 1)
            hidden = jnp.where(key <= qry, 0.0, -NEG_BIG)

        def adjust(c, st):
            if c < n_diff:
                return st if bias_idx is None else st - bias_ref[c // 2, bias_idx]
            return st if hidden is None else st - hidden

        pending = [pend_ref[c] for c in range(AHEAD)]
        for c in range(n_chains):
            nxt = c + AHEAD
            if nxt < n_chains:
                pending.append(scores(j, nxt))
            elif has_next:
                pend_ref[nxt - n_chains] = scores(j + 1, nxt - n_chains)
            p, alpha = _softmax_probs(adjust(c, pending.pop(0)), m_ref.at[c], l_ref.at[c])
            _acc_update(acc_at(c), alpha, values(j, c), p)

    def far(j, carry):
        step(j, None, True)
        return carry

    for c in range(n_chains):
        _softmax_init(m_ref.at[c], l_ref.at[c], acc_at(c))
    for c in range(AHEAD):
        pend_ref[c] = scores(0, c)
    lax.fori_loop(0, jnp.maximum(i - 1, 0), far, 0)

    @pl.when(i >= 1)
    def _():
        step(i - 1, 1, True)

    step(i, 0, False)

    lv = lam_ref[...]
    lam = (jnp.exp(jnp.sum(lv[0:1] * lv[1:2], keepdims=True)) - jnp.exp(jnp.sum(lv[2:3] * lv[3:4], keepdims=True))
           + lam_init_ref[0])
    for h in range(DIFF_HEADS):
        c0, c1 = 2 * h, 2 * h + 1
        ot = dacc_ref[c0] / l_ref[c0] - lam * (dacc_ref[c1] / l_ref[c1])
        ot = ot * lax.rsqrt(jnp.mean(ot * ot, axis=0, keepdims=True) + NORM_EPS) * g_ref[...]
        o_ref[0, :, h * LANES:(h + 1) * LANES] = (ot * (1.0 - lam_init_ref[0])).T.astype(BF16)
    base = DIFF_HEADS * LANES
    for u in range(MLA_HEADS // 2):
        ot = jnp.concatenate([macc_ref[2 * u] / l_ref[n_diff + 2 * u],
                              macc_ref[2 * u + 1] / l_ref[n_diff + 2 * u + 1]], axis=0)
        o_ref[0, :, base + u * LANES:base + (u + 1) * LANES] = ot.T.astype(BF16)


def _attention(lam_init, diff_lambda, g, bias, dq, dk, dv, mq, mk, mv):
    bsz, s, wd = dq.shape
    wq, wv = mq.shape[-1], mv.shape[-1]
    t = ATTN_TILE
    assert MAX_DISTANCE <= t and s % t == 0
    n_diff = 2 * DIFF_HEADS
    n_chains = n_diff + MLA_HEADS
    qtile = lambda w: pl.BlockSpec((1, t, w), lambda b, i: (b, i, 0))
    whole = lambda w: pl.BlockSpec((1, s, w), lambda b, i: (b, 0, 0))
    return pl.pallas_call(
        _attn_kernel,
        grid=(bsz, s // t),
        in_specs=[pl.BlockSpec(memory_space=pltpu.SMEM),
                  pl.BlockSpec(diff_lambda.shape, lambda b, i: (0, 0)),
                  pl.BlockSpec(g.shape, lambda b, i: (0, 0)),
                  pl.BlockSpec(bias.shape, lambda b, i: (0, 0, 0, 0)),
                  qtile(wd), whole(wd), whole(wd), qtile(wq), whole(wq), whole(wv)],
        out_specs=qtile(wd + wv),
        out_shape=jax.ShapeDtypeStruct((bsz, s, wd + wv), BF16),
        scratch_shapes=[pltpu.VMEM((wd, s), BF16), pltpu.VMEM((wv, s), BF16),
                        pltpu.VMEM((n_chains, 1, t), F32), pltpu.VMEM((n_chains, 1, t), F32),
                        pltpu.VMEM((n_diff, LANES, t), F32), pltpu.VMEM((MLA_HEADS, MLA_V_DIM, t), F32),
                        pltpu.VMEM((AHEAD, t, t), F32)],
        compiler_params=_params("arbitrary", "arbitrary"),
        name="attention",
    )(lam_init, diff_lambda, g, bias, dq, dk, dv, mq, mk, mv)


def _post_attn_kernel(*refs, moe):
    if moe:
        o_ref, x_ref, mod_ref, g_ref, wo_ref, wr_ref, x1_ref, h_ref, gates_ref, sel_ref = refs
    else:
        o_ref, x_ref, mod_ref, g_ref, wo_ref, x1_ref, h_ref = refs
    y = jnp.dot(o_ref[0], wo_ref[...], preferred_element_type=F32)
    gt_a, sh_f, sc_f = mod_ref[0, 2:3, :], mod_ref[0, 3:4, :], mod_ref[0, 4:5, :]
    x1 = x_ref[0] + (1.0 + gt_a) * y
    x1_ref[0] = x1
    h = _rms(x1, g_ref[...]) * (1.0 + sc_f) + sh_f
    if not moe:
        h_ref[0] = h.astype(BF16)
        return
    _rows_to_tiles(h, h_ref.at[0])
    wr = wr_ref[...]
    h_hi, wr_hi = h.astype(BF16), wr.astype(BF16)
    h_lo, wr_lo = (h - h_hi.astype(F32)).astype(BF16), (wr - wr_hi.astype(F32)).astype(BF16)
    logits = (jnp.dot(h_hi, wr_hi, preferred_element_type=F32) + jnp.dot(h_hi, wr_lo, preferred_element_type=F32)
              + jnp.dot(h_lo, wr_hi, preferred_element_type=F32))
    lane = lax.broadcasted_iota(jnp.int32, logits.shape, 1)
    logits = jnp.where(lane < N_EXPERTS, logits, -jnp.inf)
    v1 = jnp.max(logits, axis=1, keepdims=True)
    i1 = jnp.min(jnp.where(logits == v1, lane, LANES), axis=1, keepdims=True)
    rest = jnp.where(lane == i1, -jnp.inf, logits)
    v2 = jnp.max(rest, axis=1, keepdims=True)
    i2 = jnp.min(jnp.where(rest == v2, lane, LANES), axis=1, keepdims=True)
    e2 = jnp.exp(v2 - v1)
    w1 = 1.0 / (1.0 + e2)
    w2 = e2 / (1.0 + e2)
    gates_ref[0] = jnp.where(lane == i1, w1, 0.0) + jnp.where(lane == i2, w2, 0.0)
    sel_ref[0] = jnp.where(lane == i1, 1.0, 0.0) + jnp.where(lane == i2, 2.0, 0.0)


def _post_attn(o, x, mod, g, wo, w_router=None):
    bsz, s, d = x.shape
    moe = w_router is not None
    tm = TOK_TILE
    tok = lambda w: pl.BlockSpec((1, tm, w), lambda b, i: (b, i, 0))
    full = lambda a: pl.BlockSpec(a.shape, lambda b, i: (0,) * a.ndim)
    args = [o, x, mod, g, wo] + ([w_router] if moe else [])
    in_specs = [tok(o.shape[-1]), tok(d), pl.BlockSpec((1, ADA_CHUNKS, d), lambda b, i: (b, 0, 0)),
                full(g), full(wo)] + ([full(w_router)] if moe else [])
    if moe:
        assert d == ROW_SUB * LANES
        h_spec = pl.BlockSpec((1, tm * ROW_SUB, LANES), lambda b, i: (b, i, 0))
        out_specs = [tok(d), h_spec, tok(LANES), tok(LANES)]
        out_shape = [jax.ShapeDtypeStruct((bsz, s, d), F32), jax.ShapeDtypeStruct((bsz, s * ROW_SUB, LANES), F32)]
        out_shape += [jax.ShapeDtypeStruct((bsz, s, LANES), F32)] * 2
    else:
        out_specs = [tok(d), tok(d)]
        out_shape = [jax.ShapeDtypeStruct((bsz, s, d), F32), jax.ShapeDtypeStruct((bsz, s, d), BF16)]
    return pl.pallas_call(
        functools.partial(_post_attn_kernel, moe=moe),
        grid=(bsz, s // tm),
        in_specs=in_specs,
        out_specs=out_specs,
        out_shape=out_shape,
        compiler_params=_params("arbitrary", "arbitrary"),
        name="post_attn_moe" if moe else "post_attn",
    )(*args)


def _swiglu(h, w1_ref, w3_ref, w2_ref):
    d_ff = w1_ref.shape[-1]
    y = jnp.zeros((h.shape[0], w2_ref.shape[-1]), F32)
    for c0 in range(0, d_ff, FF_CHUNK):
        a = jnp.dot(h, w1_ref[:, c0:c0 + FF_CHUNK], preferred_element_type=F32)
        b = jnp.dot(h, w3_ref[:, c0:c0 + FF_CHUNK], preferred_element_type=F32)
        u = (a * jax.nn.sigmoid(a) * b).astype(BF16)
        y = y + jnp.dot(u, w2_ref[c0:c0 + FF_CHUNK, :], preferred_element_type=F32)
    return y


def _ffn_dense_kernel(h_ref, x1_ref, mod_ref, w1_ref, w3_ref, w2_ref, o_ref):
    gt_f = mod_ref[0, 5:6, :]
    o_ref[0] = x1_ref[0] + (1.0 + gt_f) * _swiglu(h_ref[0], w1_ref, w3_ref, w2_ref)


def _ffn_dense(h, x1, mod, w1, w3, w2):
    bsz, s, d = x1.shape
    tm = FFN_TILE
    tok = lambda w: pl.BlockSpec((1, tm, w), lambda b, i: (b, i, 0))
    full = lambda a: pl.BlockSpec(a.shape, lambda b, i: (0,) * a.ndim)
    return pl.pallas_call(
        _ffn_dense_kernel,
        grid=(bsz, s // tm),
        in_specs=[tok(d), tok(d), pl.BlockSpec((1, ADA_CHUNKS, d), lambda b, i: (b, 0, 0)),
                  full(w1), full(w3), full(w2)],
        out_specs=tok(d),
        out_shape=jax.ShapeDtypeStruct(x1.shape, F32),
        compiler_params=_params("arbitrary", "arbitrary"),
        name="ffn_dense",
    )(h, x1, mod, w1, w3, w2)


def _route_kernel(sel_ref, pos_ref, te_ref, cnt_ref, off_ref, run_ref, *, row_tile):
    p, t = pl.program_id(0), pl.program_id(1)
    sel_t = sel_ref[...].T
    chosen = (sel_t > 0.0).astype(F32)
    per_expert = jnp.sum(chosen, axis=1, keepdims=True)

    @pl.when((p == 0) & (t == 0))
    def _():
        cnt_ref[...] = jnp.zeros(cnt_ref.shape, F32)
        te_ref[...] = jnp.zeros(te_ref.shape, jnp.int32)

    @pl.when(p == 0)
    def _():
        cnt_ref[...] += per_expert

    @pl.when((p == 1) & (t == 0))
    def _():
        cnt = cnt_ref[...]
        padded = jnp.ceil(cnt / row_tile) * row_tile
        row = lax.broadcasted_iota(jnp.int32, cnt.shape, 0)
        off = jnp.zeros(cnt.shape, F32)
        for e in range(N_EXPERTS):
            size_e = jnp.sum(jnp.where(row == e, padded, 0.0), keepdims=True)
            off = off + jnp.where(row > e, size_e, 0.0)
        off_ref[...] = off
        run_ref[...] = jnp.zeros(run_ref.shape, F32)
        ends = off + padded
        tile_start = lax.broadcasted_iota(jnp.int32, (LANES, LANES), 1).astype(F32) * row_tile
        erow = lax.broadcasted_iota(jnp.int32, (LANES, LANES), 0)
        done = jnp.where((erow < N_EXPERTS) & (ends <= tile_start), 1.0, 0.0)
        te = jnp.sum(done, axis=0, keepdims=True).astype(jnp.int32)
        te_ref[...] = jnp.broadcast_to(te, te_ref.shape)

    @pl.when(p == 1)
    def _():
        tm = sel_t.shape[1]
        before = (lax.broadcasted_iota(jnp.int32, (tm, tm), 0)
                  < lax.broadcasted_iota(jnp.int32, (tm, tm), 1)).astype(BF16)
        rank = jnp.dot(chosen.astype(BF16), before, preferred_element_type=F32) + run_ref[...]
        base = off_ref[...] + rank
        for k in range(2):
            pos = jnp.sum(jnp.where(sel_t == float(k + 1), base, 0.0), axis=0, keepdims=True)
            pos_ref[0, k:k + 1, :] = pos.astype(jnp.int32)
        run_ref[...] += per_expert


def _route(sel, row_tile, n_row_tiles):
    n_tok = sel.shape[0]
    assert n_row_tiles <= LANES and 2 * n_tok < 2 ** 24
    tm = TOK_TILE
    nt = n_tok // tm
    pos, te = pl.pallas_call(
        functools.partial(_route_kernel, row_tile=row_tile),
        grid=(2, nt),
        in_specs=[pl.BlockSpec((tm, LANES), lambda p, t: (t, 0))],
        out_specs=[pl.BlockSpec((1, 2, tm), lambda p, t: (p * t, 0, 0)),
                   pl.BlockSpec((8, LANES), lambda p, t: (0, 0))],
        out_shape=[jax.ShapeDtypeStruct((nt, 2, tm), jnp.int32), jax.ShapeDtypeStruct((8, LANES), jnp.int32)],
        scratch_shapes=[pltpu.VMEM((LANES, 1), F32)] * 3,
        compiler_params=_params("arbitrary", "arbitrary"),
        name="moe_route",
    )(sel)
    return pos, te[0]


def _dispatch_kernel(pos_ref, h_ref, w1_ref, w3_ref, w2_ref, xs_in_ref, xs_ref, w1b_ref, w3b_ref, w2b_ref, sem):
    del xs_in_ref
    tm = h_ref.shape[0] // ROW_SUB
    w1b_ref[...] = w1_ref[...].astype(BF16)
    w3b_ref[...] = w3_ref[...].astype(BF16)
    w2b_ref[...] = w2_ref[...].astype(BF16)

    def body(r, carry):
        for k in range(2):
            pltpu.make_async_copy(_tile_row(h_ref, r), _tile_row(xs_ref, pos_ref[0, k, r]), sem).start(priority=k)
        return carry

    lax.fori_loop(0, tm, body, 0, unroll=8)
    for k in range(2):
        pltpu.make_async_copy(h_ref, xs_ref.at[pl.ds(0, tm * ROW_SUB)], sem).wait()


def _dispatch(pos, h, xs0, w1, w3, w2, layer):
    nt, _, tm = pos.shape
    n_lay, n_e, d, d_ff = w1.shape
    up_rows, down_rows = n_e * d // nt, n_e * d_ff // nt
    assert up_rows * nt == n_e * d and down_rows * nt == n_e * d_ff and up_rows % 16 == 0 and down_rows % 16 == 0
    slab = lambda rows, cols: pl.BlockSpec((rows, cols), lambda t: (layer * nt + t, 0))
    out_slab = lambda rows, cols: pl.BlockSpec((rows, cols), lambda t: (t, 0))
    xs, w1b, w3b, w2b = pl.pallas_call(
        _dispatch_kernel,
        grid=(nt,),
        in_specs=[pl.BlockSpec((1, 2, tm), lambda t: (t, 0, 0), memory_space=pltpu.SMEM),
                  pl.BlockSpec((tm * ROW_SUB, LANES), lambda t: (t, 0)),
                  slab(up_rows, d_ff), slab(up_rows, d_ff), slab(down_rows, d),
                  pl.BlockSpec(memory_space=pl.ANY)],
        out_specs=[pl.BlockSpec(memory_space=pl.ANY),
                   out_slab(up_rows, d_ff), out_slab(up_rows, d_ff), out_slab(down_rows, d)],
        out_shape=[jax.ShapeDtypeStruct(xs0.shape, xs0.dtype),
                   jax.ShapeDtypeStruct((n_e * d, d_ff), BF16), jax.ShapeDtypeStruct((n_e * d, d_ff), BF16),
                   jax.ShapeDtypeStruct((n_e * d_ff, d), BF16)],
        scratch_shapes=[pltpu.SemaphoreType.DMA(())],
        input_output_aliases={5: 0},
        compiler_params=_params("arbitrary"),
        name="moe_dispatch",
    )(pos, h, w1.reshape(n_lay * n_e * d, d_ff), w3.reshape(n_lay * n_e * d, d_ff),
      w2.reshape(n_lay * n_e * d_ff, d), xs0)
    return xs, w1b.reshape(n_e, d, d_ff), w3b.reshape(n_e, d, d_ff), w2b.reshape(n_e, d_ff, d)


def _expert_kernel(te_ref, xs_ref, w1_ref, w3_ref, w2_ref, y_ref):
    used = te_ref[pl.program_id(0)] < N_EXPERTS

    @pl.when(used)
    def _():
        y = _swiglu(_tiles_to_rows(xs_ref).astype(BF16), w1_ref.at[0], w3_ref.at[0], w2_ref.at[0])
        _rows_to_tiles(y, y_ref)

    @pl.when(jnp.logical_not(used))
    def _():
        y_ref[...] = jnp.zeros(y_ref.shape, y_ref.dtype)


def _experts(te, xs, w1, w3, w2, row_tile):
    n_rows = xs.shape[0] // ROW_SUB
    _, d, d_ff = w1.shape
    expert = lambda n, te: (jnp.minimum(te[n], N_EXPERTS - 1), 0, 0)
    rows = pl.BlockSpec((row_tile * ROW_SUB, LANES), lambda n, te: (n, 0))
    return pl.pallas_call(
        _expert_kernel,
        grid_spec=pltpu.PrefetchScalarGridSpec(
            num_scalar_prefetch=1,
            grid=(n_rows // row_tile,),
            in_specs=[rows, pl.BlockSpec((1, d, d_ff), expert), pl.BlockSpec((1, d, d_ff), expert),
                      pl.BlockSpec((1, d_ff, d), expert)],
            out_specs=rows),
        out_shape=jax.ShapeDtypeStruct(xs.shape, F32),
        compiler_params=_params("arbitrary"),
        name="moe_experts",
    )(te, xs, w1, w3, w2)


def _combine_kernel(pos_ref, x1_ref, gates_ref, sel_ref, mod_ref, gfin_ref, ys_ref, o_ref, ya_ref, yb_ref, sem,
                    *, final):
    tm = x1_ref.shape[1]
    bufs = (ya_ref, yb_ref)

    def body(r, carry):
        for k in range(2):
            pltpu.make_async_copy(_tile_row(ys_ref, pos_ref[0, k, r]), _tile_row(bufs[k], r), sem).start(priority=k)
        return carry

    lax.fori_loop(0, tm, body, 0, unroll=8)
    gates, sel = gates_ref[0], sel_ref[0]
    w_a = jnp.sum(jnp.where(sel == 1.0, gates, 0.0), axis=1, keepdims=True)
    w_b = jnp.sum(jnp.where(sel == 2.0, gates, 0.0), axis=1, keepdims=True)
    for k in range(2):
        pltpu.make_async_copy(ys_ref.at[pl.ds(0, tm * ROW_SUB)], bufs[k], sem).wait()
    gt_f = mod_ref[0, 5:6, :]
    x = x1_ref[0] + (1.0 + gt_f) * (w_a * _tiles_to_rows(ya_ref) + w_b * _tiles_to_rows(yb_ref))
    o_ref[0] = _rms(x, gfin_ref[...]) if final else x


def _combine(pos, x1, gates, sel, mod, ys, g_final=None):
    bsz, s, d = x1.shape
    nt, _, tm = pos.shape
    per_b = s // tm
    tok = lambda w: pl.BlockSpec((1, tm, w), lambda b, i: (b, i, 0))
    final = g_final is not None
    gfin = g_final if final else jnp.ones((1, d), F32)
    return pl.pallas_call(
        functools.partial(_combine_kernel, final=final),
        grid=(bsz, per_b),
        in_specs=[pl.BlockSpec((1, 2, tm), lambda b, i: (b * per_b + i, 0, 0), memory_space=pltpu.SMEM),
                  tok(d), tok(LANES), tok(LANES),
                  pl.BlockSpec((1, ADA_CHUNKS, d), lambda b, i: (b, 0, 0)),
                  pl.BlockSpec((1, d), lambda b, i: (0, 0)),
                  pl.BlockSpec(memory_space=pl.ANY)],
        out_specs=tok(d),
        out_shape=jax.ShapeDtypeStruct(x1.shape, F32),
        scratch_shapes=[pltpu.VMEM((tm * ROW_SUB, LANES), F32), pltpu.VMEM((tm * ROW_SUB, LANES), F32),
                        pltpu.SemaphoreType.DMA(())],
        compiler_params=_params("arbitrary", "arbitrary"),
        name="moe_combine",
    )(pos, x1, gates, sel, mod, gfin, ys)


def _moe(h, x1, gates, sel, mod, w1, w3, w2, layer, xs_buf, g_final):
    bsz, s, d = x1.shape
    n_tok = bsz * s
    row_tile = FFN_TILE
    n_rows = xs_buf.shape[0] // ROW_SUB
    assert n_rows == _moe_rows(n_tok)
    pos, te = _route(sel.reshape(n_tok, LANES), row_tile, n_rows // row_tile)
    xs, w1b, w3b, w2b = _dispatch(pos, h.reshape(n_tok * ROW_SUB, LANES), xs_buf, w1, w3, w2, layer)
    ys = _experts(te, xs, w1b, w3b, w2b, row_tile)
    return _combine(pos, x1, gates, sel, mod, ys, g_final), xs


def _moe_rows(n_tok):
    return 2 * n_tok + N_EXPERTS * FFN_TILE


def _final_norm_kernel(x_ref, g_ref, o_ref):
    o_ref[0] = _rms(x_ref[0], g_ref[...])


def _final_norm(x, g):
    bsz, s, d = x.shape
    tm = TOK_TILE
    tok = pl.BlockSpec((1, tm, d), lambda b, i: (b, i, 0))
    return pl.pallas_call(
        _final_norm_kernel,
        grid=(bsz, s // tm),
        in_specs=[tok, pl.BlockSpec(g.shape, lambda b, i: (0, 0))],
        out_specs=tok,
        out_shape=jax.ShapeDtypeStruct(x.shape, F32),
        compiler_params=_params("arbitrary", "arbitrary"),
        name="final_norm",
    )(x, g)


def _pad_heads(w, heads, width, lo, hi):
    k = w.shape[0]
    w = w.reshape(k, heads, width)[:, :, lo:hi]
    return jnp.pad(w, ((0, 0), (0, 0), (0, LANES - (hi - lo)))).reshape(k, heads * LANES)


def _prep_w_in(w_in):
    d = w_in.shape[0]
    kr = w_in[:, C_KR:]
    kr_block = jnp.concatenate([jnp.zeros((d, MLA_NOPE_DIM), w_in.dtype), kr,
                                jnp.zeros((d, LANES - MLA_NOPE_DIM - MLA_ROPE_DIM), w_in.dtype)], axis=1)
    return jnp.concatenate([w_in[:, :C_KR], kr_block], axis=1).astype(BF16)


def kernel(x, c, positions, w_ada, b_ada, g_attn, w_in, diff_lambda, diff_subln_g, rel_bias, mla_q_norm, w_uq, mla_kv_norm, w_ukv, w_o, g_ffn, ffn_w1, ffn_w3, ffn_w2, moe_router, moe_w1, moe_w3, moe_w2, g_final):
    depth = w_ada.shape[0]
    bsz, s, d = x.shape
    mods = _ada(c, w_ada, b_ada).reshape(depth, bsz, ADA_CHUNKS, d)
    rc, rsa, rsb = _rope_tables(positions)
    bias = _bias_tiles(rel_bias)
    qk_w = MLA_NOPE_DIM + MLA_ROPE_DIM
    kv_w = MLA_NOPE_DIM + MLA_V_DIM
    xs_buf = jnp.zeros((_moe_rows(bsz * s) * ROW_SUB, LANES), F32)
    for l in range(depth):
        mod = mods[l]
        lam_init = jnp.full((1,), 0.8 - 0.6 * math.exp(-0.3 * l), F32)
        w_uq_p = _pad_heads(w_uq[l], MLA_HEADS, qk_w, 0, qk_w).astype(BF16)
        w_uk_p = _pad_heads(w_ukv[l], MLA_HEADS, kv_w, 0, MLA_NOPE_DIM).astype(BF16)
        w_uv_p = w_ukv[l].reshape(MLA_KV_RANK, MLA_HEADS, kv_w)[:, :, MLA_NOPE_DIM:].reshape(
            MLA_KV_RANK, MLA_HEADS * MLA_V_DIM).astype(BF16)
        dq, dk, dv, mq, mk, mv = _pre_attn(
            x, mod, g_attn[l].reshape(1, d), _prep_w_in(w_in[l]), mla_q_norm[l].reshape(1, -1), w_uq_p,
            mla_kv_norm[l].reshape(1, -1), w_uk_p, w_uv_p, rc, rsa, rsb)
        o = _attention(lam_init, diff_lambda[l], diff_subln_g[l].reshape(-1, 1), bias, dq, dk, dv, mq, mk, mv)
        wo = w_o[l].astype(BF16)
        g_f = g_ffn[l].reshape(1, d)
        if l % 2 == 1:
            w_router = jnp.pad(moe_router[l // 2], ((0, 0), (0, LANES - N_EXPERTS)))
            x1, h, gates, sel = _post_attn(o, x, mod, g_f, wo, w_router)
            g_fin = g_final.reshape(1, d) if l == depth - 1 else None
            x, xs_buf = _moe(h, x1, gates, sel, mod, moe_w1, moe_w3, moe_w2, l // 2, xs_buf, g_fin)
        else:
            x1, h = _post_attn(o, x, mod, g_f, wo)
            x = _ffn_dense(h, x1, mod, ffn_w1[l // 2].astype(BF16), ffn_w3[l // 2].astype(BF16),
                           ffn_w2[l // 2].astype(BF16))
    return x if depth % 2 == 0 else _final_norm(x, g_final.reshape(1, d))
```

```python
import functools
import math

import jax
import jax.numpy as jnp
from jax import lax
from jax.experimental import pallas as pl
from jax.experimental.pallas import tpu as pltpu

F32 = jnp.float32
BF16 = jnp.bfloat16

DIFF_HEADS = 4
DIFF_QK_DIM = 64
DIFF_V_DIM = 128
MLA_HEADS = 8
MLA_NOPE_DIM = 64
MLA_ROPE_DIM = 32
MLA_V_DIM = 64
MLA_Q_RANK = 384
MLA_KV_RANK = 256
ROPE_THETA = 10000.0
N_BUCKETS = 32
MAX_EXACT = 16
MAX_DISTANCE = 128
N_EXPERTS = 8
NORM_EPS = 1e-6
ADA_CHUNKS = 6

LANES = 128
NEG_BIG = -1e30
LOG2E = math.log2(math.e)
VMEM_LIMIT = 56 * 1024 * 1024

ATTN_TILE = 256
AHEAD = 4
TOK_TILE = 512
FFN_TILE = 512
FF_CHUNK = 256

C_DQ, C_DK, C_DV, C_MQ, C_KV, C_KR, C_END = 0, 512, 1024, 1536, 1920, 2176, 2304


def _params(*sem):
    return pltpu.CompilerParams(dimension_semantics=sem, vmem_limit_bytes=VMEM_LIMIT)


def _rms(x, g):
    return x * lax.rsqrt(jnp.mean(x * x, axis=-1, keepdims=True) + NORM_EPS) * g


ROW_SUB = 8


def _rows_to_tiles(x, ref):
    n = x.shape[0]
    for j in range(ROW_SUB):
        ref[pl.ds(j, n, stride=ROW_SUB), :] = x[:, j * LANES:(j + 1) * LANES]


def _tiles_to_rows(ref):
    n = ref.shape[0] // ROW_SUB
    return jnp.concatenate([ref[pl.ds(j, n, stride=ROW_SUB), :] for j in range(ROW_SUB)], axis=1)


def _tile_row(ref, r):
    return ref.at[pl.ds(pl.multiple_of(r * ROW_SUB, ROW_SUB), ROW_SUB)]


def _ada_kernel(c_ref, w_ref, b_ref, o_ref):
    c = c_ref[...]
    cond = c * jax.nn.sigmoid(c)
    o_ref[0] = jnp.dot(cond, w_ref[0], preferred_element_type=F32,
                       precision=lax.Precision.HIGHEST) + b_ref[0]


def _ada(c, w_ada, b_ada):
    depth, d, n = w_ada.shape
    bsz = c.shape[0]
    tn = 1536
    return pl.pallas_call(
        _ada_kernel,
        grid=(depth, n // tn),
        in_specs=[pl.BlockSpec((bsz, d), lambda l, j: (0, 0)),
                  pl.BlockSpec((1, d, tn), lambda l, j: (l, 0, j)),
                  pl.BlockSpec((1, 1, tn), lambda l, j: (l, 0, j))],
        out_specs=pl.BlockSpec((1, bsz, tn), lambda l, j: (l, 0, j)),
        out_shape=jax.ShapeDtypeStruct((depth, bsz, n), F32),
        compiler_params=_params("arbitrary", "arbitrary"),
        name="ada_mod",
    )(c, w_ada, b_ada.reshape(depth, 1, n))


def _rope_tab_kernel(pos_ref, inv_ref, c_ref, sa_ref, sb_ref):
    pos = pos_ref[0].astype(F32)
    ang = pos * inv_ref[...]
    lane = lax.broadcasted_iota(jnp.int32, ang.shape, 1)
    cos, sin = jnp.cos(ang), jnp.sin(ang)
    lo = (lane >= MLA_NOPE_DIM) & (lane < MLA_NOPE_DIM + MLA_ROPE_DIM // 2)
    hi = (lane >= MLA_NOPE_DIM + MLA_ROPE_DIM // 2) & (lane < MLA_NOPE_DIM + MLA_ROPE_DIM)
    c_ref[0] = jnp.where(lane < MLA_NOPE_DIM, 1.0, jnp.where(lo | hi, cos, 0.0))
    sa_ref[0] = jnp.where(lo, -sin, 0.0)
    sb_ref[0] = jnp.where(hi, sin, 0.0)


def _rope_tables(positions):
    bsz, s = positions.shape
    half = MLA_ROPE_DIM // 2
    inv_freq = ROPE_THETA ** (-jnp.arange(half, dtype=F32) / half)
    inv_lane = jnp.concatenate([jnp.zeros((MLA_NOPE_DIM,), F32), inv_freq, inv_freq,
                                jnp.zeros((LANES - MLA_NOPE_DIM - MLA_ROPE_DIM,), F32)]).reshape(1, LANES)
    tm = TOK_TILE
    spec = pl.BlockSpec((1, tm, LANES), lambda b, i: (b, i, 0))
    shape = jax.ShapeDtypeStruct((bsz, s, LANES), F32)
    return pl.pallas_call(
        _rope_tab_kernel,
        grid=(bsz, s // tm),
        in_specs=[pl.BlockSpec((1, tm, 1), lambda b, i: (b, i, 0)),
                  pl.BlockSpec((1, LANES), lambda b, i: (0, 0))],
        out_specs=[spec, spec, spec],
        out_shape=[shape, shape, shape],
        compiler_params=_params("arbitrary", "arbitrary"),
        name="rope_tables",
    )(positions.reshape(bsz, s, 1), inv_lane)


def _bias_tile_kernel(rb_ref, o_ref):
    h, d = pl.program_id(0), pl.program_id(1)
    t = o_ref.shape[-1]
    key = lax.broadcasted_iota(jnp.int32, (t, t), 0)
    qry = lax.broadcasted_iota(jnp.int32, (t, t), 1)
    dist = d * t + qry - key
    n = jnp.maximum(dist, 0)
    nf = jnp.maximum(n, 1).astype(F32)
    large = MAX_EXACT + (jnp.log(nf / MAX_EXACT) / math.log(MAX_DISTANCE / MAX_EXACT)
                         * (N_BUCKETS - MAX_EXACT)).astype(jnp.int32)
    large = jnp.minimum(large, N_BUCKETS - 1)
    bucket = jnp.where(n < MAX_EXACT, n, large)
    val = jnp.zeros((t, t), F32)
    for j in range(N_BUCKETS):
        val = jnp.where(bucket == j, rb_ref[j, h], val)
    val = (rb_ref[N_BUCKETS - 1, h] - val) * LOG2E
    o_ref[0, 0] = jnp.where(dist < 0, -NEG_BIG, val)


def _bias_tiles(rel_bias):
    t = ATTN_TILE
    return pl.pallas_call(
        _bias_tile_kernel,
        grid=(DIFF_HEADS, 2),
        in_specs=[pl.BlockSpec(memory_space=pltpu.SMEM)],
        out_specs=pl.BlockSpec((1, 1, t, t), lambda h, d: (h, d, 0, 0)),
        out_shape=jax.ShapeDtypeStruct((DIFF_HEADS, 2, t, t), F32),
        compiler_params=_params("arbitrary", "arbitrary"),
        name="bias_tiles",
    )(rel_bias)


def _pre_attn_kernel(x_ref, mod_ref, g_ref, w_in_ref, gq_ref, w_uq_ref, gkv_ref, w_uk_ref, w_uv_ref,
                     rc_ref, rsa_ref, rsb_ref,
                     dq_ref, dk_ref, dv_ref, mq_ref, mk_ref, mv_ref):
    x = x_ref[0]
    sh, sc = mod_ref[0, 0:1, :], mod_ref[0, 1:2, :]
    h = (_rms(x, g_ref[...]) * (1.0 + sc) + sh).astype(BF16)
    proj = jnp.dot(h, w_in_ref[...], preferred_element_type=F32)
    dq_ref[0] = (proj[:, C_DQ:C_DK] * (DIFF_QK_DIM ** -0.5 * LOG2E)).astype(BF16)
    dk_ref[0] = proj[:, C_DK:C_DV].astype(BF16)
    dv_ref[0] = proj[:, C_DV:C_MQ].astype(BF16)

    rc, rsa, rsb = rc_ref[0], rsa_ref[0], rsb_ref[0]

    def rope(v):
        return v * rc + pltpu.roll(v, LANES - MLA_ROPE_DIM // 2, 1) * rsa + pltpu.roll(v, MLA_ROPE_DIM // 2, 1) * rsb

    qn = _rms(proj[:, C_MQ:C_KV], gq_ref[...]).astype(BF16)
    q = jnp.dot(qn, w_uq_ref[...], preferred_element_type=F32)
    kvn = _rms(proj[:, C_KV:C_KR], gkv_ref[...]).astype(BF16)
    kn = jnp.dot(kvn, w_uk_ref[...], preferred_element_type=F32)
    mv_ref[0] = jnp.dot(kvn, w_uv_ref[...], preferred_element_type=F32).astype(BF16)
    kr = rope(proj[:, C_KR:C_END])
    q_scale = (MLA_NOPE_DIM + MLA_ROPE_DIM) ** -0.5 * LOG2E
    for hd in range(MLA_HEADS):
        sl = slice(hd * LANES, (hd + 1) * LANES)
        mq_ref[0, :, sl] = (rope(q[:, sl]) * q_scale).astype(BF16)
        mk_ref[0, :, sl] = (kn[:, sl] + kr).astype(BF16)


def _pre_attn(x, mod, g, w_in, gq, w_uq, gkv, w_uk, w_uv, rc, rsa, rsb):
    bsz, s, d = x.shape
    tm = TOK_TILE
    tok = lambda w: pl.BlockSpec((1, tm, w), lambda b, i: (b, i, 0))
    full = lambda a: pl.BlockSpec(a.shape, lambda b, i: (0,) * a.ndim)
    widths = (512, 512, 512, MLA_HEADS * LANES, MLA_HEADS * LANES, MLA_HEADS * MLA_V_DIM)
    return pl.pallas_call(
        _pre_attn_kernel,
        grid=(bsz, s // tm),
        in_specs=[tok(d), pl.BlockSpec((1, ADA_CHUNKS, d), lambda b, i: (b, 0, 0)), full(g), full(w_in),
                  full(gq), full(w_uq), full(gkv), full(w_uk), full(w_uv), tok(LANES), tok(LANES), tok(LANES)],
        out_specs=[tok(w) for w in widths],
        out_shape=[jax.ShapeDtypeStruct((bsz, s, w), BF16) for w in widths],
        compiler_params=_params("arbitrary", "arbitrary"),
        name="pre_attn",
    )(x, mod, g, w_in, gq, w_uq, gkv, w_uk, w_uv, rc, rsa, rsb)


def _softmax_init(m_ref, l_ref, acc_ref):
    m_ref[...] = jnp.full(m_ref.shape, NEG_BIG, F32)
    l_ref[...] = jnp.zeros(l_ref.shape, F32)
    acc_ref[...] = jnp.zeros(acc_ref.shape, F32)


def _softmax_probs(st, m_ref, l_ref):
    m_prev = m_ref[...]
    m_new = jnp.maximum(m_prev, jnp.max(st, axis=0, keepdims=True))
    alpha = jnp.exp2(m_prev - m_new)
    p = jnp.exp2(st - m_new)
    l_ref[...] = alpha * l_ref[...] + jnp.sum(p, axis=0, keepdims=True)
    m_ref[...] = m_new
    return p.astype(BF16), alpha


def _acc_update(acc_ref, alpha, vt, p):
    acc_ref[...] = alpha * acc_ref[...] + jnp.dot(vt, p, preferred_element_type=F32)


def _attn_kernel(*refs, cast_weights):
    (lam_init_ref, lam_ref, g_ref, bias_ref, dq_ref, dk_ref, dv_ref, mq_ref, mk_ref, mv_ref), refs = refs[:10], refs[10:]
    if cast_weights:
        for w_ref, wb_ref in zip(refs[:3], refs[4:7]):
            wb_ref[...] = w_ref[...].astype(BF16)
        refs = refs[3:4] + refs[7:]
    o_ref, dvt_ref, mvt_ref, m_ref, l_ref, dacc_ref, macc_ref, pend_ref = refs
    i = pl.program_id(1)
    t = dq_ref.shape[1]
    n_diff = 2 * DIFF_HEADS
    n_chains = n_diff + MLA_HEADS

    @pl.when(i == 0)
    def _():
        dvt_ref[...] = dv_ref[0].T
        mvt_ref[...] = mv_ref[0].T

    qts = []
    for h in range(DIFF_HEADS):
        qt = dq_ref[0, :, h * LANES:(h + 1) * LANES].T
        row = lax.broadcasted_iota(jnp.int32, qt.shape, 0)
        qts.append(jnp.where(row < DIFF_QK_DIM, qt, jnp.zeros_like(qt)))
        qts.append(jnp.where(row >= DIFF_QK_DIM, qt, jnp.zeros_like(qt)))
    qts += [mq_ref[0, :, h * LANES:(h + 1) * LANES].T for h in range(MLA_HEADS)]

    def scores(j, c):
        rows = pl.ds(pl.multiple_of(j * t, t), t)
        if c < n_diff:
            k = dk_ref[0, rows, (c // 2) * LANES:(c // 2 + 1) * LANES]
        else:
            k = mk_ref[0, rows, (c - n_diff) * LANES:(c - n_diff + 1) * LANES]
        return jnp.dot(k, qts[c], preferred_element_type=F32)

    def values(j, c):
        cols = pl.ds(pl.multiple_of(j * t, t), t)
        if c < n_diff:
            return dvt_ref[(c // 2) * LANES:(c // 2 + 1) * LANES, cols]
        return mvt_ref[(c - n_diff) * MLA_V_DIM:(c - n_diff + 1) * MLA_V_DIM, cols]

    def acc_at(c):
        return dacc_ref.at[c] if c < n_diff else macc_ref.at[c - n_diff]

    def step(j, bias_idx, has_next):
        hidden = None
        if bias_idx == 0:
            key = lax.broadcasted_iota(jnp.int32, (t, t), 0)
            qry = lax.broadcasted_iota(jnp.int32, (t, t), 1)
            hidden = jnp.where(key <= qry, 0.0, -NEG_BIG)

        def adjust(c, st):
            if c < n_diff:
                return st if bias_idx is None else st - bias_ref[c // 2, bias_idx]
            return st if hidden is None else st - hidden

        pending = [pend_ref[c] for c in range(AHEAD)]
        for c in range(n_chains):
            nxt = c + AHEAD
            if nxt < n_chains:
                pending.append(scores(j, nxt))
            elif has_next:
                pend_ref[nxt - n_chains] = scores(j + 1, nxt - n_chains)
            p, alpha = _softmax_probs(adjust(c, pending.pop(0)), m_ref.at[c], l_ref.at[c])
            _acc_update(acc_at(c), alpha, values(j, c), p)

    def far(j, carry):
        step(j, None, True)
        return carry

    for c in range(n_chains):
        _softmax_init(m_ref.at[c], l_ref.at[c], acc_at(c))
    for c in range(AHEAD):
        pend_ref[c] = scores(0, c)
    lax.fori_loop(0, jnp.maximum(i - 1, 0), far, 0)

    @pl.when(i >= 1)
    def _():
        step(i - 1, 1, True)

    step(i, 0, False)

    lv = lam_ref[...]
    lam = (jnp.exp(jnp.sum(lv[0:1] * lv[1:2], keepdims=True)) - jnp.exp(jnp.sum(lv[2:3] * lv[3:4], keepdims=True))
           + lam_init_ref[0])
    for h in range(DIFF_HEADS):
        c0, c1 = 2 * h, 2 * h + 1
        ot = dacc_ref[c0] / l_ref[c0] - lam * (dacc_ref[c1] / l_ref[c1])
        ot = ot * lax.rsqrt(jnp.mean(ot * ot, axis=0, keepdims=True) + NORM_EPS) * g_ref[...]
        o_ref[0, :, h * LANES:(h + 1) * LANES] = (ot * (1.0 - lam_init_ref[0])).T.astype(BF16)
    base = DIFF_HEADS * LANES
    for u in range(MLA_HEADS // 2):
        ot = jnp.concatenate([macc_ref[2 * u] / l_ref[n_diff + 2 * u],
                              macc_ref[2 * u + 1] / l_ref[n_diff + 2 * u + 1]], axis=0)
        o_ref[0, :, base + u * LANES:base + (u + 1) * LANES] = ot.T.astype(BF16)


def _attention(lam_init, diff_lambda, g, bias, dq, dk, dv, mq, mk, mv, expert_w=None, expert_layer=0):
    bsz, s, wd = dq.shape
    wq, wv = mq.shape[-1], mv.shape[-1]
    t = ATTN_TILE
    assert MAX_DISTANCE <= t and s % t == 0
    n_diff = 2 * DIFF_HEADS
    n_chains = n_diff + MLA_HEADS
    per_b = s // t
    qtile = lambda w: pl.BlockSpec((1, t, w), lambda b, i: (b, i, 0))
    whole = lambda w: pl.BlockSpec((1, s, w), lambda b, i: (b, 0, 0))
    args = [lam_init, diff_lambda, g, bias, dq, dk, dv, mq, mk, mv]
    in_specs = [pl.BlockSpec(memory_space=pltpu.SMEM),
                pl.BlockSpec(diff_lambda.shape, lambda b, i: (0, 0)),
                pl.BlockSpec(g.shape, lambda b, i: (0, 0)),
                pl.BlockSpec(bias.shape, lambda b, i: (0, 0, 0, 0)),
                qtile(wd), whole(wd), whole(wd), qtile(wq), whole(wq), whole(wv)]
    out_specs = [qtile(wd + wv)]
    out_shape = [jax.ShapeDtypeStruct((bsz, s, wd + wv), BF16)]
    if expert_w is not None:
        steps = bsz * per_b
        n_lay, n_e, d, d_ff = expert_w[0].shape
        for w in expert_w:
            rows, cols = n_e * w.shape[2], w.shape[3]
            assert rows % (16 * steps) == 0
            slab = rows // steps
            args.append(w.reshape(n_lay * rows, cols))
            in_specs.append(pl.BlockSpec((slab, cols), lambda b, i: (expert_layer * steps + b * per_b + i, 0)))
            out_specs.append(pl.BlockSpec((slab, cols), lambda b, i: (b * per_b + i, 0)))
            out_shape.append(jax.ShapeDtypeStruct((rows, cols), BF16))
    outs = pl.pallas_call(
        functools.partial(_attn_kernel, cast_weights=expert_w is not None),
        grid=(bsz, per_b),
        in_specs=in_specs,
        out_specs=out_specs,
        out_shape=out_shape,
        scratch_shapes=[pltpu.VMEM((wd, s), BF16), pltpu.VMEM((wv, s), BF16),
                        pltpu.VMEM((n_chains, 1, t), F32), pltpu.VMEM((n_chains, 1, t), F32),
                        pltpu.VMEM((n_diff, LANES, t), F32), pltpu.VMEM((MLA_HEADS, MLA_V_DIM, t), F32),
                        pltpu.VMEM((AHEAD, t, t), F32)],
        compiler_params=_params("arbitrary", "arbitrary"),
        name="attention",
    )(*args)
    if expert_w is None:
        return outs[0]
    return outs[0], tuple(wb.reshape(w.shape[1:]) for wb, w in zip(outs[1:], expert_w))


def _post_attn_kernel(*refs, moe):
    if moe:
        o_ref, x_ref, mod_ref, g_ref, wo_ref, wr_ref, x1_ref, h_ref, gates_ref, sel_ref = refs
    else:
        o_ref, x_ref, mod_ref, g_ref, wo_ref, x1_ref, h_ref = refs
    y = jnp.dot(o_ref[0], wo_ref[...], preferred_element_type=F32)
    gt_a, sh_f, sc_f = mod_ref[0, 2:3, :], mod_ref[0, 3:4, :], mod_ref[0, 4:5, :]
    x1 = x_ref[0] + (1.0 + gt_a) * y
    x1_ref[0] = x1
    h = _rms(x1, g_ref[...]) * (1.0 + sc_f) + sh_f
    if not moe:
        h_ref[0] = h.astype(BF16)
        return
    _rows_to_tiles(h, h_ref.at[0])
    wr = wr_ref[...]
    h_hi, wr_hi = h.astype(BF16), wr.astype(BF16)
    h_lo, wr_lo = (h - h_hi.astype(F32)).astype(BF16), (wr - wr_hi.astype(F32)).astype(BF16)
    logits = (jnp.dot(h_hi, wr_hi, preferred_element_type=F32) + jnp.dot(h_hi, wr_lo, preferred_element_type=F32)
              + jnp.dot(h_lo, wr_hi, preferred_element_type=F32))
    lane = lax.broadcasted_iota(jnp.int32, logits.shape, 1)
    logits = jnp.where(lane < N_EXPERTS, logits, -jnp.inf)
    v1 = jnp.max(logits, axis=1, keepdims=True)
    i1 = jnp.min(jnp.where(logits == v1, lane, LANES), axis=1, keepdims=True)
    rest = jnp.where(lane == i1, -jnp.inf, logits)
    v2 = jnp.max(rest, axis=1, keepdims=True)
    i2 = jnp.min(jnp.where(rest == v2, lane, LANES), axis=1, keepdims=True)
    e2 = jnp.exp(v2 - v1)
    w1 = 1.0 / (1.0 + e2)
    w2 = e2 / (1.0 + e2)
    gates_ref[0] = jnp.where(lane == i1, w1, 0.0) + jnp.where(lane == i2, w2, 0.0)
    sel_ref[0] = jnp.where(lane == i1, 1.0, 0.0) + jnp.where(lane == i2, 2.0, 0.0)


def _post_attn(o, x, mod, g, wo, w_router=None):
    bsz, s, d = x.shape
    moe = w_router is not None
    tm = TOK_TILE
    tok = lambda w: pl.BlockSpec((1, tm, w), lambda b, i: (b, i, 0))
    full = lambda a: pl.BlockSpec(a.shape, lambda b, i: (0,) * a.ndim)
    args = [o, x, mod, g, wo] + ([w_router] if moe else [])
    in_specs = [tok(o.shape[-1]), tok(d), pl.BlockSpec((1, ADA_CHUNKS, d), lambda b, i: (b, 0, 0)),
                full(g), full(wo)] + ([full(w_router)] if moe else [])
    if moe:
        assert d == ROW_SUB * LANES
        h_spec = pl.BlockSpec((1, tm * ROW_SUB, LANES), lambda b, i: (b, i, 0))
        out_specs = [tok(d), h_spec, tok(LANES), tok(LANES)]
        out_shape = [jax.ShapeDtypeStruct((bsz, s, d), F32), jax.ShapeDtypeStruct((bsz, s * ROW_SUB, LANES), F32)]
        out_shape += [jax.ShapeDtypeStruct((bsz, s, LANES), F32)] * 2
    else:
        out_specs = [tok(d), tok(d)]
        out_shape = [jax.ShapeDtypeStruct((bsz, s, d), F32), jax.ShapeDtypeStruct((bsz, s, d), BF16)]
    return pl.pallas_call(
        functools.partial(_post_attn_kernel, moe=moe),
        grid=(bsz, s // tm),
        in_specs=in_specs,
        out_specs=out_specs,
        out_shape=out_shape,
        compiler_params=_params("arbitrary", "arbitrary"),
        name="post_attn_moe" if moe else "post_attn",
    )(*args)


def _swiglu(h, w1_ref, w3_ref, w2_ref):
    d_ff = w1_ref.shape[-1]
    y = jnp.zeros((h.shape[0], w2_ref.shape[-1]), F32)
    for c0 in range(0, d_ff, FF_CHUNK):
        a = jnp.dot(h, w1_ref[:, c0:c0 + FF_CHUNK], preferred_element_type=F32)
        b = jnp.dot(h, w3_ref[:, c0:c0 + FF_CHUNK], preferred_element_type=F32)
        u = (a * jax.nn.sigmoid(a) * b).astype(BF16)
        y = y + jnp.dot(u, w2_ref[c0:c0 + FF_CHUNK, :], preferred_element_type=F32)
    return y


def _ffn_dense_kernel(h_ref, x1_ref, mod_ref, w1_ref, w3_ref, w2_ref, o_ref):
    gt_f = mod_ref[0, 5:6, :]
    o_ref[0] = x1_ref[0] + (1.0 + gt_f) * _swiglu(h_ref[0], w1_ref, w3_ref, w2_ref)


def _ffn_dense(h, x1, mod, w1, w3, w2):
    bsz, s, d = x1.shape
    tm = FFN_TILE
    tok = lambda w: pl.BlockSpec((1, tm, w), lambda b, i: (b, i, 0))
    full = lambda a: pl.BlockSpec(a.shape, lambda b, i: (0,) * a.ndim)
    return pl.pallas_call(
        _ffn_dense_kernel,
        grid=(bsz, s // tm),
        in_specs=[tok(d), tok(d), pl.BlockSpec((1, ADA_CHUNKS, d), lambda b, i: (b, 0, 0)),
                  full(w1), full(w3), full(w2)],
        out_specs=tok(d),
        out_shape=jax.ShapeDtypeStruct(x1.shape, F32),
        compiler_params=_params("arbitrary", "arbitrary"),
        name="ffn_dense",
    )(h, x1, mod, w1, w3, w2)


def _route_kernel(sel_ref, pos_ref, te_ref, cnt_ref, off_ref, run_ref, *, row_tile):
    p, t = pl.program_id(0), pl.program_id(1)
    sel_t = sel_ref[...].T
    chosen = (sel_t > 0.0).astype(F32)
    per_expert = jnp.sum(chosen, axis=1, keepdims=True)

    @pl.when((p == 0) & (t == 0))
    def _():
        cnt_ref[...] = jnp.zeros(cnt_ref.shape, F32)
        te_ref[...] = jnp.zeros(te_ref.shape, jnp.int32)

    @pl.when(p == 0)
    def _():
        cnt_ref[...] += per_expert

    @pl.when((p == 1) & (t == 0))
    def _():
        cnt = cnt_ref[...]
        padded = jnp.ceil(cnt / row_tile) * row_tile
        row = lax.broadcasted_iota(jnp.int32, cnt.shape, 0)
        off = jnp.zeros(cnt.shape, F32)
        for e in range(N_EXPERTS):
            size_e = jnp.sum(jnp.where(row == e, padded, 0.0), keepdims=True)
            off = off + jnp.where(row > e, size_e, 0.0)
        off_ref[...] = off
        run_ref[...] = jnp.zeros(run_ref.shape, F32)
        ends = off + padded
        tile_start = lax.broadcasted_iota(jnp.int32, (LANES, LANES), 1).astype(F32) * row_tile
        erow = lax.broadcasted_iota(jnp.int32, (LANES, LANES), 0)
        done = jnp.where((erow < N_EXPERTS) & (ends <= tile_start), 1.0, 0.0)
        te = jnp.sum(done, axis=0, keepdims=True).astype(jnp.int32)
        te_ref[...] = jnp.broadcast_to(te, te_ref.shape)

    @pl.when(p == 1)
    def _():
        tm = sel_t.shape[1]
        before = (lax.broadcasted_iota(jnp.int32, (tm, tm), 0)
                  < lax.broadcasted_iota(jnp.int32, (tm, tm), 1)).astype(BF16)
        rank = jnp.dot(chosen.astype(BF16), before, preferred_element_type=F32) + run_ref[...]
        base = off_ref[...] + rank
        for k in range(2):
            pos = jnp.sum(jnp.where(sel_t == float(k + 1), base, 0.0), axis=0, keepdims=True)
            pos_ref[0, k:k + 1, :] = pos.astype(jnp.int32)
        run_ref[...] += per_expert


def _route(sel, row_tile, n_row_tiles):
    n_tok = sel.shape[0]
    assert n_row_tiles <= LANES and 2 * n_tok < 2 ** 24
    tm = TOK_TILE
    nt = n_tok // tm
    pos, te = pl.pallas_call(
        functools.partial(_route_kernel, row_tile=row_tile),
        grid=(2, nt),
        in_specs=[pl.BlockSpec((tm, LANES), lambda p, t: (t, 0))],
        out_specs=[pl.BlockSpec((1, 2, tm), lambda p, t: (p * t, 0, 0)),
                   pl.BlockSpec((8, LANES), lambda p, t: (0, 0))],
        out_shape=[jax.ShapeDtypeStruct((nt, 2, tm), jnp.int32), jax.ShapeDtypeStruct((8, LANES), jnp.int32)],
        scratch_shapes=[pltpu.VMEM((LANES, 1), F32)] * 3,
        compiler_params=_params("arbitrary", "arbitrary"),
        name="moe_route",
    )(sel)
    return pos, te[0]


def _dispatch_kernel(pos_ref, h_ref, xs_in_ref, xs_ref, sem):
    del xs_in_ref
    tm = h_ref.shape[0] // ROW_SUB

    def body(r, carry):
        for k in range(2):
            pltpu.make_async_copy(_tile_row(h_ref, r), _tile_row(xs_ref, pos_ref[0, k, r]), sem).start(priority=k)
        return carry

    lax.fori_loop(0, tm, body, 0, unroll=8)
    for k in range(2):
        pltpu.make_async_copy(h_ref, xs_ref.at[pl.ds(0, tm * ROW_SUB)], sem).wait()


def _dispatch(pos, h, xs0):
    nt, _, tm = pos.shape
    return pl.pallas_call(
        _dispatch_kernel,
        grid=(nt,),
        in_specs=[pl.BlockSpec((1, 2, tm), lambda t: (t, 0, 0), memory_space=pltpu.SMEM),
                  pl.BlockSpec((tm * ROW_SUB, LANES), lambda t: (t, 0)),
                  pl.BlockSpec(memory_space=pl.ANY)],
        out_specs=pl.BlockSpec(memory_space=pl.ANY),
        out_shape=jax.ShapeDtypeStruct(xs0.shape, xs0.dtype),
        scratch_shapes=[pltpu.SemaphoreType.DMA(())],
        input_output_aliases={2: 0},
        compiler_params=_params("arbitrary"),
        name="moe_dispatch",
    )(pos, h, xs0)


def _expert_kernel(te_ref, xs_ref, w1_ref, w3_ref, w2_ref, y_ref):
    used = te_ref[pl.program_id(0)] < N_EXPERTS

    @pl.when(used)
    def _():
        y = _swiglu(_tiles_to_rows(xs_ref).astype(BF16), w1_ref.at[0], w3_ref.at[0], w2_ref.at[0])
        _rows_to_tiles(y, y_ref)

    @pl.when(jnp.logical_not(used))
    def _():
        y_ref[...] = jnp.zeros(y_ref.shape, y_ref.dtype)


def _experts(te, xs, w1, w3, w2, row_tile):
    n_rows = xs.shape[0] // ROW_SUB
    _, d, d_ff = w1.shape
    expert = lambda n, te: (jnp.minimum(te[n], N_EXPERTS - 1), 0, 0)
    rows = pl.BlockSpec((row_tile * ROW_SUB, LANES), lambda n, te: (n, 0))
    return pl.pallas_call(
        _expert_kernel,
        grid_spec=pltpu.PrefetchScalarGridSpec(
            num_scalar_prefetch=1,
            grid=(n_rows // row_tile,),
            in_specs=[rows, pl.BlockSpec((1, d, d_ff), expert), pl.BlockSpec((1, d, d_ff), expert),
                      pl.BlockSpec((1, d_ff, d), expert)],
            out_specs=rows),
        out_shape=jax.ShapeDtypeStruct(xs.shape, F32),
        compiler_params=_params("arbitrary"),
        name="moe_experts",
    )(te, xs, w1, w3, w2)


def _combine_kernel(pos_ref, x1_ref, gates_ref, sel_ref, mod_ref, gfin_ref, ys_ref, o_ref, ya_ref, yb_ref, sem,
                    *, final):
    tm = x1_ref.shape[1]
    bufs = (ya_ref, yb_ref)

    def body(r, carry):
        for k in range(2):
            pltpu.make_async_copy(_tile_row(ys_ref, pos_ref[0, k, r]), _tile_row(bufs[k], r), sem).start(priority=k)
        return carry

    lax.fori_loop(0, tm, body, 0, unroll=8)
    gates, sel = gates_ref[0], sel_ref[0]
    w_a = jnp.sum(jnp.where(sel == 1.0, gates, 0.0), axis=1, keepdims=True)
    w_b = jnp.sum(jnp.where(sel == 2.0, gates, 0.0), axis=1, keepdims=True)
    for k in range(2):
        pltpu.make_async_copy(ys_ref.at[pl.ds(0, tm * ROW_SUB)], bufs[k], sem).wait()
    gt_f = mod_ref[0, 5:6, :]
    x = x1_ref[0] + (1.0 + gt_f) * (w_a * _tiles_to_rows(ya_ref) + w_b * _tiles_to_rows(yb_ref))
    o_ref[0] = _rms(x, gfin_ref[...]) if final else x


def _combine(pos, x1, gates, sel, mod, ys, g_final=None):
    bsz, s, d = x1.shape
    nt, _, tm = pos.shape
    per_b = s // tm
    tok = lambda w: pl.BlockSpec((1, tm, w), lambda b, i: (b, i, 0))
    final = g_final is not None
    gfin = g_final if final else jnp.ones((1, d), F32)
    return pl.pallas_call(
        functools.partial(_combine_kernel, final=final),
        grid=(bsz, per_b),
        in_specs=[pl.BlockSpec((1, 2, tm), lambda b, i: (b * per_b + i, 0, 0), memory_space=pltpu.SMEM),
                  tok(d), tok(LANES), tok(LANES),
                  pl.BlockSpec((1, ADA_CHUNKS, d), lambda b, i: (b, 0, 0)),
                  pl.BlockSpec((1, d), lambda b, i: (0, 0)),
                  pl.BlockSpec(memory_space=pl.ANY)],
        out_specs=tok(d),
        out_shape=jax.ShapeDtypeStruct(x1.shape, F32),
        scratch_shapes=[pltpu.VMEM((tm * ROW_SUB, LANES), F32), pltpu.VMEM((tm * ROW_SUB, LANES), F32),
                        pltpu.SemaphoreType.DMA(())],
        compiler_params=_params("arbitrary", "arbitrary"),
        name="moe_combine",
    )(pos, x1, gates, sel, mod, gfin, ys)


def _moe(h, x1, gates, sel, mod, w1, w3, w2, xs_buf, g_final):
    bsz, s, d = x1.shape
    n_tok = bsz * s
    row_tile = FFN_TILE
    n_rows = xs_buf.shape[0] // ROW_SUB
    assert n_rows == _moe_rows(n_tok)
    pos, te = _route(sel.reshape(n_tok, LANES), row_tile, n_rows // row_tile)
    xs = _dispatch(pos, h.reshape(n_tok * ROW_SUB, LANES), xs_buf)
    ys = _experts(te, xs, w1, w3, w2, row_tile)
    return _combine(pos, x1, gates, sel, mod, ys, g_final), xs


def _moe_rows(n_tok):
    return 2 * n_tok + N_EXPERTS * FFN_TILE


def _final_norm_kernel(x_ref, g_ref, o_ref):
    o_ref[0] = _rms(x_ref[0], g_ref[...])


def _final_norm(x, g):
    bsz, s, d = x.shape
    tm = TOK_TILE
    tok = pl.BlockSpec((1, tm, d), lambda b, i: (b, i, 0))
    return pl.pallas_call(
        _final_norm_kernel,
        grid=(bsz, s // tm),
        in_specs=[tok, pl.BlockSpec(g.shape, lambda b, i: (0, 0))],
        out_specs=tok,
        out_shape=jax.ShapeDtypeStruct(x.shape, F32),
        compiler_params=_params("arbitrary", "arbitrary"),
        name="final_norm",
    )(x, g)


def _pad_heads(w, heads, width, lo, hi):
    k = w.shape[0]
    w = w.reshape(k, heads, width)[:, :, lo:hi]
    return jnp.pad(w, ((0, 0), (0, 0), (0, LANES - (hi - lo)))).reshape(k, heads * LANES)


def _prep_w_in(w_in):
    d = w_in.shape[0]
    kr = w_in[:, C_KR:]
    kr_block = jnp.concatenate([jnp.zeros((d, MLA_NOPE_DIM), w_in.dtype), kr,
                                jnp.zeros((d, LANES - MLA_NOPE_DIM - MLA_ROPE_DIM), w_in.dtype)], axis=1)
    return jnp.concatenate([w_in[:, :C_KR], kr_block], axis=1).astype(BF16)


def kernel(x, c, positions, w_ada, b_ada, g_attn, w_in, diff_lambda, diff_subln_g, rel_bias, mla_q_norm, w_uq, mla_kv_norm, w_ukv, w_o, g_ffn, ffn_w1, ffn_w3, ffn_w2, moe_router, moe_w1, moe_w3, moe_w2, g_final):
    depth = w_ada.shape[0]
    bsz, s, d = x.shape
    mods = _ada(c, w_ada, b_ada).reshape(depth, bsz, ADA_CHUNKS, d)
    rc, rsa, rsb = _rope_tables(positions)
    bias = _bias_tiles(rel_bias)
    qk_w = MLA_NOPE_DIM + MLA_ROPE_DIM
    kv_w = MLA_NOPE_DIM + MLA_V_DIM
    xs_buf = jnp.zeros((_moe_rows(bsz * s) * ROW_SUB, LANES), F32)
    for l in range(depth):
        mod = mods[l]
        lam_init = jnp.full((1,), 0.8 - 0.6 * math.exp(-0.3 * l), F32)
        w_uq_p = _pad_heads(w_uq[l], MLA_HEADS, qk_w, 0, qk_w).astype(BF16)
        w_uk_p = _pad_heads(w_ukv[l], MLA_HEADS, kv_w, 0, MLA_NOPE_DIM).astype(BF16)
        w_uv_p = w_ukv[l].reshape(MLA_KV_RANK, MLA_HEADS, kv_w)[:, :, MLA_NOPE_DIM:].reshape(
            MLA_KV_RANK, MLA_HEADS * MLA_V_DIM).astype(BF16)
        dq, dk, dv, mq, mk, mv = _pre_attn(
            x, mod, g_attn[l].reshape(1, d), _prep_w_in(w_in[l]), mla_q_norm[l].reshape(1, -1), w_uq_p,
            mla_kv_norm[l].reshape(1, -1), w_uk_p, w_uv_p, rc, rsa, rsb)
        attn_args = (lam_init, diff_lambda[l], diff_subln_g[l].reshape(-1, 1), bias, dq, dk, dv, mq, mk, mv)
        wo = w_o[l].astype(BF16)
        g_f = g_ffn[l].reshape(1, d)
        if l % 2 == 1:
            o, expert_w = _attention(*attn_args, expert_w=(moe_w1, moe_w3, moe_w2), expert_layer=l // 2)
            w_router = jnp.pad(moe_router[l // 2], ((0, 0), (0, LANES - N_EXPERTS)))
            x1, h, gates, sel = _post_attn(o, x, mod, g_f, wo, w_router)
            g_fin = g_final.reshape(1, d) if l == depth - 1 else None
            x, xs_buf = _moe(h, x1, gates, sel, mod, *expert_w, xs_buf, g_fin)
        else:
            o = _attention(*attn_args)
            x1, h = _post_attn(o, x, mod, g_f, wo)
            x = _ffn_dense(h, x1, mod, ffn_w1[l // 2].astype(BF16), ffn_w3[l // 2].astype(BF16),
                           ffn_w2[l // 2].astype(BF16))
    return x if depth % 2 == 0 else _final_norm(x, g_final.reshape(1, d))
```

```python
import functools
import math

import jax
import jax.numpy as jnp
from jax import lax
from jax.experimental import pallas as pl
from jax.experimental.pallas import tpu as pltpu

F32 = jnp.float32
BF16 = jnp.bfloat16

DIFF_HEADS = 4
DIFF_QK_DIM = 64
DIFF_V_DIM = 128
MLA_HEADS = 8
MLA_NOPE_DIM = 64
MLA_ROPE_DIM = 32
MLA_V_DIM = 64
MLA_Q_RANK = 384
MLA_KV_RANK = 256
ROPE_THETA = 10000.0
N_BUCKETS = 32
MAX_EXACT = 16
MAX_DISTANCE = 128
N_EXPERTS = 8
NORM_EPS = 1e-6
ADA_CHUNKS = 6

LANES = 128
NEG_BIG = -1e30
LOG2E = math.log2(math.e)
VMEM_LIMIT = 56 * 1024 * 1024

ATTN_TILE = 256
AHEAD = 4
TOK_TILE = 512
PRE_ROW_GROUPS = 2
FFN_TILE = 512
FF_CHUNK = 256

C_DQ, C_DK, C_DV, C_MQ, C_KV, C_KR, C_END = 0, 512, 1024, 1536, 1920, 2176, 2304


def _params(*sem):
    return pltpu.CompilerParams(dimension_semantics=sem, vmem_limit_bytes=VMEM_LIMIT)


def _rms(x, g):
    return x * lax.rsqrt(jnp.mean(x * x, axis=-1, keepdims=True) + NORM_EPS) * g


ROW_SUB = 8


def _rows_to_tiles(x, ref):
    n = x.shape[0]
    for j in range(ROW_SUB):
        ref[pl.ds(j, n, stride=ROW_SUB), :] = x[:, j * LANES:(j + 1) * LANES]


def _tiles_to_rows(ref):
    n = ref.shape[0] // ROW_SUB
    return jnp.concatenate([ref[pl.ds(j, n, stride=ROW_SUB), :] for j in range(ROW_SUB)], axis=1)


def _tile_row(ref, r):
    return ref.at[pl.ds(pl.multiple_of(r * ROW_SUB, ROW_SUB), ROW_SUB)]


def _ada_kernel(c_ref, w_ref, b_ref, o_ref):
    c = c_ref[...]
    cond = c * jax.nn.sigmoid(c)
    o_ref[0] = jnp.dot(cond, w_ref[0], preferred_element_type=F32,
                       precision=lax.Precision.HIGHEST) + b_ref[0]


def _ada(c, w_ada, b_ada):
    depth, d, n = w_ada.shape
    bsz = c.shape[0]
    tn = 1536
    return pl.pallas_call(
        _ada_kernel,
        grid=(depth, n // tn),
        in_specs=[pl.BlockSpec((bsz, d), lambda l, j: (0, 0)),
                  pl.BlockSpec((1, d, tn), lambda l, j: (l, 0, j)),
                  pl.BlockSpec((1, 1, tn), lambda l, j: (l, 0, j))],
        out_specs=pl.BlockSpec((1, bsz, tn), lambda l, j: (l, 0, j)),
        out_shape=jax.ShapeDtypeStruct((depth, bsz, n), F32),
        compiler_params=_params("arbitrary", "arbitrary"),
        name="ada_mod",
    )(c, w_ada, b_ada.reshape(depth, 1, n))


ROPE_HALF = MLA_ROPE_DIM // 2
NOPE_SPLIT = LANES // 2 - ROPE_HALF


def _head_block(nope, rope):
    pad = jnp.zeros(nope.shape[:-1] + (LANES - MLA_NOPE_DIM - MLA_ROPE_DIM,), nope.dtype)
    return jnp.concatenate([rope[..., :ROPE_HALF], nope[..., :NOPE_SPLIT], rope[..., ROPE_HALF:],
                            nope[..., NOPE_SPLIT:], pad], axis=-1)


def _rope_tab_kernel(pos_ref, inv_ref, c_ref, s_ref):
    pos = pos_ref[0].astype(F32)
    inv = inv_ref[...]
    ang = pos * inv
    lane = lax.broadcasted_iota(jnp.int32, ang.shape, 1)
    is_rope = inv != 0.0
    used = lane < MLA_NOPE_DIM + MLA_ROPE_DIM
    c_ref[0] = jnp.where(is_rope, jnp.cos(ang), jnp.where(used, 1.0, 0.0))
    s_ref[0] = jnp.where(is_rope, jnp.where(lane < LANES // 2, -1.0, 1.0) * jnp.sin(ang), 0.0)


def _rope_tables(positions):
    bsz, s = positions.shape
    inv_freq = ROPE_THETA ** (-jnp.arange(ROPE_HALF, dtype=F32) / ROPE_HALF)
    inv_lane = _head_block(jnp.zeros((MLA_NOPE_DIM,), F32), jnp.concatenate([inv_freq, inv_freq])).reshape(1, LANES)
    tm = TOK_TILE
    spec = pl.BlockSpec((1, tm, LANES), lambda b, i: (b, i, 0))
    shape = jax.ShapeDtypeStruct((bsz, s, LANES), F32)
    return pl.pallas_call(
        _rope_tab_kernel,
        grid=(bsz, s // tm),
        in_specs=[pl.BlockSpec((1, tm, 1), lambda b, i: (b, i, 0)),
                  pl.BlockSpec((1, LANES), lambda b, i: (0, 0))],
        out_specs=[spec, spec],
        out_shape=[shape, shape],
        compiler_params=_params("arbitrary", "arbitrary"),
        name="rope_tables",
    )(positions.reshape(bsz, s, 1), inv_lane)


def _bias_tile_kernel(rb_ref, o_ref):
    h, d = pl.program_id(0), pl.program_id(1)
    t = o_ref.shape[-1]
    key = lax.broadcasted_iota(jnp.int32, (t, t), 0)
    qry = lax.broadcasted_iota(jnp.int32, (t, t), 1)
    dist = d * t + qry - key
    n = jnp.maximum(dist, 0)
    nf = jnp.maximum(n, 1).astype(F32)
    large = MAX_EXACT + (jnp.log(nf / MAX_EXACT) / math.log(MAX_DISTANCE / MAX_EXACT)
                         * (N_BUCKETS - MAX_EXACT)).astype(jnp.int32)
    large = jnp.minimum(large, N_BUCKETS - 1)
    bucket = jnp.where(n < MAX_EXACT, n, large)
    val = jnp.zeros((t, t), F32)
    for j in range(N_BUCKETS):
        val = jnp.where(bucket == j, rb_ref[j, h], val)
    val = (rb_ref[N_BUCKETS - 1, h] - val) * LOG2E
    o_ref[0, 0] = jnp.where(dist < 0, -NEG_BIG, val)


def _bias_tiles(rel_bias):
    t = ATTN_TILE
    return pl.pallas_call(
        _bias_tile_kernel,
        grid=(DIFF_HEADS, 2),
        in_specs=[pl.BlockSpec(memory_space=pltpu.SMEM)],
        out_specs=pl.BlockSpec((1, 1, t, t), lambda h, d: (h, d, 0, 0)),
        out_shape=jax.ShapeDtypeStruct((DIFF_HEADS, 2, t, t), F32),
        compiler_params=_params("arbitrary", "arbitrary"),
        name="bias_tiles",
    )(rel_bias)


def _pre_attn_kernel(x_ref, mod_ref, g_ref, w_in_ref, gq_ref, w_uq_ref, gkv_ref, w_uk_ref, w_uv_ref,
                     rc_ref, rs_ref,
                     dq_ref, dk_ref, dv_ref, mq_ref, mk_ref, mv_ref):
    tm = x_ref.shape[1]
    for r0 in range(0, tm, tm // PRE_ROW_GROUPS):
        rows = slice(r0, r0 + tm // PRE_ROW_GROUPS)
        x = x_ref[0, rows, :]
        sh, sc = mod_ref[0, 0:1, :], mod_ref[0, 1:2, :]
        h = (_rms(x, g_ref[...]) * (1.0 + sc) + sh).astype(BF16)
        proj = jnp.dot(h, w_in_ref[...], preferred_element_type=F32)
        dq_ref[0, rows, :] = (proj[:, C_DQ:C_DK] * (DIFF_QK_DIM ** -0.5 * LOG2E)).astype(BF16)
        dk_ref[0, rows, :] = proj[:, C_DK:C_DV].astype(BF16)
        dv_ref[0, rows, :] = proj[:, C_DV:C_MQ].astype(BF16)

        rc, rs = rc_ref[0, rows, :], rs_ref[0, rows, :]

        def rope(v):
            return v * rc + pltpu.roll(v, LANES // 2, 1) * rs

        qn = _rms(proj[:, C_MQ:C_KV], gq_ref[...]).astype(BF16)
        q = jnp.dot(qn, w_uq_ref[...], preferred_element_type=F32)
        kvn = _rms(proj[:, C_KV:C_KR], gkv_ref[...]).astype(BF16)
        kn = jnp.dot(kvn, w_uk_ref[...], preferred_element_type=F32)
        mv_ref[0, rows, :] = jnp.dot(kvn, w_uv_ref[...], preferred_element_type=F32).astype(BF16)
        kr = rope(proj[:, C_KR:C_END])
        q_scale = (MLA_NOPE_DIM + MLA_ROPE_DIM) ** -0.5 * LOG2E
        for hd in range(MLA_HEADS):
            sl = slice(hd * LANES, (hd + 1) * LANES)
            mq_ref[0, rows, sl] = (rope(q[:, sl]) * q_scale).astype(BF16)
            mk_ref[0, rows, sl] = (kn[:, sl] + kr).astype(BF16)


def _pre_attn(x, mod, g, w_in, gq, w_uq, gkv, w_uk, w_uv, rc, rs):
    bsz, s, d = x.shape
    tm = TOK_TILE
    tok = lambda w: pl.BlockSpec((1, tm, w), lambda b, i: (b, i, 0))
    full = lambda a: pl.BlockSpec(a.shape, lambda b, i: (0,) * a.ndim)
    widths = (512, 512, 512, MLA_HEADS * LANES, MLA_HEADS * LANES, MLA_HEADS * MLA_V_DIM)
    return pl.pallas_call(
        _pre_attn_kernel,
        grid=(bsz, s // tm),
        in_specs=[tok(d), pl.BlockSpec((1, ADA_CHUNKS, d), lambda b, i: (b, 0, 0)), full(g), full(w_in),
                  full(gq), full(w_uq), full(gkv), full(w_uk), full(w_uv), tok(LANES), tok(LANES)],
        out_specs=[tok(w) for w in widths],
        out_shape=[jax.ShapeDtypeStruct((bsz, s, w), BF16) for w in widths],
        compiler_params=_params("arbitrary", "arbitrary"),
        name="pre_attn",
    )(x, mod, g, w_in, gq, w_uq, gkv, w_uk, w_uv, rc, rs)


def _softmax_init(m_ref, l_ref, acc_ref):
    m_ref[...] = jnp.full(m_ref.shape, NEG_BIG, F32)
    l_ref[...] = jnp.zeros(l_ref.shape, F32)
    acc_ref[...] = jnp.zeros(acc_ref.shape, F32)


def _softmax_probs(st, m_ref, l_ref):
    m_prev = m_ref[...]
    m_new = jnp.maximum(m_prev, jnp.max(st, axis=0, keepdims=True))
    alpha = jnp.exp2(m_prev - m_new)
    p = jnp.exp2(st - m_new)
    l_ref[...] = alpha * l_ref[...] + jnp.sum(p, axis=0, keepdims=True)
    m_ref[...] = m_new
    return p.astype(BF16), alpha


def _acc_update(acc_ref, alpha, vt, p):
    acc_ref[...] = alpha * acc_ref[...] + jnp.dot(vt, p, preferred_element_type=F32)


def _attn_kernel(*refs, cast_weights):
    (lam_init_ref, lam_ref, g_ref, bias_ref, dq_ref, dk_ref, dv_ref, mq_ref, mk_ref, mv_ref), refs = refs[:10], refs[10:]
    if cast_weights:
        for w_ref, wb_ref in zip(refs[:3], refs[4:7]):
            wb_ref[...] = w_ref[...].astype(BF16)
        refs = refs[3:4] + refs[7:]
    o_ref, dvt_ref, mvt_ref, m_ref, l_ref, dacc_ref, macc_ref, pend_ref = refs
    i = pl.program_id(1)
    t = dq_ref.shape[1]
    n_diff = 2 * DIFF_HEADS
    n_chains = n_diff + MLA_HEADS

    @pl.when(i == 0)
    def _():
        dvt_ref[...] = dv_ref[0].T
        mvt_ref[...] = mv_ref[0].T

    qts = []
    for h in range(DIFF_HEADS):
        qt = dq_ref[0, :, h * LANES:(h + 1) * LANES].T
        row = lax.broadcasted_iota(jnp.int32, qt.shape, 0)
        qts.append(jnp.where(row < DIFF_QK_DIM, qt, jnp.zeros_like(qt)))
        qts.append(jnp.where(row >= DIFF_QK_DIM, qt, jnp.zeros_like(qt)))
    qts += [mq_ref[0, :, h * LANES:(h + 1) * LANES].T for h in range(MLA_HEADS)]

    def scores(j, c):
        rows = pl.ds(pl.multiple_of(j * t, t), t)
        if c < n_diff:
            k = dk_ref[0, rows, (c // 2) * LANES:(c // 2 + 1) * LANES]
        else:
            k = mk_ref[0, rows, (c - n_diff) * LANES:(c - n_diff + 1) * LANES]
        return jnp.dot(k, qts[c], preferred_element_type=F32)

    def values(j, c):
        cols = pl.ds(pl.multiple_of(j * t, t), t)
        if c < n_diff:
            return dvt_ref[(c // 2) * LANES:(c // 2 + 1) * LANES, cols]
        return mvt_ref[(c - n_diff) * MLA_V_DIM:(c - n_diff + 1) * MLA_V_DIM, cols]

    def acc_at(c):
        return dacc_ref.at[c] if c < n_diff else macc_ref.at[c - n_diff]

    def step(j, bias_idx, has_next):
        hidden = None
        if bias_idx == 0:
            key = lax.broadcasted_iota(jnp.int32, (t, t), 0)
            qry = lax.broadcasted_iota(jnp.int32, (t, t), 1)
            hidden = jnp.where(key <= qry, 0.0, -NEG_BIG)

        def adjust(c, st):
            if c < n_diff:
                return st if bias_idx is None else st - bias_ref[c // 2, bias_idx]
            return st if hidden is None else st - hidden

        pending = [pend_ref[c] for c in range(AHEAD)]
        for c in range(n_chains):
            nxt = c + AHEAD
            if nxt < n_chains:
                pending.append(scores(j, nxt))
            elif has_next:
                pend_ref[nxt - n_chains] = scores(j + 1, nxt - n_chains)
            p, alpha = _softmax_probs(adjust(c, pending.pop(0)), m_ref.at[c], l_ref.at[c])
            _acc_update(acc_at(c), alpha, values(j, c), p)

    def far(j, carry):
        step(j, None, True)
        return carry

    for c in range(n_chains):
        _softmax_init(m_ref.at[c], l_ref.at[c], acc_at(c))
    for c in range(AHEAD):
        pend_ref[c] = scores(0, c)
    lax.fori_loop(0, jnp.maximum(i - 1, 0), far, 0)

    @pl.when(i >= 1)
    def _():
        step(i - 1, 1, True)

    step(i, 0, False)

    lv = lam_ref[...]
    lam = (jnp.exp(jnp.sum(lv[0:1] * lv[1:2], keepdims=True)) - jnp.exp(jnp.sum(lv[2:3] * lv[3:4], keepdims=True))
           + lam_init_ref[0])
    for h in range(DIFF_HEADS):
        c0, c1 = 2 * h, 2 * h + 1
        ot = dacc_ref[c0] / l_ref[c0] - lam * (dacc_ref[c1] / l_ref[c1])
        ot = ot * lax.rsqrt(jnp.mean(ot * ot, axis=0, keepdims=True) + NORM_EPS) * g_ref[...]
        o_ref[0, :, h * LANES:(h + 1) * LANES] = (ot * (1.0 - lam_init_ref[0])).T.astype(BF16)
    base = DIFF_HEADS * LANES
    for u in range(MLA_HEADS // 2):
        ot = jnp.concatenate([macc_ref[2 * u] / l_ref[n_diff + 2 * u],
                              macc_ref[2 * u + 1] / l_ref[n_diff + 2 * u + 1]], axis=0)
        o_ref[0, :, base + u * LANES:base + (u + 1) * LANES] = ot.T.astype(BF16)


def _attention(lam_init, diff_lambda, g, bias, dq, dk, dv, mq, mk, mv, expert_w=None, expert_layer=0):
    bsz, s, wd = dq.shape
    wq, wv = mq.shape[-1], mv.shape[-1]
    t = ATTN_TILE
    assert MAX_DISTANCE <= t and s % t == 0
    n_diff = 2 * DIFF_HEADS
    n_chains = n_diff + MLA_HEADS
    per_b = s // t
    qtile = lambda w: pl.BlockSpec((1, t, w), lambda b, i: (b, i, 0))
    whole = lambda w: pl.BlockSpec((1, s, w), lambda b, i: (b, 0, 0))
    args = [lam_init, diff_lambda, g, bias, dq, dk, dv, mq, mk, mv]
    in_specs = [pl.BlockSpec(memory_space=pltpu.SMEM),
                pl.BlockSpec(diff_lambda.shape, lambda b, i: (0, 0)),
                pl.BlockSpec(g.shape, lambda b, i: (0, 0)),
                pl.BlockSpec(bias.shape, lambda b, i: (0, 0, 0, 0)),
                qtile(wd), whole(wd), whole(wd), qtile(wq), whole(wq), whole(wv)]
    out_specs = [qtile(wd + wv)]
    out_shape = [jax.ShapeDtypeStruct((bsz, s, wd + wv), BF16)]
    if expert_w is not None:
        steps = bsz * per_b
        n_lay, n_e, d, d_ff = expert_w[0].shape
        for w in expert_w:
            rows, cols = n_e * w.shape[2], w.shape[3]
            assert rows % (16 * steps) == 0
            slab = rows // steps
            args.append(w.reshape(n_lay * rows, cols))
            in_specs.append(pl.BlockSpec((slab, cols), lambda b, i: (expert_layer * steps + b * per_b + i, 0)))
            out_specs.append(pl.BlockSpec((slab, cols), lambda b, i: (b * per_b + i, 0)))
            out_shape.append(jax.ShapeDtypeStruct((rows, cols), BF16))
    outs = pl.pallas_call(
        functools.partial(_attn_kernel, cast_weights=expert_w is not None),
        grid=(bsz, per_b),
        in_specs=in_specs,
        out_specs=out_specs,
        out_shape=out_shape,
        scratch_shapes=[pltpu.VMEM((wd, s), BF16), pltpu.VMEM((wv, s), BF16),
                        pltpu.VMEM((n_chains, 1, t), F32), pltpu.VMEM((n_chains, 1, t), F32),
                        pltpu.VMEM((n_diff, LANES, t), F32), pltpu.VMEM((MLA_HEADS, MLA_V_DIM, t), F32),
                        pltpu.VMEM((AHEAD, t, t), F32)],
        compiler_params=_params("arbitrary", "arbitrary"),
        name="attention",
    )(*args)
    if expert_w is None:
        return outs[0]
    return outs[0], tuple(wb.reshape(w.shape[1:]) for wb, w in zip(outs[1:], expert_w))


def _post_attn_kernel(*refs, moe):
    if moe:
        o_ref, x_ref, mod_ref, g_ref, wo_ref, wr_ref, x1_ref, h_ref, gates_ref, sel_ref = refs
    else:
        o_ref, x_ref, mod_ref, g_ref, wo_ref, x1_ref, h_ref = refs
    y = jnp.dot(o_ref[0], wo_ref[...], preferred_element_type=F32)
    gt_a, sh_f, sc_f = mod_ref[0, 2:3, :], mod_ref[0, 3:4, :], mod_ref[0, 4:5, :]
    x1 = x_ref[0] + (1.0 + gt_a) * y
    x1_ref[0] = x1
    h = _rms(x1, g_ref[...]) * (1.0 + sc_f) + sh_f
    if not moe:
        h_ref[0] = h.astype(BF16)
        return
    _rows_to_tiles(h, h_ref.at[0])
    wr = wr_ref[...]
    h_hi, wr_hi = h.astype(BF16), wr.astype(BF16)
    h_lo, wr_lo = (h - h_hi.astype(F32)).astype(BF16), (wr - wr_hi.astype(F32)).astype(BF16)
    logits = (jnp.dot(h_hi, wr_hi, preferred_element_type=F32) + jnp.dot(h_hi, wr_lo, preferred_element_type=F32)
              + jnp.dot(h_lo, wr_hi, preferred_element_type=F32))
    lane = lax.broadcasted_iota(jnp.int32, logits.shape, 1)
    logits = jnp.where(lane < N_EXPERTS, logits, -jnp.inf)
    v1 = jnp.max(logits, axis=1, keepdims=True)
    i1 = jnp.min(jnp.where(logits == v1, lane, LANES), axis=1, keepdims=True)
    rest = jnp.where(lane == i1, -jnp.inf, logits)
    v2 = jnp.max(rest, axis=1, keepdims=True)
    i2 = jnp.min(jnp.where(rest == v2, lane, LANES), axis=1, keepdims=True)
    e2 = jnp.exp(v2 - v1)
    w1 = 1.0 / (1.0 + e2)
    w2 = e2 / (1.0 + e2)
    gates_ref[0] = jnp.where(lane == i1, w1, 0.0) + jnp.where(lane == i2, w2, 0.0)
    sel_ref[0] = jnp.where(lane == i1, 1.0, 0.0) + jnp.where(lane == i2, 2.0, 0.0)


def _post_attn(o, x, mod, g, wo, w_router=None):
    bsz, s, d = x.shape
    moe = w_router is not None
    tm = TOK_TILE
    tok = lambda w: pl.BlockSpec((1, tm, w), lambda b, i: (b, i, 0))
    full = lambda a: pl.BlockSpec(a.shape, lambda b, i: (0,) * a.ndim)
    args = [o, x, mod, g, wo] + ([w_router] if moe else [])
    in_specs = [tok(o.shape[-1]), tok(d), pl.BlockSpec((1, ADA_CHUNKS, d), lambda b, i: (b, 0, 0)),
                full(g), full(wo)] + ([full(w_router)] if moe else [])
    if moe:
        assert d == ROW_SUB * LANES
        h_spec = pl.BlockSpec((1, tm * ROW_SUB, LANES), lambda b, i: (b, i, 0))
        out_specs = [tok(d), h_spec, tok(LANES), tok(LANES)]
        out_shape = [jax.ShapeDtypeStruct((bsz, s, d), F32), jax.ShapeDtypeStruct((bsz, s * ROW_SUB, LANES), F32)]
        out_shape += [jax.ShapeDtypeStruct((bsz, s, LANES), F32)] * 2
    else:
        out_specs = [tok(d), tok(d)]
        out_shape = [jax.ShapeDtypeStruct((bsz, s, d), F32), jax.ShapeDtypeStruct((bsz, s, d), BF16)]
    return pl.pallas_call(
        functools.partial(_post_attn_kernel, moe=moe),
        grid=(bsz, s // tm),
        in_specs=in_specs,
        out_specs=out_specs,
        out_shape=out_shape,
        compiler_params=_params("arbitrary", "arbitrary"),
        name="post_attn_moe" if moe else "post_attn",
    )(*args)


def _swiglu(h, w1_ref, w3_ref, w2_ref):
    d_ff = w1_ref.shape[-1]
    y = jnp.zeros((h.shape[0], w2_ref.shape[-1]), F32)
    for c0 in range(0, d_ff, FF_CHUNK):
        a = jnp.dot(h, w1_ref[:, c0:c0 + FF_CHUNK], preferred_element_type=F32)
        b = jnp.dot(h, w3_ref[:, c0:c0 + FF_CHUNK], preferred_element_type=F32)
        u = (a * jax.nn.sigmoid(a) * b).astype(BF16)
        y = y + jnp.dot(u, w2_ref[c0:c0 + FF_CHUNK, :], preferred_element_type=F32)
    return y


def _ffn_dense_kernel(h_ref, x1_ref, mod_ref, w1_ref, w3_ref, w2_ref, o_ref):
    gt_f = mod_ref[0, 5:6, :]
    o_ref[0] = x1_ref[0] + (1.0 + gt_f) * _swiglu(h_ref[0], w1_ref, w3_ref, w2_ref)


def _ffn_dense(h, x1, mod, w1, w3, w2):
    bsz, s, d = x1.shape
    tm = FFN_TILE
    tok = lambda w: pl.BlockSpec((1, tm, w), lambda b, i: (b, i, 0))
    full = lambda a: pl.BlockSpec(a.shape, lambda b, i: (0,) * a.ndim)
    return pl.pallas_call(
        _ffn_dense_kernel,
        grid=(bsz, s // tm),
        in_specs=[tok(d), tok(d), pl.BlockSpec((1, ADA_CHUNKS, d), lambda b, i: (b, 0, 0)),
                  full(w1), full(w3), full(w2)],
        out_specs=tok(d),
        out_shape=jax.ShapeDtypeStruct(x1.shape, F32),
        compiler_params=_params("arbitrary", "arbitrary"),
        name="ffn_dense",
    )(h, x1, mod, w1, w3, w2)


def _route_kernel(sel_ref, pos_ref, te_ref, cnt_ref, off_ref, run_ref, *, row_tile):
    p, t = pl.program_id(0), pl.program_id(1)
    sel_t = sel_ref[...].T
    chosen = (sel_t > 0.0).astype(F32)
    per_expert = jnp.sum(chosen, axis=1, keepdims=True)

    @pl.when((p == 0) & (t == 0))
    def _():
        cnt_ref[...] = jnp.zeros(cnt_ref.shape, F32)
        te_ref[...] = jnp.zeros(te_ref.shape, jnp.int32)

    @pl.when(p == 0)
    def _():
        cnt_ref[...] += per_expert

    @pl.when((p == 1) & (t == 0))
    def _():
        cnt = cnt_ref[...]
        padded = jnp.ceil(cnt / row_tile) * row_tile
        row = lax.broadcasted_iota(jnp.int32, cnt.shape, 0)
        off = jnp.zeros(cnt.shape, F32)
        for e in range(N_EXPERTS):
            size_e = jnp.sum(jnp.where(row == e, padded, 0.0), keepdims=True)
            off = off + jnp.where(row > e, size_e, 0.0)
        off_ref[...] = off
        run_ref[...] = jnp.zeros(run_ref.shape, F32)
        ends = off + padded
        tile_start = lax.broadcasted_iota(jnp.int32, (LANES, LANES), 1).astype(F32) * row_tile
        erow = lax.broadcasted_iota(jnp.int32, (LANES, LANES), 0)
        done = jnp.where((erow < N_EXPERTS) & (ends <= tile_start), 1.0, 0.0)
        te = jnp.sum(done, axis=0, keepdims=True).astype(jnp.int32)
        te_ref[...] = jnp.broadcast_to(te, te_ref.shape)

    @pl.when(p == 1)
    def _():
        tm = sel_t.shape[1]
        before = (lax.broadcasted_iota(jnp.int32, (tm, tm), 0)
                  < lax.broadcasted_iota(jnp.int32, (tm, tm), 1)).astype(BF16)
        rank = jnp.dot(chosen.astype(BF16), before, preferred_element_type=F32) + run_ref[...]
        base = off_ref[...] + rank
        for k in range(2):
            pos = jnp.sum(jnp.where(sel_t == float(k + 1), base, 0.0), axis=0, keepdims=True)
            pos_ref[0, k:k + 1, :] = pos.astype(jnp.int32)
        run_ref[...] += per_expert


def _route(sel, row_tile, n_row_tiles):
    n_tok = sel.shape[0]
    assert n_row_tiles <= LANES and 2 * n_tok < 2 ** 24
    tm = TOK_TILE
    nt = n_tok // tm
    pos, te = pl.pallas_call(
        functools.partial(_route_kernel, row_tile=row_tile),
        grid=(2, nt),
        in_specs=[pl.BlockSpec((tm, LANES), lambda p, t: (t, 0))],
        out_specs=[pl.BlockSpec((1, 2, tm), lambda p, t: (p * t, 0, 0)),
                   pl.BlockSpec((8, LANES), lambda p, t: (0, 0))],
        out_shape=[jax.ShapeDtypeStruct((nt, 2, tm), jnp.int32), jax.ShapeDtypeStruct((8, LANES), jnp.int32)],
        scratch_shapes=[pltpu.VMEM((LANES, 1), F32)] * 3,
        compiler_params=_params("arbitrary", "arbitrary"),
        name="moe_route",
    )(sel)
    return pos, te[0]


def _dispatch_kernel(pos_ref, h_ref, xs_in_ref, xs_ref, sem):
    del xs_in_ref
    tm = h_ref.shape[0] // ROW_SUB

    def body(r, carry):
        for k in range(2):
            pltpu.make_async_copy(_tile_row(h_ref, r), _tile_row(xs_ref, pos_ref[0, k, r]), sem).start(priority=k)
        return carry

    lax.fori_loop(0, tm, body, 0, unroll=8)
    for k in range(2):
        pltpu.make_async_copy(h_ref, xs_ref.at[pl.ds(0, tm * ROW_SUB)], sem).wait()


def _dispatch(pos, h, xs0):
    nt, _, tm = pos.shape
    return pl.pallas_call(
        _dispatch_kernel,
        grid=(nt,),
        in_specs=[pl.BlockSpec((1, 2, tm), lambda t: (t, 0, 0), memory_space=pltpu.SMEM),
                  pl.BlockSpec((tm * ROW_SUB, LANES), lambda t: (t, 0)),
                  pl.BlockSpec(memory_space=pl.ANY)],
        out_specs=pl.BlockSpec(memory_space=pl.ANY),
        out_shape=jax.ShapeDtypeStruct(xs0.shape, xs0.dtype),
        scratch_shapes=[pltpu.SemaphoreType.DMA(())],
        input_output_aliases={2: 0},
        compiler_params=_params("arbitrary"),
        name="moe_dispatch",
    )(pos, h, xs0)


def _expert_kernel(te_ref, xs_ref, w1_ref, w3_ref, w2_ref, y_ref):
    used = te_ref[pl.program_id(0)] < N_EXPERTS

    @pl.when(used)
    def _():
        y = _swiglu(_tiles_to_rows(xs_ref).astype(BF16), w1_ref.at[0], w3_ref.at[0], w2_ref.at[0])
        _rows_to_tiles(y, y_ref)

    @pl.when(jnp.logical_not(used))
    def _():
        y_ref[...] = jnp.zeros(y_ref.shape, y_ref.dtype)


def _experts(te, xs, w1, w3, w2, row_tile):
    n_rows = xs.shape[0] // ROW_SUB
    _, d, d_ff = w1.shape
    expert = lambda n, te: (jnp.minimum(te[n], N_EXPERTS - 1), 0, 0)
    rows = pl.BlockSpec((row_tile * ROW_SUB, LANES), lambda n, te: (n, 0))
    return pl.pallas_call(
        _expert_kernel,
        grid_spec=pltpu.PrefetchScalarGridSpec(
            num_scalar_prefetch=1,
            grid=(n_rows // row_tile,),
            in_specs=[rows, pl.BlockSpec((1, d, d_ff), expert), pl.BlockSpec((1, d, d_ff), expert),
                      pl.BlockSpec((1, d_ff, d), expert)],
            out_specs=rows),
        out_shape=jax.ShapeDtypeStruct(xs.shape, F32),
        compiler_params=_params("arbitrary"),
        name="moe_experts",
    )(te, xs, w1, w3, w2)


def _combine_kernel(pos_ref, x1_ref, gates_ref, sel_ref, mod_ref, gfin_ref, ys_ref, o_ref, ya_ref, yb_ref, sem,
                    *, final):
    tm = x1_ref.shape[1]
    bufs = (ya_ref, yb_ref)

    def body(r, carry):
        for k in range(2):
            pltpu.make_async_copy(_tile_row(ys_ref, pos_ref[0, k, r]), _tile_row(bufs[k], r), sem).start(priority=k)
        return carry

    lax.fori_loop(0, tm, body, 0, unroll=8)
    gates, sel = gates_ref[0], sel_ref[0]
    w_a = jnp.sum(jnp.where(sel == 1.0, gates, 0.0), axis=1, keepdims=True)
    w_b = jnp.sum(jnp.where(sel == 2.0, gates, 0.0), axis=1, keepdims=True)
    for k in range(2):
        pltpu.make_async_copy(ys_ref.at[pl.ds(0, tm * ROW_SUB)], bufs[k], sem).wait()
    gt_f = mod_ref[0, 5:6, :]
    x = x1_ref[0] + (1.0 + gt_f) * (w_a * _tiles_to_rows(ya_ref) + w_b * _tiles_to_rows(yb_ref))
    o_ref[0] = _rms(x, gfin_ref[...]) if final else x


def _combine(pos, x1, gates, sel, mod, ys, g_final=None):
    bsz, s, d = x1.shape
    nt, _, tm = pos.shape
    per_b = s // tm
    tok = lambda w: pl.BlockSpec((1, tm, w), lambda b, i: (b, i, 0))
    final = g_final is not None
    gfin = g_final if final else jnp.ones((1, d), F32)
    return pl.pallas_call(
        functools.partial(_combine_kernel, final=final),
        grid=(bsz, per_b),
        in_specs=[pl.BlockSpec((1, 2, tm), lambda b, i: (b * per_b + i, 0, 0), memory_space=pltpu.SMEM),
                  tok(d), tok(LANES), tok(LANES),
                  pl.BlockSpec((1, ADA_CHUNKS, d), lambda b, i: (b, 0, 0)),
                  pl.BlockSpec((1, d), lambda b, i: (0, 0)),
                  pl.BlockSpec(memory_space=pl.ANY)],
        out_specs=tok(d),
        out_shape=jax.ShapeDtypeStruct(x1.shape, F32),
        scratch_shapes=[pltpu.VMEM((tm * ROW_SUB, LANES), F32), pltpu.VMEM((tm * ROW_SUB, LANES), F32),
                        pltpu.SemaphoreType.DMA(())],
        compiler_params=_params("arbitrary", "arbitrary"),
        name="moe_combine",
    )(pos, x1, gates, sel, mod, gfin, ys)


def _moe(h, x1, gates, sel, mod, w1, w3, w2, xs_buf, g_final):
    bsz, s, d = x1.shape
    n_tok = bsz * s
    row_tile = FFN_TILE
    n_rows = xs_buf.shape[0] // ROW_SUB
    assert n_rows == _moe_rows(n_tok)
    pos, te = _route(sel.reshape(n_tok, LANES), row_tile, n_rows // row_tile)
    xs = _dispatch(pos, h.reshape(n_tok * ROW_SUB, LANES), xs_buf)
    ys = _experts(te, xs, w1, w3, w2, row_tile)
    return _combine(pos, x1, gates, sel, mod, ys, g_final), xs


def _moe_rows(n_tok):
    return 2 * n_tok + N_EXPERTS * FFN_TILE


def _final_norm_kernel(x_ref, g_ref, o_ref):
    o_ref[0] = _rms(x_ref[0], g_ref[...])


def _final_norm(x, g):
    bsz, s, d = x.shape
    tm = TOK_TILE
    tok = pl.BlockSpec((1, tm, d), lambda b, i: (b, i, 0))
    return pl.pallas_call(
        _final_norm_kernel,
        grid=(bsz, s // tm),
        in_specs=[tok, pl.BlockSpec(g.shape, lambda b, i: (0, 0))],
        out_specs=tok,
        out_shape=jax.ShapeDtypeStruct(x.shape, F32),
        compiler_params=_params("arbitrary", "arbitrary"),
        name="final_norm",
    )(x, g)


def _head_blocks(w, width, nope_cols, rope_cols):
    k = w.shape[0]
    w = w.reshape(k, MLA_HEADS, width)
    nope = w[:, :, nope_cols] if nope_cols is not None else jnp.zeros((k, MLA_HEADS, MLA_NOPE_DIM), w.dtype)
    rope = w[:, :, rope_cols] if rope_cols is not None else jnp.zeros((k, MLA_HEADS, MLA_ROPE_DIM), w.dtype)
    return _head_block(nope, rope).reshape(k, MLA_HEADS * LANES)


def _prep_w_in(w_in):
    d = w_in.shape[0]
    kr_block = _head_block(jnp.zeros((d, MLA_NOPE_DIM), w_in.dtype), w_in[:, C_KR:])
    return jnp.concatenate([w_in[:, :C_KR], kr_block], axis=1).astype(BF16)


def kernel(x, c, positions, w_ada, b_ada, g_attn, w_in, diff_lambda, diff_subln_g, rel_bias, mla_q_norm, w_uq, mla_kv_norm, w_ukv, w_o, g_ffn, ffn_w1, ffn_w3, ffn_w2, moe_router, moe_w1, moe_w3, moe_w2, g_final):
    depth = w_ada.shape[0]
    bsz, s, d = x.shape
    mods = _ada(c, w_ada, b_ada).reshape(depth, bsz, ADA_CHUNKS, d)
    rc, rs = _rope_tables(positions)
    bias = _bias_tiles(rel_bias)
    qk_w = MLA_NOPE_DIM + MLA_ROPE_DIM
    kv_w = MLA_NOPE_DIM + MLA_V_DIM
    xs_buf = jnp.zeros((_moe_rows(bsz * s) * ROW_SUB, LANES), F32)
    for l in range(depth):
        mod = mods[l]
        lam_init = jnp.full((1,), 0.8 - 0.6 * math.exp(-0.3 * l), F32)
        w_uq_p = _head_blocks(w_uq[l], qk_w, slice(0, MLA_NOPE_DIM), slice(MLA_NOPE_DIM, qk_w)).astype(BF16)
        w_uk_p = _head_blocks(w_ukv[l], kv_w, slice(0, MLA_NOPE_DIM), None).astype(BF16)
        w_uv_p = w_ukv[l].reshape(MLA_KV_RANK, MLA_HEADS, kv_w)[:, :, MLA_NOPE_DIM:].reshape(
            MLA_KV_RANK, MLA_HEADS * MLA_V_DIM).astype(BF16)
        dq, dk, dv, mq, mk, mv = _pre_attn(
            x, mod, g_attn[l].reshape(1, d), _prep_w_in(w_in[l]), mla_q_norm[l].reshape(1, -1), w_uq_p,
            mla_kv_norm[l].reshape(1, -1), w_uk_p, w_uv_p, rc, rs)
        attn_args = (lam_init, diff_lambda[l], diff_subln_g[l].reshape(-1, 1), bias, dq, dk, dv, mq, mk, mv)
        wo = w_o[l].astype(BF16)
        g_f = g_ffn[l].reshape(1, d)
        if l % 2 == 1:
            o, expert_w = _attention(*attn_args, expert_w=(moe_w1, moe_w3, moe_w2), expert_layer=l // 2)
            w_router = jnp.pad(moe_router[l // 2], ((0, 0), (0, LANES - N_EXPERTS)))
            x1, h, gates, sel = _post_attn(o, x, mod, g_f, wo, w_router)
            g_fin = g_final.reshape(1, d) if l == depth - 1 else None
            x, xs_buf = _moe(h, x1, gates, sel, mod, *expert_w, xs_buf, g_fin)
        else:
            o = _attention(*attn_args)
            x1, h = _post_attn(o, x, mod, g_f, wo)
            x = _ffn_dense(h, x1, mod, ffn_w1[l // 2].astype(BF16), ffn_w3[l // 2].astype(BF16),
                           ffn_w2[l // 2].astype(BF16))
    return x if depth % 2 == 0 else _final_norm(x, g_final.reshape(1, d))
```

```python
import functools
import math

import jax
import jax.numpy as jnp
from jax import lax
from jax.experimental import pallas as pl
from jax.experimental.pallas import tpu as pltpu

F32 = jnp.float32
BF16 = jnp.bfloat16

DIFF_HEADS = 4
DIFF_QK_DIM = 64
DIFF_V_DIM = 128
MLA_HEADS = 8
MLA_NOPE_DIM = 64
MLA_ROPE_DIM = 32
MLA_V_DIM = 64
MLA_Q_RANK = 384
MLA_KV_RANK = 256
ROPE_THETA = 10000.0
N_BUCKETS = 32
MAX_EXACT = 16
MAX_DISTANCE = 128
N_EXPERTS = 8
NORM_EPS = 1e-6
ADA_CHUNKS = 6

LANES = 128
NEG_BIG = -1e30
LOG2E = math.log2(math.e)
VMEM_LIMIT = 56 * 1024 * 1024

ATTN_TILE = 256
AHEAD = 4
TOK_TILE = 512
PRE_ROW_GROUPS = 2
FFN_TILE = 512
FF_CHUNK = 256

C_DQ, C_DK, C_DV, C_MQ, C_KV, C_KR, C_END = 0, 512, 1024, 1536, 1920, 2176, 2304


def _params(*sem):
    return pltpu.CompilerParams(dimension_semantics=sem, vmem_limit_bytes=VMEM_LIMIT)


def _rms(x, g):
    return x * lax.rsqrt(jnp.mean(x * x, axis=-1, keepdims=True) + NORM_EPS) * g


ROW_SUB = 8


def _rows_to_tiles(x, ref):
    n = x.shape[0]
    for j in range(ROW_SUB):
        ref[pl.ds(j, n, stride=ROW_SUB), :] = x[:, j * LANES:(j + 1) * LANES]


def _tiles_to_rows(ref):
    n = ref.shape[0] // ROW_SUB
    return jnp.concatenate([ref[pl.ds(j, n, stride=ROW_SUB), :] for j in range(ROW_SUB)], axis=1)


def _tile_row(ref, r):
    return ref.at[pl.ds(pl.multiple_of(r * ROW_SUB, ROW_SUB), ROW_SUB)]


def _ada_kernel(c_ref, w_ref, b_ref, o_ref):
    c = c_ref[...]
    cond = c * jax.nn.sigmoid(c)
    o_ref[0] = jnp.dot(cond, w_ref[0], preferred_element_type=F32,
                       precision=lax.Precision.HIGHEST) + b_ref[0]


def _ada(c, w_ada, b_ada):
    depth, d, n = w_ada.shape
    bsz = c.shape[0]
    tn = 1536
    return pl.pallas_call(
        _ada_kernel,
        grid=(depth, n // tn),
        in_specs=[pl.BlockSpec((bsz, d), lambda l, j: (0, 0)),
                  pl.BlockSpec((1, d, tn), lambda l, j: (l, 0, j)),
                  pl.BlockSpec((1, 1, tn), lambda l, j: (l, 0, j))],
        out_specs=pl.BlockSpec((1, bsz, tn), lambda l, j: (l, 0, j)),
        out_shape=jax.ShapeDtypeStruct((depth, bsz, n), F32),
        compiler_params=_params("arbitrary", "arbitrary"),
        name="ada_mod",
    )(c, w_ada, b_ada.reshape(depth, 1, n))


ROPE_HALF = MLA_ROPE_DIM // 2
NOPE_SPLIT = LANES // 2 - ROPE_HALF


def _head_block(nope, rope):
    pad = jnp.zeros(nope.shape[:-1] + (LANES - MLA_NOPE_DIM - MLA_ROPE_DIM,), nope.dtype)
    return jnp.concatenate([rope[..., :ROPE_HALF], nope[..., :NOPE_SPLIT], rope[..., ROPE_HALF:],
                            nope[..., NOPE_SPLIT:], pad], axis=-1)


def _rope_tab_kernel(pos_ref, inv_ref, c_ref, s_ref):
    pos = pos_ref[0].astype(F32)
    inv = inv_ref[...]
    ang = pos * inv
    lane = lax.broadcasted_iota(jnp.int32, ang.shape, 1)
    is_rope = inv != 0.0
    used = lane < MLA_NOPE_DIM + MLA_ROPE_DIM
    c_ref[0] = jnp.where(is_rope, jnp.cos(ang), jnp.where(used, 1.0, 0.0))
    s_ref[0] = jnp.where(is_rope, jnp.where(lane < LANES // 2, -1.0, 1.0) * jnp.sin(ang), 0.0)


def _rope_tables(positions):
    bsz, s = positions.shape
    inv_freq = ROPE_THETA ** (-jnp.arange(ROPE_HALF, dtype=F32) / ROPE_HALF)
    inv_lane = _head_block(jnp.zeros((MLA_NOPE_DIM,), F32), jnp.concatenate([inv_freq, inv_freq])).reshape(1, LANES)
    tm = TOK_TILE
    spec = pl.BlockSpec((1, tm, LANES), lambda b, i: (b, i, 0))
    shape = jax.ShapeDtypeStruct((bsz, s, LANES), F32)
    return pl.pallas_call(
        _rope_tab_kernel,
        grid=(bsz, s // tm),
        in_specs=[pl.BlockSpec((1, tm, 1), lambda b, i: (b, i, 0)),
                  pl.BlockSpec((1, LANES), lambda b, i: (0, 0))],
        out_specs=[spec, spec],
        out_shape=[shape, shape],
        compiler_params=_params("arbitrary", "arbitrary"),
        name="rope_tables",
    )(positions.reshape(bsz, s, 1), inv_lane)


def _bias_tile_kernel(rb_ref, o_ref):
    h, d = pl.program_id(0), pl.program_id(1)
    t = o_ref.shape[-1]
    key = lax.broadcasted_iota(jnp.int32, (t, t), 0)
    qry = lax.broadcasted_iota(jnp.int32, (t, t), 1)
    dist = d * t + qry - key
    n = jnp.maximum(dist, 0)
    nf = jnp.maximum(n, 1).astype(F32)
    large = MAX_EXACT + (jnp.log(nf / MAX_EXACT) / math.log(MAX_DISTANCE / MAX_EXACT)
                         * (N_BUCKETS - MAX_EXACT)).astype(jnp.int32)
    large = jnp.minimum(large, N_BUCKETS - 1)
    bucket = jnp.where(n < MAX_EXACT, n, large)
    val = jnp.zeros((t, t), F32)
    for j in range(N_BUCKETS):
        val = jnp.where(bucket == j, rb_ref[j, h], val)
    val = (rb_ref[N_BUCKETS - 1, h] - val) * LOG2E
    o_ref[0, 0] = jnp.where(dist < 0, -NEG_BIG, val)


def _bias_tiles(rel_bias):
    t = ATTN_TILE
    return pl.pallas_call(
        _bias_tile_kernel,
        grid=(DIFF_HEADS, 2),
        in_specs=[pl.BlockSpec(memory_space=pltpu.SMEM)],
        out_specs=pl.BlockSpec((1, 1, t, t), lambda h, d: (h, d, 0, 0)),
        out_shape=jax.ShapeDtypeStruct((DIFF_HEADS, 2, t, t), F32),
        compiler_params=_params("arbitrary", "arbitrary"),
        name="bias_tiles",
    )(rel_bias)


def _pre_attn_kernel(x_ref, mod_ref, g_ref, w_in_ref, gq_ref, w_uq_ref, gkv_ref, w_uk_ref, w_uv_ref,
                     rc_ref, rs_ref,
                     dq_ref, dk_ref, dv_ref, mq_ref, mk_ref, mv_ref):
    tm = x_ref.shape[1]
    for r0 in range(0, tm, tm // PRE_ROW_GROUPS):
        rows = slice(r0, r0 + tm // PRE_ROW_GROUPS)
        x = x_ref[0, rows, :]
        sh, sc = mod_ref[0, 0:1, :], mod_ref[0, 1:2, :]
        h = (_rms(x, g_ref[...]) * (1.0 + sc) + sh).astype(BF16)
        proj = jnp.dot(h, w_in_ref[...], preferred_element_type=F32)
        dq_ref[0, rows, :] = (proj[:, C_DQ:C_DK] * (DIFF_QK_DIM ** -0.5 * LOG2E)).astype(BF16)
        dk_ref[0, rows, :] = proj[:, C_DK:C_DV].astype(BF16)
        dv_ref[0, rows, :] = proj[:, C_DV:C_MQ].astype(BF16)

        rc, rs = rc_ref[0, rows, :], rs_ref[0, rows, :]

        def rope(v):
            return v * rc + pltpu.roll(v, LANES // 2, 1) * rs

        qn = _rms(proj[:, C_MQ:C_KV], gq_ref[...]).astype(BF16)
        q = jnp.dot(qn, w_uq_ref[...], preferred_element_type=F32)
        kvn = _rms(proj[:, C_KV:C_KR], gkv_ref[...]).astype(BF16)
        kn = jnp.dot(kvn, w_uk_ref[...], preferred_element_type=F32)
        mv_ref[0, rows, :] = jnp.dot(kvn, w_uv_ref[...], preferred_element_type=F32).astype(BF16)
        kr = rope(proj[:, C_KR:C_END])
        q_scale = (MLA_NOPE_DIM + MLA_ROPE_DIM) ** -0.5 * LOG2E
        for hd in range(MLA_HEADS):
            sl = slice(hd * LANES, (hd + 1) * LANES)
            mq_ref[0, rows, sl] = (rope(q[:, sl]) * q_scale).astype(BF16)
            mk_ref[0, rows, sl] = (kn[:, sl] + kr).astype(BF16)


def _pre_attn(x, mod, g, w_in, gq, w_uq, gkv, w_uk, w_uv, rc, rs):
    bsz, s, d = x.shape
    tm = TOK_TILE
    tok = lambda w: pl.BlockSpec((1, tm, w), lambda b, i: (b, i, 0))
    full = lambda a: pl.BlockSpec(a.shape, lambda b, i: (0,) * a.ndim)
    widths = (512, 512, 512, MLA_HEADS * LANES, MLA_HEADS * LANES, MLA_HEADS * MLA_V_DIM)
    return pl.pallas_call(
        _pre_attn_kernel,
        grid=(bsz, s // tm),
        in_specs=[tok(d), pl.BlockSpec((1, ADA_CHUNKS, d), lambda b, i: (b, 0, 0)), full(g), full(w_in),
                  full(gq), full(w_uq), full(gkv), full(w_uk), full(w_uv), tok(LANES), tok(LANES)],
        out_specs=[tok(w) for w in widths],
        out_shape=[jax.ShapeDtypeStruct((bsz, s, w), BF16) for w in widths],
        compiler_params=_params("arbitrary", "arbitrary"),
        name="pre_attn",
    )(x, mod, g, w_in, gq, w_uq, gkv, w_uk, w_uv, rc, rs)


def _softmax_init(m_ref, l_ref, acc_ref):
    m_ref[...] = jnp.full(m_ref.shape, NEG_BIG, F32)
    l_ref[...] = jnp.zeros(l_ref.shape, F32)
    acc_ref[...] = jnp.zeros(acc_ref.shape, F32)


def _softmax_probs(st, m_ref, l_ref):
    m_prev = m_ref[...]
    m_new = jnp.maximum(m_prev, jnp.max(st, axis=0, keepdims=True))
    alpha = jnp.exp2(m_prev - m_new)
    p = jnp.exp2(st - m_new)
    l_ref[...] = alpha * l_ref[...] + jnp.sum(p, axis=0, keepdims=True)
    m_ref[...] = m_new
    return p.astype(BF16), alpha


def _acc_update(acc_ref, alpha, vt, p):
    acc_ref[...] = alpha * acc_ref[...] + jnp.dot(vt, p, preferred_element_type=F32)


def _attn_kernel(*refs, cast_weights):
    (lam_init_ref, lam_ref, g_ref, bias_ref, dq_ref, dk_ref, dv_ref, mq_ref, mk_ref, mv_ref), refs = refs[:10], refs[10:]
    if cast_weights:
        for w_ref, wb_ref in zip(refs[:3], refs[4:7]):
            wb_ref[...] = w_ref[...].astype(BF16)
        refs = refs[3:4] + refs[7:]
    o_ref, dvt_ref, mvt_ref, m_ref, l_ref, dacc_ref, macc_ref, pend_ref = refs
    i = pl.program_id(1)
    t = dq_ref.shape[1]
    n_diff = 2 * DIFF_HEADS
    n_chains = n_diff + MLA_HEADS

    @pl.when(i == 0)
    def _():
        dvt_ref[...] = dv_ref[0].T
        mvt_ref[...] = mv_ref[0].T

    qts = []
    for h in range(DIFF_HEADS):
        qt = dq_ref[0, :, h * LANES:(h + 1) * LANES].T
        row = lax.broadcasted_iota(jnp.int32, qt.shape, 0)
        qts.append(jnp.where(row < DIFF_QK_DIM, qt, jnp.zeros_like(qt)))
        qts.append(jnp.where(row >= DIFF_QK_DIM, qt, jnp.zeros_like(qt)))
    qts += [mq_ref[0, :, h * LANES:(h + 1) * LANES].T for h in range(MLA_HEADS)]

    def scores(j, c):
        rows = pl.ds(pl.multiple_of(j * t, t), t)
        if c < n_diff:
            k = dk_ref[0, rows, (c // 2) * LANES:(c // 2 + 1) * LANES]
        else:
            k = mk_ref[0, rows, (c - n_diff) * LANES:(c - n_diff + 1) * LANES]
        return jnp.dot(k, qts[c], preferred_element_type=F32)

    def values(j, c):
        cols = pl.ds(pl.multiple_of(j * t, t), t)
        if c < n_diff:
            return dvt_ref[(c // 2) * LANES:(c // 2 + 1) * LANES, cols]
        return mvt_ref[(c - n_diff) * MLA_V_DIM:(c - n_diff + 1) * MLA_V_DIM, cols]

    def acc_at(c):
        return dacc_ref.at[c] if c < n_diff else macc_ref.at[c - n_diff]

    def step(j, bias_idx, has_next):
        hidden = None
        if bias_idx == 0:
            key = lax.broadcasted_iota(jnp.int32, (t, t), 0)
            qry = lax.broadcasted_iota(jnp.int32, (t, t), 1)
            hidden = jnp.where(key <= qry, 0.0, -NEG_BIG)

        def adjust(c, st):
            if c < n_diff:
                return st if bias_idx is None else st - bias_ref[c // 2, bias_idx]
            return st if hidden is None else st - hidden

        pending = [pend_ref[c] for c in range(AHEAD)]
        for c in range(n_chains):
            nxt = c + AHEAD
            if nxt < n_chains:
                pending.append(scores(j, nxt))
            elif has_next:
                pend_ref[nxt - n_chains] = scores(j + 1, nxt - n_chains)
            p, alpha = _softmax_probs(adjust(c, pending.pop(0)), m_ref.at[c], l_ref.at[c])
            _acc_update(acc_at(c), alpha, values(j, c), p)

    def far(j, carry):
        step(j, None, True)
        return carry

    for c in range(n_chains):
        _softmax_init(m_ref.at[c], l_ref.at[c], acc_at(c))
    for c in range(AHEAD):
        pend_ref[c] = scores(0, c)
    lax.fori_loop(0, jnp.maximum(i - 1, 0), far, 0)

    @pl.when(i >= 1)
    def _():
        step(i - 1, 1, True)

    step(i, 0, False)

    lv = lam_ref[...]
    lam = (jnp.exp(jnp.sum(lv[0:1] * lv[1:2], keepdims=True)) - jnp.exp(jnp.sum(lv[2:3] * lv[3:4], keepdims=True))
           + lam_init_ref[0])
    for h in range(DIFF_HEADS):
        c0, c1 = 2 * h, 2 * h + 1
        ot = dacc_ref[c0] / l_ref[c0] - lam * (dacc_ref[c1] / l_ref[c1])
        ot = ot * lax.rsqrt(jnp.mean(ot * ot, axis=0, keepdims=True) + NORM_EPS) * g_ref[...]
        o_ref[0, :, h * LANES:(h + 1) * LANES] = (ot * (1.0 - lam_init_ref[0])).T.astype(BF16)
    base = DIFF_HEADS * LANES
    for u in range(MLA_HEADS // 2):
        ot = jnp.concatenate([macc_ref[2 * u] / l_ref[n_diff + 2 * u],
                              macc_ref[2 * u + 1] / l_ref[n_diff + 2 * u + 1]], axis=0)
        o_ref[0, :, base + u * LANES:base + (u + 1) * LANES] = ot.T.astype(BF16)


def _attention(lam_init, diff_lambda, g, bias, dq, dk, dv, mq, mk, mv, expert_w=None, expert_layer=0):
    bsz, s, wd = dq.shape
    wq, wv = mq.shape[-1], mv.shape[-1]
    t = ATTN_TILE
    assert MAX_DISTANCE <= t and s % t == 0
    n_diff = 2 * DIFF_HEADS
    n_chains = n_diff + MLA_HEADS
    per_b = s // t
    qtile = lambda w: pl.BlockSpec((1, t, w), lambda b, i: (b, i, 0))
    whole = lambda w: pl.BlockSpec((1, s, w), lambda b, i: (b, 0, 0))
    args = [lam_init, diff_lambda, g, bias, dq, dk, dv, mq, mk, mv]
    in_specs = [pl.BlockSpec(memory_space=pltpu.SMEM),
                pl.BlockSpec(diff_lambda.shape, lambda b, i: (0, 0)),
                pl.BlockSpec(g.shape, lambda b, i: (0, 0)),
                pl.BlockSpec(bias.shape, lambda b, i: (0, 0, 0, 0)),
                qtile(wd), whole(wd), whole(wd), qtile(wq), whole(wq), whole(wv)]
    out_specs = [qtile(wd + wv)]
    out_shape = [jax.ShapeDtypeStruct((bsz, s, wd + wv), BF16)]
    if expert_w is not None:
        steps = bsz * per_b
        n_lay, n_e, d, d_ff = expert_w[0].shape
        for w in expert_w:
            rows, cols = n_e * w.shape[2], w.shape[3]
            assert rows % (16 * steps) == 0
            slab = rows // steps
            args.append(w.reshape(n_lay * rows, cols))
            in_specs.append(pl.BlockSpec((slab, cols), lambda b, i: (expert_layer * steps + b * per_b + i, 0)))
            out_specs.append(pl.BlockSpec((slab, cols), lambda b, i: (b * per_b + i, 0)))
            out_shape.append(jax.ShapeDtypeStruct((rows, cols), BF16))
    outs = pl.pallas_call(
        functools.partial(_attn_kernel, cast_weights=expert_w is not None),
        grid=(bsz, per_b),
        in_specs=in_specs,
        out_specs=out_specs,
        out_shape=out_shape,
        scratch_shapes=[pltpu.VMEM((wd, s), BF16), pltpu.VMEM((wv, s), BF16),
                        pltpu.VMEM((n_chains, 1, t), F32), pltpu.VMEM((n_chains, 1, t), F32),
                        pltpu.VMEM((n_diff, LANES, t), F32), pltpu.VMEM((MLA_HEADS, MLA_V_DIM, t), F32),
                        pltpu.VMEM((AHEAD, t, t), F32)],
        compiler_params=_params("arbitrary", "arbitrary"),
        name="attention",
    )(*args)
    if expert_w is None:
        return outs[0]
    return outs[0], tuple(wb.reshape(w.shape[1:]) for wb, w in zip(outs[1:], expert_w))


def _post_attn_kernel(*refs, moe):
    if moe:
        o_ref, x_ref, mod_ref, g_ref, wo_ref, wr_ref, x1_ref, h_ref, gates_ref, sel_ref = refs
    else:
        o_ref, x_ref, mod_ref, g_ref, wo_ref, x1_ref, h_ref = refs
    y = jnp.dot(o_ref[0], wo_ref[...], preferred_element_type=F32)
    gt_a, sh_f, sc_f = mod_ref[0, 2:3, :], mod_ref[0, 3:4, :], mod_ref[0, 4:5, :]
    x1 = x_ref[0] + (1.0 + gt_a) * y
    x1_ref[0] = x1
    h = _rms(x1, g_ref[...]) * (1.0 + sc_f) + sh_f
    if not moe:
        h_ref[0] = h.astype(BF16)
        return
    _rows_to_tiles(h, h_ref.at[0])
    wr = wr_ref[...]
    h_hi, wr_hi = h.astype(BF16), wr.astype(BF16)
    h_lo, wr_lo = (h - h_hi.astype(F32)).astype(BF16), (wr - wr_hi.astype(F32)).astype(BF16)
    logits = (jnp.dot(h_hi, wr_hi, preferred_element_type=F32) + jnp.dot(h_hi, wr_lo, preferred_element_type=F32)
              + jnp.dot(h_lo, wr_hi, preferred_element_type=F32))
    lane = lax.broadcasted_iota(jnp.int32, logits.shape, 1)
    logits = jnp.where(lane < N_EXPERTS, logits, -jnp.inf)
    v1 = jnp.max(logits, axis=1, keepdims=True)
    i1 = jnp.min(jnp.where(logits == v1, lane, LANES), axis=1, keepdims=True)
    rest = jnp.where(lane == i1, -jnp.inf, logits)
    v2 = jnp.max(rest, axis=1, keepdims=True)
    i2 = jnp.min(jnp.where(rest == v2, lane, LANES), axis=1, keepdims=True)
    e2 = jnp.exp(v2 - v1)
    w1 = 1.0 / (1.0 + e2)
    w2 = e2 / (1.0 + e2)
    gates_ref[0] = jnp.where(lane == i1, w1, 0.0) + jnp.where(lane == i2, w2, 0.0)
    sel_ref[0] = jnp.where(lane == i1, 1.0, 0.0) + jnp.where(lane == i2, 2.0, 0.0)


def _post_attn(o, x, mod, g, wo, w_router=None):
    bsz, s, d = x.shape
    moe = w_router is not None
    tm = TOK_TILE
    tok = lambda w: pl.BlockSpec((1, tm, w), lambda b, i: (b, i, 0))
    full = lambda a: pl.BlockSpec(a.shape, lambda b, i: (0,) * a.ndim)
    args = [o, x, mod, g, wo] + ([w_router] if moe else [])
    in_specs = [tok(o.shape[-1]), tok(d), pl.BlockSpec((1, ADA_CHUNKS, d), lambda b, i: (b, 0, 0)),
                full(g), full(wo)] + ([full(w_router)] if moe else [])
    if moe:
        assert d == ROW_SUB * LANES
        h_spec = pl.BlockSpec((1, tm * ROW_SUB, LANES), lambda b, i: (b, i, 0))
        out_specs = [tok(d), h_spec, tok(LANES), tok(LANES)]
        out_shape = [jax.ShapeDtypeStruct((bsz, s, d), F32), jax.ShapeDtypeStruct((bsz, s * ROW_SUB, LANES), F32)]
        out_shape += [jax.ShapeDtypeStruct((bsz, s, LANES), F32)] * 2
    else:
        out_specs = [tok(d), tok(d)]
        out_shape = [jax.ShapeDtypeStruct((bsz, s, d), F32), jax.ShapeDtypeStruct((bsz, s, d), BF16)]
    return pl.pallas_call(
        functools.partial(_post_attn_kernel, moe=moe),
        grid=(bsz, s // tm),
        in_specs=in_specs,
        out_specs=out_specs,
        out_shape=out_shape,
        compiler_params=_params("arbitrary", "arbitrary"),
        name="post_attn_moe" if moe else "post_attn",
    )(*args)


def _swiglu(h, w1_ref, w3_ref, w2_ref):
    d_ff = w1_ref.shape[-1]
    y = jnp.zeros((h.shape[0], w2_ref.shape[-1]), F32)
    for c0 in range(0, d_ff, FF_CHUNK):
        a = jnp.dot(h, w1_ref[:, c0:c0 + FF_CHUNK], preferred_element_type=F32)
        b = jnp.dot(h, w3_ref[:, c0:c0 + FF_CHUNK], preferred_element_type=F32)
        u = (a * jax.nn.sigmoid(a) * b).astype(BF16)
        y = y + jnp.dot(u, w2_ref[c0:c0 + FF_CHUNK, :], preferred_element_type=F32)
    return y


def _ffn_dense_kernel(h_ref, x1_ref, mod_ref, w1_ref, w3_ref, w2_ref, o_ref):
    gt_f = mod_ref[0, 5:6, :]
    o_ref[0] = x1_ref[0] + (1.0 + gt_f) * _swiglu(h_ref[0], w1_ref, w3_ref, w2_ref)


def _ffn_dense(h, x1, mod, w1, w3, w2):
    bsz, s, d = x1.shape
    tm = FFN_TILE
    tok = lambda w: pl.BlockSpec((1, tm, w), lambda b, i: (b, i, 0))
    full = lambda a: pl.BlockSpec(a.shape, lambda b, i: (0,) * a.ndim)
    return pl.pallas_call(
        _ffn_dense_kernel,
        grid=(bsz, s // tm),
        in_specs=[tok(d), tok(d), pl.BlockSpec((1, ADA_CHUNKS, d), lambda b, i: (b, 0, 0)),
                  full(w1), full(w3), full(w2)],
        out_specs=tok(d),
        out_shape=jax.ShapeDtypeStruct(x1.shape, F32),
        compiler_params=_params("arbitrary", "arbitrary"),
        name="ffn_dense",
    )(h, x1, mod, w1, w3, w2)


def _route_kernel(sel_ref, pos_ref, te_ref, cnt_ref, off_ref, run_ref, *, row_tile):
    p, t = pl.program_id(0), pl.program_id(1)
    sel_t = sel_ref[...].T
    chosen = (sel_t > 0.0).astype(F32)
    per_expert = jnp.sum(chosen, axis=1, keepdims=True)

    @pl.when((p == 0) & (t == 0))
    def _():
        cnt_ref[...] = jnp.zeros(cnt_ref.shape, F32)
        te_ref[...] = jnp.zeros(te_ref.shape, jnp.int32)

    @pl.when(p == 0)
    def _():
        cnt_ref[...] += per_expert

    @pl.when((p == 1) & (t == 0))
    def _():
        cnt = cnt_ref[...]
        padded = jnp.ceil(cnt / row_tile) * row_tile
        row = lax.broadcasted_iota(jnp.int32, cnt.shape, 0)
        off = jnp.zeros(cnt.shape, F32)
        for e in range(N_EXPERTS):
            size_e = jnp.sum(jnp.where(row == e, padded, 0.0), keepdims=True)
            off = off + jnp.where(row > e, size_e, 0.0)
        off_ref[...] = off
        run_ref[...] = jnp.zeros(run_ref.shape, F32)
        ends = off + padded
        tile_start = lax.broadcasted_iota(jnp.int32, (LANES, LANES), 1).astype(F32) * row_tile
        erow = lax.broadcasted_iota(jnp.int32, (LANES, LANES), 0)
        done = jnp.where((erow < N_EXPERTS) & (ends <= tile_start), 1.0, 0.0)
        te = jnp.sum(done, axis=0, keepdims=True).astype(jnp.int32)
        te_ref[...] = jnp.broadcast_to(te, te_ref.shape)

    @pl.when(p == 1)
    def _():
        tm = sel_t.shape[1]
        before = (lax.broadcasted_iota(jnp.int32, (tm, tm), 0)
                  < lax.broadcasted_iota(jnp.int32, (tm, tm), 1)).astype(BF16)
        rank = jnp.dot(chosen.astype(BF16), before, preferred_element_type=F32) + run_ref[...]
        base = off_ref[...] + rank
        for k in range(2):
            pos = jnp.sum(jnp.where(sel_t == float(k + 1), base, 0.0), axis=0, keepdims=True)
            pos_ref[0, k:k + 1, :] = pos.astype(jnp.int32)
        run_ref[...] += per_expert


def _route(sel, row_tile, n_row_tiles):
    n_tok = sel.shape[0]
    assert n_row_tiles <= LANES and 2 * n_tok < 2 ** 24
    tm = TOK_TILE
    nt = n_tok // tm
    pos, te = pl.pallas_call(
        functools.partial(_route_kernel, row_tile=row_tile),
        grid=(2, nt),
        in_specs=[pl.BlockSpec((tm, LANES), lambda p, t: (t, 0))],
        out_specs=[pl.BlockSpec((1, 2, tm), lambda p, t: (p * t, 0, 0)),
                   pl.BlockSpec((8, LANES), lambda p, t: (0, 0))],
        out_shape=[jax.ShapeDtypeStruct((nt, 2, tm), jnp.int32), jax.ShapeDtypeStruct((8, LANES), jnp.int32)],
        scratch_shapes=[pltpu.VMEM((LANES, 1), F32)] * 3,
        compiler_params=_params("arbitrary", "arbitrary"),
        name="moe_route",
    )(sel)
    return pos, te[0]


def _dispatch_kernel(pos_ref, h_ref, xs_in_ref, xs_ref, sem):
    del xs_in_ref
    tm = h_ref.shape[0] // ROW_SUB

    def body(r, carry):
        for k in range(2):
            pltpu.make_async_copy(_tile_row(h_ref, r), _tile_row(xs_ref, pos_ref[0, k, r]), sem).start(priority=k)
        return carry

    lax.fori_loop(0, tm, body, 0, unroll=8)
    for k in range(2):
        pltpu.make_async_copy(h_ref, xs_ref.at[pl.ds(0, tm * ROW_SUB)], sem).wait()


def _dispatch(pos, h, xs0):
    nt, _, tm = pos.shape
    return pl.pallas_call(
        _dispatch_kernel,
        grid=(nt,),
        in_specs=[pl.BlockSpec((1, 2, tm), lambda t: (t, 0, 0), memory_space=pltpu.SMEM),
                  pl.BlockSpec((tm * ROW_SUB, LANES), lambda t: (t, 0)),
                  pl.BlockSpec(memory_space=pl.ANY)],
        out_specs=pl.BlockSpec(memory_space=pl.ANY),
        out_shape=jax.ShapeDtypeStruct(xs0.shape, xs0.dtype),
        scratch_shapes=[pltpu.SemaphoreType.DMA(())],
        input_output_aliases={2: 0},
        compiler_params=_params("arbitrary"),
        name="moe_dispatch",
    )(pos, h, xs0)


def _expert_kernel(te_ref, xs_ref, w1_ref, w3_ref, w2_ref, y_ref):
    used = te_ref[pl.program_id(0)] < N_EXPERTS

    @pl.when(used)
    def _():
        y = _swiglu(_tiles_to_rows(xs_ref).astype(BF16), w1_ref.at[0], w3_ref.at[0], w2_ref.at[0])
        _rows_to_tiles(y, y_ref)

    @pl.when(jnp.logical_not(used))
    def _():
        y_ref[...] = jnp.zeros(y_ref.shape, y_ref.dtype)


def _experts(te, xs, w1, w3, w2, row_tile):
    n_rows = xs.shape[0] // ROW_SUB
    _, d, d_ff = w1.shape
    expert = lambda n, te: (jnp.minimum(te[n], N_EXPERTS - 1), 0, 0)
    rows = pl.BlockSpec((row_tile * ROW_SUB, LANES), lambda n, te: (n, 0))
    return pl.pallas_call(
        _expert_kernel,
        grid_spec=pltpu.PrefetchScalarGridSpec(
            num_scalar_prefetch=1,
            grid=(n_rows // row_tile,),
            in_specs=[rows, pl.BlockSpec((1, d, d_ff), expert), pl.BlockSpec((1, d, d_ff), expert),
                      pl.BlockSpec((1, d_ff, d), expert)],
            out_specs=rows),
        out_shape=jax.ShapeDtypeStruct(xs.shape, F32),
        compiler_params=_params("arbitrary"),
        name="moe_experts",
    )(te, xs, w1, w3, w2)


def _combine_kernel(pos_ref, pos_next_ref, x1_ref, gates_ref, sel_ref, mod_ref, gfin_ref, ys_ref, o_ref,
                    ybuf_ref, sem, *, final):
    tm = x1_ref.shape[1]
    step = pl.program_id(0) * pl.num_programs(1) + pl.program_id(1)
    n_steps = pl.num_programs(0) * pl.num_programs(1)
    slot = lax.rem(step, 2)

    def gather(p_ref, into):
        def body(r, carry):
            for k in range(2):
                pltpu.make_async_copy(_tile_row(ys_ref, p_ref[0, k, r]), _tile_row(ybuf_ref.at[into, k], r),
                                      sem.at[into]).start(priority=k)
            return carry

        lax.fori_loop(0, tm, body, 0, unroll=8)

    @pl.when(step == 0)
    def _():
        gather(pos_ref, slot)

    @pl.when(step + 1 < n_steps)
    def _():
        gather(pos_next_ref, 1 - slot)

    gates, sel = gates_ref[0], sel_ref[0]
    w_a = jnp.sum(jnp.where(sel == 1.0, gates, 0.0), axis=1, keepdims=True)
    w_b = jnp.sum(jnp.where(sel == 2.0, gates, 0.0), axis=1, keepdims=True)
    for k in range(2):
        pltpu.make_async_copy(ys_ref.at[pl.ds(0, tm * ROW_SUB)], ybuf_ref.at[slot, k], sem.at[slot]).wait()
    gt_f = mod_ref[0, 5:6, :]
    y = w_a * _tiles_to_rows(ybuf_ref.at[slot, 0]) + w_b * _tiles_to_rows(ybuf_ref.at[slot, 1])
    x = x1_ref[0] + (1.0 + gt_f) * y
    o_ref[0] = _rms(x, gfin_ref[...]) if final else x


def _combine(pos, x1, gates, sel, mod, ys, g_final=None):
    bsz, s, d = x1.shape
    nt, _, tm = pos.shape
    per_b = s // tm
    tok = lambda w: pl.BlockSpec((1, tm, w), lambda b, i: (b, i, 0))
    final = g_final is not None
    gfin = g_final if final else jnp.ones((1, d), F32)
    return pl.pallas_call(
        functools.partial(_combine_kernel, final=final),
        grid=(bsz, per_b),
        in_specs=[pl.BlockSpec((1, 2, tm), lambda b, i: (b * per_b + i, 0, 0), memory_space=pltpu.SMEM),
                  pl.BlockSpec((1, 2, tm), lambda b, i: (jnp.minimum(b * per_b + i + 1, nt - 1), 0, 0),
                               memory_space=pltpu.SMEM),
                  tok(d), tok(LANES), tok(LANES),
                  pl.BlockSpec((1, ADA_CHUNKS, d), lambda b, i: (b, 0, 0)),
                  pl.BlockSpec((1, d), lambda b, i: (0, 0)),
                  pl.BlockSpec(memory_space=pl.ANY)],
        out_specs=tok(d),
        out_shape=jax.ShapeDtypeStruct(x1.shape, F32),
        scratch_shapes=[pltpu.VMEM((2, 2, tm * ROW_SUB, LANES), F32), pltpu.SemaphoreType.DMA((2,))],
        compiler_params=_params("arbitrary", "arbitrary"),
        name="moe_combine",
    )(pos, pos, x1, gates, sel, mod, gfin, ys)


def _moe(h, x1, gates, sel, mod, w1, w3, w2, xs_buf, g_final):
    bsz, s, d = x1.shape
    n_tok = bsz * s
    row_tile = FFN_TILE
    n_rows = xs_buf.shape[0] // ROW_SUB
    assert n_rows == _moe_rows(n_tok)
    pos, te = _route(sel.reshape(n_tok, LANES), row_tile, n_rows // row_tile)
    xs = _dispatch(pos, h.reshape(n_tok * ROW_SUB, LANES), xs_buf)
    ys = _experts(te, xs, w1, w3, w2, row_tile)
    return _combine(pos, x1, gates, sel, mod, ys, g_final), xs


def _moe_rows(n_tok):
    return 2 * n_tok + N_EXPERTS * FFN_TILE


def _final_norm_kernel(x_ref, g_ref, o_ref):
    o_ref[0] = _rms(x_ref[0], g_ref[...])


def _final_norm(x, g):
    bsz, s, d = x.shape
    tm = TOK_TILE
    tok = pl.BlockSpec((1, tm, d), lambda b, i: (b, i, 0))
    return pl.pallas_call(
        _final_norm_kernel,
        grid=(bsz, s // tm),
        in_specs=[tok, pl.BlockSpec(g.shape, lambda b, i: (0, 0))],
        out_specs=tok,
        out_shape=jax.ShapeDtypeStruct(x.shape, F32),
        compiler_params=_params("arbitrary", "arbitrary"),
        name="final_norm",
    )(x, g)


def _head_blocks(w, width, nope_cols, rope_cols):
    k = w.shape[0]
    w = w.reshape(k, MLA_HEADS, width)
    nope = w[:, :, nope_cols] if nope_cols is not None else jnp.zeros((k, MLA_HEADS, MLA_NOPE_DIM), w.dtype)
    rope = w[:, :, rope_cols] if rope_cols is not None else jnp.zeros((k, MLA_HEADS, MLA_ROPE_DIM), w.dtype)
    return _head_block(nope, rope).reshape(k, MLA_HEADS * LANES)


def _prep_w_in(w_in):
    d = w_in.shape[0]
    kr_block = _head_block(jnp.zeros((d, MLA_NOPE_DIM), w_in.dtype), w_in[:, C_KR:])
    return jnp.concatenate([w_in[:, :C_KR], kr_block], axis=1).astype(BF16)


def kernel(x, c, positions, w_ada, b_ada, g_attn, w_in, diff_lambda, diff_subln_g, rel_bias, mla_q_norm, w_uq, mla_kv_norm, w_ukv, w_o, g_ffn, ffn_w1, ffn_w3, ffn_w2, moe_router, moe_w1, moe_w3, moe_w2, g_final):
    depth = w_ada.shape[0]
    bsz, s, d = x.shape
    mods = _ada(c, w_ada, b_ada).reshape(depth, bsz, ADA_CHUNKS, d)
    rc, rs = _rope_tables(positions)
    bias = _bias_tiles(rel_bias)
    qk_w = MLA_NOPE_DIM + MLA_ROPE_DIM
    kv_w = MLA_NOPE_DIM + MLA_V_DIM
    xs_buf = jnp.zeros((_moe_rows(bsz * s) * ROW_SUB, LANES), F32)
    for l in range(depth):
        mod = mods[l]
        lam_init = jnp.full((1,), 0.8 - 0.6 * math.exp(-0.3 * l), F32)
        w_uq_p = _head_blocks(w_uq[l], qk_w, slice(0, MLA_NOPE_DIM), slice(MLA_NOPE_DIM, qk_w)).astype(BF16)
        w_uk_p = _head_blocks(w_ukv[l], kv_w, slice(0, MLA_NOPE_DIM), None).astype(BF16)
        w_uv_p = w_ukv[l].reshape(MLA_KV_RANK, MLA_HEADS, kv_w)[:, :, MLA_NOPE_DIM:].reshape(
            MLA_KV_RANK, MLA_HEADS * MLA_V_DIM).astype(BF16)
        dq, dk, dv, mq, mk, mv = _pre_attn(
            x, mod, g_attn[l].reshape(1, d), _prep_w_in(w_in[l]), mla_q_norm[l].reshape(1, -1), w_uq_p,
            mla_kv_norm[l].reshape(1, -1), w_uk_p, w_uv_p, rc, rs)
        attn_args = (lam_init, diff_lambda[l], diff_subln_g[l].reshape(-1, 1), bias, dq, dk, dv, mq, mk, mv)
        wo = w_o[l].astype(BF16)
        g_f = g_ffn[l].reshape(1, d)
        if l % 2 == 1:
            o, expert_w = _attention(*attn_args, expert_w=(moe_w1, moe_w3, moe_w2), expert_layer=l // 2)
            w_router = jnp.pad(moe_router[l // 2], ((0, 0), (0, LANES - N_EXPERTS)))
            x1, h, gates, sel = _post_attn(o, x, mod, g_f, wo, w_router)
            g_fin = g_final.reshape(1, d) if l == depth - 1 else None
            x, xs_buf = _moe(h, x1, gates, sel, mod, *expert_w, xs_buf, g_fin)
        else:
            o = _attention(*attn_args)
            x1, h = _post_attn(o, x, mod, g_f, wo)
            x = _ffn_dense(h, x1, mod, ffn_w1[l // 2].astype(BF16), ffn_w3[l // 2].astype(BF16),
                           ffn_w2[l // 2].astype(BF16))
    return x if depth % 2 == 0 else _final_norm(x, g_final.reshape(1, d))
```

```python
import functools
import math

import jax
import jax.numpy as jnp
from jax import lax
from jax.experimental import pallas as pl
from jax.experimental.pallas import tpu as pltpu

F32 = jnp.float32
BF16 = jnp.bfloat16

DIFF_HEADS = 4
DIFF_QK_DIM = 64
DIFF_V_DIM = 128
MLA_HEADS = 8
MLA_NOPE_DIM = 64
MLA_ROPE_DIM = 32
MLA_V_DIM = 64
MLA_Q_RANK = 384
MLA_KV_RANK = 256
ROPE_THETA = 10000.0
N_BUCKETS = 32
MAX_EXACT = 16
MAX_DISTANCE = 128
N_EXPERTS = 8
NORM_EPS = 1e-6
ADA_CHUNKS = 6

LANES = 128
NEG_BIG = -1e30
LOG2E = math.log2(math.e)
VMEM_LIMIT = 56 * 1024 * 1024

ATTN_TILE = 256
AHEAD = 4
TOK_TILE = 512
PRE_ROW_GROUPS = 2
FFN_TILE = 512
FF_CHUNK = 256

C_DQ, C_DK, C_DV, C_MQ, C_KV, C_KR, C_END = 0, 512, 1024, 1536, 1920, 2176, 2304


def _params(*sem):
    return pltpu.CompilerParams(dimension_semantics=sem, vmem_limit_bytes=VMEM_LIMIT)


def _rms(x, g):
    return x * lax.rsqrt(jnp.mean(x * x, axis=-1, keepdims=True) + NORM_EPS) * g


ROW_SUB = 8


def _rows_to_tiles(x, ref):
    n = x.shape[0]
    for j in range(ROW_SUB):
        ref[pl.ds(j, n, stride=ROW_SUB), :] = x[:, j * LANES:(j + 1) * LANES]


def _tiles_to_rows(ref):
    n = ref.shape[0] // ROW_SUB
    return jnp.concatenate([ref[pl.ds(j, n, stride=ROW_SUB), :] for j in range(ROW_SUB)], axis=1)


def _tile_row(ref, r):
    return ref.at[pl.ds(pl.multiple_of(r * ROW_SUB, ROW_SUB), ROW_SUB)]


def _ada_kernel(c_ref, w_ref, b_ref, o_ref):
    c = c_ref[...]
    cond = c * jax.nn.sigmoid(c)
    o_ref[0] = jnp.dot(cond, w_ref[0], preferred_element_type=F32,
                       precision=lax.Precision.HIGHEST) + b_ref[0]


def _ada(c, w_ada, b_ada):
    depth, d, n = w_ada.shape
    bsz = c.shape[0]
    tn = 1536
    return pl.pallas_call(
        _ada_kernel,
        grid=(depth, n // tn),
        in_specs=[pl.BlockSpec((bsz, d), lambda l, j: (0, 0)),
                  pl.BlockSpec((1, d, tn), lambda l, j: (l, 0, j)),
                  pl.BlockSpec((1, 1, tn), lambda l, j: (l, 0, j))],
        out_specs=pl.BlockSpec((1, bsz, tn), lambda l, j: (l, 0, j)),
        out_shape=jax.ShapeDtypeStruct((depth, bsz, n), F32),
        compiler_params=_params("arbitrary", "arbitrary"),
        name="ada_mod",
    )(c, w_ada, b_ada.reshape(depth, 1, n))


ROPE_HALF = MLA_ROPE_DIM // 2
NOPE_SPLIT = LANES // 2 - ROPE_HALF


def _head_block(nope, rope):
    pad = jnp.zeros(nope.shape[:-1] + (LANES - MLA_NOPE_DIM - MLA_ROPE_DIM,), nope.dtype)
    return jnp.concatenate([rope[..., :ROPE_HALF], nope[..., :NOPE_SPLIT], rope[..., ROPE_HALF:],
                            nope[..., NOPE_SPLIT:], pad], axis=-1)


def _rope_tab_kernel(pos_ref, inv_ref, c_ref, s_ref):
    pos = pos_ref[0].astype(F32)
    inv = inv_ref[...]
    ang = pos * inv
    lane = lax.broadcasted_iota(jnp.int32, ang.shape, 1)
    is_rope = inv != 0.0
    used = lane < MLA_NOPE_DIM + MLA_ROPE_DIM
    c_ref[0] = jnp.where(is_rope, jnp.cos(ang), jnp.where(used, 1.0, 0.0))
    s_ref[0] = jnp.where(is_rope, jnp.where(lane < LANES // 2, -1.0, 1.0) * jnp.sin(ang), 0.0)


def _rope_tables(positions):
    bsz, s = positions.shape
    inv_freq = ROPE_THETA ** (-jnp.arange(ROPE_HALF, dtype=F32) / ROPE_HALF)
    inv_lane = _head_block(jnp.zeros((MLA_NOPE_DIM,), F32), jnp.concatenate([inv_freq, inv_freq])).reshape(1, LANES)
    tm = TOK_TILE
    spec = pl.BlockSpec((1, tm, LANES), lambda b, i: (b, i, 0))
    shape = jax.ShapeDtypeStruct((bsz, s, LANES), F32)
    return pl.pallas_call(
        _rope_tab_kernel,
        grid=(bsz, s // tm),
        in_specs=[pl.BlockSpec((1, tm, 1), lambda b, i: (b, i, 0)),
                  pl.BlockSpec((1, LANES), lambda b, i: (0, 0))],
        out_specs=[spec, spec],
        out_shape=[shape, shape],
        compiler_params=_params("arbitrary", "arbitrary"),
        name="rope_tables",
    )(positions.reshape(bsz, s, 1), inv_lane)


def _bias_tile_kernel(rb_ref, o_ref):
    h, d = pl.program_id(0), pl.program_id(1)
    t = o_ref.shape[-1]
    key = lax.broadcasted_iota(jnp.int32, (t, t), 0)
    qry = lax.broadcasted_iota(jnp.int32, (t, t), 1)
    dist = d * t + qry - key
    n = jnp.maximum(dist, 0)
    nf = jnp.maximum(n, 1).astype(F32)
    large = MAX_EXACT + (jnp.log(nf / MAX_EXACT) / math.log(MAX_DISTANCE / MAX_EXACT)
                         * (N_BUCKETS - MAX_EXACT)).astype(jnp.int32)
    large = jnp.minimum(large, N_BUCKETS - 1)
    bucket = jnp.where(n < MAX_EXACT, n, large)
    val = jnp.zeros((t, t), F32)
    for j in range(N_BUCKETS):
        val = jnp.where(bucket == j, rb_ref[j, h], val)
    val = (rb_ref[N_BUCKETS - 1, h] - val) * LOG2E
    o_ref[0, 0] = jnp.where(dist < 0, -NEG_BIG, val)


def _bias_tiles(rel_bias):
    t = ATTN_TILE
    return pl.pallas_call(
        _bias_tile_kernel,
        grid=(DIFF_HEADS, 2),
        in_specs=[pl.BlockSpec(memory_space=pltpu.SMEM)],
        out_specs=pl.BlockSpec((1, 1, t, t), lambda h, d: (h, d, 0, 0)),
        out_shape=jax.ShapeDtypeStruct((DIFF_HEADS, 2, t, t), F32),
        compiler_params=_params("arbitrary", "arbitrary"),
        name="bias_tiles",
    )(rel_bias)


def _pre_attn_kernel(x_ref, mod_ref, g_ref, w_in_ref, gq_ref, w_uq_ref, gkv_ref, w_uk_ref, w_uv_ref,
                     rc_ref, rs_ref,
                     dq_ref, dk_ref, dv_ref, mq_ref, mk_ref, mv_ref):
    tm = x_ref.shape[1]
    for r0 in range(0, tm, tm // PRE_ROW_GROUPS):
        rows = slice(r0, r0 + tm // PRE_ROW_GROUPS)
        x = x_ref[0, rows, :]
        sh, sc = mod_ref[0, 0:1, :], mod_ref[0, 1:2, :]
        h = (_rms(x, g_ref[...]) * (1.0 + sc) + sh).astype(BF16)
        proj = jnp.dot(h, w_in_ref[...], preferred_element_type=F32)
        dq_ref[0, rows, :] = (proj[:, C_DQ:C_DK] * (DIFF_QK_DIM ** -0.5 * LOG2E)).astype(BF16)
        dk_ref[0, rows, :] = proj[:, C_DK:C_DV].astype(BF16)
        dv_ref[0, rows, :] = proj[:, C_DV:C_MQ].astype(BF16)

        rc, rs = rc_ref[0, rows, :], rs_ref[0, rows, :]

        def rope(v):
            return v * rc + pltpu.roll(v, LANES // 2, 1) * rs

        qn = _rms(proj[:, C_MQ:C_KV], gq_ref[...]).astype(BF16)
        q = jnp.dot(qn, w_uq_ref[...], preferred_element_type=F32)
        kvn = _rms(proj[:, C_KV:C_KR], gkv_ref[...]).astype(BF16)
        kn = jnp.dot(kvn, w_uk_ref[...], preferred_element_type=F32)
        mv_ref[0, rows, :] = jnp.dot(kvn, w_uv_ref[...], preferred_element_type=F32).astype(BF16)
        kr = rope(proj[:, C_KR:C_END])
        q_scale = (MLA_NOPE_DIM + MLA_ROPE_DIM) ** -0.5 * LOG2E
        for hd in range(MLA_HEADS):
            sl = slice(hd * LANES, (hd + 1) * LANES)
            mq_ref[0, rows, sl] = (rope(q[:, sl]) * q_scale).astype(BF16)
            mk_ref[0, rows, sl] = (kn[:, sl] + kr).astype(BF16)


def _pre_attn(x, mod, g, w_in, gq, w_uq, gkv, w_uk, w_uv, rc, rs):
    bsz, s, d = x.shape
    tm = TOK_TILE
    tok = lambda w: pl.BlockSpec((1, tm, w), lambda b, i: (b, i, 0))
    full = lambda a: pl.BlockSpec(a.shape, lambda b, i: (0,) * a.ndim)
    widths = (512, 512, 512, MLA_HEADS * LANES, MLA_HEADS * LANES, MLA_HEADS * MLA_V_DIM)
    return pl.pallas_call(
        _pre_attn_kernel,
        grid=(bsz, s // tm),
        in_specs=[tok(d), pl.BlockSpec((1, ADA_CHUNKS, d), lambda b, i: (b, 0, 0)), full(g), full(w_in),
                  full(gq), full(w_uq), full(gkv), full(w_uk), full(w_uv), tok(LANES), tok(LANES)],
        out_specs=[tok(w) for w in widths],
        out_shape=[jax.ShapeDtypeStruct((bsz, s, w), BF16) for w in widths],
        compiler_params=_params("arbitrary", "arbitrary"),
        name="pre_attn",
    )(x, mod, g, w_in, gq, w_uq, gkv, w_uk, w_uv, rc, rs)


ONES_ROWS = 16


def _values_t(v, dv):
    vt = v.T
    ones = jnp.ones((ONES_ROWS, vt.shape[1]), vt.dtype)
    parts = []
    for h in range(vt.shape[0] // dv):
        parts += [vt[h * dv:(h + 1) * dv], ones]
    return jnp.concatenate(parts, axis=0)


def _softmax_init(m_ref, acc_ref):
    m_ref[...] = jnp.full(m_ref.shape, NEG_BIG, F32)
    acc_ref[...] = jnp.zeros(acc_ref.shape, F32)


def _softmax_probs(st, m_ref):
    m_prev = m_ref[...]
    m_new = jnp.maximum(m_prev, jnp.max(st, axis=0, keepdims=True))
    m_ref[...] = m_new
    return jnp.exp2(st - m_new).astype(BF16), jnp.exp2(m_prev - m_new)


def _acc_update(acc_ref, alpha, vt, p):
    acc_ref[...] = alpha * acc_ref[...] + jnp.dot(vt, p, preferred_element_type=F32)


def _normalized(acc_ref, dv):
    return acc_ref[:dv] / acc_ref[dv:dv + 1]


def _attn_kernel(*refs, cast_weights):
    (lam_init_ref, lam_ref, g_ref, bias_ref, dq_ref, dk_ref, dv_ref, mq_ref, mk_ref, mv_ref), refs = refs[:10], refs[10:]
    if cast_weights:
        for w_ref, wb_ref in zip(refs[:3], refs[4:7]):
            wb_ref[...] = w_ref[...].astype(BF16)
        refs = refs[3:4] + refs[7:]
    o_ref, dvt_ref, mvt_ref, m_ref, dacc_ref, macc_ref, pend_ref = refs
    i = pl.program_id(1)
    t = dq_ref.shape[1]
    n_diff = 2 * DIFF_HEADS
    n_chains = n_diff + MLA_HEADS
    d_rows, m_rows = DIFF_V_DIM + ONES_ROWS, MLA_V_DIM + ONES_ROWS

    @pl.when(i == 0)
    def _():
        dvt_ref[...] = _values_t(dv_ref[0], DIFF_V_DIM)
        mvt_ref[...] = _values_t(mv_ref[0], MLA_V_DIM)

    qts = []
    for h in range(DIFF_HEADS):
        qt = dq_ref[0, :, h * LANES:(h + 1) * LANES].T
        row = lax.broadcasted_iota(jnp.int32, qt.shape, 0)
        qts.append(jnp.where(row < DIFF_QK_DIM, qt, jnp.zeros_like(qt)))
        qts.append(jnp.where(row >= DIFF_QK_DIM, qt, jnp.zeros_like(qt)))
    qts += [mq_ref[0, :, h * LANES:(h + 1) * LANES].T for h in range(MLA_HEADS)]

    def scores(j, c):
        rows = pl.ds(pl.multiple_of(j * t, t), t)
        if c < n_diff:
            k = dk_ref[0, rows, (c // 2) * LANES:(c // 2 + 1) * LANES]
        else:
            k = mk_ref[0, rows, (c - n_diff) * LANES:(c - n_diff + 1) * LANES]
        return jnp.dot(k, qts[c], preferred_element_type=F32)

    def values(j, c):
        cols = pl.ds(pl.multiple_of(j * t, t), t)
        if c < n_diff:
            return dvt_ref[(c // 2) * d_rows:(c // 2 + 1) * d_rows, cols]
        return mvt_ref[(c - n_diff) * m_rows:(c - n_diff + 1) * m_rows, cols]

    def acc_at(c):
        return dacc_ref.at[c] if c < n_diff else macc_ref.at[c - n_diff]

    def step(j, bias_idx, has_next):
        hidden = None
        if bias_idx == 0:
            key = lax.broadcasted_iota(jnp.int32, (t, t), 0)
            qry = lax.broadcasted_iota(jnp.int32, (t, t), 1)
            hidden = jnp.where(key <= qry, 0.0, -NEG_BIG)

        def adjust(c, st):
            if c < n_diff:
                return st if bias_idx is None else st - bias_ref[c // 2, bias_idx]
            return st if hidden is None else st - hidden

        pending = [pend_ref[c] for c in range(AHEAD)]
        for c in range(n_chains):
            nxt = c + AHEAD
            if nxt < n_chains:
                pending.append(scores(j, nxt))
            elif has_next:
                pend_ref[nxt - n_chains] = scores(j + 1, nxt - n_chains)
            p, alpha = _softmax_probs(adjust(c, pending.pop(0)), m_ref.at[c])
            _acc_update(acc_at(c), alpha, values(j, c), p)

    def far(j, carry):
        step(j, None, True)
        return carry

    for c in range(n_chains):
        _softmax_init(m_ref.at[c], acc_at(c))
    for c in range(AHEAD):
        pend_ref[c] = scores(0, c)
    lax.fori_loop(0, jnp.maximum(i - 1, 0), far, 0)

    @pl.when(i >= 1)
    def _():
        step(i - 1, 1, True)

    step(i, 0, False)

    lv = lam_ref[...]
    lam = (jnp.exp(jnp.sum(lv[0:1] * lv[1:2], keepdims=True)) - jnp.exp(jnp.sum(lv[2:3] * lv[3:4], keepdims=True))
           + lam_init_ref[0])
    for h in range(DIFF_HEADS):
        c0, c1 = 2 * h, 2 * h + 1
        ot = (_normalized(dacc_ref.at[c0], DIFF_V_DIM)
              - lam * _normalized(dacc_ref.at[c1], DIFF_V_DIM))
        ot = ot * lax.rsqrt(jnp.mean(ot * ot, axis=0, keepdims=True) + NORM_EPS) * g_ref[...]
        o_ref[0, :, h * LANES:(h + 1) * LANES] = (ot * (1.0 - lam_init_ref[0])).T.astype(BF16)
    base = DIFF_HEADS * LANES
    for u in range(MLA_HEADS // 2):
        ot = jnp.concatenate([_normalized(macc_ref.at[2 * u], MLA_V_DIM),
                              _normalized(macc_ref.at[2 * u + 1], MLA_V_DIM)], axis=0)
        o_ref[0, :, base + u * LANES:base + (u + 1) * LANES] = ot.T.astype(BF16)


def _attention(lam_init, diff_lambda, g, bias, dq, dk, dv, mq, mk, mv, expert_w=None, expert_layer=0):
    bsz, s, wd = dq.shape
    wq, wv = mq.shape[-1], mv.shape[-1]
    t = ATTN_TILE
    assert MAX_DISTANCE <= t and s % t == 0
    n_diff = 2 * DIFF_HEADS
    n_chains = n_diff + MLA_HEADS
    per_b = s // t
    qtile = lambda w: pl.BlockSpec((1, t, w), lambda b, i: (b, i, 0))
    whole = lambda w: pl.BlockSpec((1, s, w), lambda b, i: (b, 0, 0))
    args = [lam_init, diff_lambda, g, bias, dq, dk, dv, mq, mk, mv]
    in_specs = [pl.BlockSpec(memory_space=pltpu.SMEM),
                pl.BlockSpec(diff_lambda.shape, lambda b, i: (0, 0)),
                pl.BlockSpec(g.shape, lambda b, i: (0, 0)),
                pl.BlockSpec(bias.shape, lambda b, i: (0, 0, 0, 0)),
                qtile(wd), whole(wd), whole(wd), qtile(wq), whole(wq), whole(wv)]
    out_specs = [qtile(wd + wv)]
    out_shape = [jax.ShapeDtypeStruct((bsz, s, wd + wv), BF16)]
    if expert_w is not None:
        steps = bsz * per_b
        n_lay, n_e, d, d_ff = expert_w[0].shape
        for w in expert_w:
            rows, cols = n_e * w.shape[2], w.shape[3]
            assert rows % (16 * steps) == 0
            slab = rows // steps
            args.append(w.reshape(n_lay * rows, cols))
            in_specs.append(pl.BlockSpec((slab, cols), lambda b, i: (expert_layer * steps + b * per_b + i, 0)))
            out_specs.append(pl.BlockSpec((slab, cols), lambda b, i: (b * per_b + i, 0)))
            out_shape.append(jax.ShapeDtypeStruct((rows, cols), BF16))
    outs = pl.pallas_call(
        functools.partial(_attn_kernel, cast_weights=expert_w is not None),
        grid=(bsz, per_b),
        in_specs=in_specs,
        out_specs=out_specs,
        out_shape=out_shape,
        scratch_shapes=[pltpu.VMEM((DIFF_HEADS * (DIFF_V_DIM + ONES_ROWS), s), BF16),
                        pltpu.VMEM((MLA_HEADS * (MLA_V_DIM + ONES_ROWS), s), BF16),
                        pltpu.VMEM((n_chains, 1, t), F32),
                        pltpu.VMEM((n_diff, DIFF_V_DIM + ONES_ROWS, t), F32),
                        pltpu.VMEM((MLA_HEADS, MLA_V_DIM + ONES_ROWS, t), F32),
                        pltpu.VMEM((AHEAD, t, t), F32)],
        compiler_params=_params("arbitrary", "arbitrary"),
        name="attention",
    )(*args)
    if expert_w is None:
        return outs[0]
    return outs[0], tuple(wb.reshape(w.shape[1:]) for wb, w in zip(outs[1:], expert_w))


def _post_attn_kernel(*refs, moe):
    if moe:
        o_ref, x_ref, mod_ref, g_ref, wo_ref, wr_ref, x1_ref, h_ref, gates_ref, sel_ref = refs
    else:
        o_ref, x_ref, mod_ref, g_ref, wo_ref, x1_ref, h_ref = refs
    y = jnp.dot(o_ref[0], wo_ref[...], preferred_element_type=F32)
    gt_a, sh_f, sc_f = mod_ref[0, 2:3, :], mod_ref[0, 3:4, :], mod_ref[0, 4:5, :]
    x1 = x_ref[0] + (1.0 + gt_a) * y
    x1_ref[0] = x1
    h = _rms(x1, g_ref[...]) * (1.0 + sc_f) + sh_f
    if not moe:
        h_ref[0] = h.astype(BF16)
        return
    _rows_to_tiles(h, h_ref.at[0])
    wr = wr_ref[...]
    h_hi, wr_hi = h.astype(BF16), wr.astype(BF16)
    h_lo, wr_lo = (h - h_hi.astype(F32)).astype(BF16), (wr - wr_hi.astype(F32)).astype(BF16)
    logits = (jnp.dot(h_hi, wr_hi, preferred_element_type=F32) + jnp.dot(h_hi, wr_lo, preferred_element_type=F32)
              + jnp.dot(h_lo, wr_hi, preferred_element_type=F32))
    lane = lax.broadcasted_iota(jnp.int32, logits.shape, 1)
    logits = jnp.where(lane < N_EXPERTS, logits, -jnp.inf)
    v1 = jnp.max(logits, axis=1, keepdims=True)
    i1 = jnp.min(jnp.where(logits == v1, lane, LANES), axis=1, keepdims=True)
    rest = jnp.where(lane == i1, -jnp.inf, logits)
    v2 = jnp.max(rest, axis=1, keepdims=True)
    i2 = jnp.min(jnp.where(rest == v2, lane, LANES), axis=1, keepdims=True)
    e2 = jnp.exp(v2 - v1)
    w1 = 1.0 / (1.0 + e2)
    w2 = e2 / (1.0 + e2)
    gates_ref[0] = jnp.where(lane == i1, w1, 0.0) + jnp.where(lane == i2, w2, 0.0)
    sel_ref[0] = jnp.where(lane == i1, 1.0, 0.0) + jnp.where(lane == i2, 2.0, 0.0)


def _post_attn(o, x, mod, g, wo, w_router=None):
    bsz, s, d = x.shape
    moe = w_router is not None
    tm = TOK_TILE
    tok = lambda w: pl.BlockSpec((1, tm, w), lambda b, i: (b, i, 0))
    full = lambda a: pl.BlockSpec(a.shape, lambda b, i: (0,) * a.ndim)
    args = [o, x, mod, g, wo] + ([w_router] if moe else [])
    in_specs = [tok(o.shape[-1]), tok(d), pl.BlockSpec((1, ADA_CHUNKS, d), lambda b, i: (b, 0, 0)),
                full(g), full(wo)] + ([full(w_router)] if moe else [])
    if moe:
        assert d == ROW_SUB * LANES
        h_spec = pl.BlockSpec((1, tm * ROW_SUB, LANES), lambda b, i: (b, i, 0))
        out_specs = [tok(d), h_spec, tok(LANES), tok(LANES)]
        out_shape = [jax.ShapeDtypeStruct((bsz, s, d), F32), jax.ShapeDtypeStruct((bsz, s * ROW_SUB, LANES), F32)]
        out_shape += [jax.ShapeDtypeStruct((bsz, s, LANES), F32)] * 2
    else:
        out_specs = [tok(d), tok(d)]
        out_shape = [jax.ShapeDtypeStruct((bsz, s, d), F32), jax.ShapeDtypeStruct((bsz, s, d), BF16)]
    return pl.pallas_call(
        functools.partial(_post_attn_kernel, moe=moe),
        grid=(bsz, s // tm),
        in_specs=in_specs,
        out_specs=out_specs,
        out_shape=out_shape,
        compiler_params=_params("arbitrary", "arbitrary"),
        name="post_attn_moe" if moe else "post_attn",
    )(*args)


def _swiglu(h, w1_ref, w3_ref, w2_ref):
    d_ff = w1_ref.shape[-1]
    y = jnp.zeros((h.shape[0], w2_ref.shape[-1]), F32)
    for c0 in range(0, d_ff, FF_CHUNK):
        a = jnp.dot(h, w1_ref[:, c0:c0 + FF_CHUNK], preferred_element_type=F32)
        b = jnp.dot(h, w3_ref[:, c0:c0 + FF_CHUNK], preferred_element_type=F32)
        u = (a * jax.nn.sigmoid(a) * b).astype(BF16)
        y = y + jnp.dot(u, w2_ref[c0:c0 + FF_CHUNK, :], preferred_element_type=F32)
    return y


def _ffn_dense_kernel(h_ref, x1_ref, mod_ref, w1_ref, w3_ref, w2_ref, o_ref):
    gt_f = mod_ref[0, 5:6, :]
    o_ref[0] = x1_ref[0] + (1.0 + gt_f) * _swiglu(h_ref[0], w1_ref, w3_ref, w2_ref)


def _ffn_dense(h, x1, mod, w1, w3, w2):
    bsz, s, d = x1.shape
    tm = FFN_TILE
    tok = lambda w: pl.BlockSpec((1, tm, w), lambda b, i: (b, i, 0))
    full = lambda a: pl.BlockSpec(a.shape, lambda b, i: (0,) * a.ndim)
    return pl.pallas_call(
        _ffn_dense_kernel,
        grid=(bsz, s // tm),
        in_specs=[tok(d), tok(d), pl.BlockSpec((1, ADA_CHUNKS, d), lambda b, i: (b, 0, 0)),
                  full(w1), full(w3), full(w2)],
        out_specs=tok(d),
        out_shape=jax.ShapeDtypeStruct(x1.shape, F32),
        compiler_params=_params("arbitrary", "arbitrary"),
        name="ffn_dense",
    )(h, x1, mod, w1, w3, w2)


def _route_kernel(sel_ref, pos_ref, te_ref, cnt_ref, off_ref, run_ref, *, row_tile):
    p, t = pl.program_id(0), pl.program_id(1)
    sel_t = sel_ref[...].T
    chosen = (sel_t > 0.0).astype(F32)
    per_expert = jnp.sum(chosen, axis=1, keepdims=True)

    @pl.when((p == 0) & (t == 0))
    def _():
        cnt_ref[...] = jnp.zeros(cnt_ref.shape, F32)
        te_ref[...] = jnp.zeros(te_ref.shape, jnp.int32)

    @pl.when(p == 0)
    def _():
        cnt_ref[...] += per_expert

    @pl.when((p == 1) & (t == 0))
    def _():
        cnt = cnt_ref[...]
        padded = jnp.ceil(cnt / row_tile) * row_tile
        row = lax.broadcasted_iota(jnp.int32, cnt.shape, 0)
        off = jnp.zeros(cnt.shape, F32)
        for e in range(N_EXPERTS):
            size_e = jnp.sum(jnp.where(row == e, padded, 0.0), keepdims=True)
            off = off + jnp.where(row > e, size_e, 0.0)
        off_ref[...] = off
        run_ref[...] = jnp.zeros(run_ref.shape, F32)
        ends = off + padded
        tile_start = lax.broadcasted_iota(jnp.int32, (LANES, LANES), 1).astype(F32) * row_tile
        erow = lax.broadcasted_iota(jnp.int32, (LANES, LANES), 0)
        done = jnp.where((erow < N_EXPERTS) & (ends <= tile_start), 1.0, 0.0)
        te = jnp.sum(done, axis=0, keepdims=True).astype(jnp.int32)
        te_ref[...] = jnp.broadcast_to(te, te_ref.shape)

    @pl.when(p == 1)
    def _():
        tm = sel_t.shape[1]
        before = (lax.broadcasted_iota(jnp.int32, (tm, tm), 0)
                  < lax.broadcasted_iota(jnp.int32, (tm, tm), 1)).astype(BF16)
        rank = jnp.dot(chosen.astype(BF16), before, preferred_element_type=F32) + run_ref[...]
        base = off_ref[...] + rank
        for k in range(2):
            pos = jnp.sum(jnp.where(sel_t == float(k + 1), base, 0.0), axis=0, keepdims=True)
            pos_ref[0, k:k + 1, :] = pos.astype(jnp.int32)
        run_ref[...] += per_expert


def _route(sel, row_tile, n_row_tiles):
    n_tok = sel.shape[0]
    assert n_row_tiles <= LANES and 2 * n_tok < 2 ** 24
    tm = TOK_TILE
    nt = n_tok // tm
    pos, te = pl.pallas_call(
        functools.partial(_route_kernel, row_tile=row_tile),
        grid=(2, nt),
        in_specs=[pl.BlockSpec((tm, LANES), lambda p, t: (t, 0))],
        out_specs=[pl.BlockSpec((1, 2, tm), lambda p, t: (p * t, 0, 0)),
                   pl.BlockSpec((8, LANES), lambda p, t: (0, 0))],
        out_shape=[jax.ShapeDtypeStruct((nt, 2, tm), jnp.int32), jax.ShapeDtypeStruct((8, LANES), jnp.int32)],
        scratch_shapes=[pltpu.VMEM((LANES, 1), F32)] * 3,
        compiler_params=_params("arbitrary", "arbitrary"),
        name="moe_route",
    )(sel)
    return pos, te[0]


def _dispatch_kernel(pos_ref, h_ref, xs_in_ref, xs_ref, sem):
    del xs_in_ref
    tm = h_ref.shape[0] // ROW_SUB

    def body(r, carry):
        for k in range(2):
            pltpu.make_async_copy(_tile_row(h_ref, r), _tile_row(xs_ref, pos_ref[0, k, r]), sem).start(priority=k)
        return carry

    lax.fori_loop(0, tm, body, 0, unroll=8)
    for k in range(2):
        pltpu.make_async_copy(h_ref, xs_ref.at[pl.ds(0, tm * ROW_SUB)], sem).wait()


def _dispatch(pos, h, xs0):
    nt, _, tm = pos.shape
    return pl.pallas_call(
        _dispatch_kernel,
        grid=(nt,),
        in_specs=[pl.BlockSpec((1, 2, tm), lambda t: (t, 0, 0), memory_space=pltpu.SMEM),
                  pl.BlockSpec((tm * ROW_SUB, LANES), lambda t: (t, 0)),
                  pl.BlockSpec(memory_space=pl.ANY)],
        out_specs=pl.BlockSpec(memory_space=pl.ANY),
        out_shape=jax.ShapeDtypeStruct(xs0.shape, xs0.dtype),
        scratch_shapes=[pltpu.SemaphoreType.DMA(())],
        input_output_aliases={2: 0},
        compiler_params=_params("arbitrary"),
        name="moe_dispatch",
    )(pos, h, xs0)


def _expert_kernel(te_ref, xs_ref, w1_ref, w3_ref, w2_ref, y_ref):
    used = te_ref[pl.program_id(0)] < N_EXPERTS

    @pl.when(used)
    def _():
        y = _swiglu(_tiles_to_rows(xs_ref).astype(BF16), w1_ref.at[0], w3_ref.at[0], w2_ref.at[0])
        _rows_to_tiles(y, y_ref)

    @pl.when(jnp.logical_not(used))
    def _():
        y_ref[...] = jnp.zeros(y_ref.shape, y_ref.dtype)


def _experts(te, xs, w1, w3, w2, row_tile):
    n_rows = xs.shape[0] // ROW_SUB
    _, d, d_ff = w1.shape
    expert = lambda n, te: (jnp.minimum(te[n], N_EXPERTS - 1), 0, 0)
    rows = pl.BlockSpec((row_tile * ROW_SUB, LANES), lambda n, te: (n, 0))
    return pl.pallas_call(
        _expert_kernel,
        grid_spec=pltpu.PrefetchScalarGridSpec(
            num_scalar_prefetch=1,
            grid=(n_rows // row_tile,),
            in_specs=[rows, pl.BlockSpec((1, d, d_ff), expert), pl.BlockSpec((1, d, d_ff), expert),
                      pl.BlockSpec((1, d_ff, d), expert)],
            out_specs=rows),
        out_shape=jax.ShapeDtypeStruct(xs.shape, F32),
        compiler_params=_params("arbitrary"),
        name="moe_experts",
    )(te, xs, w1, w3, w2)


def _combine_kernel(pos_ref, pos_next_ref, x1_ref, gates_ref, sel_ref, mod_ref, gfin_ref, ys_ref, o_ref,
                    ybuf_ref, sem, *, final):
    tm = x1_ref.shape[1]
    step = pl.program_id(0) * pl.num_programs(1) + pl.program_id(1)
    n_steps = pl.num_programs(0) * pl.num_programs(1)
    slot = lax.rem(step, 2)

    def gather(p_ref, into):
        def body(r, carry):
            for k in range(2):
                pltpu.make_async_copy(_tile_row(ys_ref, p_ref[0, k, r]), _tile_row(ybuf_ref.at[into, k], r),
                                      sem.at[into]).start(priority=k)
            return carry

        lax.fori_loop(0, tm, body, 0, unroll=8)

    @pl.when(step == 0)
    def _():
        gather(pos_ref, slot)

    @pl.when(step + 1 < n_steps)
    def _():
        gather(pos_next_ref, 1 - slot)

    gates, sel = gates_ref[0], sel_ref[0]
    w_a = jnp.sum(jnp.where(sel == 1.0, gates, 0.0), axis=1, keepdims=True)
    w_b = jnp.sum(jnp.where(sel == 2.0, gates, 0.0), axis=1, keepdims=True)
    for k in range(2):
        pltpu.make_async_copy(ys_ref.at[pl.ds(0, tm * ROW_SUB)], ybuf_ref.at[slot, k], sem.at[slot]).wait()
    gt_f = mod_ref[0, 5:6, :]
    y = w_a * _tiles_to_rows(ybuf_ref.at[slot, 0]) + w_b * _tiles_to_rows(ybuf_ref.at[slot, 1])
    x = x1_ref[0] + (1.0 + gt_f) * y
    o_ref[0] = _rms(x, gfin_ref[...]) if final else x


def _combine(pos, x1, gates, sel, mod, ys, g_final=None):
    bsz, s, d = x1.shape
    nt, _, tm = pos.shape
    per_b = s // tm
    tok = lambda w: pl.BlockSpec((1, tm, w), lambda b, i: (b, i, 0))
    final = g_final is not None
    gfin = g_final if final else jnp.ones((1, d), F32)
    return pl.pallas_call(
        functools.partial(_combine_kernel, final=final),
        grid=(bsz, per_b),
        in_specs=[pl.BlockSpec((1, 2, tm), lambda b, i: (b * per_b + i, 0, 0), memory_space=pltpu.SMEM),
                  pl.BlockSpec((1, 2, tm), lambda b, i: (jnp.minimum(b * per_b + i + 1, nt - 1), 0, 0),
                               memory_space=pltpu.SMEM),
                  tok(d), tok(LANES), tok(LANES),
                  pl.BlockSpec((1, ADA_CHUNKS, d), lambda b, i: (b, 0, 0)),
                  pl.BlockSpec((1, d), lambda b, i: (0, 0)),
                  pl.BlockSpec(memory_space=pl.ANY)],
        out_specs=tok(d),
        out_shape=jax.ShapeDtypeStruct(x1.shape, F32),
        scratch_shapes=[pltpu.VMEM((2, 2, tm * ROW_SUB, LANES), F32), pltpu.SemaphoreType.DMA((2,))],
        compiler_params=_params("arbitrary", "arbitrary"),
        name="moe_combine",
    )(pos, pos, x1, gates, sel, mod, gfin, ys)


def _moe(h, x1, gates, sel, mod, w1, w3, w2, xs_buf, g_final):
    bsz, s, d = x1.shape
    n_tok = bsz * s
    row_tile = FFN_TILE
    n_rows = xs_buf.shape[0] // ROW_SUB
    assert n_rows == _moe_rows(n_tok)
    pos, te = _route(sel.reshape(n_tok, LANES), row_tile, n_rows // row_tile)
    xs = _dispatch(pos, h.reshape(n_tok * ROW_SUB, LANES), xs_buf)
    ys = _experts(te, xs, w1, w3, w2, row_tile)
    return _combine(pos, x1, gates, sel, mod, ys, g_final), xs


def _moe_rows(n_tok):
    return 2 * n_tok + N_EXPERTS * FFN_TILE


def _final_norm_kernel(x_ref, g_ref, o_ref):
    o_ref[0] = _rms(x_ref[0], g_ref[...])


def _final_norm(x, g):
    bsz, s, d = x.shape
    tm = TOK_TILE
    tok = pl.BlockSpec((1, tm, d), lambda b, i: (b, i, 0))
    return pl.pallas_call(
        _final_norm_kernel,
        grid=(bsz, s // tm),
        in_specs=[tok, pl.BlockSpec(g.shape, lambda b, i: (0, 0))],
        out_specs=tok,
        out_shape=jax.ShapeDtypeStruct(x.shape, F32),
        compiler_params=_params("arbitrary", "arbitrary"),
        name="final_norm",
    )(x, g)


def _head_blocks(w, width, nope_cols, rope_cols):
    k = w.shape[0]
    w = w.reshape(k, MLA_HEADS, width)
    nope = w[:, :, nope_cols] if nope_cols is not None else jnp.zeros((k, MLA_HEADS, MLA_NOPE_DIM), w.dtype)
    rope = w[:, :, rope_cols] if rope_cols is not None else jnp.zeros((k, MLA_HEADS, MLA_ROPE_DIM), w.dtype)
    return _head_block(nope, rope).reshape(k, MLA_HEADS * LANES)


def _prep_w_in(w_in):
    d = w_in.shape[0]
    kr_block = _head_block(jnp.zeros((d, MLA_NOPE_DIM), w_in.dtype), w_in[:, C_KR:])
    return jnp.concatenate([w_in[:, :C_KR], kr_block], axis=1).astype(BF16)


def kernel(x, c, positions, w_ada, b_ada, g_attn, w_in, diff_lambda, diff_subln_g, rel_bias, mla_q_norm, w_uq, mla_kv_norm, w_ukv, w_o, g_ffn, ffn_w1, ffn_w3, ffn_w2, moe_router, moe_w1, moe_w3, moe_w2, g_final):
    depth = w_ada.shape[0]
    bsz, s, d = x.shape
    mods = _ada(c, w_ada, b_ada).reshape(depth, bsz, ADA_CHUNKS, d)
    rc, rs = _rope_tables(positions)
    bias = _bias_tiles(rel_bias)
    qk_w = MLA_NOPE_DIM + MLA_ROPE_DIM
    kv_w = MLA_NOPE_DIM + MLA_V_DIM
    xs_buf = jnp.zeros((_moe_rows(bsz * s) * ROW_SUB, LANES), F32)
    for l in range(depth):
        mod = mods[l]
        lam_init = jnp.full((1,), 0.8 - 0.6 * math.exp(-0.3 * l), F32)
        w_uq_p = _head_blocks(w_uq[l], qk_w, slice(0, MLA_NOPE_DIM), slice(MLA_NOPE_DIM, qk_w)).astype(BF16)
        w_uk_p = _head_blocks(w_ukv[l], kv_w, slice(0, MLA_NOPE_DIM), None).astype(BF16)
        w_uv_p = w_ukv[l].reshape(MLA_KV_RANK, MLA_HEADS, kv_w)[:, :, MLA_NOPE_DIM:].reshape(
            MLA_KV_RANK, MLA_HEADS * MLA_V_DIM).astype(BF16)
        dq, dk, dv, mq, mk, mv = _pre_attn(
            x, mod, g_attn[l].reshape(1, d), _prep_w_in(w_in[l]), mla_q_norm[l].reshape(1, -1), w_uq_p,
            mla_kv_norm[l].reshape(1, -1), w_uk_p, w_uv_p, rc, rs)
        attn_args = (lam_init, diff_lambda[l], diff_subln_g[l].reshape(-1, 1), bias, dq, dk, dv, mq, mk, mv)
        wo = w_o[l].astype(BF16)
        g_f = g_ffn[l].reshape(1, d)
        if l % 2 == 1:
            o, expert_w = _attention(*attn_args, expert_w=(moe_w1, moe_w3, moe_w2), expert_layer=l // 2)
            w_router = jnp.pad(moe_router[l // 2], ((0, 0), (0, LANES - N_EXPERTS)))
            x1, h, gates, sel = _post_attn(o, x, mod, g_f, wo, w_router)
            g_fin = g_final.reshape(1, d) if l == depth - 1 else None
            x, xs_buf = _moe(h, x1, gates, sel, mod, *expert_w, xs_buf, g_fin)
        else:
            o = _attention(*attn_args)
            x1, h = _post_attn(o, x, mod, g_f, wo)
            x = _ffn_dense(h, x1, mod, ffn_w1[l // 2].astype(BF16), ffn_w3[l // 2].astype(BF16),
                           ffn_w2[l // 2].astype(BF16))
    return x if depth % 2 == 0 else _final_norm(x, g_final.reshape(1, d))
```

```python
import functools
import math

import jax
import jax.numpy as jnp
from jax import lax
from jax.experimental import pallas as pl
from jax.experimental.pallas import tpu as pltpu

F32 = jnp.float32
BF16 = jnp.bfloat16

DIFF_HEADS = 4
DIFF_QK_DIM = 64
DIFF_V_DIM = 128
MLA_HEADS = 8
MLA_NOPE_DIM = 64
MLA_ROPE_DIM = 32
MLA_V_DIM = 64
MLA_Q_RANK = 384
MLA_KV_RANK = 256
ROPE_THETA = 10000.0
N_BUCKETS = 32
MAX_EXACT = 16
MAX_DISTANCE = 128
N_EXPERTS = 8
NORM_EPS = 1e-6
ADA_CHUNKS = 6

LANES = 128
BF16_SUBLANES = 16
NEG_BIG = -1e30
LOG2E = math.log2(math.e)
VMEM_LIMIT = 56 * 1024 * 1024

ATTN_TILE = 256
AHEAD = 4
TOK_TILE = 512
PRE_ROW_GROUPS = 2
POST_ROW_GROUPS = 2
FFN_TILE = 512
FF_CHUNK = 256
ADA_COL_TILE = 1536

C_DQ, C_DK, C_DV, C_MQ, C_KV, C_KR, C_END = 0, 512, 1024, 1536, 1920, 2176, 2304


def _params(*sem):
    return pltpu.CompilerParams(dimension_semantics=sem, vmem_limit_bytes=VMEM_LIMIT)


def _rms(x, g):
    return x * lax.rsqrt(jnp.mean(x * x, axis=-1, keepdims=True) + NORM_EPS) * g


ROW_SUB = 8


def _rows_to_tiles(x, ref):
    n = x.shape[0]
    for j in range(ROW_SUB):
        ref[pl.ds(j, n, stride=ROW_SUB), :] = x[:, j * LANES:(j + 1) * LANES]


def _tiles_to_rows(ref):
    n = ref.shape[0] // ROW_SUB
    return jnp.concatenate([ref[pl.ds(j, n, stride=ROW_SUB), :] for j in range(ROW_SUB)], axis=1)


def _tile_row(ref, r):
    return ref.at[pl.ds(pl.multiple_of(r * ROW_SUB, ROW_SUB), ROW_SUB)]


def _ada_kernel(c_ref, w_ref, b_ref, o_ref):
    c = c_ref[...]
    cond = c * jax.nn.sigmoid(c)
    o_ref[0] = jnp.dot(cond, w_ref[0], preferred_element_type=F32,
                       precision=lax.Precision.HIGHEST) + b_ref[0]


def _ada(c, w_ada, b_ada):
    depth, d, n = w_ada.shape
    bsz = c.shape[0]
    tn = ADA_COL_TILE
    return pl.pallas_call(
        _ada_kernel,
        grid=(depth, n // tn),
        in_specs=[pl.BlockSpec((bsz, d), lambda l, j: (0, 0)),
                  pl.BlockSpec((1, d, tn), lambda l, j: (l, 0, j)),
                  pl.BlockSpec((1, 1, tn), lambda l, j: (l, 0, j))],
        out_specs=pl.BlockSpec((1, bsz, tn), lambda l, j: (l, 0, j)),
        out_shape=jax.ShapeDtypeStruct((depth, bsz, n), F32),
        compiler_params=_params("arbitrary", "arbitrary"),
        name="ada_mod",
    )(c, w_ada, b_ada.reshape(depth, 1, n))


ROPE_HALF = MLA_ROPE_DIM // 2
NOPE_SPLIT = LANES // 2 - ROPE_HALF


def _head_block(nope, rope):
    pad = jnp.zeros(nope.shape[:-1] + (LANES - MLA_NOPE_DIM - MLA_ROPE_DIM,), nope.dtype)
    return jnp.concatenate([rope[..., :ROPE_HALF], nope[..., :NOPE_SPLIT], rope[..., ROPE_HALF:],
                            nope[..., NOPE_SPLIT:], pad], axis=-1)


def _rope_tab_kernel(pos_ref, inv_ref, c_ref, s_ref):
    pos = pos_ref[0].astype(F32)
    inv = inv_ref[...]
    ang = pos * inv
    lane = lax.broadcasted_iota(jnp.int32, ang.shape, 1)
    is_rope = inv != 0.0
    used = lane < MLA_NOPE_DIM + MLA_ROPE_DIM
    c_ref[0] = jnp.where(is_rope, jnp.cos(ang), jnp.where(used, 1.0, 0.0))
    s_ref[0] = jnp.where(is_rope, jnp.where(lane < LANES // 2, -1.0, 1.0) * jnp.sin(ang), 0.0)


def _rope_tables(positions):
    bsz, s = positions.shape
    inv_freq = ROPE_THETA ** (-jnp.arange(ROPE_HALF, dtype=F32) / ROPE_HALF)
    inv_lane = _head_block(jnp.zeros((MLA_NOPE_DIM,), F32), jnp.concatenate([inv_freq, inv_freq])).reshape(1, LANES)
    tm = TOK_TILE
    spec = pl.BlockSpec((1, tm, LANES), lambda b, i: (b, i, 0))
    shape = jax.ShapeDtypeStruct((bsz, s, LANES), F32)
    return pl.pallas_call(
        _rope_tab_kernel,
        grid=(bsz, s // tm),
        in_specs=[pl.BlockSpec((1, tm, 1), lambda b, i: (b, i, 0)),
                  pl.BlockSpec((1, LANES), lambda b, i: (0, 0))],
        out_specs=[spec, spec],
        out_shape=[shape, shape],
        compiler_params=_params("arbitrary", "arbitrary"),
        name="rope_tables",
    )(positions.reshape(bsz, s, 1), inv_lane)


def _bias_tile_kernel(rb_ref, o_ref):
    h, d = pl.program_id(0), pl.program_id(1)
    t = o_ref.shape[-1]
    key = lax.broadcasted_iota(jnp.int32, (t, t), 0)
    qry = lax.broadcasted_iota(jnp.int32, (t, t), 1)
    dist = d * t + qry - key
    n = jnp.maximum(dist, 0)
    nf = jnp.maximum(n, 1).astype(F32)
    large = MAX_EXACT + (jnp.log(nf / MAX_EXACT) / math.log(MAX_DISTANCE / MAX_EXACT)
                         * (N_BUCKETS - MAX_EXACT)).astype(jnp.int32)
    large = jnp.minimum(large, N_BUCKETS - 1)
    bucket = jnp.where(n < MAX_EXACT, n, large)
    val = jnp.zeros((t, t), F32)
    for j in range(N_BUCKETS):
        val = jnp.where(bucket == j, rb_ref[j, h], val)
    val = (rb_ref[N_BUCKETS - 1, h] - val) * LOG2E
    o_ref[0, 0] = jnp.where(dist < 0, -NEG_BIG, val)


def _bias_tiles(rel_bias):
    t = ATTN_TILE
    return pl.pallas_call(
        _bias_tile_kernel,
        grid=(DIFF_HEADS, 2),
        in_specs=[pl.BlockSpec(memory_space=pltpu.SMEM)],
        out_specs=pl.BlockSpec((1, 1, t, t), lambda h, d: (h, d, 0, 0)),
        out_shape=jax.ShapeDtypeStruct((DIFF_HEADS, 2, t, t), F32),
        compiler_params=_params("arbitrary", "arbitrary"),
        name="bias_tiles",
    )(rel_bias)


def _pre_attn_kernel(x_ref, mod_ref, g_ref, w_in_ref, gq_ref, w_uq_ref, gkv_ref, w_uk_ref, w_uv_ref,
                     rc_ref, rs_ref,
                     dq_ref, dk_ref, dv_ref, mq_ref, mk_ref, mv_ref):
    tm = x_ref.shape[1]
    for r0 in range(0, tm, tm // PRE_ROW_GROUPS):
        rows = slice(r0, r0 + tm // PRE_ROW_GROUPS)
        x = x_ref[0, rows, :]
        sh, sc = mod_ref[0, 0:1, :], mod_ref[0, 1:2, :]
        h = (_rms(x, g_ref[...]) * (1.0 + sc) + sh).astype(BF16)
        proj = jnp.dot(h, w_in_ref[...], preferred_element_type=F32)
        dq_ref[0, rows, :] = (proj[:, C_DQ:C_DK] * (DIFF_QK_DIM ** -0.5 * LOG2E)).astype(BF16)
        dk_ref[0, rows, :] = proj[:, C_DK:C_DV].astype(BF16)
        dv_ref[0, rows, :] = proj[:, C_DV:C_MQ].astype(BF16)

        rc, rs = rc_ref[0, rows, :], rs_ref[0, rows, :]

        def rope(v):
            return v * rc + pltpu.roll(v, LANES // 2, 1) * rs

        qn = _rms(proj[:, C_MQ:C_KV], gq_ref[...]).astype(BF16)
        q = jnp.dot(qn, w_uq_ref[...], preferred_element_type=F32)
        kvn = _rms(proj[:, C_KV:C_KR], gkv_ref[...]).astype(BF16)
        kn = jnp.dot(kvn, w_uk_ref[...], preferred_element_type=F32)
        mv_ref[0, rows, :] = jnp.dot(kvn, w_uv_ref[...], preferred_element_type=F32).astype(BF16)
        kr = rope(proj[:, C_KR:C_END])
        q_scale = (MLA_NOPE_DIM + MLA_ROPE_DIM) ** -0.5 * LOG2E
        for hd in range(MLA_HEADS):
            sl = slice(hd * LANES, (hd + 1) * LANES)
            mq_ref[0, rows, sl] = (rope(q[:, sl]) * q_scale).astype(BF16)
            mk_ref[0, rows, sl] = (kn[:, sl] + kr).astype(BF16)


def _pre_attn(x, mod, g, w_in, gq, w_uq, gkv, w_uk, w_uv, rc, rs):
    bsz, s, d = x.shape
    tm = TOK_TILE
    tok = lambda w: pl.BlockSpec((1, tm, w), lambda b, i: (b, i, 0))
    full = lambda a: pl.BlockSpec(a.shape, lambda b, i: (0,) * a.ndim)
    widths = (512, 512, 512, MLA_HEADS * LANES, MLA_HEADS * LANES, MLA_HEADS * MLA_V_DIM)
    return pl.pallas_call(
        _pre_attn_kernel,
        grid=(bsz, s // tm),
        in_specs=[tok(d), pl.BlockSpec((1, ADA_CHUNKS, d), lambda b, i: (b, 0, 0)), full(g), full(w_in),
                  full(gq), full(w_uq), full(gkv), full(w_uk), full(w_uv), tok(LANES), tok(LANES)],
        out_specs=[tok(w) for w in widths],
        out_shape=[jax.ShapeDtypeStruct((bsz, s, w), BF16) for w in widths],
        compiler_params=_params("arbitrary", "arbitrary"),
        name="pre_attn",
    )(x, mod, g, w_in, gq, w_uq, gkv, w_uk, w_uv, rc, rs)


ONES_ROWS = BF16_SUBLANES


def _values_t(v, dv):
    vt = v.T
    ones = jnp.ones((ONES_ROWS, vt.shape[1]), vt.dtype)
    parts = []
    for h in range(vt.shape[0] // dv):
        parts += [vt[h * dv:(h + 1) * dv], ones]
    return jnp.concatenate(parts, axis=0)


def _softmax_init(m_ref, acc_ref):
    m_ref[...] = jnp.full(m_ref.shape, NEG_BIG, F32)
    acc_ref[...] = jnp.zeros(acc_ref.shape, F32)


def _softmax_probs(st, m_ref):
    m_prev = m_ref[...]
    m_new = jnp.maximum(m_prev, jnp.max(st, axis=0, keepdims=True))
    m_ref[...] = m_new
    return jnp.exp2(st - m_new).astype(BF16), jnp.exp2(m_prev - m_new)


def _acc_update(acc_ref, alpha, vt, p):
    acc_ref[...] = alpha * acc_ref[...] + jnp.dot(vt, p, preferred_element_type=F32)


def _normalized(acc_ref, dv):
    return acc_ref[:dv] / acc_ref[dv:dv + 1]


def _attn_kernel(*refs, cast_weights):
    (lam_init_ref, lam_ref, g_ref, bias_ref, dq_ref, dk_ref, dv_ref, mq_ref, mk_ref, mv_ref), refs = refs[:10], refs[10:]
    if cast_weights:
        for w_ref, wb_ref in zip(refs[:3], refs[4:7]):
            wb_ref[...] = w_ref[...].astype(BF16)
        refs = refs[3:4] + refs[7:]
    o_ref, dvt_ref, mvt_ref, m_ref, dacc_ref, macc_ref, pend_ref = refs
    i = pl.program_id(1)
    t = dq_ref.shape[1]
    n_diff = 2 * DIFF_HEADS
    n_chains = n_diff + MLA_HEADS
    d_rows, m_rows = DIFF_V_DIM + ONES_ROWS, MLA_V_DIM + ONES_ROWS

    @pl.when(i == 0)
    def _():
        dvt_ref[...] = _values_t(dv_ref[0], DIFF_V_DIM)
        mvt_ref[...] = _values_t(mv_ref[0], MLA_V_DIM)

    qts = []
    for h in range(DIFF_HEADS):
        qt = dq_ref[0, :, h * LANES:(h + 1) * LANES].T
        row = lax.broadcasted_iota(jnp.int32, qt.shape, 0)
        qts.append(jnp.where(row < DIFF_QK_DIM, qt, jnp.zeros_like(qt)))
        qts.append(jnp.where(row >= DIFF_QK_DIM, qt, jnp.zeros_like(qt)))
    qts += [mq_ref[0, :, h * LANES:(h + 1) * LANES].T for h in range(MLA_HEADS)]

    def scores(j, c):
        rows = pl.ds(pl.multiple_of(j * t, t), t)
        if c < n_diff:
            k = dk_ref[0, rows, (c // 2) * LANES:(c // 2 + 1) * LANES]
        else:
            k = mk_ref[0, rows, (c - n_diff) * LANES:(c - n_diff + 1) * LANES]
        return jnp.dot(k, qts[c], preferred_element_type=F32)

    def values(j, c):
        cols = pl.ds(pl.multiple_of(j * t, t), t)
        if c < n_diff:
            return dvt_ref[(c // 2) * d_rows:(c // 2 + 1) * d_rows, cols]
        return mvt_ref[(c - n_diff) * m_rows:(c - n_diff + 1) * m_rows, cols]

    def acc_at(c):
        return dacc_ref.at[c] if c < n_diff else macc_ref.at[c - n_diff]

    def step(j, bias_idx, has_next):
        hidden = None
        if bias_idx == 0:
            key = lax.broadcasted_iota(jnp.int32, (t, t), 0)
            qry = lax.broadcasted_iota(jnp.int32, (t, t), 1)
            hidden = jnp.where(key <= qry, 0.0, -NEG_BIG)

        def adjust(c, st):
            if c < n_diff:
                return st if bias_idx is None else st - bias_ref[c // 2, bias_idx]
            return st if hidden is None else st - hidden

        pending = [pend_ref[c] for c in range(AHEAD)]
        for c in range(n_chains):
            nxt = c + AHEAD
            if nxt < n_chains:
                pending.append(scores(j, nxt))
            elif has_next:
                pend_ref[nxt - n_chains] = scores(j + 1, nxt - n_chains)
            p, alpha = _softmax_probs(adjust(c, pending.pop(0)), m_ref.at[c])
            _acc_update(acc_at(c), alpha, values(j, c), p)

    def far(j, carry):
        step(j, None, True)
        return carry

    for c in range(n_chains):
        _softmax_init(m_ref.at[c], acc_at(c))
    for c in range(AHEAD):
        pend_ref[c] = scores(0, c)
    lax.fori_loop(0, jnp.maximum(i - 1, 0), far, 0)

    @pl.when(i >= 1)
    def _():
        step(i - 1, 1, True)

    step(i, 0, False)

    lv = lam_ref[...]
    lam = (jnp.exp(jnp.sum(lv[0:1] * lv[1:2], keepdims=True)) - jnp.exp(jnp.sum(lv[2:3] * lv[3:4], keepdims=True))
           + lam_init_ref[0])
    for h in range(DIFF_HEADS):
        c0, c1 = 2 * h, 2 * h + 1
        ot = (_normalized(dacc_ref.at[c0], DIFF_V_DIM)
              - lam * _normalized(dacc_ref.at[c1], DIFF_V_DIM))
        ot = ot * lax.rsqrt(jnp.mean(ot * ot, axis=0, keepdims=True) + NORM_EPS) * g_ref[...]
        o_ref[0, :, h * LANES:(h + 1) * LANES] = (ot * (1.0 - lam_init_ref[0])).T.astype(BF16)
    base = DIFF_HEADS * LANES
    for u in range(MLA_HEADS // 2):
        ot = jnp.concatenate([_normalized(macc_ref.at[2 * u], MLA_V_DIM),
                              _normalized(macc_ref.at[2 * u + 1], MLA_V_DIM)], axis=0)
        o_ref[0, :, base + u * LANES:base + (u + 1) * LANES] = ot.T.astype(BF16)


def _attention(lam_init, diff_lambda, g, bias, dq, dk, dv, mq, mk, mv, expert_w=None, expert_layer=0):
    bsz, s, wd = dq.shape
    wq, wv = mq.shape[-1], mv.shape[-1]
    t = ATTN_TILE
    assert MAX_DISTANCE <= t and s % t == 0
    n_diff = 2 * DIFF_HEADS
    n_chains = n_diff + MLA_HEADS
    per_b = s // t
    qtile = lambda w: pl.BlockSpec((1, t, w), lambda b, i: (b, i, 0))
    whole = lambda w: pl.BlockSpec((1, s, w), lambda b, i: (b, 0, 0))
    args = [lam_init, diff_lambda, g, bias, dq, dk, dv, mq, mk, mv]
    in_specs = [pl.BlockSpec(memory_space=pltpu.SMEM),
                pl.BlockSpec(diff_lambda.shape, lambda b, i: (0, 0)),
                pl.BlockSpec(g.shape, lambda b, i: (0, 0)),
                pl.BlockSpec(bias.shape, lambda b, i: (0, 0, 0, 0)),
                qtile(wd), whole(wd), whole(wd), qtile(wq), whole(wq), whole(wv)]
    out_specs = [qtile(wd + wv)]
    out_shape = [jax.ShapeDtypeStruct((bsz, s, wd + wv), BF16)]
    if expert_w is not None:
        steps = bsz * per_b
        n_lay, n_e, d, d_ff = expert_w[0].shape
        for w in expert_w:
            rows, cols = n_e * w.shape[2], w.shape[3]
            assert rows % (BF16_SUBLANES * steps) == 0
            slab = rows // steps
            args.append(w.reshape(n_lay * rows, cols))
            in_specs.append(pl.BlockSpec((slab, cols), lambda b, i: (expert_layer * steps + b * per_b + i, 0)))
            out_specs.append(pl.BlockSpec((slab, cols), lambda b, i: (b * per_b + i, 0)))
            out_shape.append(jax.ShapeDtypeStruct((rows, cols), BF16))
    outs = pl.pallas_call(
        functools.partial(_attn_kernel, cast_weights=expert_w is not None),
        grid=(bsz, per_b),
        in_specs=in_specs,
        out_specs=out_specs,
        out_shape=out_shape,
        scratch_shapes=[pltpu.VMEM((DIFF_HEADS * (DIFF_V_DIM + ONES_ROWS), s), BF16),
                        pltpu.VMEM((MLA_HEADS * (MLA_V_DIM + ONES_ROWS), s), BF16),
                        pltpu.VMEM((n_chains, 1, t), F32),
                        pltpu.VMEM((n_diff, DIFF_V_DIM + ONES_ROWS, t), F32),
                        pltpu.VMEM((MLA_HEADS, MLA_V_DIM + ONES_ROWS, t), F32),
                        pltpu.VMEM((AHEAD, t, t), F32)],
        compiler_params=_params("arbitrary", "arbitrary"),
        name="attention",
    )(*args)
    if expert_w is None:
        return outs[0]
    return outs[0], tuple(wb.reshape(w.shape[1:]) for wb, w in zip(outs[1:], expert_w))


def _post_attn_kernel(*refs, moe):
    if moe:
        o_ref, x_ref, mod_ref, g_ref, wo_ref, wr_ref, x1_ref, h_ref, gates_ref, sel_ref = refs
    else:
        o_ref, x_ref, mod_ref, g_ref, wo_ref, x1_ref, h_ref = refs
    gt_a, sh_f, sc_f = mod_ref[0, 2:3, :], mod_ref[0, 3:4, :], mod_ref[0, 4:5, :]
    tm = x_ref.shape[1]
    n = tm // POST_ROW_GROUPS
    for r0 in range(0, tm, n):
        rows = slice(r0, r0 + n)
        y = jnp.dot(o_ref[0, rows, :], wo_ref[...], preferred_element_type=F32)
        x1 = x_ref[0, rows, :] + (1.0 + gt_a) * y
        x1_ref[0, rows, :] = x1
        h = _rms(x1, g_ref[...]) * (1.0 + sc_f) + sh_f
        if not moe:
            h_ref[0, rows, :] = h.astype(BF16)
            continue
        _rows_to_tiles(h, h_ref.at[0, pl.ds(r0 * ROW_SUB, n * ROW_SUB)])
        wr = wr_ref[...]
        h_hi, wr_hi = h.astype(BF16), wr.astype(BF16)
        h_lo, wr_lo = (h - h_hi.astype(F32)).astype(BF16), (wr - wr_hi.astype(F32)).astype(BF16)
        logits = (jnp.dot(h_hi, wr_hi, preferred_element_type=F32)
                  + jnp.dot(h_hi, wr_lo, preferred_element_type=F32)
                  + jnp.dot(h_lo, wr_hi, preferred_element_type=F32))
        lane = lax.broadcasted_iota(jnp.int32, logits.shape, 1)
        logits = jnp.where(lane < N_EXPERTS, logits, -jnp.inf)
        v1 = jnp.max(logits, axis=1, keepdims=True)
        i1 = jnp.min(jnp.where(logits == v1, lane, LANES), axis=1, keepdims=True)
        rest = jnp.where(lane == i1, -jnp.inf, logits)
        v2 = jnp.max(rest, axis=1, keepdims=True)
        i2 = jnp.min(jnp.where(rest == v2, lane, LANES), axis=1, keepdims=True)
        e2 = jnp.exp(v2 - v1)
        w1 = 1.0 / (1.0 + e2)
        w2 = e2 / (1.0 + e2)
        gates_ref[0, rows, :] = jnp.where(lane == i1, w1, 0.0) + jnp.where(lane == i2, w2, 0.0)
        sel_ref[0, rows, :] = jnp.where(lane == i1, 1.0, 0.0) + jnp.where(lane == i2, 2.0, 0.0)


def _post_attn(o, x, mod, g, wo, w_router=None):
    bsz, s, d = x.shape
    moe = w_router is not None
    tm = TOK_TILE
    tok = lambda w: pl.BlockSpec((1, tm, w), lambda b, i: (b, i, 0))
    full = lambda a: pl.BlockSpec(a.shape, lambda b, i: (0,) * a.ndim)
    args = [o, x, mod, g, wo] + ([w_router] if moe else [])
    in_specs = [tok(o.shape[-1]), tok(d), pl.BlockSpec((1, ADA_CHUNKS, d), lambda b, i: (b, 0, 0)),
                full(g), full(wo)] + ([full(w_router)] if moe else [])
    if moe:
        assert d == ROW_SUB * LANES
        h_spec = pl.BlockSpec((1, tm * ROW_SUB, LANES), lambda b, i: (b, i, 0))
        out_specs = [tok(d), h_spec, tok(LANES), tok(LANES)]
        out_shape = [jax.ShapeDtypeStruct((bsz, s, d), F32), jax.ShapeDtypeStruct((bsz, s * ROW_SUB, LANES), F32)]
        out_shape += [jax.ShapeDtypeStruct((bsz, s, LANES), F32)] * 2
    else:
        out_specs = [tok(d), tok(d)]
        out_shape = [jax.ShapeDtypeStruct((bsz, s, d), F32), jax.ShapeDtypeStruct((bsz, s, d), BF16)]
    return pl.pallas_call(
        functools.partial(_post_attn_kernel, moe=moe),
        grid=(bsz, s // tm),
        in_specs=in_specs,
        out_specs=out_specs,
        out_shape=out_shape,
        compiler_params=_params("arbitrary", "arbitrary"),
        name="post_attn_moe" if moe else "post_attn",
    )(*args)


def _swiglu(h, w1_ref, w3_ref, w2_ref):
    d_ff = w1_ref.shape[-1]
    y = jnp.zeros((h.shape[0], w2_ref.shape[-1]), F32)
    for c0 in range(0, d_ff, FF_CHUNK):
        a = jnp.dot(h, w1_ref[:, c0:c0 + FF_CHUNK], preferred_element_type=F32)
        b = jnp.dot(h, w3_ref[:, c0:c0 + FF_CHUNK], preferred_element_type=F32)
        u = (a * jax.nn.sigmoid(a) * b).astype(BF16)
        y = y + jnp.dot(u, w2_ref[c0:c0 + FF_CHUNK, :], preferred_element_type=F32)
    return y


def _ffn_dense_kernel(h_ref, x1_ref, mod_ref, w1_ref, w3_ref, w2_ref, o_ref):
    gt_f = mod_ref[0, 5:6, :]
    o_ref[0] = x1_ref[0] + (1.0 + gt_f) * _swiglu(h_ref[0], w1_ref, w3_ref, w2_ref)


def _ffn_dense(h, x1, mod, w1, w3, w2):
    bsz, s, d = x1.shape
    tm = FFN_TILE
    tok = lambda w: pl.BlockSpec((1, tm, w), lambda b, i: (b, i, 0))
    full = lambda a: pl.BlockSpec(a.shape, lambda b, i: (0,) * a.ndim)
    return pl.pallas_call(
        _ffn_dense_kernel,
        grid=(bsz, s // tm),
        in_specs=[tok(d), tok(d), pl.BlockSpec((1, ADA_CHUNKS, d), lambda b, i: (b, 0, 0)),
                  full(w1), full(w3), full(w2)],
        out_specs=tok(d),
        out_shape=jax.ShapeDtypeStruct(x1.shape, F32),
        compiler_params=_params("arbitrary", "arbitrary"),
        name="ffn_dense",
    )(h, x1, mod, w1, w3, w2)


def _route_kernel(sel_ref, pos_ref, te_ref, cnt_ref, off_ref, run_ref, *, row_tile):
    p, t = pl.program_id(0), pl.program_id(1)
    sel_t = sel_ref[...].T
    chosen = (sel_t > 0.0).astype(F32)
    per_expert = jnp.sum(chosen, axis=1, keepdims=True)

    @pl.when((p == 0) & (t == 0))
    def _():
        cnt_ref[...] = jnp.zeros(cnt_ref.shape, F32)
        te_ref[...] = jnp.zeros(te_ref.shape, jnp.int32)

    @pl.when(p == 0)
    def _():
        cnt_ref[...] += per_expert

    @pl.when((p == 1) & (t == 0))
    def _():
        cnt = cnt_ref[...]
        padded = jnp.ceil(cnt / row_tile) * row_tile
        row = lax.broadcasted_iota(jnp.int32, cnt.shape, 0)
        off = jnp.zeros(cnt.shape, F32)
        for e in range(N_EXPERTS):
            size_e = jnp.sum(jnp.where(row == e, padded, 0.0), keepdims=True)
            off = off + jnp.where(row > e, size_e, 0.0)
        off_ref[...] = off
        run_ref[...] = jnp.zeros(run_ref.shape, F32)
        ends = off + padded
        tile_start = lax.broadcasted_iota(jnp.int32, (LANES, LANES), 1).astype(F32) * row_tile
        erow = lax.broadcasted_iota(jnp.int32, (LANES, LANES), 0)
        done = jnp.where((erow < N_EXPERTS) & (ends <= tile_start), 1.0, 0.0)
        te = jnp.sum(done, axis=0, keepdims=True).astype(jnp.int32)
        te_ref[...] = jnp.broadcast_to(te, te_ref.shape)

    @pl.when(p == 1)
    def _():
        tm = sel_t.shape[1]
        before = (lax.broadcasted_iota(jnp.int32, (tm, tm), 0)
                  < lax.broadcasted_iota(jnp.int32, (tm, tm), 1)).astype(BF16)
        rank = jnp.dot(chosen.astype(BF16), before, preferred_element_type=F32) + run_ref[...]
        base = off_ref[...] + rank
        for k in range(2):
            pos = jnp.sum(jnp.where(sel_t == float(k + 1), base, 0.0), axis=0, keepdims=True)
            pos_ref[0, k:k + 1, :] = pos.astype(jnp.int32)
        run_ref[...] += per_expert


def _route(sel, row_tile, n_row_tiles):
    n_tok = sel.shape[0]
    assert n_row_tiles <= LANES and 2 * n_tok < 2 ** 24
    tm = TOK_TILE
    nt = n_tok // tm
    pos, te = pl.pallas_call(
        functools.partial(_route_kernel, row_tile=row_tile),
        grid=(2, nt),
        in_specs=[pl.BlockSpec((tm, LANES), lambda p, t: (t, 0))],
        out_specs=[pl.BlockSpec((1, 2, tm), lambda p, t: (p * t, 0, 0)),
                   pl.BlockSpec((8, LANES), lambda p, t: (0, 0))],
        out_shape=[jax.ShapeDtypeStruct((nt, 2, tm), jnp.int32), jax.ShapeDtypeStruct((8, LANES), jnp.int32)],
        scratch_shapes=[pltpu.VMEM((LANES, 1), F32)] * 3,
        compiler_params=_params("arbitrary", "arbitrary"),
        name="moe_route",
    )(sel)
    return pos, te[0]


def _dispatch_kernel(pos_ref, h_ref, xs_in_ref, xs_ref, sem):
    del xs_in_ref
    tm = h_ref.shape[0] // ROW_SUB

    def body(r, carry):
        for k in range(2):
            pltpu.make_async_copy(_tile_row(h_ref, r), _tile_row(xs_ref, pos_ref[0, k, r]), sem).start(priority=k)
        return carry

    lax.fori_loop(0, tm, body, 0, unroll=8)
    for k in range(2):
        pltpu.make_async_copy(h_ref, xs_ref.at[pl.ds(0, tm * ROW_SUB)], sem).wait()


def _dispatch(pos, h, xs0):
    nt, _, tm = pos.shape
    return pl.pallas_call(
        _dispatch_kernel,
        grid=(nt,),
        in_specs=[pl.BlockSpec((1, 2, tm), lambda t: (t, 0, 0), memory_space=pltpu.SMEM),
                  pl.BlockSpec((tm * ROW_SUB, LANES), lambda t: (t, 0)),
                  pl.BlockSpec(memory_space=pl.ANY)],
        out_specs=pl.BlockSpec(memory_space=pl.ANY),
        out_shape=jax.ShapeDtypeStruct(xs0.shape, xs0.dtype),
        scratch_shapes=[pltpu.SemaphoreType.DMA(())],
        input_output_aliases={2: 0},
        compiler_params=_params("arbitrary"),
        name="moe_dispatch",
    )(pos, h, xs0)


def _expert_kernel(te_ref, xs_ref, w1_ref, w3_ref, w2_ref, y_ref):
    used = te_ref[pl.program_id(0)] < N_EXPERTS

    @pl.when(used)
    def _():
        y = _swiglu(_tiles_to_rows(xs_ref).astype(BF16), w1_ref.at[0], w3_ref.at[0], w2_ref.at[0])
        _rows_to_tiles(y, y_ref)

    @pl.when(jnp.logical_not(used))
    def _():
        y_ref[...] = jnp.zeros(y_ref.shape, y_ref.dtype)


def _experts(te, xs, w1, w3, w2, row_tile):
    n_rows = xs.shape[0] // ROW_SUB
    _, d, d_ff = w1.shape
    expert = lambda n, te: (jnp.minimum(te[n], N_EXPERTS - 1), 0, 0)
    rows = pl.BlockSpec((row_tile * ROW_SUB, LANES), lambda n, te: (n, 0))
    return pl.pallas_call(
        _expert_kernel,
        grid_spec=pltpu.PrefetchScalarGridSpec(
            num_scalar_prefetch=1,
            grid=(n_rows // row_tile,),
            in_specs=[rows, pl.BlockSpec((1, d, d_ff), expert), pl.BlockSpec((1, d, d_ff), expert),
                      pl.BlockSpec((1, d_ff, d), expert)],
            out_specs=rows),
        out_shape=jax.ShapeDtypeStruct(xs.shape, F32),
        compiler_params=_params("arbitrary"),
        name="moe_experts",
    )(te, xs, w1, w3, w2)


def _combine_kernel(pos_ref, pos_next_ref, x1_ref, gates_ref, sel_ref, mod_ref, gfin_ref, ys_ref, o_ref,
                    ybuf_ref, sem, *, final):
    tm = x1_ref.shape[1]
    step = pl.program_id(0) * pl.num_programs(1) + pl.program_id(1)
    n_steps = pl.num_programs(0) * pl.num_programs(1)
    slot = lax.rem(step, 2)

    def gather(p_ref, into):
        def body(r, carry):
            for k in range(2):
                pltpu.make_async_copy(_tile_row(ys_ref, p_ref[0, k, r]), _tile_row(ybuf_ref.at[into, k], r),
                                      sem.at[into]).start(priority=k)
            return carry

        lax.fori_loop(0, tm, body, 0, unroll=8)

    @pl.when(step == 0)
    def _():
        gather(pos_ref, slot)

    @pl.when(step + 1 < n_steps)
    def _():
        gather(pos_next_ref, 1 - slot)

    gates, sel = gates_ref[0], sel_ref[0]
    w_a = jnp.sum(jnp.where(sel == 1.0, gates, 0.0), axis=1, keepdims=True)
    w_b = jnp.sum(jnp.where(sel == 2.0, gates, 0.0), axis=1, keepdims=True)
    for k in range(2):
        pltpu.make_async_copy(ys_ref.at[pl.ds(0, tm * ROW_SUB)], ybuf_ref.at[slot, k], sem.at[slot]).wait()
    gt_f = mod_ref[0, 5:6, :]
    y = w_a * _tiles_to_rows(ybuf_ref.at[slot, 0]) + w_b * _tiles_to_rows(ybuf_ref.at[slot, 1])
    x = x1_ref[0] + (1.0 + gt_f) * y
    o_ref[0] = _rms(x, gfin_ref[...]) if final else x


def _combine(pos, x1, gates, sel, mod, ys, g_final=None):
    bsz, s, d = x1.shape
    nt, _, tm = pos.shape
    per_b = s // tm
    tok = lambda w: pl.BlockSpec((1, tm, w), lambda b, i: (b, i, 0))
    final = g_final is not None
    gfin = g_final if final else jnp.ones((1, d), F32)
    return pl.pallas_call(
        functools.partial(_combine_kernel, final=final),
        grid=(bsz, per_b),
        in_specs=[pl.BlockSpec((1, 2, tm), lambda b, i: (b * per_b + i, 0, 0), memory_space=pltpu.SMEM),
                  pl.BlockSpec((1, 2, tm), lambda b, i: (jnp.minimum(b * per_b + i + 1, nt - 1), 0, 0),
                               memory_space=pltpu.SMEM),
                  tok(d), tok(LANES), tok(LANES),
                  pl.BlockSpec((1, ADA_CHUNKS, d), lambda b, i: (b, 0, 0)),
                  pl.BlockSpec((1, d), lambda b, i: (0, 0)),
                  pl.BlockSpec(memory_space=pl.ANY)],
        out_specs=tok(d),
        out_shape=jax.ShapeDtypeStruct(x1.shape, F32),
        scratch_shapes=[pltpu.VMEM((2, 2, tm * ROW_SUB, LANES), F32), pltpu.SemaphoreType.DMA((2,))],
        compiler_params=_params("arbitrary", "arbitrary"),
        name="moe_combine",
    )(pos, pos, x1, gates, sel, mod, gfin, ys)


def _moe(h, x1, gates, sel, mod, w1, w3, w2, xs_buf, g_final):
    bsz, s, d = x1.shape
    n_tok = bsz * s
    row_tile = FFN_TILE
    n_rows = xs_buf.shape[0] // ROW_SUB
    assert n_rows == _moe_rows(n_tok)
    pos, te = _route(sel.reshape(n_tok, LANES), row_tile, n_rows // row_tile)
    xs = _dispatch(pos, h.reshape(n_tok * ROW_SUB, LANES), xs_buf)
    ys = _experts(te, xs, w1, w3, w2, row_tile)
    return _combine(pos, x1, gates, sel, mod, ys, g_final), xs


def _moe_rows(n_tok):
    return 2 * n_tok + N_EXPERTS * FFN_TILE


def _final_norm_kernel(x_ref, g_ref, o_ref):
    o_ref[0] = _rms(x_ref[0], g_ref[...])


def _final_norm(x, g):
    bsz, s, d = x.shape
    tm = TOK_TILE
    tok = pl.BlockSpec((1, tm, d), lambda b, i: (b, i, 0))
    return pl.pallas_call(
        _final_norm_kernel,
        grid=(bsz, s // tm),
        in_specs=[tok, pl.BlockSpec(g.shape, lambda b, i: (0, 0))],
        out_specs=tok,
        out_shape=jax.ShapeDtypeStruct(x.shape, F32),
        compiler_params=_params("arbitrary", "arbitrary"),
        name="final_norm",
    )(x, g)


def _head_blocks(w, width, nope_cols, rope_cols):
    k = w.shape[0]
    w = w.reshape(k, MLA_HEADS, width)
    nope = w[:, :, nope_cols] if nope_cols is not None else jnp.zeros((k, MLA_HEADS, MLA_NOPE_DIM), w.dtype)
    rope = w[:, :, rope_cols] if rope_cols is not None else jnp.zeros((k, MLA_HEADS, MLA_ROPE_DIM), w.dtype)
    return _head_block(nope, rope).reshape(k, MLA_HEADS * LANES)


def _prep_w_in(w_in):
    d = w_in.shape[0]
    kr_block = _head_block(jnp.zeros((d, MLA_NOPE_DIM), w_in.dtype), w_in[:, C_KR:])
    return jnp.concatenate([w_in[:, :C_KR], kr_block], axis=1).astype(BF16)


def kernel(x, c, positions, w_ada, b_ada, g_attn, w_in, diff_lambda, diff_subln_g, rel_bias, mla_q_norm, w_uq, mla_kv_norm, w_ukv, w_o, g_ffn, ffn_w1, ffn_w3, ffn_w2, moe_router, moe_w1, moe_w3, moe_w2, g_final):
    depth = w_ada.shape[0]
    bsz, s, d = x.shape
    mods = _ada(c, w_ada, b_ada).reshape(depth, bsz, ADA_CHUNKS, d)
    rc, rs = _rope_tables(positions)
    bias = _bias_tiles(rel_bias)
    qk_w = MLA_NOPE_DIM + MLA_ROPE_DIM
    kv_w = MLA_NOPE_DIM + MLA_V_DIM
    xs_buf = jnp.zeros((_moe_rows(bsz * s) * ROW_SUB, LANES), F32)
    for l in range(depth):
        mod = mods[l]
        lam_init = jnp.full((1,), 0.8 - 0.6 * math.exp(-0.3 * l), F32)
        w_uq_p = _head_blocks(w_uq[l], qk_w, slice(0, MLA_NOPE_DIM), slice(MLA_NOPE_DIM, qk_w)).astype(BF16)
        w_uk_p = _head_blocks(w_ukv[l], kv_w, slice(0, MLA_NOPE_DIM), None).astype(BF16)
        w_uv_p = w_ukv[l].reshape(MLA_KV_RANK, MLA_HEADS, kv_w)[:, :, MLA_NOPE_DIM:].reshape(
            MLA_KV_RANK, MLA_HEADS * MLA_V_DIM).astype(BF16)
        dq, dk, dv, mq, mk, mv = _pre_attn(
            x, mod, g_attn[l].reshape(1, d), _prep_w_in(w_in[l]), mla_q_norm[l].reshape(1, -1), w_uq_p,
            mla_kv_norm[l].reshape(1, -1), w_uk_p, w_uv_p, rc, rs)
        attn_args = (lam_init, diff_lambda[l], diff_subln_g[l].reshape(-1, 1), bias, dq, dk, dv, mq, mk, mv)
        wo = w_o[l].astype(BF16)
        g_f = g_ffn[l].reshape(1, d)
        if l % 2 == 1:
            o, expert_w = _attention(*attn_args, expert_w=(moe_w1, moe_w3, moe_w2), expert_layer=l // 2)
            w_router = jnp.pad(moe_router[l // 2], ((0, 0), (0, LANES - N_EXPERTS)))
            x1, h, gates, sel = _post_attn(o, x, mod, g_f, wo, w_router)
            g_fin = g_final.reshape(1, d) if l == depth - 1 else None
            x, xs_buf = _moe(h, x1, gates, sel, mod, *expert_w, xs_buf, g_fin)
        else:
            o = _attention(*attn_args)
            x1, h = _post_attn(o, x, mod, g_f, wo)
            x = _ffn_dense(h, x1, mod, ffn_w1[l // 2].astype(BF16), ffn_w3[l // 2].astype(BF16),
                           ffn_w2[l // 2].astype(BF16))
    return x if depth % 2 == 0 else _final_norm(x, g_final.reshape(1, d))
```

```python
import functools
import math

import jax
import jax.numpy as jnp
from jax import lax
from jax.experimental import pallas as pl
from jax.experimental.pallas import tpu as pltpu

F32 = jnp.float32
BF16 = jnp.bfloat16

DIFF_HEADS = 4
DIFF_QK_DIM = 64
DIFF_V_DIM = 128
MLA_HEADS = 8
MLA_NOPE_DIM = 64
MLA_ROPE_DIM = 32
MLA_V_DIM = 64
MLA_Q_RANK = 384
MLA_KV_RANK = 256
ROPE_THETA = 10000.0
N_BUCKETS = 32
MAX_EXACT = 16
MAX_DISTANCE = 128
N_EXPERTS = 8
NORM_EPS = 1e-6
ADA_CHUNKS = 6

LANES = 128
BF16_SUBLANES = 16
NEG_BIG = -1e30
LOG2E = math.log2(math.e)
VMEM_LIMIT = 56 * 1024 * 1024

ATTN_TILE = 256
AHEAD = 4
TOK_TILE = 512
PRE_ROW_GROUPS = 2
POST_ROW_GROUPS = 2
FFN_TILE = 512
FF_CHUNK = 256
ADA_COL_TILE = 1536

C_DQ, C_DK, C_DV, C_MQ, C_KV, C_KR, C_END = 0, 512, 1024, 1536, 1920, 2176, 2304


def _params(*sem):
    return pltpu.CompilerParams(dimension_semantics=sem, vmem_limit_bytes=VMEM_LIMIT)


def _rms(x, g):
    return x * lax.rsqrt(jnp.mean(x * x, axis=-1, keepdims=True) + NORM_EPS) * g


ROW_SUB = 8


def _rows_to_tiles(x, ref):
    n = x.shape[0]
    for j in range(ROW_SUB):
        ref[pl.ds(j, n, stride=ROW_SUB), :] = x[:, j * LANES:(j + 1) * LANES]


def _tiles_to_rows(ref):
    n = ref.shape[0] // ROW_SUB
    return jnp.concatenate([ref[pl.ds(j, n, stride=ROW_SUB), :] for j in range(ROW_SUB)], axis=1)


def _tile_row(ref, r):
    return ref.at[pl.ds(pl.multiple_of(r * ROW_SUB, ROW_SUB), ROW_SUB)]


def _ada_kernel(c_ref, w_ref, b_ref, o_ref):
    c = c_ref[...]
    cond = c * jax.nn.sigmoid(c)
    o_ref[0] = jnp.dot(cond, w_ref[0], preferred_element_type=F32,
                       precision=lax.Precision.HIGHEST) + b_ref[0]


def _ada(c, w_ada, b_ada):
    depth, d, n = w_ada.shape
    bsz = c.shape[0]
    tn = ADA_COL_TILE
    return pl.pallas_call(
        _ada_kernel,
        grid=(depth, n // tn),
        in_specs=[pl.BlockSpec((bsz, d), lambda l, j: (0, 0)),
                  pl.BlockSpec((1, d, tn), lambda l, j: (l, 0, j)),
                  pl.BlockSpec((1, 1, tn), lambda l, j: (l, 0, j))],
        out_specs=pl.BlockSpec((1, bsz, tn), lambda l, j: (l, 0, j)),
        out_shape=jax.ShapeDtypeStruct((depth, bsz, n), F32),
        compiler_params=_params("arbitrary", "arbitrary"),
        name="ada_mod",
    )(c, w_ada, b_ada.reshape(depth, 1, n))


ROPE_HALF = MLA_ROPE_DIM // 2
NOPE_SPLIT = LANES // 2 - ROPE_HALF


def _head_block(nope, rope):
    pad = jnp.zeros(nope.shape[:-1] + (LANES - MLA_NOPE_DIM - MLA_ROPE_DIM,), nope.dtype)
    return jnp.concatenate([rope[..., :ROPE_HALF], nope[..., :NOPE_SPLIT], rope[..., ROPE_HALF:],
                            nope[..., NOPE_SPLIT:], pad], axis=-1)


def _rope_tab_kernel(pos_ref, inv_ref, c_ref, s_ref):
    pos = pos_ref[0].astype(F32)
    inv = inv_ref[...]
    ang = pos * inv
    lane = lax.broadcasted_iota(jnp.int32, ang.shape, 1)
    is_rope = inv != 0.0
    used = lane < MLA_NOPE_DIM + MLA_ROPE_DIM
    c_ref[0] = jnp.where(is_rope, jnp.cos(ang), jnp.where(used, 1.0, 0.0))
    s_ref[0] = jnp.where(is_rope, jnp.where(lane < LANES // 2, -1.0, 1.0) * jnp.sin(ang), 0.0)


def _rope_tables(positions):
    bsz, s = positions.shape
    inv_freq = ROPE_THETA ** (-jnp.arange(ROPE_HALF, dtype=F32) / ROPE_HALF)
    inv_lane = _head_block(jnp.zeros((MLA_NOPE_DIM,), F32), jnp.concatenate([inv_freq, inv_freq])).reshape(1, LANES)
    tm = TOK_TILE
    spec = pl.BlockSpec((1, tm, LANES), lambda b, i: (b, i, 0))
    shape = jax.ShapeDtypeStruct((bsz, s, LANES), F32)
    return pl.pallas_call(
        _rope_tab_kernel,
        grid=(bsz, s // tm),
        in_specs=[pl.BlockSpec((1, tm, 1), lambda b, i: (b, i, 0)),
                  pl.BlockSpec((1, LANES), lambda b, i: (0, 0))],
        out_specs=[spec, spec],
        out_shape=[shape, shape],
        compiler_params=_params("arbitrary", "arbitrary"),
        name="rope_tables",
    )(positions.reshape(bsz, s, 1), inv_lane)


def _bias_tile_kernel(rb_ref, o_ref):
    h, d = pl.program_id(0), pl.program_id(1)
    t = o_ref.shape[-1]
    key = lax.broadcasted_iota(jnp.int32, (t, t), 0)
    qry = lax.broadcasted_iota(jnp.int32, (t, t), 1)
    dist = d * t + qry - key
    n = jnp.maximum(dist, 0)
    nf = jnp.maximum(n, 1).astype(F32)
    large = MAX_EXACT + (jnp.log(nf / MAX_EXACT) / math.log(MAX_DISTANCE / MAX_EXACT)
                         * (N_BUCKETS - MAX_EXACT)).astype(jnp.int32)
    large = jnp.minimum(large, N_BUCKETS - 1)
    bucket = jnp.where(n < MAX_EXACT, n, large)
    val = jnp.zeros((t, t), F32)
    for j in range(N_BUCKETS):
        val = jnp.where(bucket == j, rb_ref[j, h], val)
    val = (rb_ref[N_BUCKETS - 1, h] - val) * LOG2E
    o_ref[0, 0] = jnp.where(dist < 0, -NEG_BIG, val)


def _bias_tiles(rel_bias):
    t = ATTN_TILE
    return pl.pallas_call(
        _bias_tile_kernel,
        grid=(DIFF_HEADS, 2),
        in_specs=[pl.BlockSpec(memory_space=pltpu.SMEM)],
        out_specs=pl.BlockSpec((1, 1, t, t), lambda h, d: (h, d, 0, 0)),
        out_shape=jax.ShapeDtypeStruct((DIFF_HEADS, 2, t, t), F32),
        compiler_params=_params("arbitrary", "arbitrary"),
        name="bias_tiles",
    )(rel_bias)


def _pre_attn_kernel(x_ref, mod_ref, g_ref, w_in_ref, gq_ref, w_uq_ref, gkv_ref, w_uk_ref, w_uv_ref,
                     rc_ref, rs_ref,
                     dq_ref, dk_ref, dv_ref, mq_ref, mk_ref, mv_ref):
    tm = x_ref.shape[1]
    for r0 in range(0, tm, tm // PRE_ROW_GROUPS):
        rows = slice(r0, r0 + tm // PRE_ROW_GROUPS)
        x = x_ref[0, rows, :]
        sh, sc = mod_ref[0, 0:1, :], mod_ref[0, 1:2, :]
        h = (_rms(x, g_ref[...]) * (1.0 + sc) + sh).astype(BF16)
        proj = jnp.dot(h, w_in_ref[...], preferred_element_type=F32)
        dq_ref[0, rows, :] = (proj[:, C_DQ:C_DK] * (DIFF_QK_DIM ** -0.5 * LOG2E)).astype(BF16)
        dk_ref[0, rows, :] = proj[:, C_DK:C_DV].astype(BF16)
        dv_ref[0, rows, :] = proj[:, C_DV:C_MQ].astype(BF16)

        rc, rs = rc_ref[0, rows, :], rs_ref[0, rows, :]

        def rope(v):
            return v * rc + pltpu.roll(v, LANES // 2, 1) * rs

        qn = _rms(proj[:, C_MQ:C_KV], gq_ref[...]).astype(BF16)
        q = jnp.dot(qn, w_uq_ref[...], preferred_element_type=F32)
        kvn = _rms(proj[:, C_KV:C_KR], gkv_ref[...]).astype(BF16)
        kn = jnp.dot(kvn, w_uk_ref[...], preferred_element_type=F32)
        mv_ref[0, rows, :] = jnp.dot(kvn, w_uv_ref[...], preferred_element_type=F32).astype(BF16)
        kr = rope(proj[:, C_KR:C_END])
        q_scale = (MLA_NOPE_DIM + MLA_ROPE_DIM) ** -0.5 * LOG2E
        for hd in range(MLA_HEADS):
            sl = slice(hd * LANES, (hd + 1) * LANES)
            mq_ref[0, rows, sl] = (rope(q[:, sl]) * q_scale).astype(BF16)
            mk_ref[0, rows, sl] = (kn[:, sl] + kr).astype(BF16)


def _pre_attn(x, mod, g, w_in, gq, w_uq, gkv, w_uk, w_uv, rc, rs):
    bsz, s, d = x.shape
    tm = TOK_TILE
    tok = lambda w: pl.BlockSpec((1, tm, w), lambda b, i: (b, i, 0))
    full = lambda a: pl.BlockSpec(a.shape, lambda b, i: (0,) * a.ndim)
    widths = (512, 512, 512, MLA_HEADS * LANES, MLA_HEADS * LANES, MLA_HEADS * MLA_V_DIM)
    return pl.pallas_call(
        _pre_attn_kernel,
        grid=(bsz, s // tm),
        in_specs=[tok(d), pl.BlockSpec((1, ADA_CHUNKS, d), lambda b, i: (b, 0, 0)), full(g), full(w_in),
                  full(gq), full(w_uq), full(gkv), full(w_uk), full(w_uv), tok(LANES), tok(LANES)],
        out_specs=[tok(w) for w in widths],
        out_shape=[jax.ShapeDtypeStruct((bsz, s, w), BF16) for w in widths],
        compiler_params=_params("arbitrary", "arbitrary"),
        name="pre_attn",
    )(x, mod, g, w_in, gq, w_uq, gkv, w_uk, w_uv, rc, rs)


ONES_ROWS = BF16_SUBLANES


def _values_t(v, dv):
    vt = v.T
    ones = jnp.ones((ONES_ROWS, vt.shape[1]), vt.dtype)
    parts = []
    for h in range(vt.shape[0] // dv):
        parts += [vt[h * dv:(h + 1) * dv], ones]
    return jnp.concatenate(parts, axis=0)


def _softmax_init(m_ref, acc_ref):
    m_ref[...] = jnp.full(m_ref.shape, NEG_BIG, F32)
    acc_ref[...] = jnp.zeros(acc_ref.shape, F32)


def _softmax_probs(st, m_ref):
    m_prev = m_ref[...]
    m_new = jnp.maximum(m_prev, jnp.max(st, axis=0, keepdims=True))
    m_ref[...] = m_new
    return jnp.exp2(st - m_new).astype(BF16), jnp.exp2(m_prev - m_new)


def _acc_update(acc_ref, alpha, vt, p):
    acc_ref[...] = alpha * acc_ref[...] + jnp.dot(vt, p, preferred_element_type=F32)


def _normalized(acc_ref, dv):
    return acc_ref[:dv] / acc_ref[dv:dv + 1]


def _attn_kernel(*refs, cast_weights):
    (lam_init_ref, lam_ref, g_ref, bias_ref, dq_ref, dk_ref, dv_ref, mq_ref, mk_ref, mv_ref), refs = refs[:10], refs[10:]
    if cast_weights:
        for w_ref, wb_ref in zip(refs[:3], refs[4:7]):
            wb_ref[...] = w_ref[...].astype(BF16)
        refs = refs[3:4] + refs[7:]
    o_ref, dvt_ref, mvt_ref, m_ref, dacc_ref, macc_ref, pend_ref = refs
    i = pl.program_id(1)
    t = dq_ref.shape[1]
    n_diff = 2 * DIFF_HEADS
    n_chains = n_diff + MLA_HEADS
    d_rows, m_rows = DIFF_V_DIM + ONES_ROWS, MLA_V_DIM + ONES_ROWS

    @pl.when(i == 0)
    def _():
        dvt_ref[...] = _values_t(dv_ref[0], DIFF_V_DIM)
        mvt_ref[...] = _values_t(mv_ref[0], MLA_V_DIM)

    qts = []
    for h in range(DIFF_HEADS):
        qt = dq_ref[0, :, h * LANES:(h + 1) * LANES].T
        row = lax.broadcasted_iota(jnp.int32, qt.shape, 0)
        qts.append(jnp.where(row < DIFF_QK_DIM, qt, jnp.zeros_like(qt)))
        qts.append(jnp.where(row >= DIFF_QK_DIM, qt, jnp.zeros_like(qt)))
    qts += [mq_ref[0, :, h * LANES:(h + 1) * LANES].T for h in range(MLA_HEADS)]

    def scores(j, c):
        rows = pl.ds(pl.multiple_of(j * t, t), t)
        if c < n_diff:
            k = dk_ref[0, rows, (c // 2) * LANES:(c // 2 + 1) * LANES]
        else:
            k = mk_ref[0, rows, (c - n_diff) * LANES:(c - n_diff + 1) * LANES]
        return jnp.dot(k, qts[c], preferred_element_type=F32)

    def values(j, c):
        cols = pl.ds(pl.multiple_of(j * t, t), t)
        if c < n_diff:
            return dvt_ref[(c // 2) * d_rows:(c // 2 + 1) * d_rows, cols]
        return mvt_ref[(c - n_diff) * m_rows:(c - n_diff + 1) * m_rows, cols]

    def acc_at(c):
        return dacc_ref.at[c] if c < n_diff else macc_ref.at[c - n_diff]

    def step(j, bias_idx, has_next):
        hidden = None
        if bias_idx == 0:
            key = lax.broadcasted_iota(jnp.int32, (t, t), 0)
            qry = lax.broadcasted_iota(jnp.int32, (t, t), 1)
            hidden = jnp.where(key <= qry, 0.0, -NEG_BIG)

        def adjust(c, st):
            if c < n_diff:
                return st if bias_idx is None else st - bias_ref[c // 2, bias_idx]
            return st if hidden is None else st - hidden

        pending = [pend_ref[c] for c in range(AHEAD)]
        for c in range(n_chains):
            nxt = c + AHEAD
            if nxt < n_chains:
                pending.append(scores(j, nxt))
            elif has_next:
                pend_ref[nxt - n_chains] = scores(j + 1, nxt - n_chains)
            p, alpha = _softmax_probs(adjust(c, pending.pop(0)), m_ref.at[c])
            _acc_update(acc_at(c), alpha, values(j, c), p)

    def far(j, carry):
        step(j, None, True)
        return carry

    for c in range(n_chains):
        _softmax_init(m_ref.at[c], acc_at(c))
    for c in range(AHEAD):
        pend_ref[c] = scores(0, c)
    lax.fori_loop(0, jnp.maximum(i - 1, 0), far, 0)

    @pl.when(i >= 1)
    def _():
        step(i - 1, 1, True)

    step(i, 0, False)

    lv = lam_ref[...]
    lam = (jnp.exp(jnp.sum(lv[0:1] * lv[1:2], keepdims=True)) - jnp.exp(jnp.sum(lv[2:3] * lv[3:4], keepdims=True))
           + lam_init_ref[0])
    for h in range(DIFF_HEADS):
        c0, c1 = 2 * h, 2 * h + 1
        ot = (_normalized(dacc_ref.at[c0], DIFF_V_DIM)
              - lam * _normalized(dacc_ref.at[c1], DIFF_V_DIM))
        ot = ot * lax.rsqrt(jnp.mean(ot * ot, axis=0, keepdims=True) + NORM_EPS) * g_ref[...]
        o_ref[0, :, h * LANES:(h + 1) * LANES] = (ot * (1.0 - lam_init_ref[0])).T.astype(BF16)
    base = DIFF_HEADS * LANES
    for u in range(MLA_HEADS // 2):
        ot = jnp.concatenate([_normalized(macc_ref.at[2 * u], MLA_V_DIM),
                              _normalized(macc_ref.at[2 * u + 1], MLA_V_DIM)], axis=0)
        o_ref[0, :, base + u * LANES:base + (u + 1) * LANES] = ot.T.astype(BF16)


def _attention(lam_init, diff_lambda, g, bias, dq, dk, dv, mq, mk, mv, expert_w=None, expert_layer=0):
    bsz, s, wd = dq.shape
    wq, wv = mq.shape[-1], mv.shape[-1]
    t = ATTN_TILE
    assert MAX_DISTANCE <= t and s % t == 0
    n_diff = 2 * DIFF_HEADS
    n_chains = n_diff + MLA_HEADS
    per_b = s // t
    qtile = lambda w: pl.BlockSpec((1, t, w), lambda b, i: (b, i, 0))
    whole = lambda w: pl.BlockSpec((1, s, w), lambda b, i: (b, 0, 0))
    args = [lam_init, diff_lambda, g, bias, dq, dk, dv, mq, mk, mv]
    in_specs = [pl.BlockSpec(memory_space=pltpu.SMEM),
                pl.BlockSpec(diff_lambda.shape, lambda b, i: (0, 0)),
                pl.BlockSpec(g.shape, lambda b, i: (0, 0)),
                pl.BlockSpec(bias.shape, lambda b, i: (0, 0, 0, 0)),
                qtile(wd), whole(wd), whole(wd), qtile(wq), whole(wq), whole(wv)]
    out_specs = [qtile(wd + wv)]
    out_shape = [jax.ShapeDtypeStruct((bsz, s, wd + wv), BF16)]
    if expert_w is not None:
        steps = bsz * per_b
        n_lay, n_e, d, d_ff = expert_w[0].shape
        for w in expert_w:
            rows, cols = n_e * w.shape[2], w.shape[3]
            assert rows % (BF16_SUBLANES * steps) == 0
            slab = rows // steps
            args.append(w.reshape(n_lay * rows, cols))
            in_specs.append(pl.BlockSpec((slab, cols), lambda b, i: (expert_layer * steps + b * per_b + i, 0)))
            out_specs.append(pl.BlockSpec((slab, cols), lambda b, i: (b * per_b + i, 0)))
            out_shape.append(jax.ShapeDtypeStruct((rows, cols), BF16))
    outs = pl.pallas_call(
        functools.partial(_attn_kernel, cast_weights=expert_w is not None),
        grid=(bsz, per_b),
        in_specs=in_specs,
        out_specs=out_specs,
        out_shape=out_shape,
        scratch_shapes=[pltpu.VMEM((DIFF_HEADS * (DIFF_V_DIM + ONES_ROWS), s), BF16),
                        pltpu.VMEM((MLA_HEADS * (MLA_V_DIM + ONES_ROWS), s), BF16),
                        pltpu.VMEM((n_chains, 1, t), F32),
                        pltpu.VMEM((n_diff, DIFF_V_DIM + ONES_ROWS, t), F32),
                        pltpu.VMEM((MLA_HEADS, MLA_V_DIM + ONES_ROWS, t), F32),
                        pltpu.VMEM((AHEAD, t, t), F32)],
        compiler_params=_params("arbitrary", "arbitrary"),
        name="attention",
    )(*args)
    if expert_w is None:
        return outs[0]
    return outs[0], tuple(wb.reshape(w.shape[1:]) for wb, w in zip(outs[1:], expert_w))


def _post_attn_kernel(*refs, moe):
    if moe:
        o_ref, x_ref, mod_ref, g_ref, wo_ref, wr_ref, x1_ref, h_ref, gates_ref, sel_ref, cnt_ref = refs

        @pl.when((pl.program_id(0) == 0) & (pl.program_id(1) == 0))
        def _():
            cnt_ref[...] = jnp.zeros(cnt_ref.shape, F32)
    else:
        o_ref, x_ref, mod_ref, g_ref, wo_ref, x1_ref, h_ref = refs
    gt_a, sh_f, sc_f = mod_ref[0, 2:3, :], mod_ref[0, 3:4, :], mod_ref[0, 4:5, :]
    tm = x_ref.shape[1]
    n = tm // POST_ROW_GROUPS
    for r0 in range(0, tm, n):
        rows = slice(r0, r0 + n)
        y = jnp.dot(o_ref[0, rows, :], wo_ref[...], preferred_element_type=F32)
        x1 = x_ref[0, rows, :] + (1.0 + gt_a) * y
        x1_ref[0, rows, :] = x1
        h = _rms(x1, g_ref[...]) * (1.0 + sc_f) + sh_f
        if not moe:
            h_ref[0, rows, :] = h.astype(BF16)
            continue
        _rows_to_tiles(h, h_ref.at[0, pl.ds(r0 * ROW_SUB, n * ROW_SUB)])
        wr = wr_ref[...]
        h_hi, wr_hi = h.astype(BF16), wr.astype(BF16)
        h_lo, wr_lo = (h - h_hi.astype(F32)).astype(BF16), (wr - wr_hi.astype(F32)).astype(BF16)
        logits = (jnp.dot(h_hi, wr_hi, preferred_element_type=F32)
                  + jnp.dot(h_hi, wr_lo, preferred_element_type=F32)
                  + jnp.dot(h_lo, wr_hi, preferred_element_type=F32))
        lane = lax.broadcasted_iota(jnp.int32, logits.shape, 1)
        logits = jnp.where(lane < N_EXPERTS, logits, -jnp.inf)
        v1 = jnp.max(logits, axis=1, keepdims=True)
        i1 = jnp.min(jnp.where(logits == v1, lane, LANES), axis=1, keepdims=True)
        rest = jnp.where(lane == i1, -jnp.inf, logits)
        v2 = jnp.max(rest, axis=1, keepdims=True)
        i2 = jnp.min(jnp.where(rest == v2, lane, LANES), axis=1, keepdims=True)
        e2 = jnp.exp(v2 - v1)
        w1 = 1.0 / (1.0 + e2)
        w2 = e2 / (1.0 + e2)
        gates_ref[0, rows, :] = jnp.where(lane == i1, w1, 0.0) + jnp.where(lane == i2, w2, 0.0)
        sel_ref[0, rows, :] = jnp.where(lane == i1, 1.0, 0.0) + jnp.where(lane == i2, 2.0, 0.0)
        chosen = jnp.where((lane == i1) | (lane == i2), 1.0, 0.0)
        cnt_ref[...] += jnp.sum(chosen, axis=0, keepdims=True)


def _post_attn(o, x, mod, g, wo, w_router=None):
    bsz, s, d = x.shape
    moe = w_router is not None
    tm = TOK_TILE
    tok = lambda w: pl.BlockSpec((1, tm, w), lambda b, i: (b, i, 0))
    full = lambda a: pl.BlockSpec(a.shape, lambda b, i: (0,) * a.ndim)
    args = [o, x, mod, g, wo] + ([w_router] if moe else [])
    in_specs = [tok(o.shape[-1]), tok(d), pl.BlockSpec((1, ADA_CHUNKS, d), lambda b, i: (b, 0, 0)),
                full(g), full(wo)] + ([full(w_router)] if moe else [])
    if moe:
        assert d == ROW_SUB * LANES
        h_spec = pl.BlockSpec((1, tm * ROW_SUB, LANES), lambda b, i: (b, i, 0))
        out_specs = [tok(d), h_spec, tok(LANES), tok(LANES), pl.BlockSpec((8, LANES), lambda b, i: (0, 0))]
        out_shape = [jax.ShapeDtypeStruct((bsz, s, d), F32), jax.ShapeDtypeStruct((bsz, s * ROW_SUB, LANES), F32)]
        out_shape += [jax.ShapeDtypeStruct((bsz, s, LANES), F32)] * 2 + [jax.ShapeDtypeStruct((8, LANES), F32)]
    else:
        out_specs = [tok(d), tok(d)]
        out_shape = [jax.ShapeDtypeStruct((bsz, s, d), F32), jax.ShapeDtypeStruct((bsz, s, d), BF16)]
    return pl.pallas_call(
        functools.partial(_post_attn_kernel, moe=moe),
        grid=(bsz, s // tm),
        in_specs=in_specs,
        out_specs=out_specs,
        out_shape=out_shape,
        compiler_params=_params("arbitrary", "arbitrary"),
        name="post_attn_moe" if moe else "post_attn",
    )(*args)


def _swiglu(h, w1_ref, w3_ref, w2_ref):
    d_ff = w1_ref.shape[-1]
    y = jnp.zeros((h.shape[0], w2_ref.shape[-1]), F32)
    for c0 in range(0, d_ff, FF_CHUNK):
        a = jnp.dot(h, w1_ref[:, c0:c0 + FF_CHUNK], preferred_element_type=F32)
        b = jnp.dot(h, w3_ref[:, c0:c0 + FF_CHUNK], preferred_element_type=F32)
        u = (a * jax.nn.sigmoid(a) * b).astype(BF16)
        y = y + jnp.dot(u, w2_ref[c0:c0 + FF_CHUNK, :], preferred_element_type=F32)
    return y


def _ffn_dense_kernel(h_ref, x1_ref, mod_ref, w1_ref, w3_ref, w2_ref, o_ref):
    gt_f = mod_ref[0, 5:6, :]
    o_ref[0] = x1_ref[0] + (1.0 + gt_f) * _swiglu(h_ref[0], w1_ref, w3_ref, w2_ref)


def _ffn_dense(h, x1, mod, w1, w3, w2, layer):
    bsz, s, d = x1.shape
    tm = FFN_TILE
    tok = lambda w: pl.BlockSpec((1, tm, w), lambda b, i: (b, i, 0))
    one = lambda a: pl.BlockSpec((None,) + a.shape[1:], lambda b, i: (layer, 0, 0))
    return pl.pallas_call(
        _ffn_dense_kernel,
        grid=(bsz, s // tm),
        in_specs=[tok(d), tok(d), pl.BlockSpec((1, ADA_CHUNKS, d), lambda b, i: (b, 0, 0)),
                  one(w1), one(w3), one(w2)],
        out_specs=tok(d),
        out_shape=jax.ShapeDtypeStruct(x1.shape, F32),
        compiler_params=_params("arbitrary", "arbitrary"),
        name="ffn_dense",
    )(h, x1, mod, w1, w3, w2)


def _route_kernel(cnt_ref, sel_ref, pos_ref, te_ref, off_ref, run_ref, *, row_tile):
    t = pl.program_id(0)
    sel_t = sel_ref[...].T
    chosen = (sel_t > 0.0).astype(F32)
    per_expert = jnp.sum(chosen, axis=1, keepdims=True)

    @pl.when(t == 0)
    def _():
        cnt = cnt_ref[...].T[:, 0:1]
        padded = jnp.ceil(cnt / row_tile) * row_tile
        row = lax.broadcasted_iota(jnp.int32, cnt.shape, 0)
        off = jnp.zeros(cnt.shape, F32)
        for e in range(N_EXPERTS):
            size_e = jnp.sum(jnp.where(row == e, padded, 0.0), keepdims=True)
            off = off + jnp.where(row > e, size_e, 0.0)
        off_ref[...] = off
        run_ref[...] = jnp.zeros(run_ref.shape, F32)
        ends = off + padded
        tile_start = lax.broadcasted_iota(jnp.int32, (LANES, LANES), 1).astype(F32) * row_tile
        erow = lax.broadcasted_iota(jnp.int32, (LANES, LANES), 0)
        done = jnp.where((erow < N_EXPERTS) & (ends <= tile_start), 1.0, 0.0)
        te = jnp.sum(done, axis=0, keepdims=True).astype(jnp.int32)
        te_ref[...] = jnp.broadcast_to(te, te_ref.shape)

    tm = sel_t.shape[1]
    before = (lax.broadcasted_iota(jnp.int32, (tm, tm), 0)
              < lax.broadcasted_iota(jnp.int32, (tm, tm), 1)).astype(BF16)
    rank = jnp.dot(chosen.astype(BF16), before, preferred_element_type=F32) + run_ref[...]
    base = off_ref[...] + rank
    for k in range(2):
        pos = jnp.sum(jnp.where(sel_t == float(k + 1), base, 0.0), axis=0, keepdims=True)
        pos_ref[0, k:k + 1, :] = pos.astype(jnp.int32)
    run_ref[...] += per_expert


def _route(cnt, sel, row_tile, n_row_tiles):
    n_tok = sel.shape[0]
    assert n_row_tiles <= LANES and 2 * n_tok < 2 ** 24
    tm = TOK_TILE
    nt = n_tok // tm
    pos, te = pl.pallas_call(
        functools.partial(_route_kernel, row_tile=row_tile),
        grid=(nt,),
        in_specs=[pl.BlockSpec(cnt.shape, lambda t: (0, 0)), pl.BlockSpec((tm, LANES), lambda t: (t, 0))],
        out_specs=[pl.BlockSpec((1, 2, tm), lambda t: (t, 0, 0)), pl.BlockSpec((8, LANES), lambda t: (0, 0))],
        out_shape=[jax.ShapeDtypeStruct((nt, 2, tm), jnp.int32), jax.ShapeDtypeStruct((8, LANES), jnp.int32)],
        scratch_shapes=[pltpu.VMEM((LANES, 1), F32)] * 2,
        compiler_params=_params("arbitrary"),
        name="moe_route",
    )(cnt, sel)
    return pos, te[0]


def _dispatch_kernel(pos_ref, h_ref, xs_in_ref, xs_ref, sem):
    del xs_in_ref
    tm = h_ref.shape[0] // ROW_SUB

    def body(r, carry):
        for k in range(2):
            pltpu.make_async_copy(_tile_row(h_ref, r), _tile_row(xs_ref, pos_ref[0, k, r]), sem).start(priority=k)
        return carry

    lax.fori_loop(0, tm, body, 0, unroll=8)
    for k in range(2):
        pltpu.make_async_copy(h_ref, xs_ref.at[pl.ds(0, tm * ROW_SUB)], sem).wait()


def _dispatch(pos, h, xs0):
    nt, _, tm = pos.shape
    return pl.pallas_call(
        _dispatch_kernel,
        grid=(nt,),
        in_specs=[pl.BlockSpec((1, 2, tm), lambda t: (t, 0, 0), memory_space=pltpu.SMEM),
                  pl.BlockSpec((tm * ROW_SUB, LANES), lambda t: (t, 0)),
                  pl.BlockSpec(memory_space=pl.ANY)],
        out_specs=pl.BlockSpec(memory_space=pl.ANY),
        out_shape=jax.ShapeDtypeStruct(xs0.shape, xs0.dtype),
        scratch_shapes=[pltpu.SemaphoreType.DMA(())],
        input_output_aliases={2: 0},
        compiler_params=_params("arbitrary"),
        name="moe_dispatch",
    )(pos, h, xs0)


def _expert_kernel(te_ref, xs_ref, w1_ref, w3_ref, w2_ref, y_ref):
    used = te_ref[pl.program_id(0)] < N_EXPERTS

    @pl.when(used)
    def _():
        y = _swiglu(_tiles_to_rows(xs_ref).astype(BF16), w1_ref.at[0], w3_ref.at[0], w2_ref.at[0])
        _rows_to_tiles(y, y_ref)

    @pl.when(jnp.logical_not(used))
    def _():
        y_ref[...] = jnp.zeros(y_ref.shape, y_ref.dtype)


def _experts(te, xs, w1, w3, w2, row_tile):
    n_rows = xs.shape[0] // ROW_SUB
    _, d, d_ff = w1.shape
    expert = lambda n, te: (jnp.minimum(te[n], N_EXPERTS - 1), 0, 0)
    rows = pl.BlockSpec((row_tile * ROW_SUB, LANES), lambda n, te: (n, 0))
    return pl.pallas_call(
        _expert_kernel,
        grid_spec=pltpu.PrefetchScalarGridSpec(
            num_scalar_prefetch=1,
            grid=(n_rows // row_tile,),
            in_specs=[rows, pl.BlockSpec((1, d, d_ff), expert), pl.BlockSpec((1, d, d_ff), expert),
                      pl.BlockSpec((1, d_ff, d), expert)],
            out_specs=rows),
        out_shape=jax.ShapeDtypeStruct(xs.shape, F32),
        compiler_params=_params("arbitrary"),
        name="moe_experts",
    )(te, xs, w1, w3, w2)


def _combine_kernel(pos_ref, pos_next_ref, x1_ref, gates_ref, sel_ref, mod_ref, gfin_ref, ys_ref, o_ref,
                    ybuf_ref, sem, *, final):
    tm = x1_ref.shape[1]
    step = pl.program_id(0) * pl.num_programs(1) + pl.program_id(1)
    n_steps = pl.num_programs(0) * pl.num_programs(1)
    slot = lax.rem(step, 2)

    def gather(p_ref, into):
        def body(r, carry):
            for k in range(2):
                pltpu.make_async_copy(_tile_row(ys_ref, p_ref[0, k, r]), _tile_row(ybuf_ref.at[into, k], r),
                                      sem.at[into]).start(priority=k)
            return carry

        lax.fori_loop(0, tm, body, 0, unroll=8)

    @pl.when(step == 0)
    def _():
        gather(pos_ref, slot)

    @pl.when(step + 1 < n_steps)
    def _():
        gather(pos_next_ref, 1 - slot)

    gates, sel = gates_ref[0], sel_ref[0]
    w_a = jnp.sum(jnp.where(sel == 1.0, gates, 0.0), axis=1, keepdims=True)
    w_b = jnp.sum(jnp.where(sel == 2.0, gates, 0.0), axis=1, keepdims=True)
    for k in range(2):
        pltpu.make_async_copy(ys_ref.at[pl.ds(0, tm * ROW_SUB)], ybuf_ref.at[slot, k], sem.at[slot]).wait()
    gt_f = mod_ref[0, 5:6, :]
    y = w_a * _tiles_to_rows(ybuf_ref.at[slot, 0]) + w_b * _tiles_to_rows(ybuf_ref.at[slot, 1])
    x = x1_ref[0] + (1.0 + gt_f) * y
    o_ref[0] = _rms(x, gfin_ref[...]) if final else x


def _combine(pos, x1, gates, sel, mod, ys, g_final=None):
    bsz, s, d = x1.shape
    nt, _, tm = pos.shape
    per_b = s // tm
    tok = lambda w: pl.BlockSpec((1, tm, w), lambda b, i: (b, i, 0))
    final = g_final is not None
    gfin = g_final if final else jnp.ones((1, d), F32)
    return pl.pallas_call(
        functools.partial(_combine_kernel, final=final),
        grid=(bsz, per_b),
        in_specs=[pl.BlockSpec((1, 2, tm), lambda b, i: (b * per_b + i, 0, 0), memory_space=pltpu.SMEM),
                  pl.BlockSpec((1, 2, tm), lambda b, i: (jnp.minimum(b * per_b + i + 1, nt - 1), 0, 0),
                               memory_space=pltpu.SMEM),
                  tok(d), tok(LANES), tok(LANES),
                  pl.BlockSpec((1, ADA_CHUNKS, d), lambda b, i: (b, 0, 0)),
                  pl.BlockSpec((1, d), lambda b, i: (0, 0)),
                  pl.BlockSpec(memory_space=pl.ANY)],
        out_specs=tok(d),
        out_shape=jax.ShapeDtypeStruct(x1.shape, F32),
        scratch_shapes=[pltpu.VMEM((2, 2, tm * ROW_SUB, LANES), F32), pltpu.SemaphoreType.DMA((2,))],
        compiler_params=_params("arbitrary", "arbitrary"),
        name="moe_combine",
    )(pos, pos, x1, gates, sel, mod, gfin, ys)


def _moe(h, x1, gates, sel, cnt, mod, w1, w3, w2, xs_buf, g_final):
    bsz, s, d = x1.shape
    n_tok = bsz * s
    row_tile = FFN_TILE
    n_rows = xs_buf.shape[0] // ROW_SUB
    assert n_rows == _moe_rows(n_tok)
    pos, te = _route(cnt, sel.reshape(n_tok, LANES), row_tile, n_rows // row_tile)
    xs = _dispatch(pos, h.reshape(n_tok * ROW_SUB, LANES), xs_buf)
    ys = _experts(te, xs, w1, w3, w2, row_tile)
    return _combine(pos, x1, gates, sel, mod, ys, g_final), xs


def _moe_rows(n_tok):
    return 2 * n_tok + N_EXPERTS * FFN_TILE


def _final_norm_kernel(x_ref, g_ref, o_ref):
    o_ref[0] = _rms(x_ref[0], g_ref[...])


def _final_norm(x, g):
    bsz, s, d = x.shape
    tm = TOK_TILE
    tok = pl.BlockSpec((1, tm, d), lambda b, i: (b, i, 0))
    return pl.pallas_call(
        _final_norm_kernel,
        grid=(bsz, s // tm),
        in_specs=[tok, pl.BlockSpec(g.shape, lambda b, i: (0, 0))],
        out_specs=tok,
        out_shape=jax.ShapeDtypeStruct(x.shape, F32),
        compiler_params=_params("arbitrary", "arbitrary"),
        name="final_norm",
    )(x, g)


def _head_blocks(w, width, nope_cols, rope_cols):
    k = w.shape[0]
    w = w.reshape(k, MLA_HEADS, width)
    nope = w[:, :, nope_cols] if nope_cols is not None else jnp.zeros((k, MLA_HEADS, MLA_NOPE_DIM), w.dtype)
    rope = w[:, :, rope_cols] if rope_cols is not None else jnp.zeros((k, MLA_HEADS, MLA_ROPE_DIM), w.dtype)
    return _head_block(nope, rope).reshape(k, MLA_HEADS * LANES)


def _prep_w_in(w_in):
    d = w_in.shape[0]
    kr_block = _head_block(jnp.zeros((d, MLA_NOPE_DIM), w_in.dtype), w_in[:, C_KR:])
    return jnp.concatenate([w_in[:, :C_KR], kr_block], axis=1).astype(BF16)


def kernel(x, c, positions, w_ada, b_ada, g_attn, w_in, diff_lambda, diff_subln_g, rel_bias, mla_q_norm, w_uq, mla_kv_norm, w_ukv, w_o, g_ffn, ffn_w1, ffn_w3, ffn_w2, moe_router, moe_w1, moe_w3, moe_w2, g_final):
    depth = w_ada.shape[0]
    bsz, s, d = x.shape
    mods = _ada(c, w_ada, b_ada).reshape(depth, bsz, ADA_CHUNKS, d)
    rc, rs = _rope_tables(positions)
    bias = _bias_tiles(rel_bias)
    qk_w = MLA_NOPE_DIM + MLA_ROPE_DIM
    kv_w = MLA_NOPE_DIM + MLA_V_DIM
    xs_buf = jnp.zeros((_moe_rows(bsz * s) * ROW_SUB, LANES), F32)
    ffn_w = (ffn_w1.astype(BF16), ffn_w3.astype(BF16), ffn_w2.astype(BF16))
    for l in range(depth):
        mod = mods[l]
        lam_init = jnp.full((1,), 0.8 - 0.6 * math.exp(-0.3 * l), F32)
        w_uq_p = _head_blocks(w_uq[l], qk_w, slice(0, MLA_NOPE_DIM), slice(MLA_NOPE_DIM, qk_w)).astype(BF16)
        w_uk_p = _head_blocks(w_ukv[l], kv_w, slice(0, MLA_NOPE_DIM), None).astype(BF16)
        w_uv_p = w_ukv[l].reshape(MLA_KV_RANK, MLA_HEADS, kv_w)[:, :, MLA_NOPE_DIM:].reshape(
            MLA_KV_RANK, MLA_HEADS * MLA_V_DIM).astype(BF16)
        dq, dk, dv, mq, mk, mv = _pre_attn(
            x, mod, g_attn[l].reshape(1, d), _prep_w_in(w_in[l]), mla_q_norm[l].reshape(1, -1), w_uq_p,
            mla_kv_norm[l].reshape(1, -1), w_uk_p, w_uv_p, rc, rs)
        attn_args = (lam_init, diff_lambda[l], diff_subln_g[l].reshape(-1, 1), bias, dq, dk, dv, mq, mk, mv)
        wo = w_o[l].astype(BF16)
        g_f = g_ffn[l].reshape(1, d)
        if l % 2 == 1:
            o, expert_w = _attention(*attn_args, expert_w=(moe_w1, moe_w3, moe_w2), expert_layer=l // 2)
            w_router = jnp.pad(moe_router[l // 2], ((0, 0), (0, LANES - N_EXPERTS)))
            x1, h, gates, sel, cnt = _post_attn(o, x, mod, g_f, wo, w_router)
            g_fin = g_final.reshape(1, d) if l == depth - 1 else None
            x, xs_buf = _moe(h, x1, gates, sel, cnt, mod, *expert_w, xs_buf, g_fin)
        else:
            o = _attention(*attn_args)
            x1, h = _post_attn(o, x, mod, g_f, wo)
            x = _ffn_dense(h, x1, mod, *ffn_w, l // 2)
    return x if depth % 2 == 0 else _final_norm(x, g_final.reshape(1, d))
```

```python
import functools
import math

import jax
import jax.numpy as jnp
from jax import lax
from jax.experimental import pallas as pl
from jax.experimental.pallas import tpu as pltpu

F32 = jnp.float32
BF16 = jnp.bfloat16

DIFF_HEADS = 4
DIFF_QK_DIM = 64
DIFF_V_DIM = 128
MLA_HEADS = 8
MLA_NOPE_DIM = 64
MLA_ROPE_DIM = 32
MLA_V_DIM = 64
MLA_Q_RANK = 384
MLA_KV_RANK = 256
ROPE_THETA = 10000.0
N_BUCKETS = 32
MAX_EXACT = 16
MAX_DISTANCE = 128
N_EXPERTS = 8
NORM_EPS = 1e-6
ADA_CHUNKS = 6

LANES = 128
BF16_SUBLANES = 16
NEG_BIG = -1e30
LOG2E = math.log2(math.e)
VMEM_LIMIT = 56 * 1024 * 1024

ATTN_TILE = 256
AHEAD = 4
TOK_TILE = 512
PRE_ROW_GROUPS = 2
POST_ROW_GROUPS = 2
FFN_TILE = 512
FF_CHUNK = 256
ADA_COL_TILE = 1536

C_DQ, C_DK, C_DV, C_MQ, C_KV, C_KR, C_END = 0, 512, 1024, 1536, 1920, 2176, 2304


def _params(*sem):
    return pltpu.CompilerParams(dimension_semantics=sem, vmem_limit_bytes=VMEM_LIMIT)


def _rms(x, g):
    return x * lax.rsqrt(jnp.mean(x * x, axis=-1, keepdims=True) + NORM_EPS) * g


ROW_SUB = 8


def _rows_to_tiles(x, ref):
    n = x.shape[0]
    for j in range(ROW_SUB):
        ref[pl.ds(j, n, stride=ROW_SUB), :] = x[:, j * LANES:(j + 1) * LANES]


def _tiles_to_rows(ref):
    n = ref.shape[0] // ROW_SUB
    return jnp.concatenate([ref[pl.ds(j, n, stride=ROW_SUB), :] for j in range(ROW_SUB)], axis=1)


def _tile_row(ref, r):
    return ref.at[pl.ds(pl.multiple_of(r * ROW_SUB, ROW_SUB), ROW_SUB)]


def _ada_kernel(c_ref, w_ref, b_ref, o_ref):
    c = c_ref[...]
    cond = c * jax.nn.sigmoid(c)
    o_ref[0] = jnp.dot(cond, w_ref[0], preferred_element_type=F32,
                       precision=lax.Precision.HIGHEST) + b_ref[0]


def _ada(c, w_ada, b_ada):
    depth, d, n = w_ada.shape
    bsz = c.shape[0]
    tn = ADA_COL_TILE
    return pl.pallas_call(
        _ada_kernel,
        grid=(depth, n // tn),
        in_specs=[pl.BlockSpec((bsz, d), lambda l, j: (0, 0)),
                  pl.BlockSpec((1, d, tn), lambda l, j: (l, 0, j)),
                  pl.BlockSpec((1, 1, tn), lambda l, j: (l, 0, j))],
        out_specs=pl.BlockSpec((1, bsz, tn), lambda l, j: (l, 0, j)),
        out_shape=jax.ShapeDtypeStruct((depth, bsz, n), F32),
        compiler_params=_params("arbitrary", "arbitrary"),
        name="ada_mod",
    )(c, w_ada, b_ada.reshape(depth, 1, n))


ROPE_HALF = MLA_ROPE_DIM // 2
NOPE_SPLIT = LANES // 2 - ROPE_HALF


def _head_block(nope, rope):
    pad = jnp.zeros(nope.shape[:-1] + (LANES - MLA_NOPE_DIM - MLA_ROPE_DIM,), nope.dtype)
    return jnp.concatenate([rope[..., :ROPE_HALF], nope[..., :NOPE_SPLIT], rope[..., ROPE_HALF:],
                            nope[..., NOPE_SPLIT:], pad], axis=-1)


def _rope_tab_kernel(pos_ref, inv_ref, c_ref, s_ref):
    pos = pos_ref[0].astype(F32)
    inv = inv_ref[...]
    ang = pos * inv
    lane = lax.broadcasted_iota(jnp.int32, ang.shape, 1)
    is_rope = inv != 0.0
    used = lane < MLA_NOPE_DIM + MLA_ROPE_DIM
    c_ref[0] = jnp.where(is_rope, jnp.cos(ang), jnp.where(used, 1.0, 0.0))
    s_ref[0] = jnp.where(is_rope, jnp.where(lane < LANES // 2, -1.0, 1.0) * jnp.sin(ang), 0.0)


def _rope_tables(positions):
    bsz, s = positions.shape
    inv_freq = ROPE_THETA ** (-jnp.arange(ROPE_HALF, dtype=F32) / ROPE_HALF)
    inv_lane = _head_block(jnp.zeros((MLA_NOPE_DIM,), F32), jnp.concatenate([inv_freq, inv_freq])).reshape(1, LANES)
    tm = TOK_TILE
    spec = pl.BlockSpec((1, tm, LANES), lambda b, i: (b, i, 0))
    shape = jax.ShapeDtypeStruct((bsz, s, LANES), F32)
    return pl.pallas_call(
        _rope_tab_kernel,
        grid=(bsz, s // tm),
        in_specs=[pl.BlockSpec((1, tm, 1), lambda b, i: (b, i, 0)),
                  pl.BlockSpec((1, LANES), lambda b, i: (0, 0))],
        out_specs=[spec, spec],
        out_shape=[shape, shape],
        compiler_params=_params("arbitrary", "arbitrary"),
        name="rope_tables",
    )(positions.reshape(bsz, s, 1), inv_lane)


def _bias_tile_kernel(rb_ref, o_ref):
    h, d = pl.program_id(0), pl.program_id(1)
    t = o_ref.shape[-1]
    key = lax.broadcasted_iota(jnp.int32, (t, t), 0)
    qry = lax.broadcasted_iota(jnp.int32, (t, t), 1)
    dist = d * t + qry - key
    n = jnp.maximum(dist, 0)
    nf = jnp.maximum(n, 1).astype(F32)
    large = MAX_EXACT + (jnp.log(nf / MAX_EXACT) / math.log(MAX_DISTANCE / MAX_EXACT)
                         * (N_BUCKETS - MAX_EXACT)).astype(jnp.int32)
    large = jnp.minimum(large, N_BUCKETS - 1)
    bucket = jnp.where(n < MAX_EXACT, n, large)
    val = jnp.zeros((t, t), F32)
    for j in range(N_BUCKETS):
        val = jnp.where(bucket == j, rb_ref[j, h], val)
    val = (rb_ref[N_BUCKETS - 1, h] - val) * LOG2E
    o_ref[0, 0] = jnp.where(dist < 0, -NEG_BIG, val)


def _bias_tiles(rel_bias):
    t = ATTN_TILE
    return pl.pallas_call(
        _bias_tile_kernel,
        grid=(DIFF_HEADS, 2),
        in_specs=[pl.BlockSpec(memory_space=pltpu.SMEM)],
        out_specs=pl.BlockSpec((1, 1, t, t), lambda h, d: (h, d, 0, 0)),
        out_shape=jax.ShapeDtypeStruct((DIFF_HEADS, 2, t, t), F32),
        compiler_params=_params("arbitrary", "arbitrary"),
        name="bias_tiles",
    )(rel_bias)


def _pre_attn_kernel(x_ref, mod_ref, g_ref, w_in_ref, gq_ref, w_uq_ref, gkv_ref, w_uk_ref, w_uv_ref,
                     rc_ref, rs_ref,
                     dq_ref, dk_ref, dv_ref, mq_ref, mk_ref, mv_ref):
    tm = x_ref.shape[1]
    for r0 in range(0, tm, tm // PRE_ROW_GROUPS):
        rows = slice(r0, r0 + tm // PRE_ROW_GROUPS)
        x = x_ref[0, rows, :]
        sh, sc = mod_ref[0, 0:1, :], mod_ref[0, 1:2, :]
        h = (_rms(x, g_ref[...]) * (1.0 + sc) + sh).astype(BF16)
        proj = jnp.dot(h, w_in_ref[...], preferred_element_type=F32)
        dq_ref[0, rows, :] = (proj[:, C_DQ:C_DK] * (DIFF_QK_DIM ** -0.5 * LOG2E)).astype(BF16)
        dk_ref[0, rows, :] = proj[:, C_DK:C_DV].astype(BF16)
        dv_ref[0, rows, :] = proj[:, C_DV:C_MQ].astype(BF16)

        rc, rs = rc_ref[0, rows, :], rs_ref[0, rows, :]

        def rope(v):
            return v * rc + pltpu.roll(v, LANES // 2, 1) * rs

        qn = _rms(proj[:, C_MQ:C_KV], gq_ref[...]).astype(BF16)
        q = jnp.dot(qn, w_uq_ref[...], preferred_element_type=F32)
        kvn = _rms(proj[:, C_KV:C_KR], gkv_ref[...]).astype(BF16)
        kn = jnp.dot(kvn, w_uk_ref[...], preferred_element_type=F32)
        mv_ref[0, rows, :] = jnp.dot(kvn, w_uv_ref[...], preferred_element_type=F32).astype(BF16)
        kr = rope(proj[:, C_KR:C_END])
        q_scale = (MLA_NOPE_DIM + MLA_ROPE_DIM) ** -0.5 * LOG2E
        for hd in range(MLA_HEADS):
            sl = slice(hd * LANES, (hd + 1) * LANES)
            mq_ref[0, rows, sl] = (rope(q[:, sl]) * q_scale).astype(BF16)
            mk_ref[0, rows, sl] = (kn[:, sl] + kr).astype(BF16)


def _pre_attn(x, mod, g, w_in, gq, w_uq, gkv, w_uk, w_uv, rc, rs):
    bsz, s, d = x.shape
    tm = TOK_TILE
    tok = lambda w: pl.BlockSpec((1, tm, w), lambda b, i: (b, i, 0))
    full = lambda a: pl.BlockSpec(a.shape, lambda b, i: (0,) * a.ndim)
    widths = (512, 512, 512, MLA_HEADS * LANES, MLA_HEADS * LANES, MLA_HEADS * MLA_V_DIM)
    return pl.pallas_call(
        _pre_attn_kernel,
        grid=(bsz, s // tm),
        in_specs=[tok(d), pl.BlockSpec((1, ADA_CHUNKS, d), lambda b, i: (b, 0, 0)), full(g), full(w_in),
                  full(gq), full(w_uq), full(gkv), full(w_uk), full(w_uv), tok(LANES), tok(LANES)],
        out_specs=[tok(w) for w in widths],
        out_shape=[jax.ShapeDtypeStruct((bsz, s, w), BF16) for w in widths],
        compiler_params=_params("arbitrary", "arbitrary"),
        name="pre_attn",
    )(x, mod, g, w_in, gq, w_uq, gkv, w_uk, w_uv, rc, rs)


ONES_ROWS = BF16_SUBLANES


def _values_t(v, dv):
    vt = v.T
    ones = jnp.ones((ONES_ROWS, vt.shape[1]), vt.dtype)
    parts = []
    for h in range(vt.shape[0] // dv):
        parts += [vt[h * dv:(h + 1) * dv], ones]
    return jnp.concatenate(parts, axis=0)


def _softmax_init(m_ref, acc_ref):
    m_ref[...] = jnp.full(m_ref.shape, NEG_BIG, F32)
    acc_ref[...] = jnp.zeros(acc_ref.shape, F32)


def _softmax_probs(st, m_ref):
    m_prev = m_ref[...]
    m_new = jnp.maximum(m_prev, jnp.max(st, axis=0, keepdims=True))
    m_ref[...] = m_new
    return jnp.exp2(st - m_new).astype(BF16), jnp.exp2(m_prev - m_new)


def _acc_update(acc_ref, alpha, vt, p):
    acc_ref[...] = alpha * acc_ref[...] + jnp.dot(vt, p, preferred_element_type=F32)


def _normalized(acc_ref, dv):
    return acc_ref[:dv] / acc_ref[dv:dv + 1]


def _attn_kernel(*refs, cast_weights):
    (lam_init_ref, lam_ref, g_ref, bias_ref, dq_ref, dk_ref, dv_ref, mq_ref, mk_ref, mv_ref), refs = refs[:10], refs[10:]
    if cast_weights:
        for w_ref, wb_ref in zip(refs[:3], refs[4:7]):
            wb_ref[...] = w_ref[...].astype(BF16)
        refs = refs[3:4] + refs[7:]
    o_ref, dvt_ref, mvt_ref, m_ref, dacc_ref, macc_ref, pend_ref = refs
    i = pl.program_id(1)
    t = dq_ref.shape[1]
    n_diff = 2 * DIFF_HEADS
    n_chains = n_diff + MLA_HEADS
    d_rows, m_rows = DIFF_V_DIM + ONES_ROWS, MLA_V_DIM + ONES_ROWS

    @pl.when(i == 0)
    def _():
        dvt_ref[...] = _values_t(dv_ref[0], DIFF_V_DIM)
        mvt_ref[...] = _values_t(mv_ref[0], MLA_V_DIM)

    qts = []
    for h in range(DIFF_HEADS):
        qt = dq_ref[0, :, h * LANES:(h + 1) * LANES].T
        row = lax.broadcasted_iota(jnp.int32, qt.shape, 0)
        qts.append(jnp.where(row < DIFF_QK_DIM, qt, jnp.zeros_like(qt)))
        qts.append(jnp.where(row >= DIFF_QK_DIM, qt, jnp.zeros_like(qt)))
    qts += [mq_ref[0, :, h * LANES:(h + 1) * LANES].T for h in range(MLA_HEADS)]

    def scores(j, c):
        rows = pl.ds(pl.multiple_of(j * t, t), t)
        if c < n_diff:
            k = dk_ref[0, rows, (c // 2) * LANES:(c // 2 + 1) * LANES]
        else:
            k = mk_ref[0, rows, (c - n_diff) * LANES:(c - n_diff + 1) * LANES]
        return jnp.dot(k, qts[c], preferred_element_type=F32)

    def values(j, c):
        cols = pl.ds(pl.multiple_of(j * t, t), t)
        if c < n_diff:
            return dvt_ref[(c // 2) * d_rows:(c // 2 + 1) * d_rows, cols]
        return mvt_ref[(c - n_diff) * m_rows:(c - n_diff + 1) * m_rows, cols]

    def acc_at(c):
        return dacc_ref.at[c] if c < n_diff else macc_ref.at[c - n_diff]

    def step(j, bias_idx, has_next):
        hidden = None
        if bias_idx == 0:
            key = lax.broadcasted_iota(jnp.int32, (t, t), 0)
            qry = lax.broadcasted_iota(jnp.int32, (t, t), 1)
            hidden = jnp.where(key <= qry, 0.0, -NEG_BIG)

        def adjust(c, st):
            if c < n_diff:
                return st if bias_idx is None else st - bias_ref[c // 2, bias_idx]
            return st if hidden is None else st - hidden

        pending = [pend_ref[c] for c in range(AHEAD)]
        for c in range(n_chains):
            nxt = c + AHEAD
            if nxt < n_chains:
                pending.append(scores(j, nxt))
            elif has_next:
                pend_ref[nxt - n_chains] = scores(j + 1, nxt - n_chains)
            p, alpha = _softmax_probs(adjust(c, pending.pop(0)), m_ref.at[c])
            _acc_update(acc_at(c), alpha, values(j, c), p)

    def far(j, carry):
        step(j, None, True)
        return carry

    for c in range(n_chains):
        _softmax_init(m_ref.at[c], acc_at(c))
    for c in range(AHEAD):
        pend_ref[c] = scores(0, c)
    lax.fori_loop(0, jnp.maximum(i - 1, 0), far, 0)

    @pl.when(i >= 1)
    def _():
        step(i - 1, 1, True)

    step(i, 0, False)

    lv = lam_ref[...]
    lam = (jnp.exp(jnp.sum(lv[0:1] * lv[1:2], keepdims=True)) - jnp.exp(jnp.sum(lv[2:3] * lv[3:4], keepdims=True))
           + lam_init_ref[0])
    for h in range(DIFF_HEADS):
        c0, c1 = 2 * h, 2 * h + 1
        ot = (_normalized(dacc_ref.at[c0], DIFF_V_DIM)
              - lam * _normalized(dacc_ref.at[c1], DIFF_V_DIM))
        ot = ot * lax.rsqrt(jnp.mean(ot * ot, axis=0, keepdims=True) + NORM_EPS) * g_ref[...]
        o_ref[0, :, h * LANES:(h + 1) * LANES] = (ot * (1.0 - lam_init_ref[0])).T.astype(BF16)
    base = DIFF_HEADS * LANES
    for u in range(MLA_HEADS // 2):
        ot = jnp.concatenate([_normalized(macc_ref.at[2 * u], MLA_V_DIM),
                              _normalized(macc_ref.at[2 * u + 1], MLA_V_DIM)], axis=0)
        o_ref[0, :, base + u * LANES:base + (u + 1) * LANES] = ot.T.astype(BF16)


def _attention(lam_init, diff_lambda, g, bias, dq, dk, dv, mq, mk, mv, expert_w=None, expert_layer=0):
    bsz, s, wd = dq.shape
    wq, wv = mq.shape[-1], mv.shape[-1]
    t = ATTN_TILE
    assert MAX_DISTANCE <= t and s % t == 0
    n_diff = 2 * DIFF_HEADS
    n_chains = n_diff + MLA_HEADS
    per_b = s // t
    qtile = lambda w: pl.BlockSpec((1, t, w), lambda b, i: (b, i, 0))
    whole = lambda w: pl.BlockSpec((1, s, w), lambda b, i: (b, 0, 0))
    args = [lam_init, diff_lambda, g, bias, dq, dk, dv, mq, mk, mv]
    in_specs = [pl.BlockSpec(memory_space=pltpu.SMEM),
                pl.BlockSpec(diff_lambda.shape, lambda b, i: (0, 0)),
                pl.BlockSpec(g.shape, lambda b, i: (0, 0)),
                pl.BlockSpec(bias.shape, lambda b, i: (0, 0, 0, 0)),
                qtile(wd), whole(wd), whole(wd), qtile(wq), whole(wq), whole(wv)]
    out_specs = [qtile(wd + wv)]
    out_shape = [jax.ShapeDtypeStruct((bsz, s, wd + wv), BF16)]
    if expert_w is not None:
        steps = bsz * per_b
        n_lay, n_e, d, d_ff = expert_w[0].shape
        for w in expert_w:
            rows, cols = n_e * w.shape[2], w.shape[3]
            assert rows % (BF16_SUBLANES * steps) == 0
            slab = rows // steps
            args.append(w.reshape(n_lay * rows, cols))
            in_specs.append(pl.BlockSpec((slab, cols), lambda b, i: (expert_layer * steps + b * per_b + i, 0)))
            out_specs.append(pl.BlockSpec((slab, cols), lambda b, i: (b * per_b + i, 0)))
            out_shape.append(jax.ShapeDtypeStruct((rows, cols), BF16))
    outs = pl.pallas_call(
        functools.partial(_attn_kernel, cast_weights=expert_w is not None),
        grid=(bsz, per_b),
        in_specs=in_specs,
        out_specs=out_specs,
        out_shape=out_shape,
        scratch_shapes=[pltpu.VMEM((DIFF_HEADS * (DIFF_V_DIM + ONES_ROWS), s), BF16),
                        pltpu.VMEM((MLA_HEADS * (MLA_V_DIM + ONES_ROWS), s), BF16),
                        pltpu.VMEM((n_chains, 1, t), F32),
                        pltpu.VMEM((n_diff, DIFF_V_DIM + ONES_ROWS, t), F32),
                        pltpu.VMEM((MLA_HEADS, MLA_V_DIM + ONES_ROWS, t), F32),
                        pltpu.VMEM((AHEAD, t, t), F32)],
        compiler_params=_params("arbitrary", "arbitrary"),
        name="attention",
    )(*args)
    if expert_w is None:
        return outs[0]
    return outs[0], tuple(wb.reshape(w.shape[1:]) for wb, w in zip(outs[1:], expert_w))


def _post_attn_kernel(*refs, moe):
    if moe:
        o_ref, x_ref, mod_ref, g_ref, wo_ref, wr_ref, x1_ref, h_ref, gates_ref, sel_ref, cnt_ref = refs

        @pl.when((pl.program_id(0) == 0) & (pl.program_id(1) == 0))
        def _():
            cnt_ref[...] = jnp.zeros(cnt_ref.shape, F32)
    else:
        o_ref, x_ref, mod_ref, g_ref, wo_ref, x1_ref, h_ref = refs
    gt_a, sh_f, sc_f = mod_ref[0, 2:3, :], mod_ref[0, 3:4, :], mod_ref[0, 4:5, :]
    tm = x_ref.shape[1]
    n = tm // POST_ROW_GROUPS
    for r0 in range(0, tm, n):
        rows = slice(r0, r0 + n)
        y = jnp.dot(o_ref[0, rows, :], wo_ref[...], preferred_element_type=F32)
        x1 = x_ref[0, rows, :] + (1.0 + gt_a) * y
        x1_ref[0, rows, :] = x1
        h = _rms(x1, g_ref[...]) * (1.0 + sc_f) + sh_f
        if not moe:
            h_ref[0, rows, :] = h.astype(BF16)
            continue
        _rows_to_tiles(h, h_ref.at[0, pl.ds(r0 * ROW_SUB, n * ROW_SUB)])
        wr = wr_ref[...]
        h_hi, wr_hi = h.astype(BF16), wr.astype(BF16)
        h_lo, wr_lo = (h - h_hi.astype(F32)).astype(BF16), (wr - wr_hi.astype(F32)).astype(BF16)
        logits = (jnp.dot(h_hi, wr_hi, preferred_element_type=F32)
                  + jnp.dot(h_hi, wr_lo, preferred_element_type=F32)
                  + jnp.dot(h_lo, wr_hi, preferred_element_type=F32))
        lane = lax.broadcasted_iota(jnp.int32, logits.shape, 1)
        logits = jnp.where(lane < N_EXPERTS, logits, -jnp.inf)
        v1 = jnp.max(logits, axis=1, keepdims=True)
        i1 = jnp.min(jnp.where(logits == v1, lane, LANES), axis=1, keepdims=True)
        rest = jnp.where(lane == i1, -jnp.inf, logits)
        v2 = jnp.max(rest, axis=1, keepdims=True)
        i2 = jnp.min(jnp.where(rest == v2, lane, LANES), axis=1, keepdims=True)
        e2 = jnp.exp(v2 - v1)
        w1 = 1.0 / (1.0 + e2)
        w2 = e2 / (1.0 + e2)
        gates_ref[0, rows, :] = jnp.where(lane == i1, w1, 0.0) + jnp.where(lane == i2, w2, 0.0)
        sel_ref[0, rows, :] = jnp.where(lane == i1, 1.0, 0.0) + jnp.where(lane == i2, 2.0, 0.0)
        chosen = jnp.where((lane == i1) | (lane == i2), 1.0, 0.0)
        cnt_ref[...] += jnp.sum(chosen, axis=0, keepdims=True)


def _post_attn(o, x, mod, g, wo, w_router=None):
    bsz, s, d = x.shape
    moe = w_router is not None
    tm = TOK_TILE
    tok = lambda w: pl.BlockSpec((1, tm, w), lambda b, i: (b, i, 0))
    full = lambda a: pl.BlockSpec(a.shape, lambda b, i: (0,) * a.ndim)
    args = [o, x, mod, g, wo] + ([w_router] if moe else [])
    in_specs = [tok(o.shape[-1]), tok(d), pl.BlockSpec((1, ADA_CHUNKS, d), lambda b, i: (b, 0, 0)),
                full(g), full(wo)] + ([full(w_router)] if moe else [])
    if moe:
        assert d == ROW_SUB * LANES
        h_spec = pl.BlockSpec((1, tm * ROW_SUB, LANES), lambda b, i: (b, i, 0))
        out_specs = [tok(d), h_spec, tok(LANES), tok(LANES), pl.BlockSpec((8, LANES), lambda b, i: (0, 0))]
        out_shape = [jax.ShapeDtypeStruct((bsz, s, d), F32), jax.ShapeDtypeStruct((bsz, s * ROW_SUB, LANES), F32)]
        out_shape += [jax.ShapeDtypeStruct((bsz, s, LANES), F32)] * 2 + [jax.ShapeDtypeStruct((8, LANES), F32)]
    else:
        out_specs = [tok(d), tok(d)]
        out_shape = [jax.ShapeDtypeStruct((bsz, s, d), F32), jax.ShapeDtypeStruct((bsz, s, d), BF16)]
    return pl.pallas_call(
        functools.partial(_post_attn_kernel, moe=moe),
        grid=(bsz, s // tm),
        in_specs=in_specs,
        out_specs=out_specs,
        out_shape=out_shape,
        compiler_params=_params("arbitrary", "arbitrary"),
        name="post_attn_moe" if moe else "post_attn",
    )(*args)


def _swiglu(h, w1_ref, w3_ref, w2_ref):
    d_ff = w1_ref.shape[-1]
    y = jnp.zeros((h.shape[0], w2_ref.shape[-1]), F32)
    for c0 in range(0, d_ff, FF_CHUNK):
        a = jnp.dot(h, w1_ref[:, c0:c0 + FF_CHUNK], preferred_element_type=F32)
        b = jnp.dot(h, w3_ref[:, c0:c0 + FF_CHUNK], preferred_element_type=F32)
        u = (a * jax.nn.sigmoid(a) * b).astype(BF16)
        y = y + jnp.dot(u, w2_ref[c0:c0 + FF_CHUNK, :], preferred_element_type=F32)
    return y


def _ffn_dense_kernel(h_ref, x1_ref, mod_ref, w1_ref, w3_ref, w2_ref, o_ref):
    gt_f = mod_ref[0, 5:6, :]
    o_ref[0] = x1_ref[0] + (1.0 + gt_f) * _swiglu(h_ref[0], w1_ref, w3_ref, w2_ref)


def _ffn_dense(h, x1, mod, w1, w3, w2, layer):
    bsz, s, d = x1.shape
    tm = FFN_TILE
    tok = lambda w: pl.BlockSpec((1, tm, w), lambda b, i: (b, i, 0))
    one = lambda a: pl.BlockSpec((None,) + a.shape[1:], lambda b, i: (layer, 0, 0))
    return pl.pallas_call(
        _ffn_dense_kernel,
        grid=(bsz, s // tm),
        in_specs=[tok(d), tok(d), pl.BlockSpec((1, ADA_CHUNKS, d), lambda b, i: (b, 0, 0)),
                  one(w1), one(w3), one(w2)],
        out_specs=tok(d),
        out_shape=jax.ShapeDtypeStruct(x1.shape, F32),
        compiler_params=_params("arbitrary", "arbitrary"),
        name="ffn_dense",
    )(h, x1, mod, w1, w3, w2)


def _route_kernel(cnt_ref, sel_ref, pos_ref, te_ref, off_ref, run_ref, *, row_tile):
    t = pl.program_id(0)
    sel_t = sel_ref[...].T
    chosen = (sel_t > 0.0).astype(F32)
    per_expert = jnp.sum(chosen, axis=1, keepdims=True)

    @pl.when(t == 0)
    def _():
        cnt = cnt_ref[...].T[:, 0:1]
        padded = jnp.ceil(cnt / row_tile) * row_tile
        row = lax.broadcasted_iota(jnp.int32, cnt.shape, 0)
        off = jnp.zeros(cnt.shape, F32)
        for e in range(N_EXPERTS):
            size_e = jnp.sum(jnp.where(row == e, padded, 0.0), keepdims=True)
            off = off + jnp.where(row > e, size_e, 0.0)
        off_ref[...] = off
        run_ref[...] = jnp.zeros(run_ref.shape, F32)
        ends = off + padded
        tile_start = lax.broadcasted_iota(jnp.int32, (LANES, LANES), 1).astype(F32) * row_tile
        erow = lax.broadcasted_iota(jnp.int32, (LANES, LANES), 0)
        done = jnp.where((erow < N_EXPERTS) & (ends <= tile_start), 1.0, 0.0)
        te = jnp.sum(done, axis=0, keepdims=True).astype(jnp.int32)
        te_ref[...] = jnp.broadcast_to(te, te_ref.shape)

    tm = sel_t.shape[1]
    before = (lax.broadcasted_iota(jnp.int32, (tm, tm), 0)
              < lax.broadcasted_iota(jnp.int32, (tm, tm), 1)).astype(BF16)
    rank = jnp.dot(chosen.astype(BF16), before, preferred_element_type=F32) + run_ref[...]
    base = off_ref[...] + rank
    for k in range(2):
        pos = jnp.sum(jnp.where(sel_t == float(k + 1), base, 0.0), axis=0, keepdims=True)
        pos_ref[0, :, k * tm:(k + 1) * tm] = pos.astype(jnp.int32)
    run_ref[...] += per_expert


def _route(cnt, sel, row_tile, n_row_tiles):
    n_tok = sel.shape[0]
    assert n_row_tiles <= LANES and 2 * n_tok < 2 ** 24
    tm = TOK_TILE
    nt = n_tok // tm
    pos, te = pl.pallas_call(
        functools.partial(_route_kernel, row_tile=row_tile),
        grid=(nt,),
        in_specs=[pl.BlockSpec(cnt.shape, lambda t: (0, 0)), pl.BlockSpec((tm, LANES), lambda t: (t, 0))],
        out_specs=[pl.BlockSpec((1, 1, 2 * tm), lambda t: (t, 0, 0)), pl.BlockSpec((8, LANES), lambda t: (0, 0))],
        out_shape=[jax.ShapeDtypeStruct((nt, 1, 2 * tm), jnp.int32), jax.ShapeDtypeStruct((8, LANES), jnp.int32)],
        scratch_shapes=[pltpu.VMEM((LANES, 1), F32)] * 2,
        compiler_params=_params("arbitrary"),
        name="moe_route",
    )(cnt, sel)
    return pos, te[0]


def _dispatch_kernel(pos_ref, h_ref, xs_in_ref, xs_ref, sem):
    del xs_in_ref
    tm = h_ref.shape[0] // ROW_SUB

    def body(r, carry):
        for k in range(2):
            row = pos_ref[0, 0, k * tm + r]
            pltpu.make_async_copy(_tile_row(h_ref, r), _tile_row(xs_ref, row), sem).start(priority=k)
        return carry

    lax.fori_loop(0, tm, body, 0, unroll=8)
    for k in range(2):
        pltpu.make_async_copy(h_ref, xs_ref.at[pl.ds(0, tm * ROW_SUB)], sem).wait()


def _dispatch(pos, h, xs0):
    nt, tm = pos.shape[0], pos.shape[2] // 2
    return pl.pallas_call(
        _dispatch_kernel,
        grid=(nt,),
        in_specs=[pl.BlockSpec((1, 1, 2 * tm), lambda t: (t, 0, 0), memory_space=pltpu.SMEM),
                  pl.BlockSpec((tm * ROW_SUB, LANES), lambda t: (t, 0)),
                  pl.BlockSpec(memory_space=pl.ANY)],
        out_specs=pl.BlockSpec(memory_space=pl.ANY),
        out_shape=jax.ShapeDtypeStruct(xs0.shape, xs0.dtype),
        scratch_shapes=[pltpu.SemaphoreType.DMA(())],
        input_output_aliases={2: 0},
        compiler_params=_params("arbitrary"),
        name="moe_dispatch",
    )(pos, h, xs0)


def _expert_kernel(te_ref, xs_ref, w1_ref, w3_ref, w2_ref, y_ref):
    used = te_ref[pl.program_id(0)] < N_EXPERTS

    @pl.when(used)
    def _():
        y = _swiglu(_tiles_to_rows(xs_ref).astype(BF16), w1_ref.at[0], w3_ref.at[0], w2_ref.at[0])
        _rows_to_tiles(y, y_ref)

    @pl.when(jnp.logical_not(used))
    def _():
        y_ref[...] = jnp.zeros(y_ref.shape, y_ref.dtype)


def _experts(te, xs, w1, w3, w2, row_tile):
    n_rows = xs.shape[0] // ROW_SUB
    _, d, d_ff = w1.shape
    expert = lambda n, te: (jnp.minimum(te[n], N_EXPERTS - 1), 0, 0)
    rows = pl.BlockSpec((row_tile * ROW_SUB, LANES), lambda n, te: (n, 0))
    return pl.pallas_call(
        _expert_kernel,
        grid_spec=pltpu.PrefetchScalarGridSpec(
            num_scalar_prefetch=1,
            grid=(n_rows // row_tile,),
            in_specs=[rows, pl.BlockSpec((1, d, d_ff), expert), pl.BlockSpec((1, d, d_ff), expert),
                      pl.BlockSpec((1, d_ff, d), expert)],
            out_specs=rows),
        out_shape=jax.ShapeDtypeStruct(xs.shape, F32),
        compiler_params=_params("arbitrary"),
        name="moe_experts",
    )(te, xs, w1, w3, w2)


def _combine_kernel(pos_ref, pos_next_ref, x1_ref, gates_ref, sel_ref, mod_ref, gfin_ref, ys_ref, o_ref,
                    ybuf_ref, sem, *, final):
    tm = x1_ref.shape[1]
    step = pl.program_id(0) * pl.num_programs(1) + pl.program_id(1)
    n_steps = pl.num_programs(0) * pl.num_programs(1)
    slot = lax.rem(step, 2)

    def gather(p_ref, into):
        def body(r, carry):
            for k in range(2):
                pltpu.make_async_copy(_tile_row(ys_ref, p_ref[0, 0, k * tm + r]), _tile_row(ybuf_ref.at[into, k], r),
                                      sem.at[into]).start(priority=k)
            return carry

        lax.fori_loop(0, tm, body, 0, unroll=8)

    @pl.when(step == 0)
    def _():
        gather(pos_ref, slot)

    @pl.when(step + 1 < n_steps)
    def _():
        gather(pos_next_ref, 1 - slot)

    gates, sel = gates_ref[0], sel_ref[0]
    w_a = jnp.sum(jnp.where(sel == 1.0, gates, 0.0), axis=1, keepdims=True)
    w_b = jnp.sum(jnp.where(sel == 2.0, gates, 0.0), axis=1, keepdims=True)
    for k in range(2):
        pltpu.make_async_copy(ys_ref.at[pl.ds(0, tm * ROW_SUB)], ybuf_ref.at[slot, k], sem.at[slot]).wait()
    gt_f = mod_ref[0, 5:6, :]
    y = w_a * _tiles_to_rows(ybuf_ref.at[slot, 0]) + w_b * _tiles_to_rows(ybuf_ref.at[slot, 1])
    x = x1_ref[0] + (1.0 + gt_f) * y
    o_ref[0] = _rms(x, gfin_ref[...]) if final else x


def _combine(pos, x1, gates, sel, mod, ys, g_final=None):
    bsz, s, d = x1.shape
    nt, tm = pos.shape[0], pos.shape[2] // 2
    per_b = s // tm
    tok = lambda w: pl.BlockSpec((1, tm, w), lambda b, i: (b, i, 0))
    final = g_final is not None
    gfin = g_final if final else jnp.ones((1, d), F32)
    return pl.pallas_call(
        functools.partial(_combine_kernel, final=final),
        grid=(bsz, per_b),
        in_specs=[pl.BlockSpec((1, 1, 2 * tm), lambda b, i: (b * per_b + i, 0, 0), memory_space=pltpu.SMEM),
                  pl.BlockSpec((1, 1, 2 * tm), lambda b, i: (jnp.minimum(b * per_b + i + 1, nt - 1), 0, 0),
                               memory_space=pltpu.SMEM),
                  tok(d), tok(LANES), tok(LANES),
                  pl.BlockSpec((1, ADA_CHUNKS, d), lambda b, i: (b, 0, 0)),
                  pl.BlockSpec((1, d), lambda b, i: (0, 0)),
                  pl.BlockSpec(memory_space=pl.ANY)],
        out_specs=tok(d),
        out_shape=jax.ShapeDtypeStruct(x1.shape, F32),
        scratch_shapes=[pltpu.VMEM((2, 2, tm * ROW_SUB, LANES), F32), pltpu.SemaphoreType.DMA((2,))],
        compiler_params=_params("arbitrary", "arbitrary"),
        name="moe_combine",
    )(pos, pos, x1, gates, sel, mod, gfin, ys)


def _moe(h, x1, gates, sel, cnt, mod, w1, w3, w2, xs_buf, g_final):
    bsz, s, d = x1.shape
    n_tok = bsz * s
    row_tile = FFN_TILE
    n_rows = xs_buf.shape[0] // ROW_SUB
    assert n_rows == _moe_rows(n_tok)
    pos, te = _route(cnt, sel.reshape(n_tok, LANES), row_tile, n_rows // row_tile)
    xs = _dispatch(pos, h.reshape(n_tok * ROW_SUB, LANES), xs_buf)
    ys = _experts(te, xs, w1, w3, w2, row_tile)
    return _combine(pos, x1, gates, sel, mod, ys, g_final), xs


def _moe_rows(n_tok):
    return 2 * n_tok + N_EXPERTS * FFN_TILE


def _final_norm_kernel(x_ref, g_ref, o_ref):
    o_ref[0] = _rms(x_ref[0], g_ref[...])


def _final_norm(x, g):
    bsz, s, d = x.shape
    tm = TOK_TILE
    tok = pl.BlockSpec((1, tm, d), lambda b, i: (b, i, 0))
    return pl.pallas_call(
        _final_norm_kernel,
        grid=(bsz, s // tm),
        in_specs=[tok, pl.BlockSpec(g.shape, lambda b, i: (0, 0))],
        out_specs=tok,
        out_shape=jax.ShapeDtypeStruct(x.shape, F32),
        compiler_params=_params("arbitrary", "arbitrary"),
        name="final_norm",
    )(x, g)


def _head_blocks(w, width, nope_cols, rope_cols):
    k = w.shape[0]
    w = w.reshape(k, MLA_HEADS, width)
    nope = w[:, :, nope_cols] if nope_cols is not None else jnp.zeros((k, MLA_HEADS, MLA_NOPE_DIM), w.dtype)
    rope = w[:, :, rope_cols] if rope_cols is not None else jnp.zeros((k, MLA_HEADS, MLA_ROPE_DIM), w.dtype)
    return _head_block(nope, rope).reshape(k, MLA_HEADS * LANES)


def _prep_w_in(w_in):
    d = w_in.shape[0]
    kr_block = _head_block(jnp.zeros((d, MLA_NOPE_DIM), w_in.dtype), w_in[:, C_KR:])
    return jnp.concatenate([w_in[:, :C_KR], kr_block], axis=1).astype(BF16)


def kernel(x, c, positions, w_ada, b_ada, g_attn, w_in, diff_lambda, diff_subln_g, rel_bias, mla_q_norm, w_uq, mla_kv_norm, w_ukv, w_o, g_ffn, ffn_w1, ffn_w3, ffn_w2, moe_router, moe_w1, moe_w3, moe_w2, g_final):
    depth = w_ada.shape[0]
    bsz, s, d = x.shape
    mods = _ada(c, w_ada, b_ada).reshape(depth, bsz, ADA_CHUNKS, d)
    rc, rs = _rope_tables(positions)
    bias = _bias_tiles(rel_bias)
    qk_w = MLA_NOPE_DIM + MLA_ROPE_DIM
    kv_w = MLA_NOPE_DIM + MLA_V_DIM
    xs_buf = jnp.zeros((_moe_rows(bsz * s) * ROW_SUB, LANES), F32)
    ffn_w = (ffn_w1.astype(BF16), ffn_w3.astype(BF16), ffn_w2.astype(BF16))
    for l in range(depth):
        mod = mods[l]
        lam_init = jnp.full((1,), 0.8 - 0.6 * math.exp(-0.3 * l), F32)
        w_uq_p = _head_blocks(w_uq[l], qk_w, slice(0, MLA_NOPE_DIM), slice(MLA_NOPE_DIM, qk_w)).astype(BF16)
        w_uk_p = _head_blocks(w_ukv[l], kv_w, slice(0, MLA_NOPE_DIM), None).astype(BF16)
        w_uv_p = w_ukv[l].reshape(MLA_KV_RANK, MLA_HEADS, kv_w)[:, :, MLA_NOPE_DIM:].reshape(
            MLA_KV_RANK, MLA_HEADS * MLA_V_DIM).astype(BF16)
        dq, dk, dv, mq, mk, mv = _pre_attn(
            x, mod, g_attn[l].reshape(1, d), _prep_w_in(w_in[l]), mla_q_norm[l].reshape(1, -1), w_uq_p,
            mla_kv_norm[l].reshape(1, -1), w_uk_p, w_uv_p, rc, rs)
        attn_args = (lam_init, diff_lambda[l], diff_subln_g[l].reshape(-1, 1), bias, dq, dk, dv, mq, mk, mv)
        wo = w_o[l].astype(BF16)
        g_f = g_ffn[l].reshape(1, d)
        if l % 2 == 1:
            o, expert_w = _attention(*attn_args, expert_w=(moe_w1, moe_w3, moe_w2), expert_layer=l // 2)
            w_router = jnp.pad(moe_router[l // 2], ((0, 0), (0, LANES - N_EXPERTS)))
            x1, h, gates, sel, cnt = _post_attn(o, x, mod, g_f, wo, w_router)
            g_fin = g_final.reshape(1, d) if l == depth - 1 else None
            x, xs_buf = _moe(h, x1, gates, sel, cnt, mod, *expert_w, xs_buf, g_fin)
        else:
            o = _attention(*attn_args)
            x1, h = _post_attn(o, x, mod, g_f, wo)
            x = _ffn_dense(h, x1, mod, *ffn_w, l // 2)
    return x if depth % 2 == 0 else _final_norm(x, g_final.reshape(1, d))
```

```python
import functools
import math

import jax
import jax.numpy as jnp
from jax import lax
from jax.experimental import pallas as pl
from jax.experimental.pallas import tpu as pltpu

F32 = jnp.float32
BF16 = jnp.bfloat16

DIFF_HEADS = 4
DIFF_QK_DIM = 64
DIFF_V_DIM = 128
MLA_HEADS = 8
MLA_NOPE_DIM = 64
MLA_ROPE_DIM = 32
MLA_V_DIM = 64
MLA_Q_RANK = 384
MLA_KV_RANK = 256
ROPE_THETA = 10000.0
N_BUCKETS = 32
MAX_EXACT = 16
MAX_DISTANCE = 128
N_EXPERTS = 8
NORM_EPS = 1e-6
ADA_CHUNKS = 6

LANES = 128
BF16_SUBLANES = 16
NEG_BIG = -1e30
LOG2E = math.log2(math.e)
VMEM_LIMIT = 56 * 1024 * 1024

ATTN_TILE = 256
AHEAD = 4
TOK_TILE = 512
PRE_ROW_GROUPS = 2
POST_ROW_GROUPS = 2
FFN_TILE = 512
FF_CHUNK = 256
ADA_COL_TILE = 1536

C_DQ, C_DK, C_DV, C_MQ, C_KV, C_KR, C_END = 0, 512, 1024, 1536, 1920, 2176, 2304


def _params(*sem):
    return pltpu.CompilerParams(dimension_semantics=sem, vmem_limit_bytes=VMEM_LIMIT)


def _rms(x, g):
    return x * lax.rsqrt(jnp.mean(x * x, axis=-1, keepdims=True) + NORM_EPS) * g


ROW_SUB = 8


def _rows_to_tiles(x, ref):
    n = x.shape[0]
    for j in range(ROW_SUB):
        ref[pl.ds(j, n, stride=ROW_SUB), :] = x[:, j * LANES:(j + 1) * LANES]


def _tiles_to_rows(ref):
    n = ref.shape[0] // ROW_SUB
    return jnp.concatenate([ref[pl.ds(j, n, stride=ROW_SUB), :] for j in range(ROW_SUB)], axis=1)


def _tile_row(ref, r):
    return ref.at[pl.ds(pl.multiple_of(r * ROW_SUB, ROW_SUB), ROW_SUB)]


def _ada_kernel(c_ref, w_ref, b_ref, o_ref):
    c = c_ref[...]
    cond = c * jax.nn.sigmoid(c)
    o_ref[0] = jnp.dot(cond, w_ref[0], preferred_element_type=F32,
                       precision=lax.Precision.HIGHEST) + b_ref[0]


def _ada(c, w_ada, b_ada):
    depth, d, n = w_ada.shape
    bsz = c.shape[0]
    tn = ADA_COL_TILE
    return pl.pallas_call(
        _ada_kernel,
        grid=(depth, n // tn),
        in_specs=[pl.BlockSpec((bsz, d), lambda l, j: (0, 0)),
                  pl.BlockSpec((1, d, tn), lambda l, j: (l, 0, j)),
                  pl.BlockSpec((1, 1, tn), lambda l, j: (l, 0, j))],
        out_specs=pl.BlockSpec((1, bsz, tn), lambda l, j: (l, 0, j)),
        out_shape=jax.ShapeDtypeStruct((depth, bsz, n), F32),
        compiler_params=_params("arbitrary", "arbitrary"),
        name="ada_mod",
    )(c, w_ada, b_ada.reshape(depth, 1, n))


ROPE_HALF = MLA_ROPE_DIM // 2
NOPE_SPLIT = LANES // 2 - ROPE_HALF


def _head_block(nope, rope):
    pad = jnp.zeros(nope.shape[:-1] + (LANES - MLA_NOPE_DIM - MLA_ROPE_DIM,), nope.dtype)
    return jnp.concatenate([rope[..., :ROPE_HALF], nope[..., :NOPE_SPLIT], rope[..., ROPE_HALF:],
                            nope[..., NOPE_SPLIT:], pad], axis=-1)


def _rope_tab_kernel(pos_ref, inv_ref, c_ref, s_ref):
    pos = pos_ref[0].astype(F32)
    inv = inv_ref[...]
    ang = pos * inv
    lane = lax.broadcasted_iota(jnp.int32, ang.shape, 1)
    is_rope = inv != 0.0
    used = lane < MLA_NOPE_DIM + MLA_ROPE_DIM
    c_ref[0] = jnp.where(is_rope, jnp.cos(ang), jnp.where(used, 1.0, 0.0))
    s_ref[0] = jnp.where(is_rope, jnp.where(lane < LANES // 2, -1.0, 1.0) * jnp.sin(ang), 0.0)


def _rope_tables(positions):
    bsz, s = positions.shape
    inv_freq = ROPE_THETA ** (-jnp.arange(ROPE_HALF, dtype=F32) / ROPE_HALF)
    inv_lane = _head_block(jnp.zeros((MLA_NOPE_DIM,), F32), jnp.concatenate([inv_freq, inv_freq])).reshape(1, LANES)
    tm = TOK_TILE
    spec = pl.BlockSpec((1, tm, LANES), lambda b, i: (b, i, 0))
    shape = jax.ShapeDtypeStruct((bsz, s, LANES), F32)
    return pl.pallas_call(
        _rope_tab_kernel,
        grid=(bsz, s // tm),
        in_specs=[pl.BlockSpec((1, tm, 1), lambda b, i: (b, i, 0)),
                  pl.BlockSpec((1, LANES), lambda b, i: (0, 0))],
        out_specs=[spec, spec],
        out_shape=[shape, shape],
        compiler_params=_params("arbitrary", "arbitrary"),
        name="rope_tables",
    )(positions.reshape(bsz, s, 1), inv_lane)


def _bias_tile_kernel(rb_ref, o_ref):
    h, d = pl.program_id(0), pl.program_id(1)
    t = o_ref.shape[-1]
    key = lax.broadcasted_iota(jnp.int32, (t, t), 0)
    qry = lax.broadcasted_iota(jnp.int32, (t, t), 1)
    dist = d * t + qry - key
    n = jnp.maximum(dist, 0)
    nf = jnp.maximum(n, 1).astype(F32)
    large = MAX_EXACT + (jnp.log(nf / MAX_EXACT) / math.log(MAX_DISTANCE / MAX_EXACT)
                         * (N_BUCKETS - MAX_EXACT)).astype(jnp.int32)
    large = jnp.minimum(large, N_BUCKETS - 1)
    bucket = jnp.where(n < MAX_EXACT, n, large)
    val = jnp.zeros((t, t), F32)
    for j in range(N_BUCKETS):
        val = jnp.where(bucket == j, rb_ref[j, h], val)
    val = (rb_ref[N_BUCKETS - 1, h] - val) * LOG2E
    o_ref[0, 0] = jnp.where(dist < 0, -NEG_BIG, val)


def _bias_tiles(rel_bias):
    t = ATTN_TILE
    return pl.pallas_call(
        _bias_tile_kernel,
        grid=(DIFF_HEADS, 2),
        in_specs=[pl.BlockSpec(memory_space=pltpu.SMEM)],
        out_specs=pl.BlockSpec((1, 1, t, t), lambda h, d: (h, d, 0, 0)),
        out_shape=jax.ShapeDtypeStruct((DIFF_HEADS, 2, t, t), F32),
        compiler_params=_params("arbitrary", "arbitrary"),
        name="bias_tiles",
    )(rel_bias)


def _pre_attn_kernel(x_ref, mod_ref, g_ref, w_in_ref, gq_ref, w_uq_ref, gkv_ref, w_uk_ref, w_uv_ref,
                     rc_ref, rs_ref,
                     dq_ref, dk_ref, dv_ref, mq_ref, mk_ref, mv_ref):
    tm = x_ref.shape[1]
    for r0 in range(0, tm, tm // PRE_ROW_GROUPS):
        rows = slice(r0, r0 + tm // PRE_ROW_GROUPS)
        x = x_ref[0, rows, :]
        sh, sc = mod_ref[0, 0:1, :], mod_ref[0, 1:2, :]
        h = (_rms(x, g_ref[...]) * (1.0 + sc) + sh).astype(BF16)
        proj = jnp.dot(h, w_in_ref[...], preferred_element_type=F32)
        dq_ref[0, rows, :] = (proj[:, C_DQ:C_DK] * (DIFF_QK_DIM ** -0.5 * LOG2E)).astype(BF16)
        dk_ref[0, rows, :] = proj[:, C_DK:C_DV].astype(BF16)
        dv_ref[0, rows, :] = proj[:, C_DV:C_MQ].astype(BF16)

        rc, rs = rc_ref[0, rows, :], rs_ref[0, rows, :]

        def rope(v):
            return v * rc + pltpu.roll(v, LANES // 2, 1) * rs

        qn = _rms(proj[:, C_MQ:C_KV], gq_ref[...]).astype(BF16)
        q = jnp.dot(qn, w_uq_ref[...], preferred_element_type=F32)
        kvn = _rms(proj[:, C_KV:C_KR], gkv_ref[...]).astype(BF16)
        kn = jnp.dot(kvn, w_uk_ref[...], preferred_element_type=F32)
        mv_ref[0, rows, :] = jnp.dot(kvn, w_uv_ref[...], preferred_element_type=F32).astype(BF16)
        kr = rope(proj[:, C_KR:C_END])
        q_scale = (MLA_NOPE_DIM + MLA_ROPE_DIM) ** -0.5 * LOG2E
        for hd in range(MLA_HEADS):
            sl = slice(hd * LANES, (hd + 1) * LANES)
            mq_ref[0, rows, sl] = (rope(q[:, sl]) * q_scale).astype(BF16)
            mk_ref[0, rows, sl] = (kn[:, sl] + kr).astype(BF16)


def _pre_attn(x, mod, g, w_in, gq, w_uq, gkv, w_uk, w_uv, rc, rs):
    bsz, s, d = x.shape
    tm = TOK_TILE
    tok = lambda w: pl.BlockSpec((1, tm, w), lambda b, i: (b, i, 0))
    full = lambda a: pl.BlockSpec(a.shape, lambda b, i: (0,) * a.ndim)
    widths = (512, 512, 512, MLA_HEADS * LANES, MLA_HEADS * LANES, MLA_HEADS * MLA_V_DIM)
    return pl.pallas_call(
        _pre_attn_kernel,
        grid=(bsz, s // tm),
        in_specs=[tok(d), pl.BlockSpec((1, ADA_CHUNKS, d), lambda b, i: (b, 0, 0)), full(g), full(w_in),
                  full(gq), full(w_uq), full(gkv), full(w_uk), full(w_uv), tok(LANES), tok(LANES)],
        out_specs=[tok(w) for w in widths],
        out_shape=[jax.ShapeDtypeStruct((bsz, s, w), BF16) for w in widths],
        compiler_params=_params("arbitrary", "arbitrary"),
        name="pre_attn",
    )(x, mod, g, w_in, gq, w_uq, gkv, w_uk, w_uv, rc, rs)


ONES_ROWS = BF16_SUBLANES


def _values_t(v, dv):
    vt = v.T
    ones = jnp.ones((ONES_ROWS, vt.shape[1]), vt.dtype)
    parts = []
    for h in range(vt.shape[0] // dv):
        parts += [vt[h * dv:(h + 1) * dv], ones]
    return jnp.concatenate(parts, axis=0)


def _softmax_init(m_ref, acc_ref):
    m_ref[...] = jnp.full(m_ref.shape, NEG_BIG, F32)
    acc_ref[...] = jnp.zeros(acc_ref.shape, F32)


def _softmax_probs(st, m_ref):
    m_prev = m_ref[...]
    m_new = jnp.maximum(m_prev, jnp.max(st, axis=0, keepdims=True))
    m_ref[...] = m_new
    return jnp.exp2(st - m_new).astype(BF16), jnp.exp2(m_prev - m_new)


def _acc_update(acc_ref, alpha, vt, p):
    acc_ref[...] = alpha * acc_ref[...] + jnp.dot(vt, p, preferred_element_type=F32)


def _normalized(acc_ref, dv):
    return acc_ref[:dv] / acc_ref[dv:dv + 1]


def _attn_kernel(*refs, cast_weights):
    (lam_init_ref, lam_ref, g_ref, bias_ref, dq_ref, dk_ref, dv_ref, mq_ref, mk_ref, mv_ref), refs = refs[:10], refs[10:]
    if cast_weights:
        for w_ref, wb_ref in zip(refs[:3], refs[4:7]):
            wb_ref[...] = w_ref[...].astype(BF16)
        refs = refs[3:4] + refs[7:]
    o_ref, dvt_ref, mvt_ref, m_ref, dacc_ref, macc_ref, pend_ref = refs
    i = pl.program_id(1)
    t = dq_ref.shape[1]
    n_diff = 2 * DIFF_HEADS
    n_chains = n_diff + MLA_HEADS
    d_rows, m_rows = DIFF_V_DIM + ONES_ROWS, MLA_V_DIM + ONES_ROWS

    @pl.when(i == 0)
    def _():
        dvt_ref[...] = _values_t(dv_ref[0], DIFF_V_DIM)
        mvt_ref[...] = _values_t(mv_ref[0], MLA_V_DIM)

    qts = []
    for h in range(DIFF_HEADS):
        qt = dq_ref[0, :, h * LANES:(h + 1) * LANES].T
        row = lax.broadcasted_iota(jnp.int32, qt.shape, 0)
        qts.append(jnp.where(row < DIFF_QK_DIM, qt, jnp.zeros_like(qt)))
        qts.append(jnp.where(row >= DIFF_QK_DIM, qt, jnp.zeros_like(qt)))
    qts += [mq_ref[0, :, h * LANES:(h + 1) * LANES].T for h in range(MLA_HEADS)]

    def scores(j, c):
        rows = pl.ds(pl.multiple_of(j * t, t), t)
        if c < n_diff:
            k = dk_ref[0, rows, (c // 2) * LANES:(c // 2 + 1) * LANES]
        else:
            k = mk_ref[0, rows, (c - n_diff) * LANES:(c - n_diff + 1) * LANES]
        return jnp.dot(k, qts[c], preferred_element_type=F32)

    def values(j, c):
        cols = pl.ds(pl.multiple_of(j * t, t), t)
        if c < n_diff:
            return dvt_ref[(c // 2) * d_rows:(c // 2 + 1) * d_rows, cols]
        return mvt_ref[(c - n_diff) * m_rows:(c - n_diff + 1) * m_rows, cols]

    def acc_at(c):
        return dacc_ref.at[c] if c < n_diff else macc_ref.at[c - n_diff]

    def step(j, bias_idx, has_next):
        hidden = None
        if bias_idx == 0:
            key = lax.broadcasted_iota(jnp.int32, (t, t), 0)
            qry = lax.broadcasted_iota(jnp.int32, (t, t), 1)
            hidden = jnp.where(key <= qry, 0.0, -NEG_BIG)

        def adjust(c, st):
            if c < n_diff:
                return st if bias_idx is None else st - bias_ref[c // 2, bias_idx]
            return st if hidden is None else st - hidden

        pending = [pend_ref[c] for c in range(AHEAD)]
        for c in range(n_chains):
            nxt = c + AHEAD
            if nxt < n_chains:
                pending.append(scores(j, nxt))
            elif has_next:
                pend_ref[nxt - n_chains] = scores(j + 1, nxt - n_chains)
            p, alpha = _softmax_probs(adjust(c, pending.pop(0)), m_ref.at[c])
            _acc_update(acc_at(c), alpha, values(j, c), p)

    def far(j, carry):
        step(j, None, True)
        return carry

    for c in range(n_chains):
        _softmax_init(m_ref.at[c], acc_at(c))
    for c in range(AHEAD):
        pend_ref[c] = scores(0, c)
    lax.fori_loop(0, jnp.maximum(i - 1, 0), far, 0)

    @pl.when(i >= 1)
    def _():
        step(i - 1, 1, True)

    step(i, 0, False)

    lv = lam_ref[...]
    lam = (jnp.exp(jnp.sum(lv[0:1] * lv[1:2], keepdims=True)) - jnp.exp(jnp.sum(lv[2:3] * lv[3:4], keepdims=True))
           + lam_init_ref[0])
    for h in range(DIFF_HEADS):
        c0, c1 = 2 * h, 2 * h + 1
        ot = (_normalized(dacc_ref.at[c0], DIFF_V_DIM)
              - lam * _normalized(dacc_ref.at[c1], DIFF_V_DIM))
        ot = ot * lax.rsqrt(jnp.mean(ot * ot, axis=0, keepdims=True) + NORM_EPS) * g_ref[...]
        o_ref[0, :, h * LANES:(h + 1) * LANES] = (ot * (1.0 - lam_init_ref[0])).T.astype(BF16)
    base = DIFF_HEADS * LANES
    for u in range(MLA_HEADS // 2):
        ot = jnp.concatenate([_normalized(macc_ref.at[2 * u], MLA_V_DIM),
                              _normalized(macc_ref.at[2 * u + 1], MLA_V_DIM)], axis=0)
        o_ref[0, :, base + u * LANES:base + (u + 1) * LANES] = ot.T.astype(BF16)


def _attention(lam_init, diff_lambda, g, bias, dq, dk, dv, mq, mk, mv, expert_w=None, expert_layer=0):
    bsz, s, wd = dq.shape
    wq, wv = mq.shape[-1], mv.shape[-1]
    t = ATTN_TILE
    assert MAX_DISTANCE <= t and s % t == 0
    n_diff = 2 * DIFF_HEADS
    n_chains = n_diff + MLA_HEADS
    per_b = s // t
    qtile = lambda w: pl.BlockSpec((1, t, w), lambda b, i: (b, i, 0))
    whole = lambda w: pl.BlockSpec((1, s, w), lambda b, i: (b, 0, 0))
    args = [lam_init, diff_lambda, g, bias, dq, dk, dv, mq, mk, mv]
    in_specs = [pl.BlockSpec(memory_space=pltpu.SMEM),
                pl.BlockSpec(diff_lambda.shape, lambda b, i: (0, 0)),
                pl.BlockSpec(g.shape, lambda b, i: (0, 0)),
                pl.BlockSpec(bias.shape, lambda b, i: (0, 0, 0, 0)),
                qtile(wd), whole(wd), whole(wd), qtile(wq), whole(wq), whole(wv)]
    out_specs = [qtile(wd + wv)]
    out_shape = [jax.ShapeDtypeStruct((bsz, s, wd + wv), BF16)]
    if expert_w is not None:
        steps = bsz * per_b
        n_lay, n_e, d, d_ff = expert_w[0].shape
        for w in expert_w:
            rows, cols = n_e * w.shape[2], w.shape[3]
            assert rows % (BF16_SUBLANES * steps) == 0
            slab = rows // steps
            args.append(w.reshape(n_lay * rows, cols))
            in_specs.append(pl.BlockSpec((slab, cols), lambda b, i: (expert_layer * steps + b * per_b + i, 0)))
            out_specs.append(pl.BlockSpec((slab, cols), lambda b, i: (b * per_b + i, 0)))
            out_shape.append(jax.ShapeDtypeStruct((rows, cols), BF16))
    outs = pl.pallas_call(
        functools.partial(_attn_kernel, cast_weights=expert_w is not None),
        grid=(bsz, per_b),
        in_specs=in_specs,
        out_specs=out_specs,
        out_shape=out_shape,
        scratch_shapes=[pltpu.VMEM((DIFF_HEADS * (DIFF_V_DIM + ONES_ROWS), s), BF16),
                        pltpu.VMEM((MLA_HEADS * (MLA_V_DIM + ONES_ROWS), s), BF16),
                        pltpu.VMEM((n_chains, 1, t), F32),
                        pltpu.VMEM((n_diff, DIFF_V_DIM + ONES_ROWS, t), F32),
                        pltpu.VMEM((MLA_HEADS, MLA_V_DIM + ONES_ROWS, t), F32),
                        pltpu.VMEM((AHEAD, t, t), F32)],
        compiler_params=_params("arbitrary", "arbitrary"),
        name="attention",
    )(*args)
    if expert_w is None:
        return outs[0]
    return outs[0], tuple(wb.reshape(w.shape[1:]) for wb, w in zip(outs[1:], expert_w))


def _post_attn_kernel(*refs, moe):
    if moe:
        o_ref, x_ref, mod_ref, g_ref, wo_ref, wr_ref, x1_ref, h_ref, gates_ref, sel_ref, cnt_ref = refs

        @pl.when((pl.program_id(0) == 0) & (pl.program_id(1) == 0))
        def _():
            cnt_ref[...] = jnp.zeros(cnt_ref.shape, F32)
    else:
        o_ref, x_ref, mod_ref, g_ref, wo_ref, x1_ref, h_ref = refs
    gt_a, sh_f, sc_f = mod_ref[0, 2:3, :], mod_ref[0, 3:4, :], mod_ref[0, 4:5, :]
    tm = x_ref.shape[1]
    n = tm // POST_ROW_GROUPS
    for r0 in range(0, tm, n):
        rows = slice(r0, r0 + n)
        y = jnp.dot(o_ref[0, rows, :], wo_ref[...], preferred_element_type=F32)
        x1 = x_ref[0, rows, :] + (1.0 + gt_a) * y
        x1_ref[0, rows, :] = x1
        h = _rms(x1, g_ref[...]) * (1.0 + sc_f) + sh_f
        if not moe:
            h_ref[0, rows, :] = h.astype(BF16)
            continue
        _rows_to_tiles(h, h_ref.at[0, pl.ds(r0 * ROW_SUB, n * ROW_SUB)])
        wr = wr_ref[...]
        h_hi, wr_hi = h.astype(BF16), wr.astype(BF16)
        h_lo, wr_lo = (h - h_hi.astype(F32)).astype(BF16), (wr - wr_hi.astype(F32)).astype(BF16)
        logits = (jnp.dot(h_hi, wr_hi, preferred_element_type=F32)
                  + jnp.dot(h_hi, wr_lo, preferred_element_type=F32)
                  + jnp.dot(h_lo, wr_hi, preferred_element_type=F32))
        lane = lax.broadcasted_iota(jnp.int32, logits.shape, 1)
        logits = jnp.where(lane < N_EXPERTS, logits, -jnp.inf)
        v1 = jnp.max(logits, axis=1, keepdims=True)
        i1 = jnp.min(jnp.where(logits == v1, lane, LANES), axis=1, keepdims=True)
        rest = jnp.where(lane == i1, -jnp.inf, logits)
        v2 = jnp.max(rest, axis=1, keepdims=True)
        i2 = jnp.min(jnp.where(rest == v2, lane, LANES), axis=1, keepdims=True)
        e2 = jnp.exp(v2 - v1)
        w1 = 1.0 / (1.0 + e2)
        w2 = e2 / (1.0 + e2)
        gates_ref[0, rows, :] = jnp.where(lane == i1, w1, 0.0) + jnp.where(lane == i2, w2, 0.0)
        sel_ref[0, rows, :] = jnp.where(lane == i1, 1.0, 0.0) + jnp.where(lane == i2, 2.0, 0.0)
        chosen = jnp.where((lane == i1) | (lane == i2), 1.0, 0.0)
        cnt_ref[...] += jnp.sum(chosen, axis=0, keepdims=True)


def _post_attn(o, x, mod, g, wo, w_router=None):
    bsz, s, d = x.shape
    moe = w_router is not None
    tm = TOK_TILE
    tok = lambda w: pl.BlockSpec((1, tm, w), lambda b, i: (b, i, 0))
    full = lambda a: pl.BlockSpec(a.shape, lambda b, i: (0,) * a.ndim)
    args = [o, x, mod, g, wo] + ([w_router] if moe else [])
    in_specs = [tok(o.shape[-1]), tok(d), pl.BlockSpec((1, ADA_CHUNKS, d), lambda b, i: (b, 0, 0)),
                full(g), full(wo)] + ([full(w_router)] if moe else [])
    if moe:
        assert d == ROW_SUB * LANES
        h_spec = pl.BlockSpec((1, tm * ROW_SUB, LANES), lambda b, i: (b, i, 0))
        out_specs = [tok(d), h_spec, tok(LANES), tok(LANES), pl.BlockSpec((8, LANES), lambda b, i: (0, 0))]
        out_shape = [jax.ShapeDtypeStruct((bsz, s, d), F32), jax.ShapeDtypeStruct((bsz, s * ROW_SUB, LANES), F32)]
        out_shape += [jax.ShapeDtypeStruct((bsz, s, LANES), F32)] * 2 + [jax.ShapeDtypeStruct((8, LANES), F32)]
    else:
        out_specs = [tok(d), tok(d)]
        out_shape = [jax.ShapeDtypeStruct((bsz, s, d), F32), jax.ShapeDtypeStruct((bsz, s, d), BF16)]
    return pl.pallas_call(
        functools.partial(_post_attn_kernel, moe=moe),
        grid=(bsz, s // tm),
        in_specs=in_specs,
        out_specs=out_specs,
        out_shape=out_shape,
        compiler_params=_params("arbitrary", "arbitrary"),
        name="post_attn_moe" if moe else "post_attn",
    )(*args)


def _swiglu(h, w1_ref, w3_ref, w2_ref):
    d_ff = w1_ref.shape[-1]
    y = jnp.zeros((h.shape[0], w2_ref.shape[-1]), F32)
    for c0 in range(0, d_ff, FF_CHUNK):
        a = jnp.dot(h, w1_ref[:, c0:c0 + FF_CHUNK], preferred_element_type=F32)
        b = jnp.dot(h, w3_ref[:, c0:c0 + FF_CHUNK], preferred_element_type=F32)
        u = (a * jax.nn.sigmoid(a) * b).astype(BF16)
        y = y + jnp.dot(u, w2_ref[c0:c0 + FF_CHUNK, :], preferred_element_type=F32)
    return y


def _dense_layer_kernel(o_ref, x_ref, mod_ref, g_ref, wo_ref, w1_ref, w3_ref, w2_ref, out_ref):
    gt_a, sh_f, sc_f, gt_f = (mod_ref[0, j:j + 1, :] for j in (2, 3, 4, 5))
    tm = x_ref.shape[1]
    n = tm // POST_ROW_GROUPS
    x1s, hs = [], []
    for r0 in range(0, tm, n):
        rows = slice(r0, r0 + n)
        y = jnp.dot(o_ref[0, rows, :], wo_ref[...], preferred_element_type=F32)
        x1 = x_ref[0, rows, :] + (1.0 + gt_a) * y
        x1s.append(x1)
        hs.append((_rms(x1, g_ref[...]) * (1.0 + sc_f) + sh_f).astype(BF16))
    x1, h = jnp.concatenate(x1s, axis=0), jnp.concatenate(hs, axis=0)
    out_ref[0] = x1 + (1.0 + gt_f) * _swiglu(h, w1_ref, w3_ref, w2_ref)


def _dense_layer(o, x, mod, g, wo, w1, w3, w2, layer):
    bsz, s, d = x.shape
    tm = FFN_TILE
    tok = lambda w: pl.BlockSpec((1, tm, w), lambda b, i: (b, i, 0))
    full = lambda a: pl.BlockSpec(a.shape, lambda b, i: (0,) * a.ndim)
    one = lambda a: pl.BlockSpec((None,) + a.shape[1:], lambda b, i: (layer, 0, 0))
    return pl.pallas_call(
        _dense_layer_kernel,
        grid=(bsz, s // tm),
        in_specs=[tok(o.shape[-1]), tok(d), pl.BlockSpec((1, ADA_CHUNKS, d), lambda b, i: (b, 0, 0)),
                  full(g), full(wo), one(w1), one(w3), one(w2)],
        out_specs=tok(d),
        out_shape=jax.ShapeDtypeStruct(x.shape, F32),
        compiler_params=_params("arbitrary", "arbitrary"),
        name="dense_layer",
    )(o, x, mod, g, wo, w1, w3, w2)


def _route_kernel(cnt_ref, sel_ref, pos_ref, te_ref, off_ref, run_ref, *, row_tile):
    t = pl.program_id(0)
    sel_t = sel_ref[...].T
    chosen = (sel_t > 0.0).astype(F32)
    per_expert = jnp.sum(chosen, axis=1, keepdims=True)

    @pl.when(t == 0)
    def _():
        cnt = cnt_ref[...].T[:, 0:1]
        padded = jnp.ceil(cnt / row_tile) * row_tile
        row = lax.broadcasted_iota(jnp.int32, cnt.shape, 0)
        off = jnp.zeros(cnt.shape, F32)
        for e in range(N_EXPERTS):
            size_e = jnp.sum(jnp.where(row == e, padded, 0.0), keepdims=True)
            off = off + jnp.where(row > e, size_e, 0.0)
        off_ref[...] = off
        run_ref[...] = jnp.zeros(run_ref.shape, F32)
        ends = off + padded
        tile_start = lax.broadcasted_iota(jnp.int32, (LANES, LANES), 1).astype(F32) * row_tile
        erow = lax.broadcasted_iota(jnp.int32, (LANES, LANES), 0)
        done = jnp.where((erow < N_EXPERTS) & (ends <= tile_start), 1.0, 0.0)
        te = jnp.sum(done, axis=0, keepdims=True).astype(jnp.int32)
        te_ref[...] = jnp.broadcast_to(te, te_ref.shape)

    tm = sel_t.shape[1]
    before = (lax.broadcasted_iota(jnp.int32, (tm, tm), 0)
              < lax.broadcasted_iota(jnp.int32, (tm, tm), 1)).astype(BF16)
    rank = jnp.dot(chosen.astype(BF16), before, preferred_element_type=F32) + run_ref[...]
    base = off_ref[...] + rank
    for k in range(2):
        pos = jnp.sum(jnp.where(sel_t == float(k + 1), base, 0.0), axis=0, keepdims=True)
        pos_ref[0, :, k * tm:(k + 1) * tm] = pos.astype(jnp.int32)
    run_ref[...] += per_expert


def _route(cnt, sel, row_tile, n_row_tiles):
    n_tok = sel.shape[0]
    assert n_row_tiles <= LANES and 2 * n_tok < 2 ** 24
    tm = TOK_TILE
    nt = n_tok // tm
    pos, te = pl.pallas_call(
        functools.partial(_route_kernel, row_tile=row_tile),
        grid=(nt,),
        in_specs=[pl.BlockSpec(cnt.shape, lambda t: (0, 0)), pl.BlockSpec((tm, LANES), lambda t: (t, 0))],
        out_specs=[pl.BlockSpec((1, 1, 2 * tm), lambda t: (t, 0, 0)), pl.BlockSpec((8, LANES), lambda t: (0, 0))],
        out_shape=[jax.ShapeDtypeStruct((nt, 1, 2 * tm), jnp.int32), jax.ShapeDtypeStruct((8, LANES), jnp.int32)],
        scratch_shapes=[pltpu.VMEM((LANES, 1), F32)] * 2,
        compiler_params=_params("arbitrary"),
        name="moe_route",
    )(cnt, sel)
    return pos, te[0]


def _dispatch_kernel(pos_ref, h_ref, xs_in_ref, xs_ref, sem):
    del xs_in_ref
    tm = h_ref.shape[0] // ROW_SUB

    def body(r, carry):
        for k in range(2):
            row = pos_ref[0, 0, k * tm + r]
            pltpu.make_async_copy(_tile_row(h_ref, r), _tile_row(xs_ref, row), sem).start(priority=k)
        return carry

    lax.fori_loop(0, tm, body, 0, unroll=8)
    for k in range(2):
        pltpu.make_async_copy(h_ref, xs_ref.at[pl.ds(0, tm * ROW_SUB)], sem).wait()


def _dispatch(pos, h, xs0):
    nt, tm = pos.shape[0], pos.shape[2] // 2
    return pl.pallas_call(
        _dispatch_kernel,
        grid=(nt,),
        in_specs=[pl.BlockSpec((1, 1, 2 * tm), lambda t: (t, 0, 0), memory_space=pltpu.SMEM),
                  pl.BlockSpec((tm * ROW_SUB, LANES), lambda t: (t, 0)),
                  pl.BlockSpec(memory_space=pl.ANY)],
        out_specs=pl.BlockSpec(memory_space=pl.ANY),
        out_shape=jax.ShapeDtypeStruct(xs0.shape, xs0.dtype),
        scratch_shapes=[pltpu.SemaphoreType.DMA(())],
        input_output_aliases={2: 0},
        compiler_params=_params("arbitrary"),
        name="moe_dispatch",
    )(pos, h, xs0)


def _expert_kernel(te_ref, xs_ref, w1_ref, w3_ref, w2_ref, y_ref):
    used = te_ref[pl.program_id(0)] < N_EXPERTS

    @pl.when(used)
    def _():
        y = _swiglu(_tiles_to_rows(xs_ref).astype(BF16), w1_ref.at[0], w3_ref.at[0], w2_ref.at[0])
        _rows_to_tiles(y, y_ref)

    @pl.when(jnp.logical_not(used))
    def _():
        y_ref[...] = jnp.zeros(y_ref.shape, y_ref.dtype)


def _experts(te, xs, w1, w3, w2, row_tile):
    n_rows = xs.shape[0] // ROW_SUB
    _, d, d_ff = w1.shape
    expert = lambda n, te: (jnp.minimum(te[n], N_EXPERTS - 1), 0, 0)
    rows = pl.BlockSpec((row_tile * ROW_SUB, LANES), lambda n, te: (n, 0))
    return pl.pallas_call(
        _expert_kernel,
        grid_spec=pltpu.PrefetchScalarGridSpec(
            num_scalar_prefetch=1,
            grid=(n_rows // row_tile,),
            in_specs=[rows, pl.BlockSpec((1, d, d_ff), expert), pl.BlockSpec((1, d, d_ff), expert),
                      pl.BlockSpec((1, d_ff, d), expert)],
            out_specs=rows),
        out_shape=jax.ShapeDtypeStruct(xs.shape, F32),
        compiler_params=_params("arbitrary"),
        name="moe_experts",
    )(te, xs, w1, w3, w2)


def _combine_kernel(pos_ref, pos_next_ref, x1_ref, gates_ref, sel_ref, mod_ref, gfin_ref, ys_ref, o_ref,
                    ybuf_ref, sem, *, final):
    tm = x1_ref.shape[1]
    step = pl.program_id(0) * pl.num_programs(1) + pl.program_id(1)
    n_steps = pl.num_programs(0) * pl.num_programs(1)
    slot = lax.rem(step, 2)

    def gather(p_ref, into):
        def body(r, carry):
            for k in range(2):
                pltpu.make_async_copy(_tile_row(ys_ref, p_ref[0, 0, k * tm + r]), _tile_row(ybuf_ref.at[into, k], r),
                                      sem.at[into]).start(priority=k)
            return carry

        lax.fori_loop(0, tm, body, 0, unroll=8)

    @pl.when(step == 0)
    def _():
        gather(pos_ref, slot)

    @pl.when(step + 1 < n_steps)
    def _():
        gather(pos_next_ref, 1 - slot)

    gates, sel = gates_ref[0], sel_ref[0]
    w_a = jnp.sum(jnp.where(sel == 1.0, gates, 0.0), axis=1, keepdims=True)
    w_b = jnp.sum(jnp.where(sel == 2.0, gates, 0.0), axis=1, keepdims=True)
    for k in range(2):
        pltpu.make_async_copy(ys_ref.at[pl.ds(0, tm * ROW_SUB)], ybuf_ref.at[slot, k], sem.at[slot]).wait()
    gt_f = mod_ref[0, 5:6, :]
    y = w_a * _tiles_to_rows(ybuf_ref.at[slot, 0]) + w_b * _tiles_to_rows(ybuf_ref.at[slot, 1])
    x = x1_ref[0] + (1.0 + gt_f) * y
    o_ref[0] = _rms(x, gfin_ref[...]) if final else x


def _combine(pos, x1, gates, sel, mod, ys, g_final=None):
    bsz, s, d = x1.shape
    nt, tm = pos.shape[0], pos.shape[2] // 2
    per_b = s // tm
    tok = lambda w: pl.BlockSpec((1, tm, w), lambda b, i: (b, i, 0))
    final = g_final is not None
    gfin = g_final if final else jnp.ones((1, d), F32)
    return pl.pallas_call(
        functools.partial(_combine_kernel, final=final),
        grid=(bsz, per_b),
        in_specs=[pl.BlockSpec((1, 1, 2 * tm), lambda b, i: (b * per_b + i, 0, 0), memory_space=pltpu.SMEM),
                  pl.BlockSpec((1, 1, 2 * tm), lambda b, i: (jnp.minimum(b * per_b + i + 1, nt - 1), 0, 0),
                               memory_space=pltpu.SMEM),
                  tok(d), tok(LANES), tok(LANES),
                  pl.BlockSpec((1, ADA_CHUNKS, d), lambda b, i: (b, 0, 0)),
                  pl.BlockSpec((1, d), lambda b, i: (0, 0)),
                  pl.BlockSpec(memory_space=pl.ANY)],
        out_specs=tok(d),
        out_shape=jax.ShapeDtypeStruct(x1.shape, F32),
        scratch_shapes=[pltpu.VMEM((2, 2, tm * ROW_SUB, LANES), F32), pltpu.SemaphoreType.DMA((2,))],
        compiler_params=_params("arbitrary", "arbitrary"),
        name="moe_combine",
    )(pos, pos, x1, gates, sel, mod, gfin, ys)


def _moe(h, x1, gates, sel, cnt, mod, w1, w3, w2, xs_buf, g_final):
    bsz, s, d = x1.shape
    n_tok = bsz * s
    row_tile = FFN_TILE
    n_rows = xs_buf.shape[0] // ROW_SUB
    assert n_rows == _moe_rows(n_tok)
    pos, te = _route(cnt, sel.reshape(n_tok, LANES), row_tile, n_rows // row_tile)
    xs = _dispatch(pos, h.reshape(n_tok * ROW_SUB, LANES), xs_buf)
    ys = _experts(te, xs, w1, w3, w2, row_tile)
    return _combine(pos, x1, gates, sel, mod, ys, g_final), xs


def _moe_rows(n_tok):
    return 2 * n_tok + N_EXPERTS * FFN_TILE


def _final_norm_kernel(x_ref, g_ref, o_ref):
    o_ref[0] = _rms(x_ref[0], g_ref[...])


def _final_norm(x, g):
    bsz, s, d = x.shape
    tm = TOK_TILE
    tok = pl.BlockSpec((1, tm, d), lambda b, i: (b, i, 0))
    return pl.pallas_call(
        _final_norm_kernel,
        grid=(bsz, s // tm),
        in_specs=[tok, pl.BlockSpec(g.shape, lambda b, i: (0, 0))],
        out_specs=tok,
        out_shape=jax.ShapeDtypeStruct(x.shape, F32),
        compiler_params=_params("arbitrary", "arbitrary"),
        name="final_norm",
    )(x, g)


def _head_blocks(w, width, nope_cols, rope_cols):
    k = w.shape[0]
    w = w.reshape(k, MLA_HEADS, width)
    nope = w[:, :, nope_cols] if nope_cols is not None else jnp.zeros((k, MLA_HEADS, MLA_NOPE_DIM), w.dtype)
    rope = w[:, :, rope_cols] if rope_cols is not None else jnp.zeros((k, MLA_HEADS, MLA_ROPE_DIM), w.dtype)
    return _head_block(nope, rope).reshape(k, MLA_HEADS * LANES)


def _prep_w_in(w_in):
    d = w_in.shape[0]
    kr_block = _head_block(jnp.zeros((d, MLA_NOPE_DIM), w_in.dtype), w_in[:, C_KR:])
    return jnp.concatenate([w_in[:, :C_KR], kr_block], axis=1).astype(BF16)


def kernel(x, c, positions, w_ada, b_ada, g_attn, w_in, diff_lambda, diff_subln_g, rel_bias, mla_q_norm, w_uq, mla_kv_norm, w_ukv, w_o, g_ffn, ffn_w1, ffn_w3, ffn_w2, moe_router, moe_w1, moe_w3, moe_w2, g_final):
    depth = w_ada.shape[0]
    bsz, s, d = x.shape
    mods = _ada(c, w_ada, b_ada).reshape(depth, bsz, ADA_CHUNKS, d)
    rc, rs = _rope_tables(positions)
    bias = _bias_tiles(rel_bias)
    qk_w = MLA_NOPE_DIM + MLA_ROPE_DIM
    kv_w = MLA_NOPE_DIM + MLA_V_DIM
    xs_buf = jnp.zeros((_moe_rows(bsz * s) * ROW_SUB, LANES), F32)
    ffn_w = (ffn_w1.astype(BF16), ffn_w3.astype(BF16), ffn_w2.astype(BF16))
    for l in range(depth):
        mod = mods[l]
        lam_init = jnp.full((1,), 0.8 - 0.6 * math.exp(-0.3 * l), F32)
        w_uq_p = _head_blocks(w_uq[l], qk_w, slice(0, MLA_NOPE_DIM), slice(MLA_NOPE_DIM, qk_w)).astype(BF16)
        w_uk_p = _head_blocks(w_ukv[l], kv_w, slice(0, MLA_NOPE_DIM), None).astype(BF16)
        w_uv_p = w_ukv[l].reshape(MLA_KV_RANK, MLA_HEADS, kv_w)[:, :, MLA_NOPE_DIM:].reshape(
            MLA_KV_RANK, MLA_HEADS * MLA_V_DIM).astype(BF16)
        dq, dk, dv, mq, mk, mv = _pre_attn(
            x, mod, g_attn[l].reshape(1, d), _prep_w_in(w_in[l]), mla_q_norm[l].reshape(1, -1), w_uq_p,
            mla_kv_norm[l].reshape(1, -1), w_uk_p, w_uv_p, rc, rs)
        attn_args = (lam_init, diff_lambda[l], diff_subln_g[l].reshape(-1, 1), bias, dq, dk, dv, mq, mk, mv)
        wo = w_o[l].astype(BF16)
        g_f = g_ffn[l].reshape(1, d)
        if l % 2 == 1:
            o, expert_w = _attention(*attn_args, expert_w=(moe_w1, moe_w3, moe_w2), expert_layer=l // 2)
            w_router = jnp.pad(moe_router[l // 2], ((0, 0), (0, LANES - N_EXPERTS)))
            x1, h, gates, sel, cnt = _post_attn(o, x, mod, g_f, wo, w_router)
            g_fin = g_final.reshape(1, d) if l == depth - 1 else None
            x, xs_buf = _moe(h, x1, gates, sel, cnt, mod, *expert_w, xs_buf, g_fin)
        else:
            o = _attention(*attn_args)
            x = _dense_layer(o, x, mod, g_f, wo, *ffn_w, l // 2)
    return x if depth % 2 == 0 else _final_norm(x, g_final.reshape(1, d))
```

```python
import functools
import math

import jax
import jax.numpy as jnp
from jax import lax
from jax.experimental import pallas as pl
from jax.experimental.pallas import tpu as pltpu

F32 = jnp.float32
BF16 = jnp.bfloat16

DIFF_HEADS = 4
DIFF_QK_DIM = 64
DIFF_V_DIM = 128
MLA_HEADS = 8
MLA_NOPE_DIM = 64
MLA_ROPE_DIM = 32
MLA_V_DIM = 64
MLA_Q_RANK = 384
MLA_KV_RANK = 256
ROPE_THETA = 10000.0
N_BUCKETS = 32
MAX_EXACT = 16
MAX_DISTANCE = 128
N_EXPERTS = 8
NORM_EPS = 1e-6
ADA_CHUNKS = 6

LANES = 128
BF16_SUBLANES = 16
NEG_BIG = -1e30
LOG2E = math.log2(math.e)
VMEM_LIMIT = 56 * 1024 * 1024

ATTN_TILE = 256
AHEAD = 4
TOK_TILE = 512
PRE_ROW_GROUPS = 2
POST_ROW_GROUPS = 2
FFN_TILE = 512
FF_CHUNK = 256
ADA_COL_TILE = 1536

C_DQ, C_DK, C_DV, C_MQ, C_KV, C_KR, C_END = 0, 512, 1024, 1536, 1920, 2176, 2304


def _params(*sem):
    return pltpu.CompilerParams(dimension_semantics=sem, vmem_limit_bytes=VMEM_LIMIT)


def _rms(x, g):
    return x * lax.rsqrt(jnp.mean(x * x, axis=-1, keepdims=True) + NORM_EPS) * g


ROW_SUB = 8


def _rows_to_tiles(x, ref):
    n = x.shape[0]
    for j in range(ROW_SUB):
        ref[pl.ds(j, n, stride=ROW_SUB), :] = x[:, j * LANES:(j + 1) * LANES]


def _tiles_to_rows(ref):
    n = ref.shape[0] // ROW_SUB
    return jnp.concatenate([ref[pl.ds(j, n, stride=ROW_SUB), :] for j in range(ROW_SUB)], axis=1)


def _tile_row(ref, r):
    return ref.at[pl.ds(pl.multiple_of(r * ROW_SUB, ROW_SUB), ROW_SUB)]


def _ada_kernel(c_ref, w_ref, b_ref, o_ref):
    c = c_ref[...]
    cond = c * jax.nn.sigmoid(c)
    o_ref[0] = jnp.dot(cond, w_ref[0], preferred_element_type=F32,
                       precision=lax.Precision.HIGHEST) + b_ref[0]


def _ada(c, w_ada, b_ada):
    depth, d, n = w_ada.shape
    bsz = c.shape[0]
    tn = ADA_COL_TILE
    return pl.pallas_call(
        _ada_kernel,
        grid=(depth, n // tn),
        in_specs=[pl.BlockSpec((bsz, d), lambda l, j: (0, 0)),
                  pl.BlockSpec((1, d, tn), lambda l, j: (l, 0, j)),
                  pl.BlockSpec((1, 1, tn), lambda l, j: (l, 0, j))],
        out_specs=pl.BlockSpec((1, bsz, tn), lambda l, j: (l, 0, j)),
        out_shape=jax.ShapeDtypeStruct((depth, bsz, n), F32),
        compiler_params=_params("arbitrary", "arbitrary"),
        name="ada_mod",
    )(c, w_ada, b_ada.reshape(depth, 1, n))


ROPE_HALF = MLA_ROPE_DIM // 2
NOPE_SPLIT = LANES // 2 - ROPE_HALF


def _head_block(nope, rope):
    pad = jnp.zeros(nope.shape[:-1] + (LANES - MLA_NOPE_DIM - MLA_ROPE_DIM,), nope.dtype)
    return jnp.concatenate([rope[..., :ROPE_HALF], nope[..., :NOPE_SPLIT], rope[..., ROPE_HALF:],
                            nope[..., NOPE_SPLIT:], pad], axis=-1)


def _rope_tab_kernel(pos_ref, inv_ref, c_ref, s_ref):
    pos = pos_ref[0].astype(F32)
    inv = inv_ref[...]
    ang = pos * inv
    lane = lax.broadcasted_iota(jnp.int32, ang.shape, 1)
    is_rope = inv != 0.0
    used = lane < MLA_NOPE_DIM + MLA_ROPE_DIM
    c_ref[0] = jnp.where(is_rope, jnp.cos(ang), jnp.where(used, 1.0, 0.0))
    s_ref[0] = jnp.where(is_rope, jnp.where(lane < LANES // 2, -1.0, 1.0) * jnp.sin(ang), 0.0)


def _rope_tables(positions):
    bsz, s = positions.shape
    inv_freq = ROPE_THETA ** (-jnp.arange(ROPE_HALF, dtype=F32) / ROPE_HALF)
    inv_lane = _head_block(jnp.zeros((MLA_NOPE_DIM,), F32), jnp.concatenate([inv_freq, inv_freq])).reshape(1, LANES)
    tm = TOK_TILE
    spec = pl.BlockSpec((1, tm, LANES), lambda b, i: (b, i, 0))
    shape = jax.ShapeDtypeStruct((bsz, s, LANES), F32)
    return pl.pallas_call(
        _rope_tab_kernel,
        grid=(bsz, s // tm),
        in_specs=[pl.BlockSpec((1, tm, 1), lambda b, i: (b, i, 0)),
                  pl.BlockSpec((1, LANES), lambda b, i: (0, 0))],
        out_specs=[spec, spec],
        out_shape=[shape, shape],
        compiler_params=_params("arbitrary", "arbitrary"),
        name="rope_tables",
    )(positions.reshape(bsz, s, 1), inv_lane)


def _bias_tile_kernel(rb_ref, o_ref):
    h, d = pl.program_id(0), pl.program_id(1)
    t = o_ref.shape[-1]
    key = lax.broadcasted_iota(jnp.int32, (t, t), 0)
    qry = lax.broadcasted_iota(jnp.int32, (t, t), 1)
    dist = d * t + qry - key
    n = jnp.maximum(dist, 0)
    nf = jnp.maximum(n, 1).astype(F32)
    large = MAX_EXACT + (jnp.log(nf / MAX_EXACT) / math.log(MAX_DISTANCE / MAX_EXACT)
                         * (N_BUCKETS - MAX_EXACT)).astype(jnp.int32)
    large = jnp.minimum(large, N_BUCKETS - 1)
    bucket = jnp.where(n < MAX_EXACT, n, large)
    val = jnp.zeros((t, t), F32)
    for j in range(N_BUCKETS):
        val = jnp.where(bucket == j, rb_ref[j, h], val)
    val = (rb_ref[N_BUCKETS - 1, h] - val) * LOG2E
    o_ref[0, 0] = jnp.where(dist < 0, -NEG_BIG, val)


def _bias_tiles(rel_bias):
    t = ATTN_TILE
    return pl.pallas_call(
        _bias_tile_kernel,
        grid=(DIFF_HEADS, 2),
        in_specs=[pl.BlockSpec(memory_space=pltpu.SMEM)],
        out_specs=pl.BlockSpec((1, 1, t, t), lambda h, d: (h, d, 0, 0)),
        out_shape=jax.ShapeDtypeStruct((DIFF_HEADS, 2, t, t), F32),
        compiler_params=_params("arbitrary", "arbitrary"),
        name="bias_tiles",
    )(rel_bias)


def _pre_attn_kernel(x_ref, mod_ref, g_ref, w_in_ref, gq_ref, w_uq_ref, gkv_ref, w_uk_ref, w_uv_ref,
                     rc_ref, rs_ref,
                     dq_ref, dk_ref, dv_ref, mq_ref, mk_ref, mv_ref):
    tm = x_ref.shape[1]
    for r0 in range(0, tm, tm // PRE_ROW_GROUPS):
        rows = slice(r0, r0 + tm // PRE_ROW_GROUPS)
        x = x_ref[0, rows, :]
        sh, sc = mod_ref[0, 0:1, :], mod_ref[0, 1:2, :]
        h = (_rms(x, g_ref[...]) * (1.0 + sc) + sh).astype(BF16)
        proj = jnp.dot(h, w_in_ref[...], preferred_element_type=F32)
        dq_ref[0, rows, :] = (proj[:, C_DQ:C_DK] * (DIFF_QK_DIM ** -0.5 * LOG2E)).astype(BF16)
        dk_ref[0, rows, :] = proj[:, C_DK:C_DV].astype(BF16)
        dv_ref[0, rows, :] = proj[:, C_DV:C_MQ].astype(BF16)

        rc, rs = rc_ref[0, rows, :], rs_ref[0, rows, :]

        def rope(v):
            return v * rc + pltpu.roll(v, LANES // 2, 1) * rs

        qn = _rms(proj[:, C_MQ:C_KV], gq_ref[...]).astype(BF16)
        q = jnp.dot(qn, w_uq_ref[...], preferred_element_type=F32)
        kvn = _rms(proj[:, C_KV:C_KR], gkv_ref[...]).astype(BF16)
        kn = jnp.dot(kvn, w_uk_ref[...], preferred_element_type=F32)
        mv_ref[0, rows, :] = jnp.dot(kvn, w_uv_ref[...], preferred_element_type=F32).astype(BF16)
        kr = rope(proj[:, C_KR:C_END])
        q_scale = (MLA_NOPE_DIM + MLA_ROPE_DIM) ** -0.5 * LOG2E
        for hd in range(MLA_HEADS):
            sl = slice(hd * LANES, (hd + 1) * LANES)
            mq_ref[0, rows, sl] = (rope(q[:, sl]) * q_scale).astype(BF16)
            mk_ref[0, rows, sl] = (kn[:, sl] + kr).astype(BF16)


def _pre_attn(x, mod, g, w_in, gq, w_uq, gkv, w_uk, w_uv, rc, rs, layer):
    bsz, s, d = x.shape
    lay = lambda a: _layer_spec(a, layer)
    tm = TOK_TILE
    tok = lambda w: pl.BlockSpec((1, tm, w), lambda b, i: (b, i, 0))
    full = lambda a: pl.BlockSpec(a.shape, lambda b, i: (0,) * a.ndim)
    widths = (512, 512, 512, MLA_HEADS * LANES, MLA_HEADS * LANES, MLA_HEADS * MLA_V_DIM)
    return pl.pallas_call(
        _pre_attn_kernel,
        grid=(bsz, s // tm),
        in_specs=[tok(d), pl.BlockSpec((1, ADA_CHUNKS, d), lambda b, i: (b, 0, 0)), full(g), lay(w_in),
                  full(gq), lay(w_uq), full(gkv), lay(w_uk), lay(w_uv), tok(LANES), tok(LANES)],
        out_specs=[tok(w) for w in widths],
        out_shape=[jax.ShapeDtypeStruct((bsz, s, w), BF16) for w in widths],
        compiler_params=_params("arbitrary", "arbitrary"),
        name="pre_attn",
    )(x, mod, g, w_in, gq, w_uq, gkv, w_uk, w_uv, rc, rs)


ONES_ROWS = BF16_SUBLANES


def _values_t(v, dv):
    vt = v.T
    ones = jnp.ones((ONES_ROWS, vt.shape[1]), vt.dtype)
    parts = []
    for h in range(vt.shape[0] // dv):
        parts += [vt[h * dv:(h + 1) * dv], ones]
    return jnp.concatenate(parts, axis=0)


def _softmax_init(m_ref, acc_ref):
    m_ref[...] = jnp.full(m_ref.shape, NEG_BIG, F32)
    acc_ref[...] = jnp.zeros(acc_ref.shape, F32)


def _softmax_probs(st, m_ref):
    m_prev = m_ref[...]
    m_new = jnp.maximum(m_prev, jnp.max(st, axis=0, keepdims=True))
    m_ref[...] = m_new
    return jnp.exp2(st - m_new).astype(BF16), jnp.exp2(m_prev - m_new)


def _acc_update(acc_ref, alpha, vt, p):
    acc_ref[...] = alpha * acc_ref[...] + jnp.dot(vt, p, preferred_element_type=F32)


def _normalized(acc_ref, dv):
    return acc_ref[:dv] / acc_ref[dv:dv + 1]


def _attn_kernel(*refs, cast_weights):
    (lam_init_ref, lam_ref, g_ref, bias_ref, dq_ref, dk_ref, dv_ref, mq_ref, mk_ref, mv_ref), refs = refs[:10], refs[10:]
    if cast_weights:
        for w_ref, wb_ref in zip(refs[:3], refs[4:7]):
            wb_ref[...] = w_ref[...].astype(BF16)
        refs = refs[3:4] + refs[7:]
    o_ref, dvt_ref, mvt_ref, m_ref, dacc_ref, macc_ref, pend_ref = refs
    i = pl.program_id(1)
    t = dq_ref.shape[1]
    n_diff = 2 * DIFF_HEADS
    n_chains = n_diff + MLA_HEADS
    d_rows, m_rows = DIFF_V_DIM + ONES_ROWS, MLA_V_DIM + ONES_ROWS

    @pl.when(i == 0)
    def _():
        dvt_ref[...] = _values_t(dv_ref[0], DIFF_V_DIM)
        mvt_ref[...] = _values_t(mv_ref[0], MLA_V_DIM)

    qts = []
    for h in range(DIFF_HEADS):
        qt = dq_ref[0, :, h * LANES:(h + 1) * LANES].T
        row = lax.broadcasted_iota(jnp.int32, qt.shape, 0)
        qts.append(jnp.where(row < DIFF_QK_DIM, qt, jnp.zeros_like(qt)))
        qts.append(jnp.where(row >= DIFF_QK_DIM, qt, jnp.zeros_like(qt)))
    qts += [mq_ref[0, :, h * LANES:(h + 1) * LANES].T for h in range(MLA_HEADS)]

    def scores(j, c):
        rows = pl.ds(pl.multiple_of(j * t, t), t)
        if c < n_diff:
            k = dk_ref[0, rows, (c // 2) * LANES:(c // 2 + 1) * LANES]
        else:
            k = mk_ref[0, rows, (c - n_diff) * LANES:(c - n_diff + 1) * LANES]
        return jnp.dot(k, qts[c], preferred_element_type=F32)

    def values(j, c):
        cols = pl.ds(pl.multiple_of(j * t, t), t)
        if c < n_diff:
            return dvt_ref[(c // 2) * d_rows:(c // 2 + 1) * d_rows, cols]
        return mvt_ref[(c - n_diff) * m_rows:(c - n_diff + 1) * m_rows, cols]

    def acc_at(c):
        return dacc_ref.at[c] if c < n_diff else macc_ref.at[c - n_diff]

    def step(j, bias_idx, has_next):
        hidden = None
        if bias_idx == 0:
            key = lax.broadcasted_iota(jnp.int32, (t, t), 0)
            qry = lax.broadcasted_iota(jnp.int32, (t, t), 1)
            hidden = jnp.where(key <= qry, 0.0, -NEG_BIG)

        def adjust(c, st):
            if c < n_diff:
                return st if bias_idx is None else st - bias_ref[c // 2, bias_idx]
            return st if hidden is None else st - hidden

        pending = [pend_ref[c] for c in range(AHEAD)]
        for c in range(n_chains):
            nxt = c + AHEAD
            if nxt < n_chains:
                pending.append(scores(j, nxt))
            elif has_next:
                pend_ref[nxt - n_chains] = scores(j + 1, nxt - n_chains)
            p, alpha = _softmax_probs(adjust(c, pending.pop(0)), m_ref.at[c])
            _acc_update(acc_at(c), alpha, values(j, c), p)

    def far(j, carry):
        step(j, None, True)
        return carry

    for c in range(n_chains):
        _softmax_init(m_ref.at[c], acc_at(c))
    for c in range(AHEAD):
        pend_ref[c] = scores(0, c)
    lax.fori_loop(0, jnp.maximum(i - 1, 0), far, 0)

    @pl.when(i >= 1)
    def _():
        step(i - 1, 1, True)

    step(i, 0, False)

    lv = lam_ref[...]
    lam = (jnp.exp(jnp.sum(lv[0:1] * lv[1:2], keepdims=True)) - jnp.exp(jnp.sum(lv[2:3] * lv[3:4], keepdims=True))
           + lam_init_ref[0])
    for h in range(DIFF_HEADS):
        c0, c1 = 2 * h, 2 * h + 1
        ot = (_normalized(dacc_ref.at[c0], DIFF_V_DIM)
              - lam * _normalized(dacc_ref.at[c1], DIFF_V_DIM))
        ot = ot * lax.rsqrt(jnp.mean(ot * ot, axis=0, keepdims=True) + NORM_EPS) * g_ref[...]
        o_ref[0, :, h * LANES:(h + 1) * LANES] = (ot * (1.0 - lam_init_ref[0])).T.astype(BF16)
    base = DIFF_HEADS * LANES
    for u in range(MLA_HEADS // 2):
        ot = jnp.concatenate([_normalized(macc_ref.at[2 * u], MLA_V_DIM),
                              _normalized(macc_ref.at[2 * u + 1], MLA_V_DIM)], axis=0)
        o_ref[0, :, base + u * LANES:base + (u + 1) * LANES] = ot.T.astype(BF16)


def _attention(lam_init, diff_lambda, g, bias, dq, dk, dv, mq, mk, mv, expert_w=None, expert_layer=0):
    bsz, s, wd = dq.shape
    wq, wv = mq.shape[-1], mv.shape[-1]
    t = ATTN_TILE
    assert MAX_DISTANCE <= t and s % t == 0
    n_diff = 2 * DIFF_HEADS
    n_chains = n_diff + MLA_HEADS
    per_b = s // t
    qtile = lambda w: pl.BlockSpec((1, t, w), lambda b, i: (b, i, 0))
    whole = lambda w: pl.BlockSpec((1, s, w), lambda b, i: (b, 0, 0))
    args = [lam_init, diff_lambda, g, bias, dq, dk, dv, mq, mk, mv]
    in_specs = [pl.BlockSpec(memory_space=pltpu.SMEM),
                pl.BlockSpec(diff_lambda.shape, lambda b, i: (0, 0)),
                pl.BlockSpec(g.shape, lambda b, i: (0, 0)),
                pl.BlockSpec(bias.shape, lambda b, i: (0, 0, 0, 0)),
                qtile(wd), whole(wd), whole(wd), qtile(wq), whole(wq), whole(wv)]
    out_specs = [qtile(wd + wv)]
    out_shape = [jax.ShapeDtypeStruct((bsz, s, wd + wv), BF16)]
    if expert_w is not None:
        steps = bsz * per_b
        n_lay, n_e, d, d_ff = expert_w[0].shape
        for w in expert_w:
            rows, cols = n_e * w.shape[2], w.shape[3]
            assert rows % (BF16_SUBLANES * steps) == 0
            slab = rows // steps
            args.append(w.reshape(n_lay * rows, cols))
            in_specs.append(pl.BlockSpec((slab, cols), lambda b, i: (expert_layer * steps + b * per_b + i, 0)))
            out_specs.append(pl.BlockSpec((slab, cols), lambda b, i: (b * per_b + i, 0)))
            out_shape.append(jax.ShapeDtypeStruct((rows, cols), BF16))
    outs = pl.pallas_call(
        functools.partial(_attn_kernel, cast_weights=expert_w is not None),
        grid=(bsz, per_b),
        in_specs=in_specs,
        out_specs=out_specs,
        out_shape=out_shape,
        scratch_shapes=[pltpu.VMEM((DIFF_HEADS * (DIFF_V_DIM + ONES_ROWS), s), BF16),
                        pltpu.VMEM((MLA_HEADS * (MLA_V_DIM + ONES_ROWS), s), BF16),
                        pltpu.VMEM((n_chains, 1, t), F32),
                        pltpu.VMEM((n_diff, DIFF_V_DIM + ONES_ROWS, t), F32),
                        pltpu.VMEM((MLA_HEADS, MLA_V_DIM + ONES_ROWS, t), F32),
                        pltpu.VMEM((AHEAD, t, t), F32)],
        compiler_params=_params("arbitrary", "arbitrary"),
        name="attention",
    )(*args)
    if expert_w is None:
        return outs[0]
    return outs[0], tuple(wb.reshape(w.shape[1:]) for wb, w in zip(outs[1:], expert_w))


def _post_attn_kernel(o_ref, x_ref, mod_ref, g_ref, wo_ref, wr_ref, x1_ref, h_ref, gates_ref, sel_ref, cnt_ref):
    @pl.when((pl.program_id(0) == 0) & (pl.program_id(1) == 0))
    def _():
        cnt_ref[...] = jnp.zeros(cnt_ref.shape, F32)

    gt_a, sh_f, sc_f = mod_ref[0, 2:3, :], mod_ref[0, 3:4, :], mod_ref[0, 4:5, :]
    tm = x_ref.shape[1]
    n = tm // POST_ROW_GROUPS
    for r0 in range(0, tm, n):
        rows = slice(r0, r0 + n)
        y = jnp.dot(o_ref[0, rows, :], wo_ref[...], preferred_element_type=F32)
        x1 = x_ref[0, rows, :] + (1.0 + gt_a) * y
        x1_ref[0, rows, :] = x1
        h = _rms(x1, g_ref[...]) * (1.0 + sc_f) + sh_f
        _rows_to_tiles(h, h_ref.at[0, pl.ds(r0 * ROW_SUB, n * ROW_SUB)])
        wr = wr_ref[...]
        h_hi, wr_hi = h.astype(BF16), wr.astype(BF16)
        h_lo, wr_lo = (h - h_hi.astype(F32)).astype(BF16), (wr - wr_hi.astype(F32)).astype(BF16)
        logits = (jnp.dot(h_hi, wr_hi, preferred_element_type=F32)
                  + jnp.dot(h_hi, wr_lo, preferred_element_type=F32)
                  + jnp.dot(h_lo, wr_hi, preferred_element_type=F32))
        lane = lax.broadcasted_iota(jnp.int32, logits.shape, 1)
        logits = jnp.where(lane < N_EXPERTS, logits, -jnp.inf)
        v1 = jnp.max(logits, axis=1, keepdims=True)
        i1 = jnp.min(jnp.where(logits == v1, lane, LANES), axis=1, keepdims=True)
        rest = jnp.where(lane == i1, -jnp.inf, logits)
        v2 = jnp.max(rest, axis=1, keepdims=True)
        i2 = jnp.min(jnp.where(rest == v2, lane, LANES), axis=1, keepdims=True)
        e2 = jnp.exp(v2 - v1)
        w1 = 1.0 / (1.0 + e2)
        w2 = e2 / (1.0 + e2)
        gates_ref[0, rows, :] = jnp.where(lane == i1, w1, 0.0) + jnp.where(lane == i2, w2, 0.0)
        sel_ref[0, rows, :] = jnp.where(lane == i1, 1.0, 0.0) + jnp.where(lane == i2, 2.0, 0.0)
        chosen = jnp.where((lane == i1) | (lane == i2), 1.0, 0.0)
        cnt_ref[...] += jnp.sum(chosen, axis=0, keepdims=True)


def _post_attn(o, x, mod, g, wo, w_router, layer):
    bsz, s, d = x.shape
    assert d == ROW_SUB * LANES
    tm = TOK_TILE
    tok = lambda w: pl.BlockSpec((1, tm, w), lambda b, i: (b, i, 0))
    full = lambda a: pl.BlockSpec(a.shape, lambda b, i: (0,) * a.ndim)
    h_spec = pl.BlockSpec((1, tm * ROW_SUB, LANES), lambda b, i: (b, i, 0))
    return pl.pallas_call(
        _post_attn_kernel,
        grid=(bsz, s // tm),
        in_specs=[tok(o.shape[-1]), tok(d), pl.BlockSpec((1, ADA_CHUNKS, d), lambda b, i: (b, 0, 0)),
                  full(g), _layer_spec(wo, layer), full(w_router)],
        out_specs=[tok(d), h_spec, tok(LANES), tok(LANES), pl.BlockSpec((8, LANES), lambda b, i: (0, 0))],
        out_shape=[jax.ShapeDtypeStruct((bsz, s, d), F32), jax.ShapeDtypeStruct((bsz, s * ROW_SUB, LANES), F32),
                   jax.ShapeDtypeStruct((bsz, s, LANES), F32), jax.ShapeDtypeStruct((bsz, s, LANES), F32),
                   jax.ShapeDtypeStruct((8, LANES), F32)],
        compiler_params=_params("arbitrary", "arbitrary"),
        name="post_attn_moe",
    )(o, x, mod, g, wo, w_router)


def _swiglu(h, w1_ref, w3_ref, w2_ref):
    d_ff = w1_ref.shape[-1]
    y = jnp.zeros((h.shape[0], w2_ref.shape[-1]), F32)
    for c0 in range(0, d_ff, FF_CHUNK):
        a = jnp.dot(h, w1_ref[:, c0:c0 + FF_CHUNK], preferred_element_type=F32)
        b = jnp.dot(h, w3_ref[:, c0:c0 + FF_CHUNK], preferred_element_type=F32)
        u = (a * jax.nn.sigmoid(a) * b).astype(BF16)
        y = y + jnp.dot(u, w2_ref[c0:c0 + FF_CHUNK, :], preferred_element_type=F32)
    return y


def _dense_layer_kernel(o_ref, x_ref, mod_ref, g_ref, wo_ref, w1_ref, w3_ref, w2_ref, out_ref):
    gt_a, sh_f, sc_f, gt_f = (mod_ref[0, j:j + 1, :] for j in (2, 3, 4, 5))
    tm = x_ref.shape[1]
    n = tm // POST_ROW_GROUPS
    x1s, hs = [], []
    for r0 in range(0, tm, n):
        rows = slice(r0, r0 + n)
        y = jnp.dot(o_ref[0, rows, :], wo_ref[...], preferred_element_type=F32)
        x1 = x_ref[0, rows, :] + (1.0 + gt_a) * y
        x1s.append(x1)
        hs.append((_rms(x1, g_ref[...]) * (1.0 + sc_f) + sh_f).astype(BF16))
    x1, h = jnp.concatenate(x1s, axis=0), jnp.concatenate(hs, axis=0)
    out_ref[0] = x1 + (1.0 + gt_f) * _swiglu(h, w1_ref, w3_ref, w2_ref)


def _dense_layer(o, x, mod, g, wo, w1, w3, w2, layer, ffn_layer):
    bsz, s, d = x.shape
    tm = FFN_TILE
    tok = lambda w: pl.BlockSpec((1, tm, w), lambda b, i: (b, i, 0))
    full = lambda a: pl.BlockSpec(a.shape, lambda b, i: (0,) * a.ndim)
    one = lambda a: _layer_spec(a, ffn_layer)
    return pl.pallas_call(
        _dense_layer_kernel,
        grid=(bsz, s // tm),
        in_specs=[tok(o.shape[-1]), tok(d), pl.BlockSpec((1, ADA_CHUNKS, d), lambda b, i: (b, 0, 0)),
                  full(g), _layer_spec(wo, layer), one(w1), one(w3), one(w2)],
        out_specs=tok(d),
        out_shape=jax.ShapeDtypeStruct(x.shape, F32),
        compiler_params=_params("arbitrary", "arbitrary"),
        name="dense_layer",
    )(o, x, mod, g, wo, w1, w3, w2)


def _route_kernel(cnt_ref, sel_ref, pos_ref, te_ref, off_ref, run_ref, *, row_tile):
    t = pl.program_id(0)
    sel_t = sel_ref[...].T
    chosen = (sel_t > 0.0).astype(F32)
    per_expert = jnp.sum(chosen, axis=1, keepdims=True)

    @pl.when(t == 0)
    def _():
        cnt = cnt_ref[...].T[:, 0:1]
        padded = jnp.ceil(cnt / row_tile) * row_tile
        row = lax.broadcasted_iota(jnp.int32, cnt.shape, 0)
        off = jnp.zeros(cnt.shape, F32)
        for e in range(N_EXPERTS):
            size_e = jnp.sum(jnp.where(row == e, padded, 0.0), keepdims=True)
            off = off + jnp.where(row > e, size_e, 0.0)
        off_ref[...] = off
        run_ref[...] = jnp.zeros(run_ref.shape, F32)
        ends = off + padded
        tile_start = lax.broadcasted_iota(jnp.int32, (LANES, LANES), 1).astype(F32) * row_tile
        erow = lax.broadcasted_iota(jnp.int32, (LANES, LANES), 0)
        done = jnp.where((erow < N_EXPERTS) & (ends <= tile_start), 1.0, 0.0)
        te = jnp.sum(done, axis=0, keepdims=True).astype(jnp.int32)
        te_ref[...] = jnp.broadcast_to(te, te_ref.shape)

    tm = sel_t.shape[1]
    before = (lax.broadcasted_iota(jnp.int32, (tm, tm), 0)
              < lax.broadcasted_iota(jnp.int32, (tm, tm), 1)).astype(BF16)
    rank = jnp.dot(chosen.astype(BF16), before, preferred_element_type=F32) + run_ref[...]
    base = off_ref[...] + rank
    for k in range(2):
        pos = jnp.sum(jnp.where(sel_t == float(k + 1), base, 0.0), axis=0, keepdims=True)
        pos_ref[0, :, k * tm:(k + 1) * tm] = pos.astype(jnp.int32)
    run_ref[...] += per_expert


def _route(cnt, sel, row_tile, n_row_tiles):
    n_tok = sel.shape[0]
    assert n_row_tiles <= LANES and 2 * n_tok < 2 ** 24
    tm = TOK_TILE
    nt = n_tok // tm
    pos, te = pl.pallas_call(
        functools.partial(_route_kernel, row_tile=row_tile),
        grid=(nt,),
        in_specs=[pl.BlockSpec(cnt.shape, lambda t: (0, 0)), pl.BlockSpec((tm, LANES), lambda t: (t, 0))],
        out_specs=[pl.BlockSpec((1, 1, 2 * tm), lambda t: (t, 0, 0)), pl.BlockSpec((8, LANES), lambda t: (0, 0))],
        out_shape=[jax.ShapeDtypeStruct((nt, 1, 2 * tm), jnp.int32), jax.ShapeDtypeStruct((8, LANES), jnp.int32)],
        scratch_shapes=[pltpu.VMEM((LANES, 1), F32)] * 2,
        compiler_params=_params("arbitrary"),
        name="moe_route",
    )(cnt, sel)
    return pos, te[0]


def _dispatch_kernel(pos_ref, h_ref, xs_in_ref, xs_ref, sem):
    del xs_in_ref
    tm = h_ref.shape[0] // ROW_SUB

    def body(r, carry):
        for k in range(2):
            row = pos_ref[0, 0, k * tm + r]
            pltpu.make_async_copy(_tile_row(h_ref, r), _tile_row(xs_ref, row), sem).start(priority=k)
        return carry

    lax.fori_loop(0, tm, body, 0, unroll=8)
    for k in range(2):
        pltpu.make_async_copy(h_ref, xs_ref.at[pl.ds(0, tm * ROW_SUB)], sem).wait()


def _dispatch(pos, h, xs0):
    nt, tm = pos.shape[0], pos.shape[2] // 2
    return pl.pallas_call(
        _dispatch_kernel,
        grid=(nt,),
        in_specs=[pl.BlockSpec((1, 1, 2 * tm), lambda t: (t, 0, 0), memory_space=pltpu.SMEM),
                  pl.BlockSpec((tm * ROW_SUB, LANES), lambda t: (t, 0)),
                  pl.BlockSpec(memory_space=pl.ANY)],
        out_specs=pl.BlockSpec(memory_space=pl.ANY),
        out_shape=jax.ShapeDtypeStruct(xs0.shape, xs0.dtype),
        scratch_shapes=[pltpu.SemaphoreType.DMA(())],
        input_output_aliases={2: 0},
        compiler_params=_params("arbitrary"),
        name="moe_dispatch",
    )(pos, h, xs0)


def _expert_kernel(te_ref, xs_ref, w1_ref, w3_ref, w2_ref, y_ref):
    used = te_ref[pl.program_id(0)] < N_EXPERTS

    @pl.when(used)
    def _():
        y = _swiglu(_tiles_to_rows(xs_ref).astype(BF16), w1_ref.at[0], w3_ref.at[0], w2_ref.at[0])
        _rows_to_tiles(y, y_ref)

    @pl.when(jnp.logical_not(used))
    def _():
        y_ref[...] = jnp.zeros(y_ref.shape, y_ref.dtype)


def _experts(te, xs, w1, w3, w2, row_tile):
    n_rows = xs.shape[0] // ROW_SUB
    _, d, d_ff = w1.shape
    expert = lambda n, te: (jnp.minimum(te[n], N_EXPERTS - 1), 0, 0)
    rows = pl.BlockSpec((row_tile * ROW_SUB, LANES), lambda n, te: (n, 0))
    return pl.pallas_call(
        _expert_kernel,
        grid_spec=pltpu.PrefetchScalarGridSpec(
            num_scalar_prefetch=1,
            grid=(n_rows // row_tile,),
            in_specs=[rows, pl.BlockSpec((1, d, d_ff), expert), pl.BlockSpec((1, d, d_ff), expert),
                      pl.BlockSpec((1, d_ff, d), expert)],
            out_specs=rows),
        out_shape=jax.ShapeDtypeStruct(xs.shape, F32),
        compiler_params=_params("arbitrary"),
        name="moe_experts",
    )(te, xs, w1, w3, w2)


def _combine_kernel(pos_ref, pos_next_ref, x1_ref, gates_ref, sel_ref, mod_ref, gfin_ref, ys_ref, o_ref,
                    ybuf_ref, sem, *, final):
    tm = x1_ref.shape[1]
    step = pl.program_id(0) * pl.num_programs(1) + pl.program_id(1)
    n_steps = pl.num_programs(0) * pl.num_programs(1)
    slot = lax.rem(step, 2)

    def gather(p_ref, into):
        def body(r, carry):
            for k in range(2):
                pltpu.make_async_copy(_tile_row(ys_ref, p_ref[0, 0, k * tm + r]), _tile_row(ybuf_ref.at[into, k], r),
                                      sem.at[into]).start(priority=k)
            return carry

        lax.fori_loop(0, tm, body, 0, unroll=8)

    @pl.when(step == 0)
    def _():
        gather(pos_ref, slot)

    @pl.when(step + 1 < n_steps)
    def _():
        gather(pos_next_ref, 1 - slot)

    gates, sel = gates_ref[0], sel_ref[0]
    w_a = jnp.sum(jnp.where(sel == 1.0, gates, 0.0), axis=1, keepdims=True)
    w_b = jnp.sum(jnp.where(sel == 2.0, gates, 0.0), axis=1, keepdims=True)
    for k in range(2):
        pltpu.make_async_copy(ys_ref.at[pl.ds(0, tm * ROW_SUB)], ybuf_ref.at[slot, k], sem.at[slot]).wait()
    gt_f = mod_ref[0, 5:6, :]
    y = w_a * _tiles_to_rows(ybuf_ref.at[slot, 0]) + w_b * _tiles_to_rows(ybuf_ref.at[slot, 1])
    x = x1_ref[0] + (1.0 + gt_f) * y
    o_ref[0] = _rms(x, gfin_ref[...]) if final else x


def _combine(pos, x1, gates, sel, mod, ys, g_final=None):
    bsz, s, d = x1.shape
    nt, tm = pos.shape[0], pos.shape[2] // 2
    per_b = s // tm
    tok = lambda w: pl.BlockSpec((1, tm, w), lambda b, i: (b, i, 0))
    final = g_final is not None
    gfin = g_final if final else jnp.ones((1, d), F32)
    return pl.pallas_call(
        functools.partial(_combine_kernel, final=final),
        grid=(bsz, per_b),
        in_specs=[pl.BlockSpec((1, 1, 2 * tm), lambda b, i: (b * per_b + i, 0, 0), memory_space=pltpu.SMEM),
                  pl.BlockSpec((1, 1, 2 * tm), lambda b, i: (jnp.minimum(b * per_b + i + 1, nt - 1), 0, 0),
                               memory_space=pltpu.SMEM),
                  tok(d), tok(LANES), tok(LANES),
                  pl.BlockSpec((1, ADA_CHUNKS, d), lambda b, i: (b, 0, 0)),
                  pl.BlockSpec((1, d), lambda b, i: (0, 0)),
                  pl.BlockSpec(memory_space=pl.ANY)],
        out_specs=tok(d),
        out_shape=jax.ShapeDtypeStruct(x1.shape, F32),
        scratch_shapes=[pltpu.VMEM((2, 2, tm * ROW_SUB, LANES), F32), pltpu.SemaphoreType.DMA((2,))],
        compiler_params=_params("arbitrary", "arbitrary"),
        name="moe_combine",
    )(pos, pos, x1, gates, sel, mod, gfin, ys)


def _moe(h, x1, gates, sel, cnt, mod, w1, w3, w2, xs_buf, g_final):
    bsz, s, d = x1.shape
    n_tok = bsz * s
    row_tile = FFN_TILE
    n_rows = xs_buf.shape[0] // ROW_SUB
    assert n_rows == _moe_rows(n_tok)
    pos, te = _route(cnt, sel.reshape(n_tok, LANES), row_tile, n_rows // row_tile)
    xs = _dispatch(pos, h.reshape(n_tok * ROW_SUB, LANES), xs_buf)
    ys = _experts(te, xs, w1, w3, w2, row_tile)
    return _combine(pos, x1, gates, sel, mod, ys, g_final), xs


def _moe_rows(n_tok):
    return 2 * n_tok + N_EXPERTS * FFN_TILE


def _final_norm_kernel(x_ref, g_ref, o_ref):
    o_ref[0] = _rms(x_ref[0], g_ref[...])


def _final_norm(x, g):
    bsz, s, d = x.shape
    tm = TOK_TILE
    tok = pl.BlockSpec((1, tm, d), lambda b, i: (b, i, 0))
    return pl.pallas_call(
        _final_norm_kernel,
        grid=(bsz, s // tm),
        in_specs=[tok, pl.BlockSpec(g.shape, lambda b, i: (0, 0))],
        out_specs=tok,
        out_shape=jax.ShapeDtypeStruct(x.shape, F32),
        compiler_params=_params("arbitrary", "arbitrary"),
        name="final_norm",
    )(x, g)


def _head_blocks(w, width, nope_cols, rope_cols):
    lead = w.shape[:-1]
    w = w.reshape(lead + (MLA_HEADS, width))
    nope = w[..., nope_cols] if nope_cols is not None else jnp.zeros(lead + (MLA_HEADS, MLA_NOPE_DIM), w.dtype)
    rope = w[..., rope_cols] if rope_cols is not None else jnp.zeros(lead + (MLA_HEADS, MLA_ROPE_DIM), w.dtype)
    return _head_block(nope, rope).reshape(lead + (MLA_HEADS * LANES,))


def _prep_w_in(w_in):
    kr_block = _head_block(jnp.zeros(w_in.shape[:-1] + (MLA_NOPE_DIM,), w_in.dtype), w_in[..., C_KR:])
    return jnp.concatenate([w_in[..., :C_KR], kr_block], axis=-1).astype(BF16)


def _layer_spec(a, layer):
    return pl.BlockSpec((None,) + a.shape[1:], lambda b, i: (layer,) + (0,) * (a.ndim - 1))


def kernel(x, c, positions, w_ada, b_ada, g_attn, w_in, diff_lambda, diff_subln_g, rel_bias, mla_q_norm, w_uq, mla_kv_norm, w_ukv, w_o, g_ffn, ffn_w1, ffn_w3, ffn_w2, moe_router, moe_w1, moe_w3, moe_w2, g_final):
    depth = w_ada.shape[0]
    bsz, s, d = x.shape
    mods = _ada(c, w_ada, b_ada).reshape(depth, bsz, ADA_CHUNKS, d)
    rc, rs = _rope_tables(positions)
    bias = _bias_tiles(rel_bias)
    qk_w = MLA_NOPE_DIM + MLA_ROPE_DIM
    kv_w = MLA_NOPE_DIM + MLA_V_DIM
    xs_buf = jnp.zeros((_moe_rows(bsz * s) * ROW_SUB, LANES), F32)
    ffn_w = (ffn_w1.astype(BF16), ffn_w3.astype(BF16), ffn_w2.astype(BF16))
    wo = w_o.astype(BF16)
    w_in_p = _prep_w_in(w_in)
    w_uq_p = _head_blocks(w_uq, qk_w, slice(0, MLA_NOPE_DIM), slice(MLA_NOPE_DIM, qk_w)).astype(BF16)
    w_uk_p = _head_blocks(w_ukv, kv_w, slice(0, MLA_NOPE_DIM), None).astype(BF16)
    w_uv_p = w_ukv.reshape(depth, MLA_KV_RANK, MLA_HEADS, kv_w)[..., MLA_NOPE_DIM:].reshape(
        depth, MLA_KV_RANK, MLA_HEADS * MLA_V_DIM).astype(BF16)
    for l in range(depth):
        mod = mods[l]
        lam_init = jnp.full((1,), 0.8 - 0.6 * math.exp(-0.3 * l), F32)
        dq, dk, dv, mq, mk, mv = _pre_attn(
            x, mod, g_attn[l].reshape(1, d), w_in_p, mla_q_norm[l].reshape(1, -1), w_uq_p,
            mla_kv_norm[l].reshape(1, -1), w_uk_p, w_uv_p, rc, rs, l)
        attn_args = (lam_init, diff_lambda[l], diff_subln_g[l].reshape(-1, 1), bias, dq, dk, dv, mq, mk, mv)
        g_f = g_ffn[l].reshape(1, d)
        if l % 2 == 1:
            o, expert_w = _attention(*attn_args, expert_w=(moe_w1, moe_w3, moe_w2), expert_layer=l // 2)
            w_router = jnp.pad(moe_router[l // 2], ((0, 0), (0, LANES - N_EXPERTS)))
            x1, h, gates, sel, cnt = _post_attn(o, x, mod, g_f, wo, w_router, l)
            g_fin = g_final.reshape(1, d) if l == depth - 1 else None
            x, xs_buf = _moe(h, x1, gates, sel, cnt, mod, *expert_w, xs_buf, g_fin)
        else:
            o = _attention(*attn_args)
            x = _dense_layer(o, x, mod, g_f, wo, *ffn_w, l, l // 2)
    return x if depth % 2 == 0 else _final_norm(x, g_final.reshape(1, d))
```

```python
import functools
import math

import jax
import jax.numpy as jnp
from jax import lax
from jax.experimental import pallas as pl
from jax.experimental.pallas import tpu as pltpu

F32 = jnp.float32
BF16 = jnp.bfloat16

DIFF_HEADS = 4
DIFF_QK_DIM = 64
DIFF_V_DIM = 128
MLA_HEADS = 8
MLA_NOPE_DIM = 64
MLA_ROPE_DIM = 32
MLA_V_DIM = 64
MLA_Q_RANK = 384
MLA_KV_RANK = 256
ROPE_THETA = 10000.0
N_BUCKETS = 32
MAX_EXACT = 16
MAX_DISTANCE = 128
N_EXPERTS = 8
NORM_EPS = 1e-6
ADA_CHUNKS = 6

LANES = 128
BF16_SUBLANES = 16
NEG_BIG = -1e30
LOG2E = math.log2(math.e)
VMEM_LIMIT = 56 * 1024 * 1024

ATTN_TILE = 256
AHEAD = 4
TOK_TILE = 512
PRE_TILE = 1024
MOE_TILE = 1024
COMBINE_TILE = 512
PRE_ROW_GROUPS = 2
POST_ROW_GROUPS = 2
FFN_TILE = 512
FF_CHUNK = 256
ADA_COL_TILE = 1536

C_DQ, C_DK, C_DV, C_MQ, C_KV, C_KR, C_END = 0, 512, 1024, 1536, 1920, 2176, 2304


def _params(*sem):
    return pltpu.CompilerParams(dimension_semantics=sem, vmem_limit_bytes=VMEM_LIMIT)


def _rms(x, g):
    return x * lax.rsqrt(jnp.mean(x * x, axis=-1, keepdims=True) + NORM_EPS) * g


ROW_SUB = 8


def _rows_to_tiles(x, ref):
    n = x.shape[0]
    for j in range(ROW_SUB):
        ref[pl.ds(j, n, stride=ROW_SUB), :] = x[:, j * LANES:(j + 1) * LANES]


def _tiles_to_rows(ref):
    n = ref.shape[0] // ROW_SUB
    return jnp.concatenate([ref[pl.ds(j, n, stride=ROW_SUB), :] for j in range(ROW_SUB)], axis=1)


def _tile_row(ref, r):
    return ref.at[pl.ds(pl.multiple_of(r * ROW_SUB, ROW_SUB), ROW_SUB)]


def _ada_kernel(c_ref, w_ref, b_ref, o_ref):
    c = c_ref[...]
    cond = c * jax.nn.sigmoid(c)
    o_ref[0] = jnp.dot(cond, w_ref[0], preferred_element_type=F32,
                       precision=lax.Precision.HIGHEST) + b_ref[0]


def _ada(c, w_ada, b_ada):
    depth, d, n = w_ada.shape
    bsz = c.shape[0]
    tn = ADA_COL_TILE
    return pl.pallas_call(
        _ada_kernel,
        grid=(depth, n // tn),
        in_specs=[pl.BlockSpec((bsz, d), lambda l, j: (0, 0)),
                  pl.BlockSpec((1, d, tn), lambda l, j: (l, 0, j)),
                  pl.BlockSpec((1, 1, tn), lambda l, j: (l, 0, j))],
        out_specs=pl.BlockSpec((1, bsz, tn), lambda l, j: (l, 0, j)),
        out_shape=jax.ShapeDtypeStruct((depth, bsz, n), F32),
        compiler_params=_params("arbitrary", "arbitrary"),
        name="ada_mod",
    )(c, w_ada, b_ada.reshape(depth, 1, n))


ROPE_HALF = MLA_ROPE_DIM // 2
NOPE_SPLIT = LANES // 2 - ROPE_HALF


def _head_block(nope, rope):
    pad = jnp.zeros(nope.shape[:-1] + (LANES - MLA_NOPE_DIM - MLA_ROPE_DIM,), nope.dtype)
    return jnp.concatenate([rope[..., :ROPE_HALF], nope[..., :NOPE_SPLIT], rope[..., ROPE_HALF:],
                            nope[..., NOPE_SPLIT:], pad], axis=-1)


def _rope_tab_kernel(pos_ref, inv_ref, c_ref, s_ref):
    pos = pos_ref[0].astype(F32)
    inv = inv_ref[...]
    ang = pos * inv
    lane = lax.broadcasted_iota(jnp.int32, ang.shape, 1)
    is_rope = inv != 0.0
    used = lane < MLA_NOPE_DIM + MLA_ROPE_DIM
    c_ref[0] = jnp.where(is_rope, jnp.cos(ang), jnp.where(used, 1.0, 0.0))
    s_ref[0] = jnp.where(is_rope, jnp.where(lane < LANES // 2, -1.0, 1.0) * jnp.sin(ang), 0.0)


def _rope_tables(positions):
    bsz, s = positions.shape
    inv_freq = ROPE_THETA ** (-jnp.arange(ROPE_HALF, dtype=F32) / ROPE_HALF)
    inv_lane = _head_block(jnp.zeros((MLA_NOPE_DIM,), F32), jnp.concatenate([inv_freq, inv_freq])).reshape(1, LANES)
    tm = PRE_TILE
    spec = pl.BlockSpec((1, tm, LANES), lambda b, i: (b, i, 0))
    shape = jax.ShapeDtypeStruct((bsz, s, LANES), F32)
    return pl.pallas_call(
        _rope_tab_kernel,
        grid=(bsz, s // tm),
        in_specs=[pl.BlockSpec((1, tm, 1), lambda b, i: (b, i, 0)),
                  pl.BlockSpec((1, LANES), lambda b, i: (0, 0))],
        out_specs=[spec, spec],
        out_shape=[shape, shape],
        compiler_params=_params("arbitrary", "arbitrary"),
        name="rope_tables",
    )(positions.reshape(bsz, s, 1), inv_lane)


def _bias_tile_kernel(rb_ref, o_ref):
    h, d = pl.program_id(0), pl.program_id(1)
    t = o_ref.shape[-1]
    key = lax.broadcasted_iota(jnp.int32, (t, t), 0)
    qry = lax.broadcasted_iota(jnp.int32, (t, t), 1)
    dist = d * t + qry - key
    n = jnp.maximum(dist, 0)
    nf = jnp.maximum(n, 1).astype(F32)
    large = MAX_EXACT + (jnp.log(nf / MAX_EXACT) / math.log(MAX_DISTANCE / MAX_EXACT)
                         * (N_BUCKETS - MAX_EXACT)).astype(jnp.int32)
    large = jnp.minimum(large, N_BUCKETS - 1)
    bucket = jnp.where(n < MAX_EXACT, n, large)
    val = jnp.zeros((t, t), F32)
    for j in range(N_BUCKETS):
        val = jnp.where(bucket == j, rb_ref[j, h], val)
    val = (rb_ref[N_BUCKETS - 1, h] - val) * LOG2E
    o_ref[0, 0] = jnp.where(dist < 0, -NEG_BIG, val)


def _bias_tiles(rel_bias):
    t = ATTN_TILE
    return pl.pallas_call(
        _bias_tile_kernel,
        grid=(DIFF_HEADS, 2),
        in_specs=[pl.BlockSpec(memory_space=pltpu.SMEM)],
        out_specs=pl.BlockSpec((1, 1, t, t), lambda h, d: (h, d, 0, 0)),
        out_shape=jax.ShapeDtypeStruct((DIFF_HEADS, 2, t, t), F32),
        compiler_params=_params("arbitrary", "arbitrary"),
        name="bias_tiles",
    )(rel_bias)


def _pre_attn_kernel(x_ref, mod_ref, g_ref, w_in_ref, gq_ref, w_uq_ref, gkv_ref, w_uk_ref, w_uv_ref,
                     rc_ref, rs_ref,
                     dq_ref, dk_ref, dv_ref, mq_ref, mk_ref, mv_ref):
    tm = x_ref.shape[1]
    for r0 in range(0, tm, tm // PRE_ROW_GROUPS):
        rows = slice(r0, r0 + tm // PRE_ROW_GROUPS)
        x = x_ref[0, rows, :]
        sh, sc = mod_ref[0, 0:1, :], mod_ref[0, 1:2, :]
        h = (_rms(x, g_ref[...]) * (1.0 + sc) + sh).astype(BF16)
        proj = jnp.dot(h, w_in_ref[...], preferred_element_type=F32)
        dq_ref[0, rows, :] = (proj[:, C_DQ:C_DK] * (DIFF_QK_DIM ** -0.5 * LOG2E)).astype(BF16)
        dk_ref[0, rows, :] = proj[:, C_DK:C_DV].astype(BF16)
        dv_ref[0, rows, :] = proj[:, C_DV:C_MQ].astype(BF16)

        rc, rs = rc_ref[0, rows, :], rs_ref[0, rows, :]

        def rope(v):
            return v * rc + pltpu.roll(v, LANES // 2, 1) * rs

        qn = _rms(proj[:, C_MQ:C_KV], gq_ref[...]).astype(BF16)
        q = jnp.dot(qn, w_uq_ref[...], preferred_element_type=F32)
        kvn = _rms(proj[:, C_KV:C_KR], gkv_ref[...]).astype(BF16)
        kn = jnp.dot(kvn, w_uk_ref[...], preferred_element_type=F32)
        mv_ref[0, rows, :] = jnp.dot(kvn, w_uv_ref[...], preferred_element_type=F32).astype(BF16)
        kr = rope(proj[:, C_KR:C_END])
        q_scale = (MLA_NOPE_DIM + MLA_ROPE_DIM) ** -0.5 * LOG2E
        for hd in range(MLA_HEADS):
            sl = slice(hd * LANES, (hd + 1) * LANES)
            mq_ref[0, rows, sl] = (rope(q[:, sl]) * q_scale).astype(BF16)
            mk_ref[0, rows, sl] = (kn[:, sl] + kr).astype(BF16)


def _pre_attn(x, mod, g, w_in, gq, w_uq, gkv, w_uk, w_uv, rc, rs, layer):
    bsz, s, d = x.shape
    lay = lambda a: _layer_spec(a, layer)
    tm = PRE_TILE
    tok = lambda w: pl.BlockSpec((1, tm, w), lambda b, i: (b, i, 0))
    full = lambda a: pl.BlockSpec(a.shape, lambda b, i: (0,) * a.ndim)
    widths = (512, 512, 512, MLA_HEADS * LANES, MLA_HEADS * LANES, MLA_HEADS * MLA_V_DIM)
    return pl.pallas_call(
        _pre_attn_kernel,
        grid=(bsz, s // tm),
        in_specs=[tok(d), pl.BlockSpec((1, ADA_CHUNKS, d), lambda b, i: (b, 0, 0)), full(g), lay(w_in),
                  full(gq), lay(w_uq), full(gkv), lay(w_uk), lay(w_uv), tok(LANES), tok(LANES)],
        out_specs=[tok(w) for w in widths],
        out_shape=[jax.ShapeDtypeStruct((bsz, s, w), BF16) for w in widths],
        compiler_params=_params("arbitrary", "arbitrary"),
        name="pre_attn",
    )(x, mod, g, w_in, gq, w_uq, gkv, w_uk, w_uv, rc, rs)


ONES_ROWS = BF16_SUBLANES


def _values_t(v, dv):
    vt = v.T
    ones = jnp.ones((ONES_ROWS, vt.shape[1]), vt.dtype)
    parts = []
    for h in range(vt.shape[0] // dv):
        parts += [vt[h * dv:(h + 1) * dv], ones]
    return jnp.concatenate(parts, axis=0)


def _softmax_init(m_ref, acc_ref):
    m_ref[...] = jnp.full(m_ref.shape, NEG_BIG, F32)
    acc_ref[...] = jnp.zeros(acc_ref.shape, F32)


def _softmax_probs(st, m_ref):
    m_prev = m_ref[...]
    m_new = jnp.maximum(m_prev, jnp.max(st, axis=0, keepdims=True))
    m_ref[...] = m_new
    return jnp.exp2(st - m_new).astype(BF16), jnp.exp2(m_prev - m_new)


def _acc_update(acc_ref, alpha, vt, p):
    acc_ref[...] = alpha * acc_ref[...] + jnp.dot(vt, p, preferred_element_type=F32)


def _normalized(acc_ref, dv):
    return acc_ref[:dv] / acc_ref[dv:dv + 1]


def _attn_kernel(*refs, cast_weights):
    (lam_init_ref, lam_ref, g_ref, bias_ref, dq_ref, dk_ref, dv_ref, mq_ref, mk_ref, mv_ref), refs = refs[:10], refs[10:]
    if cast_weights:
        for w_ref, wb_ref in zip(refs[:3], refs[4:7]):
            wb_ref[...] = w_ref[...].astype(BF16)
        refs = refs[3:4] + refs[7:]
    o_ref, dvt_ref, mvt_ref, m_ref, dacc_ref, macc_ref, pend_ref = refs
    i = pl.program_id(1)
    t = dq_ref.shape[1]
    n_diff = 2 * DIFF_HEADS
    n_chains = n_diff + MLA_HEADS
    d_rows, m_rows = DIFF_V_DIM + ONES_ROWS, MLA_V_DIM + ONES_ROWS

    @pl.when(i == 0)
    def _():
        dvt_ref[...] = _values_t(dv_ref[0], DIFF_V_DIM)
        mvt_ref[...] = _values_t(mv_ref[0], MLA_V_DIM)

    qts = []
    for h in range(DIFF_HEADS):
        qt = dq_ref[0, :, h * LANES:(h + 1) * LANES].T
        row = lax.broadcasted_iota(jnp.int32, qt.shape, 0)
        qts.append(jnp.where(row < DIFF_QK_DIM, qt, jnp.zeros_like(qt)))
        qts.append(jnp.where(row >= DIFF_QK_DIM, qt, jnp.zeros_like(qt)))
    qts += [mq_ref[0, :, h * LANES:(h + 1) * LANES].T for h in range(MLA_HEADS)]

    def scores(j, c):
        rows = pl.ds(pl.multiple_of(j * t, t), t)
        if c < n_diff:
            k = dk_ref[0, rows, (c // 2) * LANES:(c // 2 + 1) * LANES]
        else:
            k = mk_ref[0, rows, (c - n_diff) * LANES:(c - n_diff + 1) * LANES]
        return jnp.dot(k, qts[c], preferred_element_type=F32)

    def values(j, c):
        cols = pl.ds(pl.multiple_of(j * t, t), t)
        if c < n_diff:
            return dvt_ref[(c // 2) * d_rows:(c // 2 + 1) * d_rows, cols]
        return mvt_ref[(c - n_diff) * m_rows:(c - n_diff + 1) * m_rows, cols]

    def acc_at(c):
        return dacc_ref.at[c] if c < n_diff else macc_ref.at[c - n_diff]

    def step(j, bias_idx, has_next):
        hidden = None
        if bias_idx == 0:
            key = lax.broadcasted_iota(jnp.int32, (t, t), 0)
            qry = lax.broadcasted_iota(jnp.int32, (t, t), 1)
            hidden = jnp.where(key <= qry, 0.0, -NEG_BIG)

        def adjust(c, st):
            if c < n_diff:
                return st if bias_idx is None else st - bias_ref[c // 2, bias_idx]
            return st if hidden is None else st - hidden

        pending = [pend_ref[c] for c in range(AHEAD)]
        for c in range(n_chains):
            nxt = c + AHEAD
            if nxt < n_chains:
                pending.append(scores(j, nxt))
            elif has_next:
                pend_ref[nxt - n_chains] = scores(j + 1, nxt - n_chains)
            p, alpha = _softmax_probs(adjust(c, pending.pop(0)), m_ref.at[c])
            _acc_update(acc_at(c), alpha, values(j, c), p)

    def far(j, carry):
        step(j, None, True)
        return carry

    for c in range(n_chains):
        _softmax_init(m_ref.at[c], acc_at(c))
    for c in range(AHEAD):
        pend_ref[c] = scores(0, c)
    lax.fori_loop(0, jnp.maximum(i - 1, 0), far, 0)

    @pl.when(i >= 1)
    def _():
        step(i - 1, 1, True)

    step(i, 0, False)

    lv = lam_ref[...]
    lam = (jnp.exp(jnp.sum(lv[0:1] * lv[1:2], keepdims=True)) - jnp.exp(jnp.sum(lv[2:3] * lv[3:4], keepdims=True))
           + lam_init_ref[0])
    for h in range(DIFF_HEADS):
        c0, c1 = 2 * h, 2 * h + 1
        ot = (_normalized(dacc_ref.at[c0], DIFF_V_DIM)
              - lam * _normalized(dacc_ref.at[c1], DIFF_V_DIM))
        ot = ot * lax.rsqrt(jnp.mean(ot * ot, axis=0, keepdims=True) + NORM_EPS) * g_ref[...]
        o_ref[0, :, h * LANES:(h + 1) * LANES] = (ot * (1.0 - lam_init_ref[0])).T.astype(BF16)
    base = DIFF_HEADS * LANES
    for u in range(MLA_HEADS // 2):
        ot = jnp.concatenate([_normalized(macc_ref.at[2 * u], MLA_V_DIM),
                              _normalized(macc_ref.at[2 * u + 1], MLA_V_DIM)], axis=0)
        o_ref[0, :, base + u * LANES:base + (u + 1) * LANES] = ot.T.astype(BF16)


def _attention(lam_init, diff_lambda, g, bias, dq, dk, dv, mq, mk, mv, expert_w=None, expert_layer=0):
    bsz, s, wd = dq.shape
    wq, wv = mq.shape[-1], mv.shape[-1]
    t = ATTN_TILE
    assert MAX_DISTANCE <= t and s % t == 0
    n_diff = 2 * DIFF_HEADS
    n_chains = n_diff + MLA_HEADS
    per_b = s // t
    qtile = lambda w: pl.BlockSpec((1, t, w), lambda b, i: (b, i, 0))
    whole = lambda w: pl.BlockSpec((1, s, w), lambda b, i: (b, 0, 0))
    args = [lam_init, diff_lambda, g, bias, dq, dk, dv, mq, mk, mv]
    in_specs = [pl.BlockSpec(memory_space=pltpu.SMEM),
                pl.BlockSpec(diff_lambda.shape, lambda b, i: (0, 0)),
                pl.BlockSpec(g.shape, lambda b, i: (0, 0)),
                pl.BlockSpec(bias.shape, lambda b, i: (0, 0, 0, 0)),
                qtile(wd), whole(wd), whole(wd), qtile(wq), whole(wq), whole(wv)]
    out_specs = [qtile(wd + wv)]
    out_shape = [jax.ShapeDtypeStruct((bsz, s, wd + wv), BF16)]
    if expert_w is not None:
        steps = bsz * per_b
        n_lay, n_e, d, d_ff = expert_w[0].shape
        for w in expert_w:
            rows, cols = n_e * w.shape[2], w.shape[3]
            assert rows % (BF16_SUBLANES * steps) == 0
            slab = rows // steps
            args.append(w.reshape(n_lay * rows, cols))
            in_specs.append(pl.BlockSpec((slab, cols), lambda b, i: (expert_layer * steps + b * per_b + i, 0)))
            out_specs.append(pl.BlockSpec((slab, cols), lambda b, i: (b * per_b + i, 0)))
            out_shape.append(jax.ShapeDtypeStruct((rows, cols), BF16))
    outs = pl.pallas_call(
        functools.partial(_attn_kernel, cast_weights=expert_w is not None),
        grid=(bsz, per_b),
        in_specs=in_specs,
        out_specs=out_specs,
        out_shape=out_shape,
        scratch_shapes=[pltpu.VMEM((DIFF_HEADS * (DIFF_V_DIM + ONES_ROWS), s), BF16),
                        pltpu.VMEM((MLA_HEADS * (MLA_V_DIM + ONES_ROWS), s), BF16),
                        pltpu.VMEM((n_chains, 1, t), F32),
                        pltpu.VMEM((n_diff, DIFF_V_DIM + ONES_ROWS, t), F32),
                        pltpu.VMEM((MLA_HEADS, MLA_V_DIM + ONES_ROWS, t), F32),
                        pltpu.VMEM((AHEAD, t, t), F32)],
        compiler_params=_params("arbitrary", "arbitrary"),
        name="attention",
    )(*args)
    if expert_w is None:
        return outs[0]
    return outs[0], tuple(wb.reshape(w.shape[1:]) for wb, w in zip(outs[1:], expert_w))


def _post_attn_kernel(o_ref, x_ref, mod_ref, g_ref, wo_ref, wr_ref, x1_ref, h_ref, gates_ref, sel_ref, cnt_ref):
    @pl.when((pl.program_id(0) == 0) & (pl.program_id(1) == 0))
    def _():
        cnt_ref[...] = jnp.zeros(cnt_ref.shape, F32)

    gt_a, sh_f, sc_f = mod_ref[0, 2:3, :], mod_ref[0, 3:4, :], mod_ref[0, 4:5, :]
    tm = x_ref.shape[1]
    n = tm // POST_ROW_GROUPS
    for r0 in range(0, tm, n):
        rows = slice(r0, r0 + n)
        y = jnp.dot(o_ref[0, rows, :], wo_ref[...], preferred_element_type=F32)
        x1 = x_ref[0, rows, :] + (1.0 + gt_a) * y
        x1_ref[0, rows, :] = x1
        h = _rms(x1, g_ref[...]) * (1.0 + sc_f) + sh_f
        _rows_to_tiles(h, h_ref.at[0, pl.ds(r0 * ROW_SUB, n * ROW_SUB)])
        wr = wr_ref[...]
        h_hi, wr_hi = h.astype(BF16), wr.astype(BF16)
        h_lo, wr_lo = (h - h_hi.astype(F32)).astype(BF16), (wr - wr_hi.astype(F32)).astype(BF16)
        logits = (jnp.dot(h_hi, wr_hi, preferred_element_type=F32)
                  + jnp.dot(h_hi, wr_lo, preferred_element_type=F32)
                  + jnp.dot(h_lo, wr_hi, preferred_element_type=F32))
        lane = lax.broadcasted_iota(jnp.int32, logits.shape, 1)
        logits = jnp.where(lane < N_EXPERTS, logits, -jnp.inf)
        v1 = jnp.max(logits, axis=1, keepdims=True)
        i1 = jnp.min(jnp.where(logits == v1, lane, LANES), axis=1, keepdims=True)
        rest = jnp.where(lane == i1, -jnp.inf, logits)
        v2 = jnp.max(rest, axis=1, keepdims=True)
        i2 = jnp.min(jnp.where(rest == v2, lane, LANES), axis=1, keepdims=True)
        e2 = jnp.exp(v2 - v1)
        w1 = 1.0 / (1.0 + e2)
        w2 = e2 / (1.0 + e2)
        gates_ref[0, rows, :] = jnp.where(lane == i1, w1, 0.0) + jnp.where(lane == i2, w2, 0.0)
        sel_ref[0, rows, :] = jnp.where(lane == i1, 1.0, 0.0) + jnp.where(lane == i2, 2.0, 0.0)
        chosen = jnp.where((lane == i1) | (lane == i2), 1.0, 0.0)
        cnt_ref[...] += jnp.sum(chosen, axis=0, keepdims=True)


def _post_attn(o, x, mod, g, wo, w_router, layer):
    bsz, s, d = x.shape
    assert d == ROW_SUB * LANES
    tm = TOK_TILE
    tok = lambda w: pl.BlockSpec((1, tm, w), lambda b, i: (b, i, 0))
    full = lambda a: pl.BlockSpec(a.shape, lambda b, i: (0,) * a.ndim)
    h_spec = pl.BlockSpec((1, tm * ROW_SUB, LANES), lambda b, i: (b, i, 0))
    return pl.pallas_call(
        _post_attn_kernel,
        grid=(bsz, s // tm),
        in_specs=[tok(o.shape[-1]), tok(d), pl.BlockSpec((1, ADA_CHUNKS, d), lambda b, i: (b, 0, 0)),
                  full(g), _layer_spec(wo, layer), full(w_router)],
        out_specs=[tok(d), h_spec, tok(LANES), tok(LANES), pl.BlockSpec((8, LANES), lambda b, i: (0, 0))],
        out_shape=[jax.ShapeDtypeStruct((bsz, s, d), F32), jax.ShapeDtypeStruct((bsz, s * ROW_SUB, LANES), F32),
                   jax.ShapeDtypeStruct((bsz, s, LANES), F32), jax.ShapeDtypeStruct((bsz, s, LANES), F32),
                   jax.ShapeDtypeStruct((8, LANES), F32)],
        compiler_params=_params("arbitrary", "arbitrary"),
        name="post_attn_moe",
    )(o, x, mod, g, wo, w_router)


def _swiglu(h, w1_ref, w3_ref, w2_ref):
    d_ff = w1_ref.shape[-1]
    y = jnp.zeros((h.shape[0], w2_ref.shape[-1]), F32)
    for c0 in range(0, d_ff, FF_CHUNK):
        a = jnp.dot(h, w1_ref[:, c0:c0 + FF_CHUNK], preferred_element_type=F32)
        b = jnp.dot(h, w3_ref[:, c0:c0 + FF_CHUNK], preferred_element_type=F32)
        u = (a * jax.nn.sigmoid(a) * b).astype(BF16)
        y = y + jnp.dot(u, w2_ref[c0:c0 + FF_CHUNK, :], preferred_element_type=F32)
    return y


def _dense_layer_kernel(o_ref, x_ref, mod_ref, g_ref, wo_ref, w1_ref, w3_ref, w2_ref, out_ref):
    gt_a, sh_f, sc_f, gt_f = (mod_ref[0, j:j + 1, :] for j in (2, 3, 4, 5))
    tm = x_ref.shape[1]
    n = tm // POST_ROW_GROUPS
    x1s, hs = [], []
    for r0 in range(0, tm, n):
        rows = slice(r0, r0 + n)
        y = jnp.dot(o_ref[0, rows, :], wo_ref[...], preferred_element_type=F32)
        x1 = x_ref[0, rows, :] + (1.0 + gt_a) * y
        x1s.append(x1)
        hs.append((_rms(x1, g_ref[...]) * (1.0 + sc_f) + sh_f).astype(BF16))
    x1, h = jnp.concatenate(x1s, axis=0), jnp.concatenate(hs, axis=0)
    out_ref[0] = x1 + (1.0 + gt_f) * _swiglu(h, w1_ref, w3_ref, w2_ref)


def _dense_layer(o, x, mod, g, wo, w1, w3, w2, layer, ffn_layer):
    bsz, s, d = x.shape
    tm = FFN_TILE
    tok = lambda w: pl.BlockSpec((1, tm, w), lambda b, i: (b, i, 0))
    full = lambda a: pl.BlockSpec(a.shape, lambda b, i: (0,) * a.ndim)
    one = lambda a: _layer_spec(a, ffn_layer)
    return pl.pallas_call(
        _dense_layer_kernel,
        grid=(bsz, s // tm),
        in_specs=[tok(o.shape[-1]), tok(d), pl.BlockSpec((1, ADA_CHUNKS, d), lambda b, i: (b, 0, 0)),
                  full(g), _layer_spec(wo, layer), one(w1), one(w3), one(w2)],
        out_specs=tok(d),
        out_shape=jax.ShapeDtypeStruct(x.shape, F32),
        compiler_params=_params("arbitrary", "arbitrary"),
        name="dense_layer",
    )(o, x, mod, g, wo, w1, w3, w2)


def _route_kernel(cnt_ref, sel_ref, pos_ref, te_ref, off_ref, run_ref, *, row_tile):
    t = pl.program_id(0)
    sel_t = sel_ref[...].T
    chosen = (sel_t > 0.0).astype(F32)
    per_expert = jnp.sum(chosen, axis=1, keepdims=True)

    @pl.when(t == 0)
    def _():
        cnt = cnt_ref[...].T[:, 0:1]
        padded = jnp.ceil(cnt / row_tile) * row_tile
        row = lax.broadcasted_iota(jnp.int32, cnt.shape, 0)
        off = jnp.zeros(cnt.shape, F32)
        for e in range(N_EXPERTS):
            size_e = jnp.sum(jnp.where(row == e, padded, 0.0), keepdims=True)
            off = off + jnp.where(row > e, size_e, 0.0)
        off_ref[...] = off
        run_ref[...] = jnp.zeros(run_ref.shape, F32)
        ends = off + padded
        tile_start = lax.broadcasted_iota(jnp.int32, (LANES, LANES), 1).astype(F32) * row_tile
        erow = lax.broadcasted_iota(jnp.int32, (LANES, LANES), 0)
        done = jnp.where((erow < N_EXPERTS) & (ends <= tile_start), 1.0, 0.0)
        te = jnp.sum(done, axis=0, keepdims=True).astype(jnp.int32)
        te_ref[...] = jnp.broadcast_to(te, te_ref.shape)

    tm = sel_t.shape[1]
    before = (lax.broadcasted_iota(jnp.int32, (tm, tm), 0)
              < lax.broadcasted_iota(jnp.int32, (tm, tm), 1)).astype(BF16)
    rank = jnp.dot(chosen.astype(BF16), before, preferred_element_type=F32) + run_ref[...]
    base = off_ref[...] + rank
    for k in range(2):
        pos = jnp.sum(jnp.where(sel_t == float(k + 1), base, 0.0), axis=0, keepdims=True)
        pos_ref[0, :, k * tm:(k + 1) * tm] = pos.astype(jnp.int32)
    run_ref[...] += per_expert


def _route(cnt, sel, row_tile, n_row_tiles):
    n_tok = sel.shape[0]
    assert n_row_tiles <= LANES and 2 * n_tok < 2 ** 24
    tm = MOE_TILE
    nt = n_tok // tm
    pos, te = pl.pallas_call(
        functools.partial(_route_kernel, row_tile=row_tile),
        grid=(nt,),
        in_specs=[pl.BlockSpec(cnt.shape, lambda t: (0, 0)), pl.BlockSpec((tm, LANES), lambda t: (t, 0))],
        out_specs=[pl.BlockSpec((1, 1, 2 * tm), lambda t: (t, 0, 0)), pl.BlockSpec((8, LANES), lambda t: (0, 0))],
        out_shape=[jax.ShapeDtypeStruct((nt, 1, 2 * tm), jnp.int32), jax.ShapeDtypeStruct((8, LANES), jnp.int32)],
        scratch_shapes=[pltpu.VMEM((LANES, 1), F32)] * 2,
        compiler_params=_params("arbitrary"),
        name="moe_route",
    )(cnt, sel)
    return pos, te[0]


def _dispatch_kernel(pos_ref, h_ref, xs_in_ref, xs_ref, sem):
    del xs_in_ref
    tm = h_ref.shape[0] // ROW_SUB

    def body(r, carry):
        for k in range(2):
            row = pos_ref[0, 0, k * tm + r]
            pltpu.make_async_copy(_tile_row(h_ref, r), _tile_row(xs_ref, row), sem).start(priority=k)
        return carry

    lax.fori_loop(0, tm, body, 0, unroll=8)
    for k in range(2):
        pltpu.make_async_copy(h_ref, xs_ref.at[pl.ds(0, tm * ROW_SUB)], sem).wait()


def _dispatch(pos, h, xs0):
    nt, tm = pos.shape[0], pos.shape[2] // 2
    return pl.pallas_call(
        _dispatch_kernel,
        grid=(nt,),
        in_specs=[pl.BlockSpec((1, 1, 2 * tm), lambda t: (t, 0, 0), memory_space=pltpu.SMEM),
                  pl.BlockSpec((tm * ROW_SUB, LANES), lambda t: (t, 0)),
                  pl.BlockSpec(memory_space=pl.ANY)],
        out_specs=pl.BlockSpec(memory_space=pl.ANY),
        out_shape=jax.ShapeDtypeStruct(xs0.shape, xs0.dtype),
        scratch_shapes=[pltpu.SemaphoreType.DMA(())],
        input_output_aliases={2: 0},
        compiler_params=_params("arbitrary"),
        name="moe_dispatch",
    )(pos, h, xs0)


def _expert_kernel(te_ref, xs_ref, w1_ref, w3_ref, w2_ref, y_ref):
    used = te_ref[pl.program_id(0)] < N_EXPERTS

    @pl.when(used)
    def _():
        y = _swiglu(_tiles_to_rows(xs_ref).astype(BF16), w1_ref.at[0], w3_ref.at[0], w2_ref.at[0])
        _rows_to_tiles(y, y_ref)

    @pl.when(jnp.logical_not(used))
    def _():
        y_ref[...] = jnp.zeros(y_ref.shape, y_ref.dtype)


def _experts(te, xs, w1, w3, w2, row_tile):
    n_rows = xs.shape[0] // ROW_SUB
    _, d, d_ff = w1.shape
    expert = lambda n, te: (jnp.minimum(te[n], N_EXPERTS - 1), 0, 0)
    rows = pl.BlockSpec((row_tile * ROW_SUB, LANES), lambda n, te: (n, 0))
    return pl.pallas_call(
        _expert_kernel,
        grid_spec=pltpu.PrefetchScalarGridSpec(
            num_scalar_prefetch=1,
            grid=(n_rows // row_tile,),
            in_specs=[rows, pl.BlockSpec((1, d, d_ff), expert), pl.BlockSpec((1, d, d_ff), expert),
                      pl.BlockSpec((1, d_ff, d), expert)],
            out_specs=rows),
        out_shape=jax.ShapeDtypeStruct(xs.shape, F32),
        compiler_params=_params("arbitrary"),
        name="moe_experts",
    )(te, xs, w1, w3, w2)


def _combine_kernel(pos_ref, pos_next_ref, x1_ref, gates_ref, sel_ref, mod_ref, gfin_ref, ys_ref, o_ref,
                    ybuf_ref, sem, *, final):
    tm = x1_ref.shape[1]
    table_tm = pos_ref.shape[2] // 2
    step = pl.program_id(0) * pl.num_programs(1) + pl.program_id(1)
    n_steps = pl.num_programs(0) * pl.num_programs(1)
    slot = lax.rem(step, 2)

    def gather(p_ref, s, into):
        first = lax.rem(s, table_tm // tm) * tm

        def body(r, carry):
            for k in range(2):
                row = p_ref[0, 0, k * table_tm + first + r]
                pltpu.make_async_copy(_tile_row(ys_ref, row), _tile_row(ybuf_ref.at[into, k], r),
                                      sem.at[into]).start(priority=k)
            return carry

        lax.fori_loop(0, tm, body, 0, unroll=8)

    @pl.when(step == 0)
    def _():
        gather(pos_ref, step, slot)

    @pl.when(step + 1 < n_steps)
    def _():
        gather(pos_next_ref, step + 1, 1 - slot)

    gates, sel = gates_ref[0], sel_ref[0]
    w_a = jnp.sum(jnp.where(sel == 1.0, gates, 0.0), axis=1, keepdims=True)
    w_b = jnp.sum(jnp.where(sel == 2.0, gates, 0.0), axis=1, keepdims=True)
    for k in range(2):
        pltpu.make_async_copy(ys_ref.at[pl.ds(0, tm * ROW_SUB)], ybuf_ref.at[slot, k], sem.at[slot]).wait()
    gt_f = mod_ref[0, 5:6, :]
    y = w_a * _tiles_to_rows(ybuf_ref.at[slot, 0]) + w_b * _tiles_to_rows(ybuf_ref.at[slot, 1])
    x = x1_ref[0] + (1.0 + gt_f) * y
    o_ref[0] = _rms(x, gfin_ref[...]) if final else x


def _combine(pos, x1, gates, sel, mod, ys, g_final=None):
    bsz, s, d = x1.shape
    nt, table_tm = pos.shape[0], pos.shape[2] // 2
    tm = COMBINE_TILE
    ratio = table_tm // tm
    assert ratio * tm == table_tm and s % tm == 0
    per_b = s // tm
    tok = lambda w: pl.BlockSpec((1, tm, w), lambda b, i: (b, i, 0))
    final = g_final is not None
    gfin = g_final if final else jnp.ones((1, d), F32)
    table = lambda ahead: pl.BlockSpec(
        (1, 1, 2 * table_tm), lambda b, i: (jnp.minimum((b * per_b + i + ahead) // ratio, nt - 1), 0, 0),
        memory_space=pltpu.SMEM)
    return pl.pallas_call(
        functools.partial(_combine_kernel, final=final),
        grid=(bsz, per_b),
        in_specs=[table(0), table(1),
                  tok(d), tok(LANES), tok(LANES),
                  pl.BlockSpec((1, ADA_CHUNKS, d), lambda b, i: (b, 0, 0)),
                  pl.BlockSpec((1, d), lambda b, i: (0, 0)),
                  pl.BlockSpec(memory_space=pl.ANY)],
        out_specs=tok(d),
        out_shape=jax.ShapeDtypeStruct(x1.shape, F32),
        scratch_shapes=[pltpu.VMEM((2, 2, tm * ROW_SUB, LANES), F32), pltpu.SemaphoreType.DMA((2,))],
        compiler_params=_params("arbitrary", "arbitrary"),
        name="moe_combine",
    )(pos, pos, x1, gates, sel, mod, gfin, ys)


def _moe(h, x1, gates, sel, cnt, mod, w1, w3, w2, xs_buf, g_final):
    bsz, s, d = x1.shape
    n_tok = bsz * s
    row_tile = FFN_TILE
    n_rows = xs_buf.shape[0] // ROW_SUB
    assert n_rows == _moe_rows(n_tok)
    pos, te = _route(cnt, sel.reshape(n_tok, LANES), row_tile, n_rows // row_tile)
    xs = _dispatch(pos, h.reshape(n_tok * ROW_SUB, LANES), xs_buf)
    ys = _experts(te, xs, w1, w3, w2, row_tile)
    return _combine(pos, x1, gates, sel, mod, ys, g_final), xs


def _moe_rows(n_tok):
    return 2 * n_tok + N_EXPERTS * FFN_TILE


def _final_norm_kernel(x_ref, g_ref, o_ref):
    o_ref[0] = _rms(x_ref[0], g_ref[...])


def _final_norm(x, g):
    bsz, s, d = x.shape
    tm = TOK_TILE
    tok = pl.BlockSpec((1, tm, d), lambda b, i: (b, i, 0))
    return pl.pallas_call(
        _final_norm_kernel,
        grid=(bsz, s // tm),
        in_specs=[tok, pl.BlockSpec(g.shape, lambda b, i: (0, 0))],
        out_specs=tok,
        out_shape=jax.ShapeDtypeStruct(x.shape, F32),
        compiler_params=_params("arbitrary", "arbitrary"),
        name="final_norm",
    )(x, g)


def _head_blocks(w, width, nope_cols, rope_cols):
    lead = w.shape[:-1]
    w = w.reshape(lead + (MLA_HEADS, width))
    nope = w[..., nope_cols] if nope_cols is not None else jnp.zeros(lead + (MLA_HEADS, MLA_NOPE_DIM), w.dtype)
    rope = w[..., rope_cols] if rope_cols is not None else jnp.zeros(lead + (MLA_HEADS, MLA_ROPE_DIM), w.dtype)
    return _head_block(nope, rope).reshape(lead + (MLA_HEADS * LANES,))


def _prep_w_in(w_in):
    kr_block = _head_block(jnp.zeros(w_in.shape[:-1] + (MLA_NOPE_DIM,), w_in.dtype), w_in[..., C_KR:])
    return jnp.concatenate([w_in[..., :C_KR], kr_block], axis=-1).astype(BF16)


def _layer_spec(a, layer):
    return pl.BlockSpec((None,) + a.shape[1:], lambda b, i: (layer,) + (0,) * (a.ndim - 1))


def kernel(x, c, positions, w_ada, b_ada, g_attn, w_in, diff_lambda, diff_subln_g, rel_bias, mla_q_norm, w_uq, mla_kv_norm, w_ukv, w_o, g_ffn, ffn_w1, ffn_w3, ffn_w2, moe_router, moe_w1, moe_w3, moe_w2, g_final):
    depth = w_ada.shape[0]
    bsz, s, d = x.shape
    mods = _ada(c, w_ada, b_ada).reshape(depth, bsz, ADA_CHUNKS, d)
    rc, rs = _rope_tables(positions)
    bias = _bias_tiles(rel_bias)
    qk_w = MLA_NOPE_DIM + MLA_ROPE_DIM
    kv_w = MLA_NOPE_DIM + MLA_V_DIM
    xs_buf = jnp.zeros((_moe_rows(bsz * s) * ROW_SUB, LANES), F32)
    ffn_w = (ffn_w1.astype(BF16), ffn_w3.astype(BF16), ffn_w2.astype(BF16))
    wo = w_o.astype(BF16)
    w_in_p = _prep_w_in(w_in)
    w_uq_p = _head_blocks(w_uq, qk_w, slice(0, MLA_NOPE_DIM), slice(MLA_NOPE_DIM, qk_w)).astype(BF16)
    w_uk_p = _head_blocks(w_ukv, kv_w, slice(0, MLA_NOPE_DIM), None).astype(BF16)
    w_uv_p = w_ukv.reshape(depth, MLA_KV_RANK, MLA_HEADS, kv_w)[..., MLA_NOPE_DIM:].reshape(
        depth, MLA_KV_RANK, MLA_HEADS * MLA_V_DIM).astype(BF16)
    for l in range(depth):
        mod = mods[l]
        lam_init = jnp.full((1,), 0.8 - 0.6 * math.exp(-0.3 * l), F32)
        dq, dk, dv, mq, mk, mv = _pre_attn(
            x, mod, g_attn[l].reshape(1, d), w_in_p, mla_q_norm[l].reshape(1, -1), w_uq_p,
            mla_kv_norm[l].reshape(1, -1), w_uk_p, w_uv_p, rc, rs, l)
        attn_args = (lam_init, diff_lambda[l], diff_subln_g[l].reshape(-1, 1), bias, dq, dk, dv, mq, mk, mv)
        g_f = g_ffn[l].reshape(1, d)
        if l % 2 == 1:
            o, expert_w = _attention(*attn_args, expert_w=(moe_w1, moe_w3, moe_w2), expert_layer=l // 2)
            w_router = jnp.pad(moe_router[l // 2], ((0, 0), (0, LANES - N_EXPERTS)))
            x1, h, gates, sel, cnt = _post_attn(o, x, mod, g_f, wo, w_router, l)
            g_fin = g_final.reshape(1, d) if l == depth - 1 else None
            x, xs_buf = _moe(h, x1, gates, sel, cnt, mod, *expert_w, xs_buf, g_fin)
        else:
            o = _attention(*attn_args)
            x = _dense_layer(o, x, mod, g_f, wo, *ffn_w, l, l // 2)
    return x if depth % 2 == 0 else _final_norm(x, g_final.reshape(1, d))
```

```python
import functools
import math

import jax
import jax.numpy as jnp
from jax import lax
from jax.experimental import pallas as pl
from jax.experimental.pallas import tpu as pltpu

F32 = jnp.float32
BF16 = jnp.bfloat16

DIFF_HEADS = 4
DIFF_QK_DIM = 64
DIFF_V_DIM = 128
MLA_HEADS = 8
MLA_NOPE_DIM = 64
MLA_ROPE_DIM = 32
MLA_V_DIM = 64
MLA_Q_RANK = 384
MLA_KV_RANK = 256
ROPE_THETA = 10000.0
N_BUCKETS = 32
MAX_EXACT = 16
MAX_DISTANCE = 128
N_EXPERTS = 8
NORM_EPS = 1e-6
ADA_CHUNKS = 6

LANES = 128
BF16_SUBLANES = 16
NEG_BIG = -1e30
LOG2E = math.log2(math.e)
VMEM_LIMIT = 56 * 1024 * 1024

ATTN_TILE = 256
AHEAD = 4
TOK_TILE = 512
PRE_TILE = 1024
MOE_TILE = 1024
COMBINE_TILE = 512
PRE_ROW_GROUPS = 2
POST_ROW_GROUPS = 2
FFN_TILE = 512
DENSE_TILE = 1024
FF_CHUNK = 256
ADA_COL_TILE = 1536

C_DQ, C_DK, C_DV, C_MQ, C_KV, C_KR, C_END = 0, 512, 1024, 1536, 1920, 2176, 2304


def _params(*sem):
    return pltpu.CompilerParams(dimension_semantics=sem, vmem_limit_bytes=VMEM_LIMIT)


def _rms(x, g):
    return x * lax.rsqrt(jnp.mean(x * x, axis=-1, keepdims=True) + NORM_EPS) * g


ROW_SUB = 8


def _rows_to_tiles(x, ref):
    n = x.shape[0]
    for j in range(ROW_SUB):
        ref[pl.ds(j, n, stride=ROW_SUB), :] = x[:, j * LANES:(j + 1) * LANES]


def _tiles_to_rows(ref):
    n = ref.shape[0] // ROW_SUB
    return jnp.concatenate([ref[pl.ds(j, n, stride=ROW_SUB), :] for j in range(ROW_SUB)], axis=1)


def _tile_row(ref, r):
    return ref.at[pl.ds(pl.multiple_of(r * ROW_SUB, ROW_SUB), ROW_SUB)]


def _ada_kernel(c_ref, w_ref, b_ref, o_ref):
    c = c_ref[...]
    cond = c * jax.nn.sigmoid(c)
    o_ref[0] = jnp.dot(cond, w_ref[0], preferred_element_type=F32,
                       precision=lax.Precision.HIGHEST) + b_ref[0]


def _ada(c, w_ada, b_ada):
    depth, d, n = w_ada.shape
    bsz = c.shape[0]
    tn = ADA_COL_TILE
    return pl.pallas_call(
        _ada_kernel,
        grid=(depth, n // tn),
        in_specs=[pl.BlockSpec((bsz, d), lambda l, j: (0, 0)),
                  pl.BlockSpec((1, d, tn), lambda l, j: (l, 0, j)),
                  pl.BlockSpec((1, 1, tn), lambda l, j: (l, 0, j))],
        out_specs=pl.BlockSpec((1, bsz, tn), lambda l, j: (l, 0, j)),
        out_shape=jax.ShapeDtypeStruct((depth, bsz, n), F32),
        compiler_params=_params("arbitrary", "arbitrary"),
        name="ada_mod",
    )(c, w_ada, b_ada.reshape(depth, 1, n))


ROPE_HALF = MLA_ROPE_DIM // 2
NOPE_SPLIT = LANES // 2 - ROPE_HALF


def _head_block(nope, rope):
    pad = jnp.zeros(nope.shape[:-1] + (LANES - MLA_NOPE_DIM - MLA_ROPE_DIM,), nope.dtype)
    return jnp.concatenate([rope[..., :ROPE_HALF], nope[..., :NOPE_SPLIT], rope[..., ROPE_HALF:],
                            nope[..., NOPE_SPLIT:], pad], axis=-1)


def _rope_tab_kernel(pos_ref, inv_ref, c_ref, s_ref):
    pos = pos_ref[0].astype(F32)
    inv = inv_ref[...]
    ang = pos * inv
    lane = lax.broadcasted_iota(jnp.int32, ang.shape, 1)
    is_rope = inv != 0.0
    used = lane < MLA_NOPE_DIM + MLA_ROPE_DIM
    c_ref[0] = jnp.where(is_rope, jnp.cos(ang), jnp.where(used, 1.0, 0.0))
    s_ref[0] = jnp.where(is_rope, jnp.where(lane < LANES // 2, -1.0, 1.0) * jnp.sin(ang), 0.0)


def _rope_tables(positions):
    bsz, s = positions.shape
    inv_freq = ROPE_THETA ** (-jnp.arange(ROPE_HALF, dtype=F32) / ROPE_HALF)
    inv_lane = _head_block(jnp.zeros((MLA_NOPE_DIM,), F32), jnp.concatenate([inv_freq, inv_freq])).reshape(1, LANES)
    tm = PRE_TILE
    spec = pl.BlockSpec((1, tm, LANES), lambda b, i: (b, i, 0))
    shape = jax.ShapeDtypeStruct((bsz, s, LANES), F32)
    return pl.pallas_call(
        _rope_tab_kernel,
        grid=(bsz, s // tm),
        in_specs=[pl.BlockSpec((1, tm, 1), lambda b, i: (b, i, 0)),
                  pl.BlockSpec((1, LANES), lambda b, i: (0, 0))],
        out_specs=[spec, spec],
        out_shape=[shape, shape],
        compiler_params=_params("arbitrary", "arbitrary"),
        name="rope_tables",
    )(positions.reshape(bsz, s, 1), inv_lane)


def _bias_tile_kernel(rb_ref, o_ref):
    h, d = pl.program_id(0), pl.program_id(1)
    t = o_ref.shape[-1]
    key = lax.broadcasted_iota(jnp.int32, (t, t), 0)
    qry = lax.broadcasted_iota(jnp.int32, (t, t), 1)
    dist = d * t + qry - key
    n = jnp.maximum(dist, 0)
    nf = jnp.maximum(n, 1).astype(F32)
    large = MAX_EXACT + (jnp.log(nf / MAX_EXACT) / math.log(MAX_DISTANCE / MAX_EXACT)
                         * (N_BUCKETS - MAX_EXACT)).astype(jnp.int32)
    large = jnp.minimum(large, N_BUCKETS - 1)
    bucket = jnp.where(n < MAX_EXACT, n, large)
    val = jnp.zeros((t, t), F32)
    for j in range(N_BUCKETS):
        val = jnp.where(bucket == j, rb_ref[j, h], val)
    val = (rb_ref[N_BUCKETS - 1, h] - val) * LOG2E
    o_ref[0, 0] = jnp.where(dist < 0, -NEG_BIG, val)


def _bias_tiles(rel_bias):
    t = ATTN_TILE
    return pl.pallas_call(
        _bias_tile_kernel,
        grid=(DIFF_HEADS, 2),
        in_specs=[pl.BlockSpec(memory_space=pltpu.SMEM)],
        out_specs=pl.BlockSpec((1, 1, t, t), lambda h, d: (h, d, 0, 0)),
        out_shape=jax.ShapeDtypeStruct((DIFF_HEADS, 2, t, t), F32),
        compiler_params=_params("arbitrary", "arbitrary"),
        name="bias_tiles",
    )(rel_bias)


def _pre_attn_kernel(x_ref, mod_ref, g_ref, w_in_ref, gq_ref, w_uq_ref, gkv_ref, w_uk_ref, w_uv_ref,
                     rc_ref, rs_ref,
                     dq_ref, dk_ref, dv_ref, mq_ref, mk_ref, mv_ref):
    tm = x_ref.shape[1]
    for r0 in range(0, tm, tm // PRE_ROW_GROUPS):
        rows = slice(r0, r0 + tm // PRE_ROW_GROUPS)
        x = x_ref[0, rows, :]
        sh, sc = mod_ref[0, 0:1, :], mod_ref[0, 1:2, :]
        h = (_rms(x, g_ref[...]) * (1.0 + sc) + sh).astype(BF16)
        proj = jnp.dot(h, w_in_ref[...], preferred_element_type=F32)
        dq_ref[0, rows, :] = (proj[:, C_DQ:C_DK] * (DIFF_QK_DIM ** -0.5 * LOG2E)).astype(BF16)
        dk_ref[0, rows, :] = proj[:, C_DK:C_DV].astype(BF16)
        dv_ref[0, rows, :] = proj[:, C_DV:C_MQ].astype(BF16)

        rc, rs = rc_ref[0, rows, :], rs_ref[0, rows, :]

        def rope(v):
            return v * rc + pltpu.roll(v, LANES // 2, 1) * rs

        qn = _rms(proj[:, C_MQ:C_KV], gq_ref[...]).astype(BF16)
        q = jnp.dot(qn, w_uq_ref[...], preferred_element_type=F32)
        kvn = _rms(proj[:, C_KV:C_KR], gkv_ref[...]).astype(BF16)
        kn = jnp.dot(kvn, w_uk_ref[...], preferred_element_type=F32)
        mv_ref[0, rows, :] = jnp.dot(kvn, w_uv_ref[...], preferred_element_type=F32).astype(BF16)
        kr = rope(proj[:, C_KR:C_END])
        q_scale = (MLA_NOPE_DIM + MLA_ROPE_DIM) ** -0.5 * LOG2E
        for hd in range(MLA_HEADS):
            sl = slice(hd * LANES, (hd + 1) * LANES)
            mq_ref[0, rows, sl] = (rope(q[:, sl]) * q_scale).astype(BF16)
            mk_ref[0, rows, sl] = (kn[:, sl] + kr).astype(BF16)


def _pre_attn(x, mod, g, w_in, gq, w_uq, gkv, w_uk, w_uv, rc, rs, layer):
    bsz, s, d = x.shape
    lay = lambda a: _layer_spec(a, layer)
    tm = PRE_TILE
    tok = lambda w: pl.BlockSpec((1, tm, w), lambda b, i: (b, i, 0))
    full = lambda a: pl.BlockSpec(a.shape, lambda b, i: (0,) * a.ndim)
    widths = (512, 512, 512, MLA_HEADS * LANES, MLA_HEADS * LANES, MLA_HEADS * MLA_V_DIM)
    return pl.pallas_call(
        _pre_attn_kernel,
        grid=(bsz, s // tm),
        in_specs=[tok(d), pl.BlockSpec((1, ADA_CHUNKS, d), lambda b, i: (b, 0, 0)), full(g), lay(w_in),
                  full(gq), lay(w_uq), full(gkv), lay(w_uk), lay(w_uv), tok(LANES), tok(LANES)],
        out_specs=[tok(w) for w in widths],
        out_shape=[jax.ShapeDtypeStruct((bsz, s, w), BF16) for w in widths],
        compiler_params=_params("arbitrary", "arbitrary"),
        name="pre_attn",
    )(x, mod, g, w_in, gq, w_uq, gkv, w_uk, w_uv, rc, rs)


ONES_ROWS = BF16_SUBLANES


def _values_t(v, dv):
    vt = v.T
    ones = jnp.ones((ONES_ROWS, vt.shape[1]), vt.dtype)
    parts = []
    for h in range(vt.shape[0] // dv):
        parts += [vt[h * dv:(h + 1) * dv], ones]
    return jnp.concatenate(parts, axis=0)


def _softmax_init(m_ref, acc_ref):
    m_ref[...] = jnp.full(m_ref.shape, NEG_BIG, F32)
    acc_ref[...] = jnp.zeros(acc_ref.shape, F32)


def _softmax_probs(st, m_ref):
    m_prev = m_ref[...]
    m_new = jnp.maximum(m_prev, jnp.max(st, axis=0, keepdims=True))
    m_ref[...] = m_new
    return jnp.exp2(st - m_new).astype(BF16), jnp.exp2(m_prev - m_new)


def _acc_update(acc_ref, alpha, vt, p):
    acc_ref[...] = alpha * acc_ref[...] + jnp.dot(vt, p, preferred_element_type=F32)


def _normalized(acc_ref, dv):
    return acc_ref[:dv] / acc_ref[dv:dv + 1]


def _attn_kernel(*refs, cast_weights):
    (lam_init_ref, lam_ref, g_ref, bias_ref, dq_ref, dk_ref, dv_ref, mq_ref, mk_ref, mv_ref), refs = refs[:10], refs[10:]
    if cast_weights:
        for w_ref, wb_ref in zip(refs[:3], refs[4:7]):
            wb_ref[...] = w_ref[...].astype(BF16)
        refs = refs[3:4] + refs[7:]
    o_ref, dvt_ref, mvt_ref, m_ref, dacc_ref, macc_ref, pend_ref = refs
    i = pl.program_id(1)
    t = dq_ref.shape[1]
    n_diff = 2 * DIFF_HEADS
    n_chains = n_diff + MLA_HEADS
    d_rows, m_rows = DIFF_V_DIM + ONES_ROWS, MLA_V_DIM + ONES_ROWS

    @pl.when(i == 0)
    def _():
        dvt_ref[...] = _values_t(dv_ref[0], DIFF_V_DIM)
        mvt_ref[...] = _values_t(mv_ref[0], MLA_V_DIM)

    qts = []
    for h in range(DIFF_HEADS):
        qt = dq_ref[0, :, h * LANES:(h + 1) * LANES].T
        row = lax.broadcasted_iota(jnp.int32, qt.shape, 0)
        qts.append(jnp.where(row < DIFF_QK_DIM, qt, jnp.zeros_like(qt)))
        qts.append(jnp.where(row >= DIFF_QK_DIM, qt, jnp.zeros_like(qt)))
    qts += [mq_ref[0, :, h * LANES:(h + 1) * LANES].T for h in range(MLA_HEADS)]

    def scores(j, c):
        rows = pl.ds(pl.multiple_of(j * t, t), t)
        if c < n_diff:
            k = dk_ref[0, rows, (c // 2) * LANES:(c // 2 + 1) * LANES]
        else:
            k = mk_ref[0, rows, (c - n_diff) * LANES:(c - n_diff + 1) * LANES]
        return jnp.dot(k, qts[c], preferred_element_type=F32)

    def values(j, c):
        cols = pl.ds(pl.multiple_of(j * t, t), t)
        if c < n_diff:
            return dvt_ref[(c // 2) * d_rows:(c // 2 + 1) * d_rows, cols]
        return mvt_ref[(c - n_diff) * m_rows:(c - n_diff + 1) * m_rows, cols]

    def acc_at(c):
        return dacc_ref.at[c] if c < n_diff else macc_ref.at[c - n_diff]

    def step(j, bias_idx, has_next):
        hidden = None
        if bias_idx == 0:
            key = lax.broadcasted_iota(jnp.int32, (t, t), 0)
            qry = lax.broadcasted_iota(jnp.int32, (t, t), 1)
            hidden = jnp.where(key <= qry, 0.0, -NEG_BIG)

        def adjust(c, st):
            if c < n_diff:
                return st if bias_idx is None else st - bias_ref[c // 2, bias_idx]
            return st if hidden is None else st - hidden

        pending = [pend_ref[c] for c in range(AHEAD)]
        for c in range(n_chains):
            nxt = c + AHEAD
            if nxt < n_chains:
                pending.append(scores(j, nxt))
            elif has_next:
                pend_ref[nxt - n_chains] = scores(j + 1, nxt - n_chains)
            p, alpha = _softmax_probs(adjust(c, pending.pop(0)), m_ref.at[c])
            _acc_update(acc_at(c), alpha, values(j, c), p)

    def far(j, carry):
        step(j, None, True)
        return carry

    for c in range(n_chains):
        _softmax_init(m_ref.at[c], acc_at(c))
    for c in range(AHEAD):
        pend_ref[c] = scores(0, c)
    lax.fori_loop(0, jnp.maximum(i - 1, 0), far, 0)

    @pl.when(i >= 1)
    def _():
        step(i - 1, 1, True)

    step(i, 0, False)

    lv = lam_ref[...]
    lam = (jnp.exp(jnp.sum(lv[0:1] * lv[1:2], keepdims=True)) - jnp.exp(jnp.sum(lv[2:3] * lv[3:4], keepdims=True))
           + lam_init_ref[0])
    for h in range(DIFF_HEADS):
        c0, c1 = 2 * h, 2 * h + 1
        ot = (_normalized(dacc_ref.at[c0], DIFF_V_DIM)
              - lam * _normalized(dacc_ref.at[c1], DIFF_V_DIM))
        ot = ot * lax.rsqrt(jnp.mean(ot * ot, axis=0, keepdims=True) + NORM_EPS) * g_ref[...]
        o_ref[0, :, h * LANES:(h + 1) * LANES] = (ot * (1.0 - lam_init_ref[0])).T.astype(BF16)
    base = DIFF_HEADS * LANES
    for u in range(MLA_HEADS // 2):
        ot = jnp.concatenate([_normalized(macc_ref.at[2 * u], MLA_V_DIM),
                              _normalized(macc_ref.at[2 * u + 1], MLA_V_DIM)], axis=0)
        o_ref[0, :, base + u * LANES:base + (u + 1) * LANES] = ot.T.astype(BF16)


def _attention(lam_init, diff_lambda, g, bias, dq, dk, dv, mq, mk, mv, expert_w=None, expert_layer=0):
    bsz, s, wd = dq.shape
    wq, wv = mq.shape[-1], mv.shape[-1]
    t = ATTN_TILE
    assert MAX_DISTANCE <= t and s % t == 0
    n_diff = 2 * DIFF_HEADS
    n_chains = n_diff + MLA_HEADS
    per_b = s // t
    qtile = lambda w: pl.BlockSpec((1, t, w), lambda b, i: (b, i, 0))
    whole = lambda w: pl.BlockSpec((1, s, w), lambda b, i: (b, 0, 0))
    args = [lam_init, diff_lambda, g, bias, dq, dk, dv, mq, mk, mv]
    in_specs = [pl.BlockSpec(memory_space=pltpu.SMEM),
                pl.BlockSpec(diff_lambda.shape, lambda b, i: (0, 0)),
                pl.BlockSpec(g.shape, lambda b, i: (0, 0)),
                pl.BlockSpec(bias.shape, lambda b, i: (0, 0, 0, 0)),
                qtile(wd), whole(wd), whole(wd), qtile(wq), whole(wq), whole(wv)]
    out_specs = [qtile(wd + wv)]
    out_shape = [jax.ShapeDtypeStruct((bsz, s, wd + wv), BF16)]
    if expert_w is not None:
        steps = bsz * per_b
        n_lay, n_e, d, d_ff = expert_w[0].shape
        for w in expert_w:
            rows, cols = n_e * w.shape[2], w.shape[3]
            assert rows % (BF16_SUBLANES * steps) == 0
            slab = rows // steps
            args.append(w.reshape(n_lay * rows, cols))
            in_specs.append(pl.BlockSpec((slab, cols), lambda b, i: (expert_layer * steps + b * per_b + i, 0)))
            out_specs.append(pl.BlockSpec((slab, cols), lambda b, i: (b * per_b + i, 0)))
            out_shape.append(jax.ShapeDtypeStruct((rows, cols), BF16))
    outs = pl.pallas_call(
        functools.partial(_attn_kernel, cast_weights=expert_w is not None),
        grid=(bsz, per_b),
        in_specs=in_specs,
        out_specs=out_specs,
        out_shape=out_shape,
        scratch_shapes=[pltpu.VMEM((DIFF_HEADS * (DIFF_V_DIM + ONES_ROWS), s), BF16),
                        pltpu.VMEM((MLA_HEADS * (MLA_V_DIM + ONES_ROWS), s), BF16),
                        pltpu.VMEM((n_chains, 1, t), F32),
                        pltpu.VMEM((n_diff, DIFF_V_DIM + ONES_ROWS, t), F32),
                        pltpu.VMEM((MLA_HEADS, MLA_V_DIM + ONES_ROWS, t), F32),
                        pltpu.VMEM((AHEAD, t, t), F32)],
        compiler_params=_params("arbitrary", "arbitrary"),
        name="attention",
    )(*args)
    if expert_w is None:
        return outs[0]
    return outs[0], tuple(wb.reshape(w.shape[1:]) for wb, w in zip(outs[1:], expert_w))


def _post_attn_kernel(o_ref, x_ref, mod_ref, g_ref, wo_ref, wr_ref, x1_ref, h_ref, gates_ref, sel_ref, cnt_ref):
    @pl.when((pl.program_id(0) == 0) & (pl.program_id(1) == 0))
    def _():
        cnt_ref[...] = jnp.zeros(cnt_ref.shape, F32)

    gt_a, sh_f, sc_f = mod_ref[0, 2:3, :], mod_ref[0, 3:4, :], mod_ref[0, 4:5, :]
    tm = x_ref.shape[1]
    n = tm // POST_ROW_GROUPS
    for r0 in range(0, tm, n):
        rows = slice(r0, r0 + n)
        y = jnp.dot(o_ref[0, rows, :], wo_ref[...], preferred_element_type=F32)
        x1 = x_ref[0, rows, :] + (1.0 + gt_a) * y
        x1_ref[0, rows, :] = x1
        h = _rms(x1, g_ref[...]) * (1.0 + sc_f) + sh_f
        _rows_to_tiles(h, h_ref.at[0, pl.ds(r0 * ROW_SUB, n * ROW_SUB)])
        wr = wr_ref[...]
        h_hi, wr_hi = h.astype(BF16), wr.astype(BF16)
        h_lo, wr_lo = (h - h_hi.astype(F32)).astype(BF16), (wr - wr_hi.astype(F32)).astype(BF16)
        logits = (jnp.dot(h_hi, wr_hi, preferred_element_type=F32)
                  + jnp.dot(h_hi, wr_lo, preferred_element_type=F32)
                  + jnp.dot(h_lo, wr_hi, preferred_element_type=F32))
        lane = lax.broadcasted_iota(jnp.int32, logits.shape, 1)
        logits = jnp.where(lane < N_EXPERTS, logits, -jnp.inf)
        v1 = jnp.max(logits, axis=1, keepdims=True)
        i1 = jnp.min(jnp.where(logits == v1, lane, LANES), axis=1, keepdims=True)
        rest = jnp.where(lane == i1, -jnp.inf, logits)
        v2 = jnp.max(rest, axis=1, keepdims=True)
        i2 = jnp.min(jnp.where(rest == v2, lane, LANES), axis=1, keepdims=True)
        e2 = jnp.exp(v2 - v1)
        w1 = 1.0 / (1.0 + e2)
        w2 = e2 / (1.0 + e2)
        gates_ref[0, rows, :] = jnp.where(lane == i1, w1, 0.0) + jnp.where(lane == i2, w2, 0.0)
        sel_ref[0, rows, :] = jnp.where(lane == i1, 1.0, 0.0) + jnp.where(lane == i2, 2.0, 0.0)
        chosen = jnp.where((lane == i1) | (lane == i2), 1.0, 0.0)
        cnt_ref[...] += jnp.sum(chosen, axis=0, keepdims=True)


def _post_attn(o, x, mod, g, wo, w_router, layer):
    bsz, s, d = x.shape
    assert d == ROW_SUB * LANES
    tm = TOK_TILE
    tok = lambda w: pl.BlockSpec((1, tm, w), lambda b, i: (b, i, 0))
    full = lambda a: pl.BlockSpec(a.shape, lambda b, i: (0,) * a.ndim)
    h_spec = pl.BlockSpec((1, tm * ROW_SUB, LANES), lambda b, i: (b, i, 0))
    return pl.pallas_call(
        _post_attn_kernel,
        grid=(bsz, s // tm),
        in_specs=[tok(o.shape[-1]), tok(d), pl.BlockSpec((1, ADA_CHUNKS, d), lambda b, i: (b, 0, 0)),
                  full(g), _layer_spec(wo, layer), full(w_router)],
        out_specs=[tok(d), h_spec, tok(LANES), tok(LANES), pl.BlockSpec((8, LANES), lambda b, i: (0, 0))],
        out_shape=[jax.ShapeDtypeStruct((bsz, s, d), F32), jax.ShapeDtypeStruct((bsz, s * ROW_SUB, LANES), F32),
                   jax.ShapeDtypeStruct((bsz, s, LANES), F32), jax.ShapeDtypeStruct((bsz, s, LANES), F32),
                   jax.ShapeDtypeStruct((8, LANES), F32)],
        compiler_params=_params("arbitrary", "arbitrary"),
        name="post_attn_moe",
    )(o, x, mod, g, wo, w_router)


def _swiglu(h, w1_ref, w3_ref, w2_ref):
    d_ff = w1_ref.shape[-1]
    y = jnp.zeros((h.shape[0], w2_ref.shape[-1]), F32)
    for c0 in range(0, d_ff, FF_CHUNK):
        a = jnp.dot(h, w1_ref[:, c0:c0 + FF_CHUNK], preferred_element_type=F32)
        b = jnp.dot(h, w3_ref[:, c0:c0 + FF_CHUNK], preferred_element_type=F32)
        u = (a * jax.nn.sigmoid(a) * b).astype(BF16)
        y = y + jnp.dot(u, w2_ref[c0:c0 + FF_CHUNK, :], preferred_element_type=F32)
    return y


def _dense_layer_kernel(o_ref, x_ref, mod_ref, g_ref, wo_ref, w1_ref, w3_ref, w2_ref, out_ref):
    gt_a, sh_f, sc_f, gt_f = (mod_ref[0, j:j + 1, :] for j in (2, 3, 4, 5))
    tm = x_ref.shape[1]
    n = tm // POST_ROW_GROUPS
    x1s, hs = [], []
    for r0 in range(0, tm, n):
        rows = slice(r0, r0 + n)
        y = jnp.dot(o_ref[0, rows, :], wo_ref[...], preferred_element_type=F32)
        x1 = x_ref[0, rows, :] + (1.0 + gt_a) * y
        x1s.append(x1)
        hs.append((_rms(x1, g_ref[...]) * (1.0 + sc_f) + sh_f).astype(BF16))
    x1, h = jnp.concatenate(x1s, axis=0), jnp.concatenate(hs, axis=0)
    out_ref[0] = x1 + (1.0 + gt_f) * _swiglu(h, w1_ref, w3_ref, w2_ref)


def _dense_layer(o, x, mod, g, wo, w1, w3, w2, layer, ffn_layer):
    bsz, s, d = x.shape
    tm = DENSE_TILE
    tok = lambda w: pl.BlockSpec((1, tm, w), lambda b, i: (b, i, 0))
    full = lambda a: pl.BlockSpec(a.shape, lambda b, i: (0,) * a.ndim)
    one = lambda a: _layer_spec(a, ffn_layer, pl.Buffered(1))
    return pl.pallas_call(
        _dense_layer_kernel,
        grid=(bsz, s // tm),
        in_specs=[tok(o.shape[-1]), tok(d), pl.BlockSpec((1, ADA_CHUNKS, d), lambda b, i: (b, 0, 0)),
                  full(g), _layer_spec(wo, layer, pl.Buffered(1)), one(w1), one(w3), one(w2)],
        out_specs=tok(d),
        out_shape=jax.ShapeDtypeStruct(x.shape, F32),
        compiler_params=_params("arbitrary", "arbitrary"),
        name="dense_layer",
    )(o, x, mod, g, wo, w1, w3, w2)


def _route_kernel(cnt_ref, sel_ref, pos_ref, te_ref, off_ref, run_ref, *, row_tile):
    t = pl.program_id(0)
    sel_t = sel_ref[...].T
    chosen = (sel_t > 0.0).astype(F32)
    per_expert = jnp.sum(chosen, axis=1, keepdims=True)

    @pl.when(t == 0)
    def _():
        cnt = cnt_ref[...].T[:, 0:1]
        padded = jnp.ceil(cnt / row_tile) * row_tile
        row = lax.broadcasted_iota(jnp.int32, cnt.shape, 0)
        off = jnp.zeros(cnt.shape, F32)
        for e in range(N_EXPERTS):
            size_e = jnp.sum(jnp.where(row == e, padded, 0.0), keepdims=True)
            off = off + jnp.where(row > e, size_e, 0.0)
        off_ref[...] = off
        run_ref[...] = jnp.zeros(run_ref.shape, F32)
        ends = off + padded
        tile_start = lax.broadcasted_iota(jnp.int32, (LANES, LANES), 1).astype(F32) * row_tile
        erow = lax.broadcasted_iota(jnp.int32, (LANES, LANES), 0)
        done = jnp.where((erow < N_EXPERTS) & (ends <= tile_start), 1.0, 0.0)
        te = jnp.sum(done, axis=0, keepdims=True).astype(jnp.int32)
        te_ref[...] = jnp.broadcast_to(te, te_ref.shape)

    tm = sel_t.shape[1]
    before = (lax.broadcasted_iota(jnp.int32, (tm, tm), 0)
              < lax.broadcasted_iota(jnp.int32, (tm, tm), 1)).astype(BF16)
    rank = jnp.dot(chosen.astype(BF16), before, preferred_element_type=F32) + run_ref[...]
    base = off_ref[...] + rank
    for k in range(2):
        pos = jnp.sum(jnp.where(sel_t == float(k + 1), base, 0.0), axis=0, keepdims=True)
        pos_ref[0, :, k * tm:(k + 1) * tm] = pos.astype(jnp.int32)
    run_ref[...] += per_expert


def _route(cnt, sel, row_tile, n_row_tiles):
    n_tok = sel.shape[0]
    assert n_row_tiles <= LANES and 2 * n_tok < 2 ** 24
    tm = MOE_TILE
    nt = n_tok // tm
    pos, te = pl.pallas_call(
        functools.partial(_route_kernel, row_tile=row_tile),
        grid=(nt,),
        in_specs=[pl.BlockSpec(cnt.shape, lambda t: (0, 0)), pl.BlockSpec((tm, LANES), lambda t: (t, 0))],
        out_specs=[pl.BlockSpec((1, 1, 2 * tm), lambda t: (t, 0, 0)), pl.BlockSpec((8, LANES), lambda t: (0, 0))],
        out_shape=[jax.ShapeDtypeStruct((nt, 1, 2 * tm), jnp.int32), jax.ShapeDtypeStruct((8, LANES), jnp.int32)],
        scratch_shapes=[pltpu.VMEM((LANES, 1), F32)] * 2,
        compiler_params=_params("arbitrary"),
        name="moe_route",
    )(cnt, sel)
    return pos, te[0]


def _dispatch_kernel(pos_ref, h_ref, xs_in_ref, xs_ref, sem):
    del xs_in_ref
    tm = h_ref.shape[0] // ROW_SUB

    def body(r, carry):
        for k in range(2):
            row = pos_ref[0, 0, k * tm + r]
            pltpu.make_async_copy(_tile_row(h_ref, r), _tile_row(xs_ref, row), sem).start(priority=k)
        return carry

    lax.fori_loop(0, tm, body, 0, unroll=8)
    for k in range(2):
        pltpu.make_async_copy(h_ref, xs_ref.at[pl.ds(0, tm * ROW_SUB)], sem).wait()


def _dispatch(pos, h, xs0):
    nt, tm = pos.shape[0], pos.shape[2] // 2
    return pl.pallas_call(
        _dispatch_kernel,
        grid=(nt,),
        in_specs=[pl.BlockSpec((1, 1, 2 * tm), lambda t: (t, 0, 0), memory_space=pltpu.SMEM),
                  pl.BlockSpec((tm * ROW_SUB, LANES), lambda t: (t, 0)),
                  pl.BlockSpec(memory_space=pl.ANY)],
        out_specs=pl.BlockSpec(memory_space=pl.ANY),
        out_shape=jax.ShapeDtypeStruct(xs0.shape, xs0.dtype),
        scratch_shapes=[pltpu.SemaphoreType.DMA(())],
        input_output_aliases={2: 0},
        compiler_params=_params("arbitrary"),
        name="moe_dispatch",
    )(pos, h, xs0)


def _expert_kernel(te_ref, xs_ref, w1_ref, w3_ref, w2_ref, y_ref):
    used = te_ref[pl.program_id(0)] < N_EXPERTS

    @pl.when(used)
    def _():
        y = _swiglu(_tiles_to_rows(xs_ref).astype(BF16), w1_ref.at[0], w3_ref.at[0], w2_ref.at[0])
        _rows_to_tiles(y, y_ref)

    @pl.when(jnp.logical_not(used))
    def _():
        y_ref[...] = jnp.zeros(y_ref.shape, y_ref.dtype)


def _experts(te, xs, w1, w3, w2, row_tile):
    n_rows = xs.shape[0] // ROW_SUB
    _, d, d_ff = w1.shape
    expert = lambda n, te: (jnp.minimum(te[n], N_EXPERTS - 1), 0, 0)
    rows = pl.BlockSpec((row_tile * ROW_SUB, LANES), lambda n, te: (n, 0))
    return pl.pallas_call(
        _expert_kernel,
        grid_spec=pltpu.PrefetchScalarGridSpec(
            num_scalar_prefetch=1,
            grid=(n_rows // row_tile,),
            in_specs=[rows, pl.BlockSpec((1, d, d_ff), expert), pl.BlockSpec((1, d, d_ff), expert),
                      pl.BlockSpec((1, d_ff, d), expert)],
            out_specs=rows),
        out_shape=jax.ShapeDtypeStruct(xs.shape, F32),
        compiler_params=_params("arbitrary"),
        name="moe_experts",
    )(te, xs, w1, w3, w2)


def _combine_kernel(pos_ref, pos_next_ref, x1_ref, gates_ref, sel_ref, mod_ref, gfin_ref, ys_ref, o_ref,
                    ybuf_ref, sem, *, final):
    tm = x1_ref.shape[1]
    table_tm = pos_ref.shape[2] // 2
    step = pl.program_id(0) * pl.num_programs(1) + pl.program_id(1)
    n_steps = pl.num_programs(0) * pl.num_programs(1)
    slot = lax.rem(step, 2)

    def gather(p_ref, s, into):
        first = lax.rem(s, table_tm // tm) * tm

        def body(r, carry):
            for k in range(2):
                row = p_ref[0, 0, k * table_tm + first + r]
                pltpu.make_async_copy(_tile_row(ys_ref, row), _tile_row(ybuf_ref.at[into, k], r),
                                      sem.at[into]).start(priority=k)
            return carry

        lax.fori_loop(0, tm, body, 0, unroll=8)

    @pl.when(step == 0)
    def _():
        gather(pos_ref, step, slot)

    @pl.when(step + 1 < n_steps)
    def _():
        gather(pos_next_ref, step + 1, 1 - slot)

    gates, sel = gates_ref[0], sel_ref[0]
    w_a = jnp.sum(jnp.where(sel == 1.0, gates, 0.0), axis=1, keepdims=True)
    w_b = jnp.sum(jnp.where(sel == 2.0, gates, 0.0), axis=1, keepdims=True)
    for k in range(2):
        pltpu.make_async_copy(ys_ref.at[pl.ds(0, tm * ROW_SUB)], ybuf_ref.at[slot, k], sem.at[slot]).wait()
    gt_f = mod_ref[0, 5:6, :]
    y = w_a * _tiles_to_rows(ybuf_ref.at[slot, 0]) + w_b * _tiles_to_rows(ybuf_ref.at[slot, 1])
    x = x1_ref[0] + (1.0 + gt_f) * y
    o_ref[0] = _rms(x, gfin_ref[...]) if final else x


def _combine(pos, x1, gates, sel, mod, ys, g_final=None):
    bsz, s, d = x1.shape
    nt, table_tm = pos.shape[0], pos.shape[2] // 2
    tm = COMBINE_TILE
    ratio = table_tm // tm
    assert ratio * tm == table_tm and s % tm == 0
    per_b = s // tm
    tok = lambda w: pl.BlockSpec((1, tm, w), lambda b, i: (b, i, 0))
    final = g_final is not None
    gfin = g_final if final else jnp.ones((1, d), F32)
    table = lambda ahead: pl.BlockSpec(
        (1, 1, 2 * table_tm), lambda b, i: (jnp.minimum((b * per_b + i + ahead) // ratio, nt - 1), 0, 0),
        memory_space=pltpu.SMEM)
    return pl.pallas_call(
        functools.partial(_combine_kernel, final=final),
        grid=(bsz, per_b),
        in_specs=[table(0), table(1),
                  tok(d), tok(LANES), tok(LANES),
                  pl.BlockSpec((1, ADA_CHUNKS, d), lambda b, i: (b, 0, 0)),
                  pl.BlockSpec((1, d), lambda b, i: (0, 0)),
                  pl.BlockSpec(memory_space=pl.ANY)],
        out_specs=tok(d),
        out_shape=jax.ShapeDtypeStruct(x1.shape, F32),
        scratch_shapes=[pltpu.VMEM((2, 2, tm * ROW_SUB, LANES), F32), pltpu.SemaphoreType.DMA((2,))],
        compiler_params=_params("arbitrary", "arbitrary"),
        name="moe_combine",
    )(pos, pos, x1, gates, sel, mod, gfin, ys)


def _moe(h, x1, gates, sel, cnt, mod, w1, w3, w2, xs_buf, g_final):
    bsz, s, d = x1.shape
    n_tok = bsz * s
    row_tile = FFN_TILE
    n_rows = xs_buf.shape[0] // ROW_SUB
    assert n_rows == _moe_rows(n_tok)
    pos, te = _route(cnt, sel.reshape(n_tok, LANES), row_tile, n_rows // row_tile)
    xs = _dispatch(pos, h.reshape(n_tok * ROW_SUB, LANES), xs_buf)
    ys = _experts(te, xs, w1, w3, w2, row_tile)
    return _combine(pos, x1, gates, sel, mod, ys, g_final), xs


def _moe_rows(n_tok):
    return 2 * n_tok + N_EXPERTS * FFN_TILE


def _final_norm_kernel(x_ref, g_ref, o_ref):
    o_ref[0] = _rms(x_ref[0], g_ref[...])


def _final_norm(x, g):
    bsz, s, d = x.shape
    tm = TOK_TILE
    tok = pl.BlockSpec((1, tm, d), lambda b, i: (b, i, 0))
    return pl.pallas_call(
        _final_norm_kernel,
        grid=(bsz, s // tm),
        in_specs=[tok, pl.BlockSpec(g.shape, lambda b, i: (0, 0))],
        out_specs=tok,
        out_shape=jax.ShapeDtypeStruct(x.shape, F32),
        compiler_params=_params("arbitrary", "arbitrary"),
        name="final_norm",
    )(x, g)


def _head_blocks(w, width, nope_cols, rope_cols):
    lead = w.shape[:-1]
    w = w.reshape(lead + (MLA_HEADS, width))
    nope = w[..., nope_cols] if nope_cols is not None else jnp.zeros(lead + (MLA_HEADS, MLA_NOPE_DIM), w.dtype)
    rope = w[..., rope_cols] if rope_cols is not None else jnp.zeros(lead + (MLA_HEADS, MLA_ROPE_DIM), w.dtype)
    return _head_block(nope, rope).reshape(lead + (MLA_HEADS * LANES,))


def _prep_w_in(w_in):
    kr_block = _head_block(jnp.zeros(w_in.shape[:-1] + (MLA_NOPE_DIM,), w_in.dtype), w_in[..., C_KR:])
    return jnp.concatenate([w_in[..., :C_KR], kr_block], axis=-1).astype(BF16)


def _layer_spec(a, layer, pipeline_mode=None):
    return pl.BlockSpec((None,) + a.shape[1:], lambda b, i: (layer,) + (0,) * (a.ndim - 1),
                        pipeline_mode=pipeline_mode)


def kernel(x, c, positions, w_ada, b_ada, g_attn, w_in, diff_lambda, diff_subln_g, rel_bias, mla_q_norm, w_uq, mla_kv_norm, w_ukv, w_o, g_ffn, ffn_w1, ffn_w3, ffn_w2, moe_router, moe_w1, moe_w3, moe_w2, g_final):
    depth = w_ada.shape[0]
    bsz, s, d = x.shape
    mods = _ada(c, w_ada, b_ada).reshape(depth, bsz, ADA_CHUNKS, d)
    rc, rs = _rope_tables(positions)
    bias = _bias_tiles(rel_bias)
    qk_w = MLA_NOPE_DIM + MLA_ROPE_DIM
    kv_w = MLA_NOPE_DIM + MLA_V_DIM
    xs_buf = jnp.zeros((_moe_rows(bsz * s) * ROW_SUB, LANES), F32)
    ffn_w = (ffn_w1.astype(BF16), ffn_w3.astype(BF16), ffn_w2.astype(BF16))
    wo = w_o.astype(BF16)
    w_in_p = _prep_w_in(w_in)
    w_uq_p = _head_blocks(w_uq, qk_w, slice(0, MLA_NOPE_DIM), slice(MLA_NOPE_DIM, qk_w)).astype(BF16)
    w_uk_p = _head_blocks(w_ukv, kv_w, slice(0, MLA_NOPE_DIM), None).astype(BF16)
    w_uv_p = w_ukv.reshape(depth, MLA_KV_RANK, MLA_HEADS, kv_w)[..., MLA_NOPE_DIM:].reshape(
        depth, MLA_KV_RANK, MLA_HEADS * MLA_V_DIM).astype(BF16)
    for l in range(depth):
        mod = mods[l]
        lam_init = jnp.full((1,), 0.8 - 0.6 * math.exp(-0.3 * l), F32)
        dq, dk, dv, mq, mk, mv = _pre_attn(
            x, mod, g_attn[l].reshape(1, d), w_in_p, mla_q_norm[l].reshape(1, -1), w_uq_p,
            mla_kv_norm[l].reshape(1, -1), w_uk_p, w_uv_p, rc, rs, l)
        attn_args = (lam_init, diff_lambda[l], diff_subln_g[l].reshape(-1, 1), bias, dq, dk, dv, mq, mk, mv)
        g_f = g_ffn[l].reshape(1, d)
        if l % 2 == 1:
            o, expert_w = _attention(*attn_args, expert_w=(moe_w1, moe_w3, moe_w2), expert_layer=l // 2)
            w_router = jnp.pad(moe_router[l // 2], ((0, 0), (0, LANES - N_EXPERTS)))
            x1, h, gates, sel, cnt = _post_attn(o, x, mod, g_f, wo, w_router, l)
            g_fin = g_final.reshape(1, d) if l == depth - 1 else None
            x, xs_buf = _moe(h, x1, gates, sel, cnt, mod, *expert_w, xs_buf, g_fin)
        else:
            o = _attention(*attn_args)
            x = _dense_layer(o, x, mod, g_f, wo, *ffn_w, l, l // 2)
    return x if depth % 2 == 0 else _final_norm(x, g_final.reshape(1, d))
```
